```python
import jax
import jax.numpy as jnp
from jax import lax
import numpy as np

D_MODEL = 4096
BATCH = 16
SEQ = 2048
DEPTH = 1

HEAD_DIM = 128
N_HEADS_TOTAL = D_MODEL // HEAD_DIM
N_HEADS_DIL = N_HEADS_TOTAL // 2
N_HEADS_SB = N_HEADS_TOTAL - N_HEADS_DIL
DIL_WIDTH = N_HEADS_DIL * HEAD_DIM
SB_WIDTH = N_HEADS_SB * HEAD_DIM
MIX_WIDTH = DIL_WIDTH + SB_WIDTH
IN_PROJ_WIDTH = 3 * DIL_WIDTH + 3 * SB_WIDTH
DILATED_PAIRS = ((128, 1), (512, 4), (2048, 16))
BLOCK = 128
ROPE_THETA = 500000.0
ROPE_DIM = HEAD_DIM // 4
D_FF = -(-8 * D_MODEL // (3 * 256)) * 256
RMS_EPS = 1e-5

kernel_name = "hymba_dilated_stickbreaking_swiglu"


def rmsnorm(x, g):
    xf = x.astype(jnp.float32)
    y = xf * lax.rsqrt(jnp.mean(xf * xf, axis=-1, keepdims=True) + RMS_EPS)
    return (y * g.astype(jnp.float32)).astype(x.dtype)


def apply_partial_rope(a, positions):
    half = ROPE_DIM // 2
    inv_freq = jnp.power(jnp.float32(ROPE_THETA), -jnp.arange(half, dtype=jnp.float32) / half)
    ang = positions.astype(jnp.float32)[..., None] * inv_freq
    cos = jnp.cos(ang)[:, :, None, :]
    sin = jnp.sin(ang)[:, :, None, :]
    a1 = a[..., :half].astype(jnp.float32)
    a2 = a[..., half:ROPE_DIM].astype(jnp.float32)
    rot = jnp.concatenate([a1 * cos - a2 * sin, a2 * cos + a1 * sin], axis=-1).astype(a.dtype)
    return jnp.concatenate([rot, a[..., ROPE_DIM:]], axis=-1)


def dilated_branch(q, k, v, window, dilation):
    B, T, H, Dh = q.shape
    steps = window // dilation
    L = T // dilation
    nb = -(-L // BLOCK)
    Lp = nb * BLOCK
    pad_tail = ((0, 0), (0, Lp - L), (0, 0), (0, 0), (0, 0))

    def to_sub(a):
        return jnp.pad(a.reshape(B, L, dilation, H, Dh), pad_tail)

    def with_prev(a):
        a = jnp.pad(a, ((0, 0), (BLOCK, 0), (0, 0), (0, 0), (0, 0)))
        a = a.reshape(B, nb + 1, BLOCK, dilation, H, Dh)
        return jnp.concatenate([a[:, :-1], a[:, 1:]], axis=2)

    qb = to_sub(q).reshape(B, nb, BLOCK, dilation, H, Dh)
    kc = with_prev(to_sub(k))
    vc = with_prev(to_sub(v))
    s = jnp.einsum('bnqrhd,bnkrhd->bnrhqk', qb, kc).astype(jnp.float32) * (Dh ** -0.5)
    qi = jnp.arange(BLOCK)[:, None] + BLOCK
    ki = jnp.arange(2 * BLOCK)[None, :]
    dist = qi - ki
    k_glob = jnp.arange(nb)[:, None, None] * BLOCK + ki[None] - BLOCK
    mask = ((dist >= 0) & (dist <= steps))[None] & (k_glob >= 0)
    s = jnp.where(mask[None, :, None, None], s, -jnp.inf)
    m = jnp.max(s, axis=-1, keepdims=True)
    p = jnp.exp(s - m)
    den = jnp.sum(p, axis=-1, keepdims=True)
    o = jnp.einsum('bnrhqk,bnkrhd->bnqrhd', p / den, vc.astype(jnp.float32))
    o = o.reshape(B, Lp, dilation, H, Dh)[:, :L].reshape(B, T, H, Dh)
    lse = (m + jnp.log(den))[..., 0]
    lse = lse.transpose(0, 1, 4, 2, 3).reshape(B, Lp, dilation, H)[:, :L].reshape(B, T, H)
    return o, lse


def dilated_attention(q, k, v):
    outs, lses = [], []
    for window, dilation in DILATED_PAIRS:
        o, lse = dilated_branch(q, k, v, window, dilation)
        outs.append(o)
        lses.append(lse)
    alpha = jax.nn.softmax(jnp.stack(lses, axis=0), axis=0)
    return jnp.sum(alpha[..., None] * jnp.stack(outs, axis=0), axis=0)


def stick_breaking_attention(q, k, v):
    B, T, H, Dh = q.shape
    outs = []
    for i in range(T // BLOCK):
        end = (i + 1) * BLOCK
        qi = q[:, i * BLOCK:end]
        z = jnp.einsum('bqhd,bkhd->bhqk', qi, k[:, :end]).astype(jnp.float32) * (Dh ** -0.5)
        q_pos = i * BLOCK + jnp.arange(BLOCK)[:, None]
        k_pos = jnp.arange(end)[None, :]
        strict = k_pos < q_pos
        log_not = jnp.where(strict, -jax.nn.softplus(z), 0.0)
        excl = lax.cumsum(log_not, axis=3, reverse=True) - log_not
        a = jnp.where(strict, jnp.exp(jax.nn.log_sigmoid(z) + excl), 0.0)
        outs.append(jnp.einsum('bhqk,bkhd->bqhd', a, v[:, :end].astype(jnp.float32)))
    return jnp.concatenate(outs, axis=1)


def head_rmsnorm(o, g, n_heads):
    of = o.astype(jnp.float32)
    y = of * lax.rsqrt(jnp.mean(of * of, axis=-1, keepdims=True) + RMS_EPS)
    y = y * g.astype(jnp.float32).reshape(n_heads, HEAD_DIM)
    return y.reshape(o.shape[0], o.shape[1], n_heads * HEAD_DIM)


def _fwd_setup_inputs(seed: int = 0) -> dict:
    key = jax.random.key(seed)
    ks = jax.random.split(key, 12)

    def normal(k, shape, scale):
        return jax.random.normal(k, shape, jnp.float32) * scale

    x = normal(ks[0], (BATCH, SEQ, D_MODEL), 1.0)
    offsets = jax.random.randint(ks[1], (BATCH, 1), 0, 4096, dtype=jnp.int32)
    positions = offsets + jnp.arange(SEQ, dtype=jnp.int32)[None, :]
    norm_mix_g = 1.0 + normal(ks[2], (DEPTH, D_MODEL), 0.02)
    w_in = normal(ks[3], (DEPTH, D_MODEL, IN_PROJ_WIDTH), D_MODEL ** -0.5)
    norm_out_dil_g = 1.0 + normal(ks[4], (DEPTH, DIL_WIDTH), 0.02)
    norm_out_sb_g = 1.0 + normal(ks[5], (DEPTH, SB_WIDTH), 0.02)
    w_out = normal(ks[6], (DEPTH, MIX_WIDTH, D_MODEL), MIX_WIDTH ** -0.5)
    norm_ffn_g = 1.0 + normal(ks[7], (DEPTH, D_MODEL), 0.02)
    w_gate = normal(ks[8], (DEPTH, D_MODEL, D_FF), D_MODEL ** -0.5)
    w_up = normal(ks[9], (DEPTH, D_MODEL, D_FF), D_MODEL ** -0.5)
    w_down = normal(ks[10], (DEPTH, D_FF, D_MODEL), D_FF ** -0.5)
    norm_final_g = 1.0 + normal(ks[11], (D_MODEL,), 0.02)
    return {"x": x, "positions": positions, "norm_mix_g": norm_mix_g, "w_in": w_in,
            "norm_out_dil_g": norm_out_dil_g, "norm_out_sb_g": norm_out_sb_g, "w_out": w_out,
            "norm_ffn_g": norm_ffn_g, "w_gate": w_gate, "w_up": w_up, "w_down": w_down,
            "norm_final_g": norm_final_g}


def _fwd_reference(x, positions, norm_mix_g, w_in, norm_out_dil_g, norm_out_sb_g, w_out,
              norm_ffn_g, w_gate, w_up, w_down, norm_final_g):
    B, T, _ = x.shape
    h = x
    for layer in range(DEPTH):
        hn = rmsnorm(h, norm_mix_g[layer])
        proj = jnp.einsum('btd,de->bte', hn, w_in[layer])
        qa, ka, va, qb, kb, vb = jnp.split(
            proj, [DIL_WIDTH, 2 * DIL_WIDTH, 3 * DIL_WIDTH,
                   3 * DIL_WIDTH + SB_WIDTH, 3 * DIL_WIDTH + 2 * SB_WIDTH], axis=-1)
        qa = apply_partial_rope(qa.reshape(B, T, N_HEADS_DIL, HEAD_DIM), positions)
        ka = apply_partial_rope(ka.reshape(B, T, N_HEADS_DIL, HEAD_DIM), positions)
        va = va.reshape(B, T, N_HEADS_DIL, HEAD_DIM)
        qb = qb.reshape(B, T, N_HEADS_SB, HEAD_DIM)
        kb = kb.reshape(B, T, N_HEADS_SB, HEAD_DIM)
        vb = vb.reshape(B, T, N_HEADS_SB, HEAD_DIM)
        o_dil = head_rmsnorm(dilated_attention(qa, ka, va), norm_out_dil_g[layer], N_HEADS_DIL)
        o_sb = head_rmsnorm(stick_breaking_attention(qb, kb, vb), norm_out_sb_g[layer], N_HEADS_SB)
        o_mix = jnp.concatenate([o_dil, o_sb], axis=-1).astype(h.dtype)
        h = h + jnp.einsum('bte,ed->btd', o_mix, w_out[layer])
        hn = rmsnorm(h, norm_ffn_g[layer])
        gate = jnp.einsum('btd,df->btf', hn, w_gate[layer])
        up = jnp.einsum('btd,df->btf', hn, w_up[layer])
        h = h + jnp.einsum('btf,fd->btd', jax.nn.silu(gate) * up, w_down[layer])
    return rmsnorm(h, norm_final_g)


import jax as _jax
import jax.numpy as _jnp

TWIN_FORMAT = 'train_step'
FWD_PARAMS = ['x', 'positions', 'norm_mix_g', 'w_in', 'norm_out_dil_g', 'norm_out_sb_g', 'w_out', 'norm_ffn_g', 'w_gate', 'w_up', 'w_down', 'norm_final_g']
TWIN_WEIGHTS = ['norm_mix_g', 'w_in', 'norm_out_dil_g', 'norm_out_sb_g', 'w_out', 'norm_ffn_g', 'w_gate', 'w_up', 'w_down', 'norm_final_g']
TWIN_DIFF_INPUT = 'x'
TWIN_INPUTS = ['x', 'positions', 'norm_mix_g', 'w_in', 'norm_out_dil_g', 'norm_out_sb_g', 'w_out', 'norm_ffn_g', 'w_gate', 'w_up', 'w_down', 'norm_final_g', 'loss_target', 'm_norm_mix_g', 'm_w_in', 'm_norm_out_dil_g', 'm_norm_out_sb_g', 'm_w_out', 'm_norm_ffn_g', 'm_w_gate', 'm_w_up', 'm_w_down', 'm_norm_final_g', 'v_norm_mix_g', 'v_w_in', 'v_norm_out_dil_g', 'v_norm_out_sb_g', 'v_w_out', 'v_norm_ffn_g', 'v_w_gate', 'v_w_up', 'v_w_down', 'v_norm_final_g']
TWIN_OUTPUTS = ['loss', 'grad_x', 'grad_norm_mix_g', 'grad_w_in', 'grad_norm_out_dil_g', 'grad_norm_out_sb_g', 'grad_w_out', 'grad_norm_ffn_g', 'grad_w_gate', 'grad_w_up', 'grad_w_down', 'grad_norm_final_g', 'delta_norm_mix_g', 'delta_w_in', 'delta_norm_out_dil_g', 'delta_norm_out_sb_g', 'delta_w_out', 'delta_norm_ffn_g', 'delta_w_gate', 'delta_w_up', 'delta_w_down', 'delta_norm_final_g', 'new_m_norm_mix_g', 'new_m_w_in', 'new_m_norm_out_dil_g', 'new_m_norm_out_sb_g', 'new_m_w_out', 'new_m_norm_ffn_g', 'new_m_w_gate', 'new_m_w_up', 'new_m_w_down', 'new_m_norm_final_g', 'new_v_norm_mix_g', 'new_v_w_in', 'new_v_norm_out_dil_g', 'new_v_norm_out_sb_g', 'new_v_w_out', 'new_v_norm_ffn_g', 'new_v_w_gate', 'new_v_w_up', 'new_v_w_down', 'new_v_norm_final_g']
TWIN_LEAF_KINDS = {'loss': 'loss', 'grad_x': 'grad_x', 'grad_norm_mix_g': 'grad_w', 'grad_w_in': 'grad_w', 'grad_norm_out_dil_g': 'grad_w', 'grad_norm_out_sb_g': 'grad_w', 'grad_w_out': 'grad_w', 'grad_norm_ffn_g': 'grad_w', 'grad_w_gate': 'grad_w', 'grad_w_up': 'grad_w', 'grad_w_down': 'grad_w', 'grad_norm_final_g': 'grad_w', 'delta_norm_mix_g': 'delta_w', 'delta_w_in': 'delta_w', 'delta_norm_out_dil_g': 'delta_w', 'delta_norm_out_sb_g': 'delta_w', 'delta_w_out': 'delta_w', 'delta_norm_ffn_g': 'delta_w', 'delta_w_gate': 'delta_w', 'delta_w_up': 'delta_w', 'delta_w_down': 'delta_w', 'delta_norm_final_g': 'delta_w', 'new_m_norm_mix_g': 'new_m', 'new_m_w_in': 'new_m', 'new_m_norm_out_dil_g': 'new_m', 'new_m_norm_out_sb_g': 'new_m', 'new_m_w_out': 'new_m', 'new_m_norm_ffn_g': 'new_m', 'new_m_w_gate': 'new_m', 'new_m_w_up': 'new_m', 'new_m_w_down': 'new_m', 'new_m_norm_final_g': 'new_m', 'new_v_norm_mix_g': 'new_v', 'new_v_w_in': 'new_v', 'new_v_norm_out_dil_g': 'new_v', 'new_v_norm_out_sb_g': 'new_v', 'new_v_w_out': 'new_v', 'new_v_norm_ffn_g': 'new_v', 'new_v_w_gate': 'new_v', 'new_v_w_up': 'new_v', 'new_v_w_down': 'new_v', 'new_v_norm_final_g': 'new_v'}


def _forward(args):
    return _fwd_reference(*[args[k] for k in FWD_PARAMS])


def _output_shape():
    def fwd():
        inp = _fwd_setup_inputs(0)
        return _fwd_reference(*[inp[k] for k in FWD_PARAMS])
    out = _jax.eval_shape(fwd)
    return out.shape, out.dtype

N_MICROBATCH = 1
ADAM_LR = 0.001
ADAM_B1 = 0.9
ADAM_B2 = 0.999
ADAM_EPS = 1e-08
ADAM_WD = 0.01
ADAM_STEP = 10
PER_EXAMPLE_BATCH_AXIS = {'x': 0, 'positions': 0, 'loss_target': 0}
SHARED_INPUTS = []
_WEIGHT_DTYPES = {'norm_mix_g': _jnp.float32, 'w_in': _jnp.float32, 'norm_out_dil_g': _jnp.float32, 'norm_out_sb_g': _jnp.float32, 'w_out': _jnp.float32, 'norm_ffn_g': _jnp.float32, 'w_gate': _jnp.float32, 'w_up': _jnp.float32, 'w_down': _jnp.float32, 'norm_final_g': _jnp.float32}
MOMENT_SCALE = {'norm_mix_g': 5.044686e-02, 'w_in': 2.877741e-02, 'norm_out_dil_g': 3.446406e-02, 'norm_out_sb_g': 3.366935e-02, 'w_out': 3.380331e-02, 'norm_ffn_g': 2.530315e-02, 'w_gate': 1.083995e-02, 'w_up': 1.049881e-02, 'w_down': 1.720079e-02, 'norm_final_g': 7.996303e+00}


def _to_microbatches(a, axis):
    t = _jnp.moveaxis(a, axis, 0)
    t = t.reshape((N_MICROBATCH, t.shape[0] // N_MICROBATCH) + t.shape[1:])
    return _jnp.moveaxis(t, 1, axis + 1)


def setup_inputs(seed: int = 0) -> dict:
    inp = _fwd_setup_inputs(seed)
    key = _jax.random.fold_in(_jax.random.key(seed), 7919)
    shape, _ = _output_shape()
    out = dict(inp)
    out["loss_target"] = _jax.random.normal(_jax.random.fold_in(key, 0), shape, _jnp.float32)
    for i, name in enumerate(TWIN_WEIGHTS):
        w = inp[name].astype(_jnp.float32)
        if MOMENT_SCALE is None:
            s = _jnp.sqrt(_jnp.mean(_jnp.square(w)) + 1e-30)
        else:
            s = MOMENT_SCALE[name]
        km, kv = _jax.random.split(_jax.random.fold_in(key, i + 1))
        out[name] = w
        out["m_" + name] = s * _jax.random.normal(km, w.shape, _jnp.float32)
        out["v_" + name] = (s * s) * _jax.random.uniform(kv, w.shape, _jnp.float32, 0.5, 1.5)
    if N_MICROBATCH > 1:
        for name, axis in PER_EXAMPLE_BATCH_AXIS.items():
            out[name] = _to_microbatches(out[name], axis)
    return {'x': out['x'], 'positions': out['positions'], 'norm_mix_g': out['norm_mix_g'], 'w_in': out['w_in'], 'norm_out_dil_g': out['norm_out_dil_g'], 'norm_out_sb_g': out['norm_out_sb_g'], 'w_out': out['w_out'], 'norm_ffn_g': out['norm_ffn_g'], 'w_gate': out['w_gate'], 'w_up': out['w_up'], 'w_down': out['w_down'], 'norm_final_g': out['norm_final_g'], 'loss_target': out['loss_target'], 'm_norm_mix_g': out['m_norm_mix_g'], 'm_w_in': out['m_w_in'], 'm_norm_out_dil_g': out['m_norm_out_dil_g'], 'm_norm_out_sb_g': out['m_norm_out_sb_g'], 'm_w_out': out['m_w_out'], 'm_norm_ffn_g': out['m_norm_ffn_g'], 'm_w_gate': out['m_w_gate'], 'm_w_up': out['m_w_up'], 'm_w_down': out['m_w_down'], 'm_norm_final_g': out['m_norm_final_g'], 'v_norm_mix_g': out['v_norm_mix_g'], 'v_w_in': out['v_w_in'], 'v_norm_out_dil_g': out['v_norm_out_dil_g'], 'v_norm_out_sb_g': out['v_norm_out_sb_g'], 'v_w_out': out['v_w_out'], 'v_norm_ffn_g': out['v_norm_ffn_g'], 'v_w_gate': out['v_w_gate'], 'v_w_up': out['v_w_up'], 'v_w_down': out['v_w_down'], 'v_norm_final_g': out['v_norm_final_g']}


def _loss(weights, diff, rest, loss_target):
    with _jax.named_scope("forward"):
        args = {**rest, TWIN_DIFF_INPUT: diff, **{k: w.astype(_WEIGHT_DTYPES[k]) for k, w in weights.items()}}
        y = _forward(args)
    with _jax.named_scope("loss_head"):
        err = _jnp.square(y.astype(_jnp.float32) - loss_target)
        return 0.5 * _jnp.sum(_jnp.mean(err, axis=-1)) if err.ndim else 0.5 * err


def _adamw(w, g, m, v):
    m = ADAM_B1 * m + (1.0 - ADAM_B1) * g
    v = ADAM_B2 * v + (1.0 - ADAM_B2) * _jnp.square(g)
    m_hat = m / (1.0 - ADAM_B1 ** ADAM_STEP)
    v_hat = v / (1.0 - ADAM_B2 ** ADAM_STEP)
    delta = -ADAM_LR * (m_hat / (_jnp.sqrt(v_hat) + ADAM_EPS) + ADAM_WD * w)
    return delta, m, v


def reference(x, positions, norm_mix_g, w_in, norm_out_dil_g, norm_out_sb_g, w_out, norm_ffn_g, w_gate, w_up, w_down, norm_final_g, loss_target, m_norm_mix_g, m_w_in, m_norm_out_dil_g, m_norm_out_sb_g, m_w_out, m_norm_ffn_g, m_w_gate, m_w_up, m_w_down, m_norm_final_g, v_norm_mix_g, v_w_in, v_norm_out_dil_g, v_norm_out_sb_g, v_w_out, v_norm_ffn_g, v_w_gate, v_w_up, v_w_down, v_norm_final_g):
    given = dict(x=x, positions=positions, norm_mix_g=norm_mix_g, w_in=w_in, norm_out_dil_g=norm_out_dil_g, norm_out_sb_g=norm_out_sb_g, w_out=w_out, norm_ffn_g=norm_ffn_g, w_gate=w_gate, w_up=w_up, w_down=w_down, norm_final_g=norm_final_g, loss_target=loss_target, m_norm_mix_g=m_norm_mix_g, m_w_in=m_w_in, m_norm_out_dil_g=m_norm_out_dil_g, m_norm_out_sb_g=m_norm_out_sb_g, m_w_out=m_w_out, m_norm_ffn_g=m_norm_ffn_g, m_w_gate=m_w_gate, m_w_up=m_w_up, m_w_down=m_w_down, m_norm_final_g=m_norm_final_g, v_norm_mix_g=v_norm_mix_g, v_w_in=v_w_in, v_norm_out_dil_g=v_norm_out_dil_g, v_norm_out_sb_g=v_norm_out_sb_g, v_w_out=v_w_out, v_norm_ffn_g=v_norm_ffn_g, v_w_gate=v_w_gate, v_w_up=v_w_up, v_w_down=v_w_down, v_norm_final_g=v_norm_final_g)
    weights = {n: given[n] for n in TWIN_WEIGHTS}
    shared = {n: given[n] for n in SHARED_INPUTS}
    per_example = {n: given[n] for n in ['x', 'positions']}
    grad_fn = _jax.value_and_grad(_loss, argnums=(0, 1))

    def one_microbatch(ex, loss_target):
        ex = dict(ex)
        diff = ex.pop(TWIN_DIFF_INPUT)
        return grad_fn(weights, diff, {**shared, **ex}, loss_target)

    if N_MICROBATCH == 1:
        loss, (grad_w, grad_x) = one_microbatch(per_example, given["loss_target"])
    else:
        def body(carry, xs):
            loss_sum, grad_sum = carry
            l_k, (gw_k, gx_k) = one_microbatch(xs[0], xs[1])
            with _jax.named_scope("update"):
                return (loss_sum + l_k, _jax.tree.map(_jnp.add, grad_sum, gw_k)), gx_k

        init = (_jnp.zeros((), _jnp.float32), _jax.tree.map(_jnp.zeros_like, weights))
        (loss, grad_w), grad_x = _jax.lax.scan(body, init, (per_example, given["loss_target"]))
    with _jax.named_scope("update"):
        delta_w, new_m, new_v = {}, {}, {}
        for n in TWIN_WEIGHTS:
            delta_w[n], new_m[n], new_v[n] = _adamw(weights[n], grad_w[n], given["m_" + n], given["v_" + n])
    return (loss, grad_x, *[grad_w[n] for n in TWIN_WEIGHTS], *[delta_w[n] for n in TWIN_WEIGHTS],
            *[new_m[n] for n in TWIN_WEIGHTS], *[new_v[n] for n in TWIN_WEIGHTS])
```

```python
import functools

import jax
import jax.numpy as jnp
from jax import lax
from jax.experimental import pallas as pl
from jax.experimental.pallas import tpu as pltpu

F32 = jnp.float32
BF16 = jnp.bfloat16
MESH = pl.DeviceIdType.MESH
ANY = pl.BlockSpec(memory_space=pl.ANY)

LANES = 128
HEAD_DIM = 128
BLOCK = 128
DILATIONS = (1, 4, 16)
ROPE_DIM = 32
ROPE_THETA = 500000.0
RMS_EPS = 1e-5
ADAM_LR, ADAM_B1, ADAM_B2, ADAM_EPS, ADAM_WD, ADAM_STEP = 0.001, 0.9, 0.999, 1e-08, 0.01, 10
N_CHIPS = 4
VMEM_LIMIT = 48 * 1024 * 1024

NN = ((1,), (0,))
NT = ((1,), (1,))
TN = ((0,), (0,))


def _dot(a, b, dims):
    return lax.dot_general(a, b, (dims, ((), ())), preferred_element_type=F32)


def _tile(dim, pref, unit=LANES):
    if dim <= pref:
        return dim
    t = (pref // unit) * unit
    while t > unit and dim % t:
        t -= unit
    assert dim % t == 0, (dim, pref, unit)
    return t


def _params(sem=None):
    return pltpu.CompilerParams(dimension_semantics=sem, vmem_limit_bytes=VMEM_LIMIT)


def _cast_pad(w, cols_out, name):
    R, C = w.shape
    tr = _tile(R, 256, 16)

    def body(w_ref, o_ref):
        if cols_out != C:
            o_ref[...] = jnp.zeros(o_ref.shape, BF16)
            o_ref[:, :C] = w_ref[...].astype(BF16)
        else:
            o_ref[...] = w_ref[...].astype(BF16)

    return pl.pallas_call(
        body, name=name, grid=(R // tr,),
        in_specs=[pl.BlockSpec((tr, C), lambda i: (i, 0))],
        out_specs=pl.BlockSpec((tr, cols_out), lambda i: (i, 0)),
        out_shape=jax.ShapeDtypeStruct((R, cols_out), BF16),
        compiler_params=_params(("parallel",)),
    )(w)


def _matmul(a, b, mode, out_dtype, name, res=None, tm=1024, tn=1024, tk=1024):
    if mode == "nn":
        (M, K), (_, N) = a.shape, b.shape
    elif mode == "nt":
        (M, K), (N, _) = a.shape, b.shape
    else:
        (K, M), (_, N) = a.shape, b.shape
    tm, tn, tk = _tile(M, tm), _tile(N, tn), _tile(K, tk)
    nk = K // tk
    dims = {"nn": NN, "nt": NT, "tn": TN}[mode]
    a_spec = (pl.BlockSpec((tk, tm), lambda i, j, k: (k, i)) if mode == "tn"
              else pl.BlockSpec((tm, tk), lambda i, j, k: (i, k)))
    b_spec = (pl.BlockSpec((tn, tk), lambda i, j, k: (j, k)) if mode == "nt"
              else pl.BlockSpec((tk, tn), lambda i, j, k: (k, j)))
    o_spec = pl.BlockSpec((tm, tn), lambda i, j, k: (i, j))

    def body(*refs):
        if res is None:
            a_ref, b_ref, o_ref, acc_ref = refs
        else:
            a_ref, b_ref, r_ref, o_ref, acc_ref = refs
        k = pl.program_id(2)
        part = _dot(a_ref[...], b_ref[...], dims)

        @pl.when(k == 0)
        def _():
            acc_ref[...] = part

        @pl.when(k > 0)
        def _():
            acc_ref[...] += part

        @pl.when(k == nk - 1)
        def _():
            out = acc_ref[...]
            if res is not None:
                out = out + r_ref[...]
            o_ref[...] = out.astype(out_dtype)

    in_specs = [a_spec, b_spec] + ([o_spec] if res is not None else [])
    args = (a, b) + ((res,) if res is not None else ())
    return pl.pallas_call(
        body, name=name, grid=(M // tm, N // tn, nk),
        in_specs=in_specs, out_specs=o_spec,
        out_shape=jax.ShapeDtypeStruct((M, N), out_dtype),
        scratch_shapes=[pltpu.VMEM((tm, tn), F32)],
        compiler_params=_params(("parallel", "parallel", "arbitrary")),
    )(*args)


def _gate_up(hn, wg, wu, name):
    M, K = hn.shape
    N = wg.shape[1]
    tm, tn, tk = _tile(M, 1024), _tile(N, 1024), _tile(K, 1024)
    nk = K // tk

    def body(a_ref, g_ref, u_ref, og_ref, ou_ref, oa_ref, accg, accu):
        k = pl.program_id(2)
        a = a_ref[...]
        pg = _dot(a, g_ref[...], NN)
        pu = _dot(a, u_ref[...], NN)

        @pl.when(k == 0)
        def _():
            accg[...] = pg
            accu[...] = pu

        @pl.when(k > 0)
        def _():
            accg[...] += pg
            accu[...] += pu

        @pl.when(k == nk - 1)
        def _():
            g = accg[...]
            u = accu[...]
            og_ref[...] = g.astype(BF16)
            ou_ref[...] = u.astype(BF16)
            oa_ref[...] = (g * jax.nn.sigmoid(g) * u).astype(BF16)

    w_spec = pl.BlockSpec((tk, tn), lambda i, j, k: (k, j))
    o_spec = pl.BlockSpec((tm, tn), lambda i, j, k: (i, j))
    sds = jax.ShapeDtypeStruct((M, N), BF16)
    return pl.pallas_call(
        body, name=name, grid=(M // tm, N // tn, nk),
        in_specs=[pl.BlockSpec((tm, tk), lambda i, j, k: (i, k)), w_spec, w_spec],
        out_specs=[o_spec, o_spec, o_spec], out_shape=[sds, sds, sds],
        scratch_shapes=[pltpu.VMEM((tm, tn), F32), pltpu.VMEM((tm, tn), F32)],
        compiler_params=_params(("parallel", "parallel", "arbitrary")),
    )(hn, wg, wu)


def _swiglu_bwd(d_act, gate, up, name):
    M, N = d_act.shape
    tm, tn = _tile(M, 512, 16), _tile(N, 1024)

    def body(d_ref, g_ref, u_ref, dg_ref, du_ref):
        d = d_ref[...].astype(F32)
        g = g_ref[...].astype(F32)
        u = u_ref[...].astype(F32)
        sig = jax.nn.sigmoid(g)
        du_ref[...] = (d * g * sig).astype(BF16)
        dg_ref[...] = (d * u * sig * (1.0 + g * (1.0 - sig))).astype(BF16)

    spec = pl.BlockSpec((tm, tn), lambda i, j: (i, j))
    sds = jax.ShapeDtypeStruct((M, N), BF16)
    return pl.pallas_call(
        body, name=name, grid=(M // tm, N // tn),
        in_specs=[spec, spec, spec], out_specs=[spec, spec], out_shape=[sds, sds],
        compiler_params=_params(("parallel", "parallel")),
    )(d_act, gate, up)


def _rmsnorm_fwd(x, g, name):
    M, D = x.shape
    tm = _tile(M, 256, 16)

    def body(x_ref, g_ref, o_ref):
        xv = x_ref[...]
        r = lax.rsqrt(jnp.mean(xv * xv, axis=-1, keepdims=True) + RMS_EPS)
        o_ref[...] = (xv * r * g_ref[...]).astype(BF16)

    return pl.pallas_call(
        body, name=name, grid=(M // tm,),
        in_specs=[pl.BlockSpec((tm, D), lambda i: (i, 0)), pl.BlockSpec((1, D), lambda i: (0, 0))],
        out_specs=pl.BlockSpec((tm, D), lambda i: (i, 0)),
        out_shape=jax.ShapeDtypeStruct((M, D), BF16),
        compiler_params=_params(("parallel",)),
    )(x, g)


def _fold8(v):
    tm, D = v.shape
    return jnp.sum(v.reshape(tm // 8, 8, D), axis=0)


def _rmsnorm_bwd(x, dy, g, res, name, want_bf16):
    M, D = x.shape
    tm = _tile(M, 128, 16)

    def body(x_ref, dy_ref, g_ref, r_ref, *outs):
        dx_ref, dg_ref = outs[0], outs[-1]
        xv = x_ref[...]
        dyv = dy_ref[...].astype(F32)
        r = lax.rsqrt(jnp.mean(xv * xv, axis=-1, keepdims=True) + RMS_EPS)
        u = dyv * g_ref[...]
        dot = jnp.sum(xv * u, axis=-1, keepdims=True)
        dx = r * u - xv * (r * r * r * (1.0 / D)) * dot + r_ref[...]
        dx_ref[...] = dx
        if want_bf16:
            outs[1][...] = dx.astype(BF16)
        part = _fold8(dyv * xv * r)

        @pl.when(pl.program_id(0) == 0)
        def _():
            dg_ref[...] = part

        @pl.when(pl.program_id(0) > 0)
        def _():
            dg_ref[...] += part

    row = pl.BlockSpec((tm, D), lambda i: (i, 0))
    out_specs = [row] + ([row] if want_bf16 else []) + [pl.BlockSpec((8, D), lambda i: (0, 0))]
    out_shape = ([jax.ShapeDtypeStruct((M, D), F32)] + ([jax.ShapeDtypeStruct((M, D), BF16)] if want_bf16 else [])
                 + [jax.ShapeDtypeStruct((8, D), F32)])
    return pl.pallas_call(
        body, name=name, grid=(M // tm,),
        in_specs=[row, row, pl.BlockSpec((1, D), lambda i: (0, 0)), row],
        out_specs=out_specs, out_shape=out_shape,
        compiler_params=_params(("arbitrary",)),
    )(x, dy, g, res)


def _loss_head(h, tgt, g, name):
    M, D = h.shape
    tm = _tile(M, 128, 16)

    def body(h_ref, t_ref, g_ref, dh_ref, dhb_ref, l_ref, dg_ref):
        hv = h_ref[...]
        gv = g_ref[...]
        r = lax.rsqrt(jnp.mean(hv * hv, axis=-1, keepdims=True) + RMS_EPS)
        n = hv * r
        e = n * gv - t_ref[...]
        dy = e * (1.0 / D)
        u = dy * gv
        dot = jnp.sum(hv * u, axis=-1, keepdims=True)
        dh = r * u - hv * (r * r * r * (1.0 / D)) * dot
        dh_ref[...] = dh
        dhb_ref[...] = dh.astype(BF16)
        rows = jnp.sum(e * e, axis=-1, keepdims=True)
        lpart = jnp.broadcast_to(jnp.sum(rows, axis=0, keepdims=True) * (0.5 / D), (8, LANES))
        gpart = _fold8(dy * n)

        @pl.when(pl.program_id(0) == 0)
        def _():
            l_ref[...] = lpart
            dg_ref[...] = gpart

        @pl.when(pl.program_id(0) > 0)
        def _():
            l_ref[...] += lpart
            dg_ref[...] += gpart

    row = pl.BlockSpec((tm, D), lambda i: (i, 0))
    return pl.pallas_call(
        body, name=name, grid=(M // tm,),
        in_specs=[row, row, pl.BlockSpec((1, D), lambda i: (0, 0))],
        out_specs=[row, row, pl.BlockSpec((8, LANES), lambda i: (0, 0)), pl.BlockSpec((8, D), lambda i: (0, 0))],
        out_shape=[jax.ShapeDtypeStruct((M, D), F32), jax.ShapeDtypeStruct((M, D), BF16),
                   jax.ShapeDtypeStruct((8, LANES), F32), jax.ShapeDtypeStruct((8, D), F32)],
        compiler_params=_params(("arbitrary",)),
    )(h, tgt, g)


def _headnorm_fwd(o_dil, o_sb, g, name):
    M, W = o_dil.shape
    D = 2 * W
    tm = _tile(M, 256, 16)

    def body(a_ref, b_ref, g_ref, o_ref):
        for h in range(D // HEAD_DIM):
            src = a_ref if h < W // HEAD_DIM else b_ref
            lo = (h * HEAD_DIM) % W
            v = src[:, lo:lo + HEAD_DIM]
            r = lax.rsqrt(jnp.mean(v * v, axis=-1, keepdims=True) + RMS_EPS)
            o_ref[:, h * HEAD_DIM:(h + 1) * HEAD_DIM] = (v * r * g_ref[:, h * HEAD_DIM:(h + 1) * HEAD_DIM]).astype(BF16)

    half = pl.BlockSpec((tm, W), lambda i: (i, 0))
    return pl.pallas_call(
        body, name=name, grid=(M // tm,),
        in_specs=[half, half, pl.BlockSpec((1, D), lambda i: (0, 0))],
        out_specs=pl.BlockSpec((tm, D), lambda i: (i, 0)),
        out_shape=jax.ShapeDtypeStruct((M, D), BF16),
        compiler_params=_params(("parallel",)),
    )(o_dil, o_sb, g)


def _headnorm_bwd(d_mix, o_dil, o_sb, g, name):
    M, W = o_dil.shape
    D = 2 * W
    tm = _tile(M, 128, 16)

    def body(d_ref, a_ref, b_ref, g_ref, do_ref, dg_ref):
        @pl.when(pl.program_id(0) == 0)
        def _():
            dg_ref[...] = jnp.zeros(dg_ref.shape, F32)

        for h in range(D // HEAD_DIM):
            src = a_ref if h < W // HEAD_DIM else b_ref
            lo = (h * HEAD_DIM) % W
            cols = slice(h * HEAD_DIM, (h + 1) * HEAD_DIM)
            v = src[:, lo:lo + HEAD_DIM]
            dy = d_ref[:, cols].astype(F32)
            r = lax.rsqrt(jnp.mean(v * v, axis=-1, keepdims=True) + RMS_EPS)
            u = dy * g_ref[:, cols]
            dot = jnp.sum(v * u, axis=-1, keepdims=True)
            do_ref[:, cols] = r * u - v * (r * r * r * (1.0 / HEAD_DIM)) * dot
            dg_ref[:, cols] += _fold8(dy * v * r)

    half = pl.BlockSpec((tm, W), lambda i: (i, 0))
    row = pl.BlockSpec((tm, D), lambda i: (i, 0))
    return pl.pallas_call(
        body, name=name, grid=(M // tm,),
        in_specs=[row, half, half, pl.BlockSpec((1, D), lambda i: (0, 0))],
        out_specs=[row, pl.BlockSpec((8, D), lambda i: (0, 0))],
        out_shape=[jax.ShapeDtypeStruct((M, D), F32), jax.ShapeDtypeStruct((8, D), F32)],
        compiler_params=_params(("arbitrary",)),
    )(d_mix, o_dil, o_sb, g)


def _rope_tables(positions):
    half = ROPE_DIM // 2
    inv_freq = jnp.power(jnp.float32(ROPE_THETA), -jnp.arange(half, dtype=F32) / half)
    ang = positions.astype(F32)[..., None] * inv_freq
    cos, sin = jnp.cos(ang), jnp.sin(ang)
    rest = HEAD_DIM - ROPE_DIM
    one = jnp.ones(cos.shape[:-1] + (rest,), F32)
    z16 = jnp.zeros_like(sin)
    zr = jnp.zeros_like(one)
    c = jnp.concatenate([cos, cos, one], axis=-1)
    s1 = jnp.concatenate([-sin, z16, zr], axis=-1)
    s2 = jnp.concatenate([z16, sin, zr], axis=-1)
    return c, s1, s2


def _rope(a, c, s1, s2):
    half = ROPE_DIM // 2
    return a * c + pltpu.roll(a, HEAD_DIM - half, 1) * s1 + pltpu.roll(a, half, 1) * s2


def _iota2():
    row = lax.broadcasted_iota(jnp.int32, (BLOCK, BLOCK), 0)
    col = lax.broadcasted_iota(jnp.int32, (BLOCK, BLOCK), 1)
    return row, col


def _dil_specs(d, N3, Hd, D):
    nb3, nbd = N3 // LANES, D // LANES
    return nb3, nbd


def _dil_fwd(proj3, tabs, d, Hd, name):
    B, T, N3 = proj3.shape
    L = T // d
    nb = L // BLOCK
    W = Hd * HEAD_DIM
    nb3 = N3 // LANES
    p = proj3.reshape(B, L, d * N3)
    tb = [t.reshape(B, L, d * HEAD_DIM) for t in tabs]
    scale = HEAD_DIM ** -0.5

    def body(q_ref, k_ref, v_ref, c_ref, s1_ref, s2_ref, o_ref, l_ref):
        c, s1, s2 = c_ref[...], s1_ref[...], s2_ref[...]
        q = _rope(q_ref[...].astype(F32), c, s1, s2).astype(BF16)
        k = _rope(k_ref[...].astype(F32), c, s1, s2).astype(BF16)
        v = v_ref[...]
        row, col = _iota2()
        for n in range(nb):
            cur = slice(n * BLOCK, (n + 1) * BLOCK)
            qn = q[cur]
            s = jnp.where(col <= row, _dot(qn, k[cur], NT) * scale, -jnp.inf)
            vv = v[cur]
            if n > 0:
                prev = slice((n - 1) * BLOCK, n * BLOCK)
                sp = jnp.where(col >= row, _dot(qn, k[prev], NT) * scale, -jnp.inf)
                s = jnp.concatenate([sp, s], axis=1)
                vv = v[(n - 1) * BLOCK:(n + 1) * BLOCK]
            m = jnp.max(s, axis=-1, keepdims=True)
            e = jnp.exp(s - m)
            den = jnp.sum(e, axis=-1, keepdims=True)
            o_ref[cur, :] = _dot(e.astype(BF16), vv, NN) / den
            l_ref[cur, :] = jnp.broadcast_to(m + jnp.log(den), (BLOCK, HEAD_DIM))

    def col_spec(base, per_r):
        return pl.BlockSpec((None, L, HEAD_DIM), lambda b, h, r: (b, 0, r * per_r + base + h))

    tab_spec = pl.BlockSpec((None, L, HEAD_DIM), lambda b, h, r: (b, 0, r))
    out_spec = col_spec(0, Hd)
    sds = jax.ShapeDtypeStruct((B, L, d * W), F32)
    o, lse = pl.pallas_call(
        body, name=name, grid=(B, Hd, d),
        in_specs=[col_spec(0, nb3), col_spec(Hd, nb3), col_spec(2 * Hd, nb3), tab_spec, tab_spec, tab_spec],
        out_specs=[out_spec, out_spec], out_shape=[sds, sds],
        compiler_params=_params(("parallel", "parallel", "parallel")),
    )(p, p, p, *tb)
    return o.reshape(B * T, W), lse.reshape(B * T, W)


def _dil_combine(outs, lses, name):
    M, W = outs[0].shape
    tm, tn = _tile(M, 256, 8), _tile(W, 1024)

    def body(*refs):
        o_refs, l_refs, (o_ref, l_ref) = refs[:3], refs[3:6], refs[6:]
        ls = [r[...] for r in l_refs]
        m = jnp.maximum(jnp.maximum(ls[0], ls[1]), ls[2])
        ws = [jnp.exp(l - m) for l in ls]
        den = ws[0] + ws[1] + ws[2]
        o_ref[...] = (ws[0] * o_refs[0][...] + ws[1] * o_refs[1][...] + ws[2] * o_refs[2][...]) / den
        l_ref[...] = m + jnp.log(den)

    spec = pl.BlockSpec((tm, tn), lambda i, j: (i, j))
    sds = jax.ShapeDtypeStruct((M, W), F32)
    return pl.pallas_call(
        body, name=name, grid=(M // tm, W // tn),
        in_specs=[spec] * 6, out_specs=[spec, spec], out_shape=[sds, sds],
        compiler_params=_params(("parallel", "parallel")),
    )(*outs, *lses)


def _dil_bwd(proj3, tabs, d_o3, o3, lse3, acc, d, Hd, name):
    B, T, N3 = proj3.shape
    D = d_o3.shape[-1]
    L = T // d
    nb = L // BLOCK
    W = Hd * HEAD_DIM
    nb3, nbd = N3 // LANES, D // LANES
    p = proj3.reshape(B, L, d * N3)
    tb = [t.reshape(B, L, d * HEAD_DIM) for t in tabs]
    d_o = d_o3.reshape(B, L, d * D)
    o = o3.reshape(B, L, d * W)
    lse = lse3.reshape(B, L, d * W)
    scale = HEAD_DIM ** -0.5
    n_acc = 0 if acc is None else 3

    def body(*refs):
        q_ref, k_ref, v_ref, c_ref, s1_ref, s2_ref, do_ref, o_ref, l_ref = refs[:9]
        a_refs = refs[9:9 + n_acc]
        dq_ref, dk_ref, dv_ref, dqs, dks, dvs = refs[9 + n_acc:]
        c, s1, s2 = c_ref[...], s1_ref[...], s2_ref[...]
        q = _rope(q_ref[...].astype(F32), c, s1, s2).astype(BF16)
        k = _rope(k_ref[...].astype(F32), c, s1, s2).astype(BF16)
        v = v_ref[...]
        do = do_ref[...]
        dob = do.astype(BF16)
        dterm = jnp.sum(do * o_ref[...], axis=-1, keepdims=True)
        lsev = l_ref[...]
        dks[...] = jnp.zeros(dks.shape, F32)
        dvs[...] = jnp.zeros(dvs.shape, F32)
        row, col = _iota2()
        for n in range(nb):
            cur = slice(n * BLOCK, (n + 1) * BLOCK)
            qn, don, dn, ln = q[cur], dob[cur], dterm[cur], lsev[cur]
            dqn = jnp.zeros((BLOCK, HEAD_DIM), F32)
            blocks = [(cur, col <= row)]
            if n > 0:
                blocks.append((slice((n - 1) * BLOCK, n * BLOCK), col >= row))
            for blk, msk in blocks:
                kb, vb = k[blk], v[blk]
                s = _dot(qn, kb, NT) * scale
                pr = jnp.where(msk, jnp.exp(s - ln), 0.0)
                dp = _dot(don, vb, NT)
                ds = (pr * (dp - dn) * scale).astype(BF16)
                dqn = dqn + _dot(ds, kb, NN)
                dks[blk, :] += _dot(ds, qn, TN)
                dvs[blk, :] += _dot(pr.astype(BF16), don, TN)
            dqs[cur, :] = dqn
        dq = _rope(dqs[...], c, -s1, -s2)
        dk = _rope(dks[...], c, -s1, -s2)
        dv = dvs[...]
        if n_acc:
            dq, dk, dv = dq + a_refs[0][...], dk + a_refs[1][...], dv + a_refs[2][...]
        dq_ref[...] = dq
        dk_ref[...] = dk
        dv_ref[...] = dv

    def col_spec(base, per_r):
        return pl.BlockSpec((None, L, HEAD_DIM), lambda b, h, r: (b, 0, r * per_r + base + h))

    tab_spec = pl.BlockSpec((None, L, HEAD_DIM), lambda b, h, r: (b, 0, r))
    w_spec = col_spec(0, Hd)
    sds = jax.ShapeDtypeStruct((B, L, d * W), F32)
    acc_in = [] if acc is None else [a.reshape(B, L, d * W) for a in acc]
    outs = pl.pallas_call(
        body, name=name, grid=(B, Hd, d),
        in_specs=[col_spec(0, nb3), col_spec(Hd, nb3), col_spec(2 * Hd, nb3), tab_spec, tab_spec, tab_spec,
                  col_spec(0, nbd), w_spec, w_spec] + [w_spec] * n_acc,
        out_specs=[w_spec] * 3, out_shape=[sds] * 3,
        scratch_shapes=[pltpu.VMEM((L, HEAD_DIM), F32)] * 3,
        compiler_params=_params(("parallel", "parallel", "parallel")),
    )(p, p, p, *tb, d_o, o, lse, *acc_in)
    return tuple(a.reshape(B, T, W) for a in outs)


def _softplus(z):
    return jnp.maximum(z, 0.0) + jnp.log(1.0 + jnp.exp(-jnp.abs(z)))


def _dot_split(x, m01):
    hi = x.astype(BF16)
    lo = (x - hi.astype(F32)).astype(BF16)
    return _dot(hi, m01, NN) + _dot(lo, m01, NN)


def _sb_fwd(proj3, Hd, Hs, name):
    B, T, N3 = proj3.shape
    nq = T // BLOCK
    qb, kb_, vb_ = 3 * Hd, 3 * Hd + Hs, 3 * Hd + 2 * Hs
    scale = HEAD_DIM ** -0.5

    def body(q_ref, k_ref, v_ref, o_ref):
        i = pl.program_id(2)
        q = q_ref[...]
        row, col = _iota2()
        mrev = jnp.where(row > col, 1.0, 0.0).astype(BF16)

        def step(t, carry):
            acc, cs = carry
            off = pl.multiple_of((i - t) * BLOCK, BLOCK)
            kj = k_ref[pl.ds(off, BLOCK), :]
            vj = v_ref[pl.ds(off, BLOCK), :]
            z = _dot(q, kj, NT) * scale
            msk = col < row + jnp.where(t > 0, BLOCK, 0)
            sp = _softplus(z)
            ln = jnp.where(msk, -sp, 0.0)
            excl = _dot_split(ln, mrev) + cs
            a = jnp.where(msk, jnp.exp(z - sp + excl), 0.0)
            acc = acc + _dot(a.astype(BF16), vj, NN)
            cs = cs + jnp.sum(ln, axis=-1, keepdims=True)
            return acc, cs

        acc, _ = lax.fori_loop(0, i + 1, step,
                               (jnp.zeros((BLOCK, HEAD_DIM), F32), jnp.zeros((BLOCK, 1), F32)))
        o_ref[...] = acc

    full = lambda base: pl.BlockSpec((None, T, HEAD_DIM), lambda b, h, i: (b, 0, base + h))
    return pl.pallas_call(
        body, name=name, grid=(B, Hs, nq),
        in_specs=[pl.BlockSpec((None, BLOCK, HEAD_DIM), lambda b, h, i: (b, i, qb + h)), full(kb_), full(vb_)],
        out_specs=pl.BlockSpec((None, BLOCK, HEAD_DIM), lambda b, h, i: (b, i, h)),
        out_shape=jax.ShapeDtypeStruct((B, T, Hs * HEAD_DIM), F32),
        compiler_params=_params(("parallel", "parallel", "arbitrary")),
    )(proj3, proj3, proj3)


def _sb_bwd(proj3, d_o3, Hd, Hs, name):
    B, T, N3 = proj3.shape
    nq = T // BLOCK
    qb, kb_, vb_ = 3 * Hd, 3 * Hd + Hs, 3 * Hd + 2 * Hs
    scale = HEAD_DIM ** -0.5

    def body(q_ref, k_ref, v_ref, do_ref, dq_ref, dk_ref, dv_ref, g_buf, sig_buf):
        i = pl.program_id(2)

        @pl.when(i == 0)
        def _():
            dk_ref[...] = jnp.zeros(dk_ref.shape, F32)
            dv_ref[...] = jnp.zeros(dv_ref.shape, F32)

        q = q_ref[...]
        dob = do_ref[...].astype(BF16)
        row, col = _iota2()
        mrev = jnp.where(row > col, 1.0, 0.0).astype(BF16)
        mfwd = jnp.where(row < col, 1.0, 0.0).astype(BF16)

        def mask_of(j):
            return col < row + jnp.where(j < i, BLOCK, 0)

        def down(t, cs):
            j = i - t
            off = pl.multiple_of(j * BLOCK, BLOCK)
            kj = k_ref[pl.ds(off, BLOCK), :]
            vj = v_ref[pl.ds(off, BLOCK), :]
            z = _dot(q, kj, NT) * scale
            msk = mask_of(j)
            sp = _softplus(z)
            ln = jnp.where(msk, -sp, 0.0)
            excl = _dot_split(ln, mrev) + cs
            a = jnp.where(msk, jnp.exp(z - sp + excl), 0.0)
            da = _dot(dob, vj, NT)
            g_buf[j] = a * da
            sig_buf[j] = jnp.exp(z - sp)
            dv_ref[pl.ds(off, BLOCK), :] += _dot(a.astype(BF16), dob, TN)
            return cs + jnp.sum(ln, axis=-1, keepdims=True)

        lax.fori_loop(0, i + 1, down, jnp.zeros((BLOCK, 1), F32))

        def up(j, carry):
            dq, gc = carry
            off = pl.multiple_of(j * BLOCK, BLOCK)
            kj = k_ref[pl.ds(off, BLOCK), :]
            g = g_buf[j]
            sig = sig_buf[j]
            big = _dot_split(g, mfwd) + gc
            dz = (jnp.where(mask_of(j), g * (1.0 - sig) - big * sig, 0.0) * scale).astype(BF16)
            dq = dq + _dot(dz, kj, NN)
            dk_ref[pl.ds(off, BLOCK), :] += _dot(dz, q, TN)
            return dq, gc + jnp.sum(g, axis=-1, keepdims=True)

        dq, _ = lax.fori_loop(0, i + 1, up, (jnp.zeros((BLOCK, HEAD_DIM), F32), jnp.zeros((BLOCK, 1), F32)))
        dq_ref[...] = dq

    full = lambda base: pl.BlockSpec((None, T, HEAD_DIM), lambda b, h, i: (b, 0, base + h))
    blk = lambda base: pl.BlockSpec((None, BLOCK, HEAD_DIM), lambda b, h, i: (b, i, base + h))
    sds = jax.ShapeDtypeStruct((B, T, Hs * HEAD_DIM), F32)
    return pl.pallas_call(
        body, name=name, grid=(B, Hs, nq),
        in_specs=[blk(qb), full(kb_), full(vb_), blk(Hd)],
        out_specs=[blk(0), full(0), full(0)], out_shape=[sds, sds, sds],
        scratch_shapes=[pltpu.VMEM((nq, BLOCK, BLOCK), F32), pltpu.VMEM((nq, BLOCK, BLOCK), F32)],
        compiler_params=_params(("parallel", "parallel", "arbitrary")),
    )(proj3, proj3, proj3, d_o3)


def _place():
    x, y, c = lax.axis_index("x"), lax.axis_index("y"), lax.axis_index("c")
    chips = [(1 - x, y), (x, 1 - y), (1 - x, 1 - y)]
    return x, y, c, chips


class _Weight:
    def __init__(self, kind, shard_shape, slot, valid):
        self.kind, self.slot, self.valid = kind, slot, valid
        self.R, self.C = shard_shape
        if kind == "col":
            self.full = (self.R, N_CHIPS * slot)
            self.half = (self.R // 2, N_CHIPS * slot)
            self.piece = (self.R // 2, slot)
            self.final_half = (self.R // 2, valid)
            self.grad = (self.R, valid)
        else:
            self.full = (N_CHIPS * slot, self.C)
            self.half = (N_CHIPS * slot, self.C // 2)
            self.piece = (valid, self.C // 2)
            self.final_half = (valid, self.C // 2)
            self.grad = (valid, self.C)

    def shard_half(self, ref, hc):
        if self.kind == "col":
            return ref.at[pl.ds(hc * (self.R // 2), self.R // 2), :]
        return ref.at[:, pl.ds(hc * (self.C // 2), self.C // 2)]

    def full_half(self, ref, jj, hc):
        if self.kind == "col":
            return ref.at[pl.ds(hc * (self.R // 2), self.R // 2), pl.ds(pl.multiple_of(jj * self.slot, LANES), self.C)]
        return ref.at[pl.ds(pl.multiple_of(jj * self.slot, 16), self.R), pl.ds(hc * (self.C // 2), self.C // 2)]

    def full_shard(self, ref, jj):
        if self.kind == "col":
            return ref.at[:, pl.ds(pl.multiple_of(jj * self.slot, LANES), self.C)]
        return ref.at[pl.ds(pl.multiple_of(jj * self.slot, 16), self.R), :]

    def region_half(self, ref, hc):
        if self.kind == "col":
            return ref.at[pl.ds(hc * (self.R // 2), self.R // 2), :]
        return ref.at[:, pl.ds(hc * (self.C // 2), self.C // 2)]

    def half_piece(self, ref, jj):
        if self.kind == "col":
            return ref.at[:, pl.ds(pl.multiple_of(jj * self.slot, LANES), self.slot)]
        return ref.at[pl.ds(pl.multiple_of(jj * self.slot, 16), self.valid), :]

    def grad_half(self, ref, hc):
        if self.kind == "col":
            return ref.at[pl.ds(hc * (self.R // 2), self.R // 2), :]
        return ref.at[:, pl.ds(hc * (self.C // 2), self.C // 2)]


def _all_gather_weights(shards, geo, zero_rows):
    nw = len(shards)
    pad_w = [w for w in range(nw) if geo[w].kind == "row" and geo[w].slot != geo[w].R]

    def body(*refs):
        s_refs, z_ref, f_refs = refs[:nw], refs[nw], refs[nw + 1:2 * nw + 1]
        send, recv, loc, zsem = refs[2 * nw + 1:]
        x, y, c, chips = _place()
        j = 2 * x + y
        local = []
        for w in range(nw):
            cp = pltpu.make_async_copy(s_refs[w], geo[w].full_shard(f_refs[w], j), loc.at[w])
            cp.start()
            local.append(cp)
        for n, w in enumerate(pad_w):
            g = geo[w]
            for jj in range(N_CHIPS):
                cp = pltpu.make_async_copy(z_ref, f_refs[w].at[pl.ds(jj * g.slot + g.R, g.slot - g.R), :],
                                           zsem.at[n * N_CHIPS + jj])
                cp.start()
                local.append(cp)

        def remote(w, k, jj, hc, to, src=None):
            dst = geo[w].full_half(f_refs[w], jj, hc)
            return pltpu.make_async_remote_copy(
                src_ref=dst if src is None else src, dst_ref=dst,
                send_sem=send.at[w, k], recv_sem=recv.at[w, k], device_id=to, device_id_type=MESH)

        sends = []
        for w in range(nw):
            for k, chip in enumerate(chips):
                cp = remote(w, k, j, c, (*chip, c), src=geo[w].shard_half(s_refs[w], c))
                cp.start()
                sends.append(cp)
        for w in range(nw):
            for k, (cx, cy) in enumerate(chips):
                remote(w, k, 2 * cx + cy, c, (x, y, c)).wait_recv()
                cp = remote(w, 3 + k, 2 * cx + cy, c, (x, y, 1 - c))
                cp.start()
                sends.append(cp)
        for w in range(nw):
            for k, (cx, cy) in enumerate(chips):
                remote(w, 3 + k, 2 * cx + cy, 1 - c, (x, y, c)).wait_recv()
        for cp in sends:
            cp.wait_send()
        for cp in local:
            cp.wait()

    return pl.pallas_call(
        body, name="all_gather_weights",
        in_specs=[ANY] * (nw + 1), out_specs=[ANY] * nw,
        out_shape=[jax.ShapeDtypeStruct(g.full, BF16) for g in geo],
        scratch_shapes=[pltpu.SemaphoreType.DMA((nw, 6)), pltpu.SemaphoreType.DMA((nw, 6)),
                        pltpu.SemaphoreType.DMA((nw,)), pltpu.SemaphoreType.DMA((max(1, len(pad_w)) * N_CHIPS,))],
    )(*shards, zero_rows)


def _sibling_exchange(grads_full, geo):
    nw = len(grads_full)

    def body(*refs):
        g_refs, r_refs, send, recv = refs[:nw], refs[nw:2 * nw], refs[2 * nw], refs[2 * nw + 1]
        x, y, c, _ = _place()
        cps = []
        for w in range(nw):
            cp = pltpu.make_async_remote_copy(
                src_ref=geo[w].region_half(g_refs[w], 1 - c), dst_ref=r_refs[w],
                send_sem=send.at[w], recv_sem=recv.at[w], device_id=(x, y, 1 - c), device_id_type=MESH)
            cp.start()
            cps.append(cp)
        for cp in cps:
            cp.wait()

    return pl.pallas_call(
        body, name="grad_sibling_exchange",
        in_specs=[ANY] * nw, out_specs=[ANY] * nw,
        out_shape=[jax.ShapeDtypeStruct(g.half, BF16) for g in geo],
        scratch_shapes=[pltpu.SemaphoreType.DMA((nw,)), pltpu.SemaphoreType.DMA((nw,))],
    )(*grads_full)


def _chip_sum(g_full, got, geo, c_arr, name):
    Rh, Ch = geo.half
    tr, tc = _tile(Rh, 256, 16), _tile(Ch, 2048)
    nrb, ncb = Rh // tr, Ch // tc

    def body(c_ref, a_ref, b_ref, o_ref):
        o_ref[...] = (a_ref[...].astype(F32) + b_ref[...].astype(F32)).astype(BF16)

    if geo.kind == "col":
        a_spec = pl.BlockSpec((tr, tc), lambda i, j, c_ref: (c_ref[0] * nrb + i, j))
    else:
        a_spec = pl.BlockSpec((tr, tc), lambda i, j, c_ref: (i, c_ref[0] * ncb + j))
    spec = pl.BlockSpec((tr, tc), lambda i, j, c_ref: (i, j))
    return pl.pallas_call(
        body, name=name,
        grid_spec=pltpu.PrefetchScalarGridSpec(num_scalar_prefetch=1, grid=(nrb, ncb),
                                               in_specs=[a_spec, spec], out_specs=spec),
        out_shape=jax.ShapeDtypeStruct(geo.half, BF16),
        compiler_params=_params(("parallel", "parallel")),
    )(c_arr, g_full, got)


def _chip_exchange(chip_sums, geo):
    nw = len(chip_sums)

    def body(*refs):
        s_refs, r_refs, send, recv = refs[:nw], refs[nw:2 * nw], refs[2 * nw], refs[2 * nw + 1]
        x, y, c, chips = _place()
        cps = []
        for w in range(nw):
            for k, (cx, cy) in enumerate(chips):
                cp = pltpu.make_async_remote_copy(
                    src_ref=geo[w].half_piece(s_refs[w], 2 * cx + cy), dst_ref=r_refs[w].at[k],
                    send_sem=send.at[w, k], recv_sem=recv.at[w, k], device_id=(cx, cy, c), device_id_type=MESH)
                cp.start()
                cps.append(cp)
        for cp in cps:
            cp.wait()

    return pl.pallas_call(
        body, name="grad_chip_exchange",
        in_specs=[ANY] * nw, out_specs=[ANY] * nw,
        out_shape=[jax.ShapeDtypeStruct((3,) + ((g.piece[0], g.slot) if g.kind == "col" else g.piece), BF16)
                   for g in geo],
        scratch_shapes=[pltpu.SemaphoreType.DMA((nw, 3)), pltpu.SemaphoreType.DMA((nw, 3))],
    )(*chip_sums)


def _final_sum(chip_sum, got, geo, j_arr, name):
    Rp, Cp = (geo.piece[0], geo.slot) if geo.kind == "col" else geo.piece
    Ro, Co = geo.final_half
    if geo.kind == "col":
        tr = _tile(Rp, 128, 16)
        grid = (Rp // tr,)
        a_spec = pl.BlockSpec((tr, Cp), lambda i, j_ref: (i, j_ref[0]))
        r_specs = [pl.BlockSpec((None, tr, Cp), functools.partial(lambda i, j_ref, k: (k, i, 0), k=k)) for k in range(3)]
        o_spec = pl.BlockSpec((tr, Co), lambda i, j_ref: (i, 0))
    else:
        tr = 64 if (Rp % 128 or geo.slot % 128) else 128
        assert Rp % tr == 0 and geo.slot % tr == 0
        spb = geo.slot // tr
        grid = (Rp // tr,)
        a_spec = pl.BlockSpec((tr, Cp), lambda i, j_ref: (j_ref[0] * spb + i, 0))
        r_specs = [pl.BlockSpec((None, tr, Cp), functools.partial(lambda i, j_ref, k: (k, i, 0), k=k)) for k in range(3)]
        o_spec = pl.BlockSpec((tr, Co), lambda i, j_ref: (i, 0))

    def body(j_ref, a_ref, r0, r1, r2, o_ref):
        def ld(ref):
            return ref[:, :Co].astype(F32)
        o_ref[...] = ((ld(a_ref) + ld(r0)) + ld(r1)) + ld(r2)

    return pl.pallas_call(
        body, name=name,
        grid_spec=pltpu.PrefetchScalarGridSpec(num_scalar_prefetch=1, grid=grid,
                                               in_specs=[a_spec] + r_specs, out_specs=o_spec),
        out_shape=jax.ShapeDtypeStruct(geo.final_half, F32),
        compiler_params=_params(("parallel",)),
    )(j_arr, chip_sum, got, got, got)


def _final_exchange(halves, geo):
    nw = len(halves)

    def body(*refs):
        h_refs, g_refs, send, recv, loc = refs[:nw], refs[nw:2 * nw], refs[2 * nw], refs[2 * nw + 1], refs[2 * nw + 2]
        x, y, c, _ = _place()
        cps, local = [], []
        for w in range(nw):
            lc = pltpu.make_async_copy(h_refs[w], geo[w].grad_half(g_refs[w], c), loc.at[w])
            lc.start()
            local.append(lc)
            cp = pltpu.make_async_remote_copy(
                src_ref=h_refs[w], dst_ref=geo[w].grad_half(g_refs[w], c),
                send_sem=send.at[w], recv_sem=recv.at[w], device_id=(x, y, 1 - c), device_id_type=MESH)
            cp.start()
            cps.append(cp)
        for w in range(nw):
            pltpu.make_async_remote_copy(
                src_ref=h_refs[w], dst_ref=geo[w].grad_half(g_refs[w], 1 - c),
                send_sem=send.at[w], recv_sem=recv.at[w], device_id=(x, y, c), device_id_type=MESH).wait_recv()
        for cp in cps:
            cp.wait_send()
        for lc in local:
            lc.wait()

    return pl.pallas_call(
        body, name="grad_final_exchange",
        in_specs=[ANY] * nw, out_specs=[ANY] * nw,
        out_shape=[jax.ShapeDtypeStruct(g.grad, F32) for g in geo],
        scratch_shapes=[pltpu.SemaphoreType.DMA((nw,)), pltpu.SemaphoreType.DMA((nw,)), pltpu.SemaphoreType.DMA((nw,))],
    )(*halves)


def _adam_math(w, g, m, v):
    m = ADAM_B1 * m + (1.0 - ADAM_B1) * g
    v = ADAM_B2 * v + (1.0 - ADAM_B2) * (g * g)
    m_hat = m / (1.0 - ADAM_B1 ** ADAM_STEP)
    v_hat = v / (1.0 - ADAM_B2 ** ADAM_STEP)
    delta = -ADAM_LR * (m_hat / (jnp.sqrt(v_hat) + ADAM_EPS) + ADAM_WD * w)
    return delta, m, v


def _adam(w, g, m, v, name):
    R, C = w.shape
    tr = _tile(R, 128, 8)
    if R % tr:
        tr = 64
    assert R % tr == 0

    def body(w_ref, g_ref, m_ref, v_ref, og, od, om, ov):
        gv = g_ref[...]
        delta, mn, vn = _adam_math(w_ref[...], gv, m_ref[...], v_ref[...])
        og[...] = gv
        od[...] = delta
        om[...] = mn
        ov[...] = vn

    spec = pl.BlockSpec((tr, C), lambda i: (i, 0))
    sds = jax.ShapeDtypeStruct((R, C), F32)
    return pl.pallas_call(
        body, name=name, grid=(R // tr,),
        in_specs=[spec] * 4, out_specs=[spec] * 4, out_shape=[sds] * 4,
        compiler_params=_params(("parallel",)),
    )(w, g, m, v)


def _small_all_reduce(parts, loss_part, D):
    n = len(parts)

    def body(*refs):
        p_refs, l_ref, o_ref, vec, buf, send, recv = refs[:n], refs[n], refs[n + 1], *refs[n + 2:]
        x, y, c, _ = _place()
        me = 4 * x + 2 * y + c
        vec[...] = jnp.zeros(vec.shape, F32)
        for r in range(n):
            vec[r:r + 1, :] = jnp.sum(p_refs[r][...], axis=0, keepdims=True)
        vec[n:n + 1, 0:LANES] = l_ref[0:1, :]
        buf[me] = vec[...]
        cps = []
        for dd in range(1, 8):
            bx, by, bc = (dd >> 2) & 1, (dd >> 1) & 1, dd & 1
            peer = (x + bx - 2 * x * bx, y + by - 2 * y * by, c + bc - 2 * c * bc)
            cp = pltpu.make_async_remote_copy(
                src_ref=vec, dst_ref=buf.at[me], send_sem=send.at[dd - 1], recv_sem=recv.at[dd - 1],
                device_id=peer, device_id_type=MESH)
            cp.start()
            cps.append(cp)
        for cp in cps:
            cp.wait()
        tot = buf[0]
        for s in range(1, 8):
            tot = tot + buf[s]
        o_ref[...] = tot

    vm = pl.BlockSpec(memory_space=pltpu.VMEM)
    return pl.pallas_call(
        body, name="small_all_reduce",
        in_specs=[vm] * (n + 1), out_specs=vm,
        out_shape=jax.ShapeDtypeStruct((8, D), F32),
        scratch_shapes=[pltpu.VMEM((8, D), F32), pltpu.VMEM((8, 8, D), F32),
                        pltpu.SemaphoreType.DMA((7,)), pltpu.SemaphoreType.DMA((7,))],
    )(*parts, loss_part)


def _small_adam(tot, ws, ms, vs, rows):
    n = len(ws)

    def body(*refs):
        t_ref = refs[0]
        w_refs, m_refs, v_refs = refs[1:1 + n], refs[1 + n:1 + 2 * n], refs[1 + 2 * n:1 + 3 * n]
        outs = refs[1 + 3 * n:]
        for i in range(n):
            r, c0 = rows[i]
            width = w_refs[i].shape[1]
            g = t_ref[r:r + 1, c0:c0 + width]
            delta, mn, vn = _adam_math(w_refs[i][...], g, m_refs[i][...], v_refs[i][...])
            outs[4 * i][...] = g
            outs[4 * i + 1][...] = delta
            outs[4 * i + 2][...] = mn
            outs[4 * i + 3][...] = vn

    vm = pl.BlockSpec(memory_space=pltpu.VMEM)
    out_shape = []
    for w in ws:
        out_shape += [jax.ShapeDtypeStruct(w.shape, F32)] * 4
    return pl.pallas_call(
        body, name="small_adam",
        in_specs=[vm] * (1 + 3 * n), out_specs=[vm] * (4 * n), out_shape=out_shape,
    )(tot, *ws, *ms, *vs)


def kernel(x, positions, norm_mix_g, w_in, norm_out_dil_g, norm_out_sb_g, w_out, norm_ffn_g, w_gate, w_up, w_down, norm_final_g, loss_target, m_norm_mix_g, m_w_in, m_norm_out_dil_g, m_norm_out_sb_g, m_w_out, m_norm_ffn_g, m_w_gate, m_w_up, m_w_down, m_norm_final_g, v_norm_mix_g, v_w_in, v_norm_out_dil_g, v_norm_out_sb_g, v_w_out, v_norm_ffn_g, v_w_gate, v_w_up, v_w_down, v_norm_final_g):
    B, T, D = x.shape
    M = B * T
    n_heads = D // HEAD_DIM
    Hd = n_heads // 2
    Hs = n_heads - Hd
    W = Hd * HEAD_DIM
    N3 = 3 * D
    fs = w_gate.shape[2]
    fp = -(-fs // LANES) * LANES
    assert T % (BLOCK * max(DILATIONS)) == 0 and W == Hs * HEAD_DIM

    x2 = x.reshape(M, D)
    tgt = loss_target.reshape(M, D)
    big = [w_in[0], w_out[0], w_gate[0], w_up[0], w_down[0]]
    big_m = [m_w_in[0], m_w_out[0], m_w_gate[0], m_w_up[0], m_w_down[0]]
    big_v = [v_w_in[0], v_w_out[0], v_w_gate[0], v_w_up[0], v_w_down[0]]
    names = ["w_in", "w_out", "w_gate", "w_up", "w_down"]

    ns_in = w_in.shape[2]
    ks_out = w_out.shape[1]
    geo = [
        _Weight("col", (D, ns_in), ns_in, ns_in),
        _Weight("row", (ks_out, D), ks_out, ks_out),
        _Weight("col", (D, fp), fp, fs),
        _Weight("col", (D, fp), fp, fs),
        _Weight("row", (fs, D), fp, fs),
    ]
    shards = [_cast_pad(big[0], ns_in, "cast_w_in"), _cast_pad(big[1], D, "cast_w_out"),
              _cast_pad(big[2], fp, "cast_w_gate"), _cast_pad(big[3], fp, "cast_w_up"),
              _cast_pad(big[4], D, "cast_w_down")]
    zero_rows = jnp.zeros((max(fp - fs, 16), D), BF16)
    Win, Wout, Wg, Wu, Wd = _all_gather_weights(shards, geo, zero_rows)

    hn = _rmsnorm_fwd(x2, norm_mix_g, "norm_mix")
    proj = _matmul(hn, Win, "nn", BF16, "in_proj")
    proj3 = proj.reshape(B, T, N3)
    tabs = _rope_tables(positions)
    outs, lses = [], []
    for d in DILATIONS:
        o_d, l_d = _dil_fwd(proj3, tabs, d, Hd, "dil_fwd_%d" % d)
        outs.append(o_d)
        lses.append(l_d)
    o_dil, lse_dil = _dil_combine(outs, lses, "dil_combine")
    o_sb = _sb_fwd(proj3, Hd, Hs, "sb_fwd").reshape(M, W)
    g_heads = jnp.concatenate([norm_out_dil_g, norm_out_sb_g], axis=1)
    o_mix = _headnorm_fwd(o_dil, o_sb, g_heads, "head_norm")
    h1 = _matmul(o_mix, Wout, "nn", F32, "out_proj", res=x2)
    hn2 = _rmsnorm_fwd(h1, norm_ffn_g, "norm_ffn")
    gate, up, act = _gate_up(hn2, Wg, Wu, "gate_up")
    h2 = _matmul(act, Wd, "nn", F32, "down_proj", res=h1)
    dh2, dh2_b, loss_part, dg_final = _loss_head(h2, tgt, norm_final_g.reshape(1, D), "loss_head")

    dWd = _matmul(act, dh2_b, "tn", BF16, "dw_down")
    d_act = _matmul(dh2_b, Wd, "nt", BF16, "d_act")
    d_gate, d_up = _swiglu_bwd(d_act, gate, up, "swiglu_bwd")
    dWg = _matmul(hn2, d_gate, "tn", BF16, "dw_gate")
    dWu = _matmul(hn2, d_up, "tn", BF16, "dw_up")
    d_hn2 = _matmul(d_gate, Wg, "nt", F32, "d_hn2_gate")
    d_hn2 = _matmul(d_up, Wu, "nt", F32, "d_hn2_up", res=d_hn2)
    dh1, dh1_b, dg_ffn = _rmsnorm_bwd(h1, d_hn2, norm_ffn_g, dh2, "norm_ffn_bwd", True)
    dWo = _matmul(o_mix, dh1_b, "tn", BF16, "dw_out")
    d_mix = _matmul(dh1_b, Wout, "nt", BF16, "d_mix")
    d_o, dg_heads = _headnorm_bwd(d_mix, o_dil, o_sb, g_heads, "head_norm_bwd")
    d_o3 = d_o.reshape(B, T, D)
    acc = None
    for d in DILATIONS:
        acc = _dil_bwd(proj3, tabs, d_o3, o_dil.reshape(B, T, W), lse_dil.reshape(B, T, W), acc, d, Hd,
                       "dil_bwd_%d" % d)
    dqb, dkb, dvb = _sb_bwd(proj3, d_o3, Hd, Hs, "sb_bwd")
    dproj = jnp.concatenate([a.astype(BF16) for a in (*acc, dqb, dkb, dvb)], axis=-1).reshape(M, N3)
    dWin = _matmul(hn, dproj, "tn", BF16, "dw_in")
    d_hn = _matmul(dproj, Win, "nt", F32, "d_hn")
    dx, dg_mix = _rmsnorm_bwd(x2, d_hn, norm_mix_g, dh1, "norm_mix_bwd", False)

    c_arr = jnp.reshape(lax.axis_index("c"), (1,)).astype(jnp.int32)
    j_arr = jnp.reshape(2 * lax.axis_index("x") + lax.axis_index("y"), (1,)).astype(jnp.int32)
    grads_full = [dWin, dWo, dWg, dWu, dWd]
    from_sibling = _sibling_exchange(grads_full, geo)
    chip_sums = [_chip_sum(grads_full[w], from_sibling[w], geo[w], c_arr, "chip_sum_" + names[w]) for w in range(5)]
    from_chips = _chip_exchange(chip_sums, geo)
    halves = [_final_sum(chip_sums[w], from_chips[w], geo[w], j_arr, "final_sum_" + names[w]) for w in range(5)]
    grads = _final_exchange(halves, geo)
    big_out = [_adam(big[w], grads[w], big_m[w], big_v[w], "adam_" + names[w]) for w in range(5)]

    tot = _small_all_reduce([dg_mix, dg_heads, dg_ffn, dg_final], loss_part, D)
    loss = tot[4, 0]
    small_w = [norm_mix_g, norm_out_dil_g, norm_out_sb_g, norm_ffn_g, norm_final_g.reshape(1, D)]
    small_m = [m_norm_mix_g, m_norm_out_dil_g, m_norm_out_sb_g, m_norm_ffn_g, m_norm_final_g.reshape(1, D)]
    small_v = [v_norm_mix_g, v_norm_out_dil_g, v_norm_out_sb_g, v_norm_ffn_g, v_norm_final_g.reshape(1, D)]
    so = _small_adam(tot, small_w, small_m, small_v, [(0, 0), (1, 0), (1, W), (2, 0), (3, 0)])
    small_out = [so[4 * i:4 * i + 4] for i in range(5)]
    small_out[4] = [a.reshape(D) for a in small_out[4]]

    per_weight = [small_out[0], big_out[0], small_out[1], small_out[2], big_out[1], small_out[3],
                  big_out[2], big_out[3], big_out[4], small_out[4]]

    def field(i):
        res = []
        for n_, o in enumerate(per_weight):
            a = o[i]
            res.append(a[None] if n_ in (1, 4, 6, 7, 8) else a)
        return res

    return (loss, dx.reshape(B, T, D), *field(0), *field(1), *field(2), *field(3))
```

```python
import functools
import math

import jax
import jax.numpy as jnp
from jax import lax
from jax.experimental import pallas as pl
from jax.experimental.pallas import tpu as pltpu

F32 = jnp.float32
BF16 = jnp.bfloat16
MESH = pl.DeviceIdType.MESH
ANY = pl.BlockSpec(memory_space=pl.ANY)

LANES = 128
HEAD_DIM = 128
DIL_STEPS = 128
DILATIONS = (1, 4, 16)
TQ = 256
TK_DIL = 512
TK_SB = 256
NEG = -1e30
ROPE_DIM = 32
ROPE_THETA = 500000.0
RMS_EPS = 1e-5
ADAM_LR, ADAM_B1, ADAM_B2, ADAM_EPS, ADAM_WD, ADAM_STEP = 0.001, 0.9, 0.999, 1e-08, 0.01, 10
N_CHIPS = 4
VMEM_LIMIT = 48 * 1024 * 1024

NN = ((1,), (0,))
NT = ((1,), (1,))
TN = ((0,), (0,))


def _dot(a, b, dims):
    return lax.dot_general(a, b, (dims, ((), ())), preferred_element_type=F32)


def _tile(dim, pref, unit=LANES):
    if dim <= pref:
        return dim
    t = (pref // unit) * unit
    while t > unit and dim % t:
        t -= unit
    assert dim % t == 0, (dim, pref, unit)
    return t


def _params(sem=None):
    return pltpu.CompilerParams(dimension_semantics=sem, vmem_limit_bytes=VMEM_LIMIT)


def _rope(a, c, s1, s2):
    half = ROPE_DIM // 2
    return a * c + pltpu.roll(a, HEAD_DIM - half, 1) * s1 + pltpu.roll(a, half, 1) * s2


def _matmul(a, b, mode, out_dtype, name, res=None, rope=None, tm=1024, tn=1024, tk=1024):
    if mode == "nn":
        (M, K), (_, N) = a.shape, b.shape
    elif mode == "nt":
        (M, K), (N, _) = a.shape, b.shape
    else:
        (K, M), (_, N) = a.shape, b.shape
    tm, tk = _tile(M, tm), _tile(K, tk)
    tn = _tile(N, tn) if rope is None else _tile(math.gcd(N, rope[1]), tn)
    nk = K // tk
    dims = {"nn": NN, "nt": NT, "tn": TN}[mode]
    a_spec = (pl.BlockSpec((tk, tm), lambda i, j, k: (k, i)) if mode == "tn"
              else pl.BlockSpec((tm, tk), lambda i, j, k: (i, k)))
    b_spec = (pl.BlockSpec((tn, tk), lambda i, j, k: (j, k)) if mode == "nt"
              else pl.BlockSpec((tk, tn), lambda i, j, k: (k, j)))
    o_spec = pl.BlockSpec((tm, tn), lambda i, j, k: (i, j))
    n_extra = (1 if res is not None else 0) + (3 if rope is not None else 0)
    if rope is not None:
        assert rope[1] % tn == 0

    def body(*refs):
        a_ref, b_ref = refs[:2]
        extra = refs[2:2 + n_extra]
        o_ref, acc_ref = refs[2 + n_extra:]
        k = pl.program_id(2)
        part = _dot(a_ref[...], b_ref[...], dims)

        @pl.when(k == 0)
        def _():
            acc_ref[...] = part

        @pl.when(k > 0)
        def _():
            acc_ref[...] += part

        if rope is None:
            @pl.when(k == nk - 1)
            def _():
                out = acc_ref[...]
                if res is not None:
                    out = out + extra[0][...]
                o_ref[...] = out.astype(out_dtype)
        else:
            roped = pl.program_id(1) * tn < rope[1]

            @pl.when(jnp.logical_and(k == nk - 1, roped))
            def _():
                c, s1, s2 = extra[0][...], extra[1][...], extra[2][...]
                for h in range(tn // HEAD_DIM):
                    cols = slice(h * HEAD_DIM, (h + 1) * HEAD_DIM)
                    o_ref[:, cols] = _rope(acc_ref[:, cols], c, s1, s2).astype(out_dtype)

            @pl.when(jnp.logical_and(k == nk - 1, jnp.logical_not(roped)))
            def _():
                o_ref[...] = acc_ref[...].astype(out_dtype)

    in_specs, args = [a_spec, b_spec], [a, b]
    if res is not None:
        in_specs.append(o_spec)
        args.append(res)
    if rope is not None:
        in_specs += [pl.BlockSpec((tm, HEAD_DIM), lambda i, j, k: (i, 0))] * 3
        args += list(rope[0])
    return pl.pallas_call(
        body, name=name, grid=(M // tm, N // tn, nk),
        in_specs=in_specs, out_specs=o_spec,
        out_shape=jax.ShapeDtypeStruct((M, N), out_dtype),
        scratch_shapes=[pltpu.VMEM((tm, tn), F32)],
        compiler_params=_params(("parallel", "parallel", "arbitrary")),
    )(*args)


def _gate_up(hn, wg, wu, name):
    M, K = hn.shape
    N = wg.shape[0]
    tm, tn, tk = _tile(M, 1024), _tile(N, 1024), _tile(K, 1024)
    nk = K // tk

    def body(a_ref, g_ref, u_ref, og_ref, ou_ref, oa_ref, accg, accu):
        k = pl.program_id(2)
        a = a_ref[...]
        pg = _dot(a, g_ref[...], NT)
        pu = _dot(a, u_ref[...], NT)

        @pl.when(k == 0)
        def _():
            accg[...] = pg
            accu[...] = pu

        @pl.when(k > 0)
        def _():
            accg[...] += pg
            accu[...] += pu

        @pl.when(k == nk - 1)
        def _():
            g = accg[...]
            u = accu[...]
            og_ref[...] = g.astype(BF16)
            ou_ref[...] = u.astype(BF16)
            oa_ref[...] = (g * jax.nn.sigmoid(g) * u).astype(BF16)

    w_spec = pl.BlockSpec((tn, tk), lambda i, j, k: (j, k))
    o_spec = pl.BlockSpec((tm, tn), lambda i, j, k: (i, j))
    sds = jax.ShapeDtypeStruct((M, N), BF16)
    return pl.pallas_call(
        body, name=name, grid=(M // tm, N // tn, nk),
        in_specs=[pl.BlockSpec((tm, tk), lambda i, j, k: (i, k)), w_spec, w_spec],
        out_specs=[o_spec, o_spec, o_spec], out_shape=[sds, sds, sds],
        scratch_shapes=[pltpu.VMEM((tm, tn), F32), pltpu.VMEM((tm, tn), F32)],
        compiler_params=_params(("parallel", "parallel", "arbitrary")),
    )(hn, wg, wu)


def _swiglu_bwd(d_act, gate, up, name):
    M, N = d_act.shape
    tm, tn = _tile(M, 512, 16), _tile(N, 1024)

    def body(d_ref, g_ref, u_ref, dg_ref, du_ref):
        d = d_ref[...].astype(F32)
        g = g_ref[...].astype(F32)
        u = u_ref[...].astype(F32)
        sig = jax.nn.sigmoid(g)
        du_ref[...] = (d * g * sig).astype(BF16)
        dg_ref[...] = (d * u * sig * (1.0 + g * (1.0 - sig))).astype(BF16)

    spec = pl.BlockSpec((tm, tn), lambda i, j: (i, j))
    sds = jax.ShapeDtypeStruct((M, N), BF16)
    return pl.pallas_call(
        body, name=name, grid=(M // tm, N // tn),
        in_specs=[spec, spec, spec], out_specs=[spec, spec], out_shape=[sds, sds],
        compiler_params=_params(("parallel", "parallel")),
    )(d_act, gate, up)


def _rmsnorm_fwd(x, g, name):
    M, D = x.shape
    tm = _tile(M, 256, 16)

    def body(x_ref, g_ref, o_ref):
        xv = x_ref[...]
        r = lax.rsqrt(jnp.mean(xv * xv, axis=-1, keepdims=True) + RMS_EPS)
        o_ref[...] = (xv * r * g_ref[...]).astype(BF16)

    return pl.pallas_call(
        body, name=name, grid=(M // tm,),
        in_specs=[pl.BlockSpec((tm, D), lambda i: (i, 0)), pl.BlockSpec((1, D), lambda i: (0, 0))],
        out_specs=pl.BlockSpec((tm, D), lambda i: (i, 0)),
        out_shape=jax.ShapeDtypeStruct((M, D), BF16),
        compiler_params=_params(("parallel",)),
    )(x, g)


def _fold8(v):
    tm, D = v.shape
    return jnp.sum(v.reshape(tm // 8, 8, D), axis=0)


def _rmsnorm_bwd(x, dy, g, res, name, want_bf16):
    M, D = x.shape
    tm = _tile(M, 128, 16)

    def body(x_ref, dy_ref, g_ref, r_ref, *outs):
        dx_ref, dg_ref = outs[0], outs[-1]
        xv = x_ref[...]
        dyv = dy_ref[...].astype(F32)
        r = lax.rsqrt(jnp.mean(xv * xv, axis=-1, keepdims=True) + RMS_EPS)
        u = dyv * g_ref[...]
        dot = jnp.sum(xv * u, axis=-1, keepdims=True)
        dx = r * u - xv * (r * r * r * (1.0 / D)) * dot + r_ref[...]
        dx_ref[...] = dx
        if want_bf16:
            outs[1][...] = dx.astype(BF16)
        part = _fold8(dyv * xv * r)

        @pl.when(pl.program_id(0) == 0)
        def _():
            dg_ref[...] = part

        @pl.when(pl.program_id(0) > 0)
        def _():
            dg_ref[...] += part

    row = pl.BlockSpec((tm, D), lambda i: (i, 0))
    out_specs = [row] + ([row] if want_bf16 else []) + [pl.BlockSpec((8, D), lambda i: (0, 0))]
    out_shape = ([jax.ShapeDtypeStruct((M, D), F32)] + ([jax.ShapeDtypeStruct((M, D), BF16)] if want_bf16 else [])
                 + [jax.ShapeDtypeStruct((8, D), F32)])
    return pl.pallas_call(
        body, name=name, grid=(M // tm,),
        in_specs=[row, row, pl.BlockSpec((1, D), lambda i: (0, 0)), row],
        out_specs=out_specs, out_shape=out_shape,
        compiler_params=_params(("arbitrary",)),
    )(x, dy, g, res)


def _loss_head(h, tgt, g, name):
    M, D = h.shape
    tm = _tile(M, 128, 16)

    def body(h_ref, t_ref, g_ref, dh_ref, dhb_ref, l_ref, dg_ref):
        hv = h_ref[...]
        gv = g_ref[...]
        r = lax.rsqrt(jnp.mean(hv * hv, axis=-1, keepdims=True) + RMS_EPS)
        n = hv * r
        e = n * gv - t_ref[...]
        dy = e * (1.0 / D)
        u = dy * gv
        dot = jnp.sum(hv * u, axis=-1, keepdims=True)
        dh = r * u - hv * (r * r * r * (1.0 / D)) * dot
        dh_ref[...] = dh
        dhb_ref[...] = dh.astype(BF16)
        rows = jnp.sum(e * e, axis=-1, keepdims=True)
        lpart = jnp.broadcast_to(jnp.sum(rows, axis=0, keepdims=True) * (0.5 / D), (8, LANES))
        gpart = _fold8(dy * n)

        @pl.when(pl.program_id(0) == 0)
        def _():
            l_ref[...] = lpart
            dg_ref[...] = gpart

        @pl.when(pl.program_id(0) > 0)
        def _():
            l_ref[...] += lpart
            dg_ref[...] += gpart

    row = pl.BlockSpec((tm, D), lambda i: (i, 0))
    return pl.pallas_call(
        body, name=name, grid=(M // tm,),
        in_specs=[row, row, pl.BlockSpec((1, D), lambda i: (0, 0))],
        out_specs=[row, row, pl.BlockSpec((8, LANES), lambda i: (0, 0)), pl.BlockSpec((8, D), lambda i: (0, 0))],
        out_shape=[jax.ShapeDtypeStruct((M, D), F32), jax.ShapeDtypeStruct((M, D), BF16),
                   jax.ShapeDtypeStruct((8, LANES), F32), jax.ShapeDtypeStruct((8, D), F32)],
        compiler_params=_params(("arbitrary",)),
    )(h, tgt, g)


def _headnorm_fwd(o_dil, o_sb, g, name):
    M, W = o_dil.shape
    D = 2 * W
    tm = _tile(M, 256, 16)

    def body(a_ref, b_ref, g_ref, o_ref):
        for h in range(D // HEAD_DIM):
            src = a_ref if h < W // HEAD_DIM else b_ref
            lo = (h * HEAD_DIM) % W
            v = src[:, lo:lo + HEAD_DIM]
            r = lax.rsqrt(jnp.mean(v * v, axis=-1, keepdims=True) + RMS_EPS)
            o_ref[:, h * HEAD_DIM:(h + 1) * HEAD_DIM] = (v * r * g_ref[:, h * HEAD_DIM:(h + 1) * HEAD_DIM]).astype(BF16)

    half = pl.BlockSpec((tm, W), lambda i: (i, 0))
    return pl.pallas_call(
        body, name=name, grid=(M // tm,),
        in_specs=[half, half, pl.BlockSpec((1, D), lambda i: (0, 0))],
        out_specs=pl.BlockSpec((tm, D), lambda i: (i, 0)),
        out_shape=jax.ShapeDtypeStruct((M, D), BF16),
        compiler_params=_params(("parallel",)),
    )(o_dil, o_sb, g)


def _headnorm_bwd(d_mix, o_dil, o_sb, g, name):
    M, W = o_dil.shape
    D = 2 * W
    tm = _tile(M, 128, 16)

    def body(d_ref, a_ref, b_ref, g_ref, do_ref, dg_ref):
        @pl.when(pl.program_id(0) == 0)
        def _():
            dg_ref[...] = jnp.zeros(dg_ref.shape, F32)

        for h in range(D // HEAD_DIM):
            src = a_ref if h < W // HEAD_DIM else b_ref
            lo = (h * HEAD_DIM) % W
            cols = slice(h * HEAD_DIM, (h + 1) * HEAD_DIM)
            v = src[:, lo:lo + HEAD_DIM]
            dy = d_ref[:, cols].astype(F32)
            r = lax.rsqrt(jnp.mean(v * v, axis=-1, keepdims=True) + RMS_EPS)
            u = dy * g_ref[:, cols]
            dot = jnp.sum(v * u, axis=-1, keepdims=True)
            do_ref[:, cols] = r * u - v * (r * r * r * (1.0 / HEAD_DIM)) * dot
            dg_ref[:, cols] += _fold8(dy * v * r)

    half = pl.BlockSpec((tm, W), lambda i: (i, 0))
    row = pl.BlockSpec((tm, D), lambda i: (i, 0))
    return pl.pallas_call(
        body, name=name, grid=(M // tm,),
        in_specs=[row, half, half, pl.BlockSpec((1, D), lambda i: (0, 0))],
        out_specs=[row, pl.BlockSpec((8, D), lambda i: (0, 0))],
        out_shape=[jax.ShapeDtypeStruct((M, D), F32), jax.ShapeDtypeStruct((8, D), F32)],
        compiler_params=_params(("arbitrary",)),
    )(d_mix, o_dil, o_sb, g)


def _rope_tables(positions):
    half = ROPE_DIM // 2
    inv_freq = jnp.power(jnp.float32(ROPE_THETA), -jnp.arange(half, dtype=F32) / half)
    ang = positions.astype(F32)[..., None] * inv_freq
    cos, sin = jnp.cos(ang), jnp.sin(ang)
    rest = HEAD_DIM - ROPE_DIM
    one = jnp.ones(cos.shape[:-1] + (rest,), F32)
    z16 = jnp.zeros_like(sin)
    zr = jnp.zeros_like(one)
    c = jnp.concatenate([cos, cos, one], axis=-1)
    s1 = jnp.concatenate([-sin, z16, zr], axis=-1)
    s2 = jnp.concatenate([z16, sin, zr], axis=-1)
    return c, s1, s2


def _dil_bias(T):
    ne = T // TQ
    e = jnp.arange(ne, dtype=jnp.int32)[:, None, None]
    r = jnp.arange(TQ, dtype=jnp.int32)[None, :, None]
    c = jnp.arange(TK_DIL, dtype=jnp.int32)[None, None, :]
    dist = e * TQ + r - c
    mult = jnp.zeros(dist.shape, F32)
    for d in DILATIONS:
        mult = mult + jnp.where((dist % d == 0) & (dist <= DIL_STEPS * d), 1.0, 0.0)
    return jnp.where((dist >= 0) & (mult > 0), jnp.log(jnp.maximum(mult, 1.0)), NEG)


def _dil_fwd(proj3, bias, Hd, name):
    B, T, N3 = proj3.shape
    nq = T // TQ
    per = TK_DIL // TQ
    scale = HEAD_DIM ** -0.5

    def body(q_ref, k_ref, v_ref, b_ref, o_ref, l_ref):
        i = pl.program_id(2)
        q = q_ref[...]
        last = i // per

        def step(t, carry):
            m, l, acc = carry
            j = last - t
            off = pl.multiple_of(j * TK_DIL, TK_DIL)
            kj = k_ref[pl.ds(off, TK_DIL), :]
            vj = v_ref[pl.ds(off, TK_DIL), :]
            s = _dot(q, kj, NT) * scale + b_ref[i - per * j]
            m_new = jnp.maximum(m, jnp.max(s, axis=-1, keepdims=True))
            p = jnp.exp(s - m_new)
            corr = jnp.exp(m - m_new)
            l = l * corr + jnp.sum(p, axis=-1, keepdims=True)
            acc = acc * corr + _dot(p.astype(BF16), vj, NN)
            return m_new, l, acc

        m, l, acc = lax.fori_loop(0, last + 1, step, (jnp.full((TQ, 1), NEG, F32), jnp.zeros((TQ, 1), F32),
                                                       jnp.zeros((TQ, HEAD_DIM), F32)))
        o_ref[...] = acc / l
        l_ref[...] = jnp.broadcast_to(m + jnp.log(l), (TQ, HEAD_DIM))

    full = lambda base: pl.BlockSpec((None, T, HEAD_DIM), lambda b, h, i: (b, 0, base + h))
    blk = lambda base: pl.BlockSpec((None, TQ, HEAD_DIM), lambda b, h, i: (b, i, base + h))
    sds = jax.ShapeDtypeStruct((B, T, Hd * HEAD_DIM), F32)
    return pl.pallas_call(
        body, name=name, grid=(B, Hd, nq),
        in_specs=[blk(0), full(Hd), full(2 * Hd), pl.BlockSpec(bias.shape, lambda b, h, i: (0, 0, 0))],
        out_specs=[blk(0), blk(0)], out_shape=[sds, sds],
        compiler_params=_params(("parallel", "parallel", "arbitrary")),
    )(proj3, proj3, proj3, bias)


def _dil_bwd(proj3, bias, tabs, d_o3, o3, lse3, Hd, name):
    B, T, N3 = proj3.shape
    nq = T // TQ
    per = TK_DIL // TQ
    scale = HEAD_DIM ** -0.5

    def body(q_ref, k_ref, v_ref, b_ref, do_ref, o_ref, l_ref, cq, s1q, s2q, ct, s1t, s2t,
             dq_ref, dk_ref, dv_ref, dks, dvs):
        i = pl.program_id(2)

        @pl.when(i == 0)
        def _():
            dks[...] = jnp.zeros(dks.shape, F32)
            dvs[...] = jnp.zeros(dvs.shape, F32)

        q = q_ref[...]
        do = do_ref[...]
        dob = do.astype(BF16)
        dterm = jnp.sum(do * o_ref[...], axis=-1, keepdims=True)
        lse = jnp.concatenate([l_ref[...]] * (TK_DIL // HEAD_DIM), axis=1)
        last = i // per

        def step(j, dq):
            off = pl.multiple_of(j * TK_DIL, TK_DIL)
            kj = k_ref[pl.ds(off, TK_DIL), :]
            vj = v_ref[pl.ds(off, TK_DIL), :]
            s = _dot(q, kj, NT) * scale + b_ref[i - per * j]
            p = jnp.exp(s - lse)
            dp = _dot(dob, vj, NT)
            ds = (p * (dp - dterm) * scale).astype(BF16)
            dks[pl.ds(off, TK_DIL), :] += _dot(ds, q, TN)
            dvs[pl.ds(off, TK_DIL), :] += _dot(p.astype(BF16), dob, TN)
            return dq + _dot(ds, kj, NN)

        dq = lax.fori_loop(0, last + 1, step, jnp.zeros((TQ, HEAD_DIM), F32))
        dq_ref[...] = _rope(dq, cq[...], -s1q[...], -s2q[...]).astype(BF16)

        @pl.when(i == nq - 1)
        def _():
            dk_ref[...] = _rope(dks[...], ct[...], -s1t[...], -s2t[...]).astype(BF16)
            dv_ref[...] = dvs[...].astype(BF16)

    full = lambda base: pl.BlockSpec((None, T, HEAD_DIM), lambda b, h, i: (b, 0, base + h))
    blk = lambda base: pl.BlockSpec((None, TQ, HEAD_DIM), lambda b, h, i: (b, i, base + h))
    tab_q = pl.BlockSpec((None, TQ, HEAD_DIM), lambda b, h, i: (b, i, 0))
    tab_t = pl.BlockSpec((None, T, HEAD_DIM), lambda b, h, i: (b, 0, 0))
    sds = jax.ShapeDtypeStruct((B, T, Hd * HEAD_DIM), BF16)
    return pl.pallas_call(
        body, name=name, grid=(B, Hd, nq),
        in_specs=[blk(0), full(Hd), full(2 * Hd), pl.BlockSpec(bias.shape, lambda b, h, i: (0, 0, 0)),
                  blk(0), blk(0), blk(0), tab_q, tab_q, tab_q, tab_t, tab_t, tab_t],
        out_specs=[blk(0), full(0), full(0)], out_shape=[sds, sds, sds],
        scratch_shapes=[pltpu.VMEM((T, HEAD_DIM), F32), pltpu.VMEM((T, HEAD_DIM), F32)],
        compiler_params=_params(("parallel", "parallel", "arbitrary")),
    )(proj3, proj3, proj3, bias, d_o3, o3, lse3, *tabs, *tabs)


def _softplus(z):
    return jnp.maximum(z, 0.0) + jnp.log(1.0 + jnp.exp(-jnp.abs(z)))


def _tri_sum(x, m01):
    n = x.shape[0]
    hi = x.astype(BF16)
    lo = (x - hi.astype(F32)).astype(BF16)
    both = _dot(jnp.concatenate([hi, lo], axis=0), m01, NN)
    return both[:n] + both[n:]


def _sb_iota():
    row = lax.broadcasted_iota(jnp.int32, (TQ, TK_SB), 0)
    col = lax.broadcasted_iota(jnp.int32, (TQ, TK_SB), 1)
    return row, col


def _sb_fwd(proj3, Hd, Hs, name):
    B, T, N3 = proj3.shape
    nq = T // TQ
    qb, kb_, vb_ = 3 * Hd, 3 * Hd + Hs, 3 * Hd + 2 * Hs
    scale = HEAD_DIM ** -0.5

    def body(q_ref, k_ref, v_ref, o_ref, lt_ref):
        i = pl.program_id(2)
        q = q_ref[...]
        row, col = _sb_iota()
        after = jnp.where(row > col, 1.0, 0.0).astype(BF16)

        def step(t, carry):
            acc, cs = carry
            off = pl.multiple_of((i - t) * TK_SB, TK_SB)
            kj = k_ref[pl.ds(off, TK_SB), :]
            vj = v_ref[pl.ds(off, TK_SB), :]
            z = _dot(q, kj, NT) * scale
            msk = col < row + jnp.where(t > 0, TK_SB, 0)
            sp = _softplus(z)
            ln = jnp.where(msk, -sp, 0.0)
            excl = _tri_sum(ln, after) + cs
            a = jnp.where(msk, jnp.exp(z - sp + excl), 0.0)
            acc = acc + _dot(a.astype(BF16), vj, NN)
            cs = cs + jnp.sum(ln, axis=-1, keepdims=True)
            return acc, cs

        acc, cs = lax.fori_loop(0, i + 1, step, (jnp.zeros((TQ, HEAD_DIM), F32), jnp.zeros((TQ, 1), F32)))
        o_ref[...] = acc
        lt_ref[...] = jnp.broadcast_to(cs, (TQ, HEAD_DIM))

    full = lambda base: pl.BlockSpec((None, T, HEAD_DIM), lambda b, h, i: (b, 0, base + h))
    blk = lambda base: pl.BlockSpec((None, TQ, HEAD_DIM), lambda b, h, i: (b, i, base + h))
    sds = jax.ShapeDtypeStruct((B, T, Hs * HEAD_DIM), F32)
    return pl.pallas_call(
        body, name=name, grid=(B, Hs, nq),
        in_specs=[blk(qb), full(kb_), full(vb_)],
        out_specs=[blk(0), blk(0)], out_shape=[sds, sds],
        compiler_params=_params(("parallel", "parallel", "arbitrary")),
    )(proj3, proj3, proj3)


def _sb_bwd(proj3, d_o3, lt3, Hd, Hs, name):
    B, T, N3 = proj3.shape
    nq = T // TQ
    qb, kb_, vb_ = 3 * Hd, 3 * Hd + Hs, 3 * Hd + 2 * Hs
    scale = HEAD_DIM ** -0.5

    def body(q_ref, k_ref, v_ref, do_ref, lt_ref, dq_ref, dk_ref, dv_ref, dks, dvs):
        i = pl.program_id(2)

        @pl.when(i == 0)
        def _():
            dks[...] = jnp.zeros(dks.shape, F32)
            dvs[...] = jnp.zeros(dvs.shape, F32)

        q = q_ref[...]
        dob = do_ref[...].astype(BF16)
        total = jnp.concatenate([lt_ref[...]] * (TK_SB // HEAD_DIM), axis=1)
        row, col = _sb_iota()
        before = jnp.where(row < col, 1.0, 0.0).astype(BF16)

        def step(j, carry):
            dq, pc, gc = carry
            off = pl.multiple_of(j * TK_SB, TK_SB)
            kj = k_ref[pl.ds(off, TK_SB), :]
            vj = v_ref[pl.ds(off, TK_SB), :]
            z = _dot(q, kj, NT) * scale
            msk = col < row + jnp.where(j < i, TK_SB, 0)
            sp = _softplus(z)
            ln = jnp.where(msk, -sp, 0.0)
            excl = total - (_tri_sum(ln, before) + ln + pc)
            a = jnp.where(msk, jnp.exp(z - sp + excl), 0.0)
            g = a * _dot(dob, vj, NT)
            big = _tri_sum(g, before) + gc
            sig = jnp.exp(z - sp)
            dz = (jnp.where(msk, g * (1.0 - sig) - big * sig, 0.0) * scale).astype(BF16)
            dks[pl.ds(off, TK_SB), :] += _dot(dz, q, TN)
            dvs[pl.ds(off, TK_SB), :] += _dot(a.astype(BF16), dob, TN)
            return (dq + _dot(dz, kj, NN), pc + jnp.sum(ln, axis=-1, keepdims=True),
                    gc + jnp.sum(g, axis=-1, keepdims=True))

        zero1 = jnp.zeros((TQ, 1), F32)
        dq, _, _ = lax.fori_loop(0, i + 1, step, (jnp.zeros((TQ, HEAD_DIM), F32), zero1, zero1))
        dq_ref[...] = dq.astype(BF16)

        @pl.when(i == nq - 1)
        def _():
            dk_ref[...] = dks[...].astype(BF16)
            dv_ref[...] = dvs[...].astype(BF16)

    full = lambda base: pl.BlockSpec((None, T, HEAD_DIM), lambda b, h, i: (b, 0, base + h))
    blk = lambda base: pl.BlockSpec((None, TQ, HEAD_DIM), lambda b, h, i: (b, i, base + h))
    sds = jax.ShapeDtypeStruct((B, T, Hs * HEAD_DIM), BF16)
    return pl.pallas_call(
        body, name=name, grid=(B, Hs, nq),
        in_specs=[blk(qb), full(kb_), full(vb_), blk(Hd), blk(0)],
        out_specs=[blk(0), full(0), full(0)], out_shape=[sds, sds, sds],
        scratch_shapes=[pltpu.VMEM((T, HEAD_DIM), F32), pltpu.VMEM((T, HEAD_DIM), F32)],
        compiler_params=_params(("parallel", "parallel", "arbitrary")),
    )(proj3, proj3, proj3, d_o3, lt3)


def _place():
    x, y, c = lax.axis_index("x"), lax.axis_index("y"), lax.axis_index("c")
    chips = [(1 - x, y), (x, 1 - y), (1 - x, 1 - y)]
    return x, y, c, chips


class _Weight:
    def __init__(self, kind, shard_shape, slot):
        self.kind, self.slot = kind, slot
        self.R, self.C = shard_shape
        if kind == "col":
            assert slot == self.C
            self.full = (self.R, N_CHIPS * slot)
            self.half = (self.R // 2, N_CHIPS * slot)
            self.piece = (self.R // 2, slot)
        else:
            self.full = (N_CHIPS * slot, self.C)
            self.half = (N_CHIPS * slot, self.C // 2)
            self.piece = (self.R, self.C // 2)

    def _rows(self, jj, n):
        return pl.ds(pl.multiple_of(jj * self.slot, 16), n)

    def full_half(self, ref, jj, hc):
        if self.kind == "col":
            return ref.at[pl.ds(hc * (self.R // 2), self.R // 2), pl.ds(pl.multiple_of(jj * self.slot, LANES), self.C)]
        return ref.at[self._rows(jj, self.R), pl.ds(hc * (self.C // 2), self.C // 2)]

    def region_half(self, ref, hc):
        if self.kind == "col":
            return ref.at[pl.ds(hc * (self.R // 2), self.R // 2), :]
        return ref.at[:, pl.ds(hc * (self.C // 2), self.C // 2)]

    def half_piece(self, ref, jj):
        if self.kind == "col":
            return ref.at[:, pl.ds(pl.multiple_of(jj * self.slot, LANES), self.slot)]
        return ref.at[self._rows(jj, self.R), :]


def _cast_into_full(w, geo, j_arr, name):
    R, C = w.shape
    tr = _tile(R, 256, 16)
    if geo.kind == "col":
        o_spec = pl.BlockSpec((tr, C), lambda i, j_ref: (i, j_ref[0]))
    else:
        while geo.slot % tr or R % tr:
            tr -= 16
        spb = geo.slot // tr
        o_spec = pl.BlockSpec((tr, C), lambda i, j_ref: (j_ref[0] * spb + i, 0))

    def body(j_ref, w_ref, o_ref):
        o_ref[...] = w_ref[...].astype(BF16)

    return pl.pallas_call(
        body, name=name,
        grid_spec=pltpu.PrefetchScalarGridSpec(
            num_scalar_prefetch=1, grid=(R // tr,),
            in_specs=[pl.BlockSpec((tr, C), lambda i, j_ref: (i, 0))], out_specs=o_spec),
        out_shape=jax.ShapeDtypeStruct(geo.full, BF16),
        compiler_params=_params(("parallel",)),
    )(j_arr, w)


def _all_gather_weights(fulls, geo, zero_rows):
    nw = len(fulls)
    pad_w = [w for w in range(nw) if geo[w].kind == "row" and geo[w].slot != geo[w].R]

    def body(*refs):
        own_refs, z_ref, f_refs = refs[:nw], refs[nw], refs[nw + 1:2 * nw + 1]
        send, recv, zsem = refs[2 * nw + 1:]
        x, y, c, chips = _place()
        j = 2 * x + y
        local = []
        for n, w in enumerate(pad_w):
            g = geo[w]
            for jj in range(N_CHIPS):
                cp = pltpu.make_async_copy(z_ref, f_refs[w].at[pl.ds(jj * g.slot + g.R, g.slot - g.R), :],
                                           zsem.at[n * N_CHIPS + jj])
                cp.start()
                local.append(cp)

        def remote(w, k, jj, hc, to, own=False):
            part = geo[w].full_half(f_refs[w], jj, hc)
            return pltpu.make_async_remote_copy(
                src_ref=geo[w].full_half(own_refs[w], jj, hc) if own else part, dst_ref=part,
                send_sem=send.at[w, k], recv_sem=recv.at[w, k], device_id=to, device_id_type=MESH)

        sends = []
        for w in range(nw):
            for k, chip in enumerate(chips):
                cp = remote(w, k, j, c, (*chip, c), own=True)
                cp.start()
                sends.append(cp)
        for w in range(nw):
            for k, (cx, cy) in enumerate(chips):
                remote(w, k, 2 * cx + cy, c, (x, y, c)).wait_recv()
                cp = remote(w, 3 + k, 2 * cx + cy, c, (x, y, 1 - c))
                cp.start()
                sends.append(cp)
        for w in range(nw):
            for k, (cx, cy) in enumerate(chips):
                remote(w, 3 + k, 2 * cx + cy, 1 - c, (x, y, c)).wait_recv()
        for cp in sends:
            cp.wait_send()
        for cp in local:
            cp.wait()

    return pl.pallas_call(
        body, name="all_gather_weights",
        in_specs=[ANY] * (nw + 1), out_specs=[ANY] * nw,
        out_shape=[jax.ShapeDtypeStruct(g.full, BF16) for g in geo],
        input_output_aliases={w: w for w in range(nw)},
        scratch_shapes=[pltpu.SemaphoreType.DMA((nw, 6)), pltpu.SemaphoreType.DMA((nw, 6)),
                        pltpu.SemaphoreType.DMA((max(1, len(pad_w)) * N_CHIPS,))],
    )(*fulls, zero_rows)


def _sibling_exchange(grads_full, geo):
    nw = len(grads_full)

    def body(*refs):
        g_refs, r_refs, send, recv = refs[:nw], refs[nw:2 * nw], refs[2 * nw], refs[2 * nw + 1]
        x, y, c, _ = _place()
        cps = []
        for w in range(nw):
            cp = pltpu.make_async_remote_copy(
                src_ref=geo[w].region_half(g_refs[w], 1 - c), dst_ref=r_refs[w],
                send_sem=send.at[w], recv_sem=recv.at[w], device_id=(x, y, 1 - c), device_id_type=MESH)
            cp.start()
            cps.append(cp)
        for cp in cps:
            cp.wait()

    return pl.pallas_call(
        body, name="grad_sibling_exchange",
        in_specs=[ANY] * nw, out_specs=[ANY] * nw,
        out_shape=[jax.ShapeDtypeStruct(g.half, BF16) for g in geo],
        scratch_shapes=[pltpu.SemaphoreType.DMA((nw,)), pltpu.SemaphoreType.DMA((nw,))],
    )(*grads_full)


def _chip_sum(g_full, got, geo, c_arr, name):
    Rh, Ch = geo.half
    tr, tc = _tile(Rh, 256, 16), _tile(Ch, 2048)
    nrb, ncb = Rh // tr, Ch // tc

    def body(c_ref, a_ref, b_ref, o_ref):
        o_ref[...] = (a_ref[...].astype(F32) + b_ref[...].astype(F32)).astype(BF16)

    if geo.kind == "col":
        a_spec = pl.BlockSpec((tr, tc), lambda i, j, c_ref: (c_ref[0] * nrb + i, j))
    else:
        a_spec = pl.BlockSpec((tr, tc), lambda i, j, c_ref: (i, c_ref[0] * ncb + j))
    spec = pl.BlockSpec((tr, tc), lambda i, j, c_ref: (i, j))
    return pl.pallas_call(
        body, name=name,
        grid_spec=pltpu.PrefetchScalarGridSpec(num_scalar_prefetch=1, grid=(nrb, ncb),
                                               in_specs=[a_spec, spec], out_specs=spec),
        out_shape=jax.ShapeDtypeStruct(geo.half, BF16),
        compiler_params=_params(("parallel", "parallel")),
    )(c_arr, g_full, got)


def _chip_exchange(chip_sums, geo):
    nw = len(chip_sums)

    def body(*refs):
        s_refs, r_refs, send, recv = refs[:nw], refs[nw:2 * nw], refs[2 * nw], refs[2 * nw + 1]
        x, y, c, chips = _place()
        cps = []
        for w in range(nw):
            for k, (cx, cy) in enumerate(chips):
                cp = pltpu.make_async_remote_copy(
                    src_ref=geo[w].half_piece(s_refs[w], 2 * cx + cy), dst_ref=r_refs[w].at[k],
                    send_sem=send.at[w, k], recv_sem=recv.at[w, k], device_id=(cx, cy, c), device_id_type=MESH)
                cp.start()
                cps.append(cp)
        for cp in cps:
            cp.wait()

    return pl.pallas_call(
        body, name="grad_chip_exchange",
        in_specs=[ANY] * nw, out_specs=[ANY] * nw,
        out_shape=[jax.ShapeDtypeStruct((3,) + g.piece, BF16) for g in geo],
        scratch_shapes=[pltpu.SemaphoreType.DMA((nw, 3)), pltpu.SemaphoreType.DMA((nw, 3))],
    )(*chip_sums)


def _final_sum(chip_sum, got, geo, j_arr, name):
    Rp, Cp = geo.piece
    if geo.kind == "col":
        tr = _tile(Rp, 128, 16)
        a_spec = pl.BlockSpec((tr, Cp), lambda i, j_ref: (i, j_ref[0]))
    else:
        tr = _tile(Rp, 128, 16)
        while geo.slot % tr:
            tr -= 16
        assert Rp % tr == 0 and geo.slot % tr == 0
        spb = geo.slot // tr
        a_spec = pl.BlockSpec((tr, Cp), lambda i, j_ref: (j_ref[0] * spb + i, 0))
    r_specs = [pl.BlockSpec((None, tr, Cp), functools.partial(lambda i, j_ref, k: (k, i, 0), k=k)) for k in range(3)]
    o_spec = pl.BlockSpec((tr, Cp), lambda i, j_ref: (i, 0))

    def body(j_ref, a_ref, r0, r1, r2, o_ref):
        o_ref[...] = ((a_ref[...].astype(F32) + r0[...].astype(F32)) + r1[...].astype(F32)) + r2[...].astype(F32)

    return pl.pallas_call(
        body, name=name,
        grid_spec=pltpu.PrefetchScalarGridSpec(num_scalar_prefetch=1, grid=(Rp // tr,),
                                               in_specs=[a_spec] + r_specs, out_specs=o_spec),
        out_shape=jax.ShapeDtypeStruct(geo.piece, F32),
        compiler_params=_params(("parallel",)),
    )(j_arr, chip_sum, got, got, got)


def _final_exchange(halves, geo):
    nw = len(halves)

    def body(*refs):
        h_refs, r_refs, send, recv = refs[:nw], refs[nw:2 * nw], refs[2 * nw], refs[2 * nw + 1]
        x, y, c, _ = _place()
        cps = []
        for w in range(nw):
            cp = pltpu.make_async_remote_copy(
                src_ref=h_refs[w], dst_ref=r_refs[w], send_sem=send.at[w], recv_sem=recv.at[w],
                device_id=(x, y, 1 - c), device_id_type=MESH)
            cp.start()
            cps.append(cp)
        for cp in cps:
            cp.wait()

    return pl.pallas_call(
        body, name="grad_final_exchange",
        in_specs=[ANY] * nw, out_specs=[ANY] * nw,
        out_shape=[jax.ShapeDtypeStruct(g.piece, F32) for g in geo],
        scratch_shapes=[pltpu.SemaphoreType.DMA((nw,)), pltpu.SemaphoreType.DMA((nw,))],
    )(*halves)


def _adam_math(w, g, m, v):
    m = ADAM_B1 * m + (1.0 - ADAM_B1) * g
    v = ADAM_B2 * v + (1.0 - ADAM_B2) * (g * g)
    m_hat = m / (1.0 - ADAM_B1 ** ADAM_STEP)
    v_hat = v / (1.0 - ADAM_B2 ** ADAM_STEP)
    delta = -ADAM_LR * (m_hat / (jnp.sqrt(v_hat) + ADAM_EPS) + ADAM_WD * w)
    return delta, m, v


def _adam(w, mine, theirs, m, v, geo, c_arr, name):
    R, C = w.shape
    tr = _tile(R, 128, 8)
    nrb = R // tr
    spec = pl.BlockSpec((tr, C), lambda i, c_ref: (i, 0))
    if geo.kind == "col":
        assert nrb % 2 == 0
        h_spec = pl.BlockSpec((tr, C), lambda i, c_ref: (i % (nrb // 2), 0))
    else:
        h_spec = pl.BlockSpec((tr, C // 2), lambda i, c_ref: (i, 0))

    def body(c_ref, w_ref, a_ref, b_ref, m_ref, v_ref, og, od, om, ov):
        c = c_ref[0]
        if geo.kind == "col":
            gv = jnp.where(pl.program_id(0) // (nrb // 2) == c, a_ref[...], b_ref[...])
        else:
            a, b = a_ref[...], b_ref[...]
            gv = jnp.concatenate([jnp.where(c == 0, a, b), jnp.where(c == 0, b, a)], axis=1)
        delta, mn, vn = _adam_math(w_ref[...], gv, m_ref[...], v_ref[...])
        og[...] = gv
        od[...] = delta
        om[...] = mn
        ov[...] = vn

    sds = jax.ShapeDtypeStruct((R, C), F32)
    return pl.pallas_call(
        body, name=name,
        grid_spec=pltpu.PrefetchScalarGridSpec(num_scalar_prefetch=1, grid=(nrb,),
                                               in_specs=[spec, h_spec, h_spec, spec, spec], out_specs=[spec] * 4),
        out_shape=[sds] * 4,
        compiler_params=_params(("parallel",)),
    )(c_arr, w, mine, theirs, m, v)


def _small_all_reduce(parts, loss_part, D):
    n = len(parts)

    def body(*refs):
        p_refs, l_ref, o_ref, vec, buf, send, recv = refs[:n], refs[n], refs[n + 1], *refs[n + 2:]
        x, y, c, _ = _place()
        me = 4 * x + 2 * y + c
        vec[...] = jnp.zeros(vec.shape, F32)
        for r in range(n):
            vec[r:r + 1, :] = jnp.sum(p_refs[r][...], axis=0, keepdims=True)
        vec[n:n + 1, 0:LANES] = l_ref[0:1, :]
        buf[me] = vec[...]
        cps = []
        for dd in range(1, 8):
            bx, by, bc = (dd >> 2) & 1, (dd >> 1) & 1, dd & 1
            peer = (x + bx - 2 * x * bx, y + by - 2 * y * by, c + bc - 2 * c * bc)
            cp = pltpu.make_async_remote_copy(
                src_ref=vec, dst_ref=buf.at[me], send_sem=send.at[dd - 1], recv_sem=recv.at[dd - 1],
                device_id=peer, device_id_type=MESH)
            cp.start()
            cps.append(cp)
        for cp in cps:
            cp.wait()
        tot = buf[0]
        for s in range(1, 8):
            tot = tot + buf[s]
        o_ref[...] = tot

    vm = pl.BlockSpec(memory_space=pltpu.VMEM)
    return pl.pallas_call(
        body, name="small_all_reduce",
        in_specs=[vm] * (n + 1), out_specs=vm,
        out_shape=jax.ShapeDtypeStruct((8, D), F32),
        scratch_shapes=[pltpu.VMEM((8, D), F32), pltpu.VMEM((8, 8, D), F32),
                        pltpu.SemaphoreType.DMA((7,)), pltpu.SemaphoreType.DMA((7,))],
    )(*parts, loss_part)


def _small_adam(tot, ws, ms, vs, rows):
    n = len(ws)

    def body(*refs):
        t_ref = refs[0]
        w_refs, m_refs, v_refs = refs[1:1 + n], refs[1 + n:1 + 2 * n], refs[1 + 2 * n:1 + 3 * n]
        outs = refs[1 + 3 * n:]
        for i in range(n):
            r, c0 = rows[i]
            width = w_refs[i].shape[1]
            g = t_ref[r:r + 1, c0:c0 + width]
            delta, mn, vn = _adam_math(w_refs[i][...], g, m_refs[i][...], v_refs[i][...])
            outs[4 * i][...] = g
            outs[4 * i + 1][...] = delta
            outs[4 * i + 2][...] = mn
            outs[4 * i + 3][...] = vn

    vm = pl.BlockSpec(memory_space=pltpu.VMEM)
    out_shape = []
    for w in ws:
        out_shape += [jax.ShapeDtypeStruct(w.shape, F32)] * 4
    return pl.pallas_call(
        body, name="small_adam",
        in_specs=[vm] * (1 + 3 * n), out_specs=[vm] * (4 * n), out_shape=out_shape,
    )(tot, *ws, *ms, *vs)


def kernel(x, positions, norm_mix_g, w_in, norm_out_dil_g, norm_out_sb_g, w_out, norm_ffn_g, w_gate, w_up, w_down, norm_final_g, loss_target, m_norm_mix_g, m_w_in, m_norm_out_dil_g, m_norm_out_sb_g, m_w_out, m_norm_ffn_g, m_w_gate, m_w_up, m_w_down, m_norm_final_g, v_norm_mix_g, v_w_in, v_norm_out_dil_g, v_norm_out_sb_g, v_w_out, v_norm_ffn_g, v_w_gate, v_w_up, v_w_down, v_norm_final_g):
    B, T, D = x.shape
    M = B * T
    n_heads = D // HEAD_DIM
    Hd = n_heads // 2
    Hs = n_heads - Hd
    W = Hd * HEAD_DIM
    N3 = 3 * D
    fs = w_gate.shape[2]
    fp = -(-fs // LANES) * LANES
    assert T % TK_DIL == 0 and W == Hs * HEAD_DIM and fp > fs

    x2 = x.reshape(M, D)
    tgt = loss_target.reshape(M, D)
    tr_ = jnp.transpose
    big = [w_in[0], w_out[0], tr_(w_gate[0]), tr_(w_up[0]), w_down[0]]
    big_m = [m_w_in[0], m_w_out[0], tr_(m_w_gate[0]), tr_(m_w_up[0]), m_w_down[0]]
    big_v = [v_w_in[0], v_w_out[0], tr_(v_w_gate[0]), tr_(v_w_up[0]), v_w_down[0]]
    names = ["w_in", "w_out", "w_gate", "w_up", "w_down"]
    c_arr = jnp.reshape(lax.axis_index("c"), (1,)).astype(jnp.int32)
    j_arr = jnp.reshape(2 * lax.axis_index("x") + lax.axis_index("y"), (1,)).astype(jnp.int32)

    ns_in = w_in.shape[2]
    ks_out = w_out.shape[1]
    geo = [_Weight("col", (D, ns_in), ns_in), _Weight("row", (ks_out, D), ks_out),
           _Weight("row", (fs, D), fp), _Weight("row", (fs, D), fp), _Weight("row", (fs, D), fp)]
    fulls = [_cast_into_full(big[w], geo[w], j_arr, "cast_" + names[w]) for w in range(5)]
    zero_rows = jnp.zeros((fp - fs, D), BF16)
    Win, Wout, WgT, WuT, Wd = _all_gather_weights(fulls, geo, zero_rows)

    tabs3 = _rope_tables(positions)
    tabs2 = [t.reshape(M, HEAD_DIM) for t in tabs3]
    hn = _rmsnorm_fwd(x2, norm_mix_g, "norm_mix")
    proj = _matmul(hn, Win, "nn", BF16, "in_proj", rope=(tabs2, 2 * W))
    proj3 = proj.reshape(B, T, N3)
    bias = _dil_bias(T)
    o_dil3, lse3 = _dil_fwd(proj3, bias, Hd, "dil_fwd")
    o_sb3, lt3 = _sb_fwd(proj3, Hd, Hs, "sb_fwd")
    o_dil, o_sb = o_dil3.reshape(M, W), o_sb3.reshape(M, W)
    g_heads = jnp.concatenate([norm_out_dil_g, norm_out_sb_g], axis=1)
    o_mix = _headnorm_fwd(o_dil, o_sb, g_heads, "head_norm")
    h1 = _matmul(o_mix, Wout, "nn", F32, "out_proj", res=x2)
    hn2 = _rmsnorm_fwd(h1, norm_ffn_g, "norm_ffn")
    gate, up, act = _gate_up(hn2, WgT, WuT, "gate_up")
    h2 = _matmul(act, Wd, "nn", F32, "down_proj", res=h1)
    dh2, dh2_b, loss_part, dg_final = _loss_head(h2, tgt, norm_final_g.reshape(1, D), "loss_head")

    dWd = _matmul(act, dh2_b, "tn", BF16, "dw_down")
    d_act = _matmul(dh2_b, Wd, "nt", BF16, "d_act")
    d_gate, d_up = _swiglu_bwd(d_act, gate, up, "swiglu_bwd")
    dWg = _matmul(d_gate, hn2, "tn", BF16, "dw_gate")
    dWu = _matmul(d_up, hn2, "tn", BF16, "dw_up")
    d_hn2 = _matmul(d_gate, WgT, "nn", F32, "d_hn2_gate")
    d_hn2 = _matmul(d_up, WuT, "nn", F32, "d_hn2_up", res=d_hn2)
    dh1, dh1_b, dg_ffn = _rmsnorm_bwd(h1, d_hn2, norm_ffn_g, dh2, "norm_ffn_bwd", True)
    dWo = _matmul(o_mix, dh1_b, "tn", BF16, "dw_out")
    d_mix = _matmul(dh1_b, Wout, "nt", BF16, "d_mix")
    d_o, dg_heads = _headnorm_bwd(d_mix, o_dil, o_sb, g_heads, "head_norm_bwd")
    d_o3 = d_o.reshape(B, T, D)
    dqkv_dil = _dil_bwd(proj3, bias, tabs3, d_o3, o_dil3, lse3, Hd, "dil_bwd")
    dqkv_sb = _sb_bwd(proj3, d_o3, lt3, Hd, Hs, "sb_bwd")
    dproj = jnp.concatenate([*dqkv_dil, *dqkv_sb], axis=-1).reshape(M, N3)
    dWin = _matmul(hn, dproj, "tn", BF16, "dw_in")
    d_hn = _matmul(dproj, Win, "nt", F32, "d_hn")
    dx, dg_mix = _rmsnorm_bwd(x2, d_hn, norm_mix_g, dh1, "norm_mix_bwd", False)

    grads_full = [dWin, dWo, dWg, dWu, dWd]
    from_sibling = _sibling_exchange(grads_full, geo)
    chip_sums = [_chip_sum(grads_full[w], from_sibling[w], geo[w], c_arr, "chip_sum_" + names[w]) for w in range(5)]
    from_chips = _chip_exchange(chip_sums, geo)
    halves = [_final_sum(chip_sums[w], from_chips[w], geo[w], j_arr, "final_sum_" + names[w]) for w in range(5)]
    other_halves = _final_exchange(halves, geo)
    big_out = [_adam(big[w], halves[w], other_halves[w], big_m[w], big_v[w], geo[w], c_arr, "adam_" + names[w])
               for w in range(5)]
    for w in (2, 3):
        big_out[w] = [tr_(a) for a in big_out[w]]

    tot = _small_all_reduce([dg_mix, dg_heads, dg_ffn, dg_final], loss_part, D)
    loss = tot[4, 0]
    small_w = [norm_mix_g, norm_out_dil_g, norm_out_sb_g, norm_ffn_g, norm_final_g.reshape(1, D)]
    small_m = [m_norm_mix_g, m_norm_out_dil_g, m_norm_out_sb_g, m_norm_ffn_g, m_norm_final_g.reshape(1, D)]
    small_v = [v_norm_mix_g, v_norm_out_dil_g, v_norm_out_sb_g, v_norm_ffn_g, v_norm_final_g.reshape(1, D)]
    so = _small_adam(tot, small_w, small_m, small_v, [(0, 0), (1, 0), (1, W), (2, 0), (3, 0)])
    small_out = [so[4 * i:4 * i + 4] for i in range(5)]
    small_out[4] = [a.reshape(D) for a in small_out[4]]

    per_weight = [small_out[0], big_out[0], small_out[1], small_out[2], big_out[1], small_out[3],
                  big_out[2], big_out[3], big_out[4], small_out[4]]

    def field(i):
        res = []
        for n_, o in enumerate(per_weight):
            a = o[i]
            res.append(a[None] if n_ in (1, 4, 6, 7, 8) else a)
        return res

    return (loss, dx.reshape(B, T, D), *field(0), *field(1), *field(2), *field(3))
```

```python
import functools
import math

import jax
import jax.numpy as jnp
from jax import lax
from jax.experimental import pallas as pl
from jax.experimental.pallas import tpu as pltpu

F32 = jnp.float32
BF16 = jnp.bfloat16
MESH = pl.DeviceIdType.MESH
ANY = pl.BlockSpec(memory_space=pl.ANY)

LANES = 128
HEAD_DIM = 128
DIL_STEPS = 128
DILATIONS = (1, 4, 16)
TQ = 256
TK_DIL = 512
TK_SB = 256
NEG = -1e30
ROPE_DIM = 32
ROPE_THETA = 500000.0
RMS_EPS = 1e-5
ADAM_LR, ADAM_B1, ADAM_B2, ADAM_EPS, ADAM_WD, ADAM_STEP = 0.001, 0.9, 0.999, 1e-08, 0.01, 10
N_CHIPS = 4
VMEM_LIMIT = 48 * 1024 * 1024

NN = ((1,), (0,))
NT = ((1,), (1,))
TN = ((0,), (0,))


def _dot(a, b, dims):
    return lax.dot_general(a, b, (dims, ((), ())), preferred_element_type=F32)


def _tile(dim, pref, unit=LANES):
    if dim <= pref:
        return dim
    t = (pref // unit) * unit
    while t > unit and dim % t:
        t -= unit
    assert dim % t == 0, (dim, pref, unit)
    return t


def _params(sem=None):
    return pltpu.CompilerParams(dimension_semantics=sem, vmem_limit_bytes=VMEM_LIMIT)


class _Job:
    def __init__(self, inputs, outputs, aliases, nsem, copies):
        self.inputs, self.outputs, self.aliases, self.nsem, self.copies = inputs, outputs, aliases, nsem, copies

    def start(self, ins, outs, send, recv):
        for cp in self.copies(ins, outs, send, recv)[0]:
            cp.start()

    def finish(self, ins, outs, send, recv):
        started, landing = self.copies(ins, outs, send, recv)
        for make in landing:
            make().wait_recv()
        for cp in started:
            cp.wait_send()


def _hosted_call(body, name, grid, in_specs, out_specs, out_shape, scratch_shapes, args, jobs, semantics):
    nbi, nbo, nbs = len(in_specs), len(out_specs), len(scratch_shapes)
    if not jobs:
        res = pl.pallas_call(
            body, name=name, grid=grid, in_specs=list(in_specs), out_specs=list(out_specs), out_shape=list(out_shape),
            scratch_shapes=list(scratch_shapes), compiler_params=_params(semantics))(*args)
        return list(res), []
    j_in = [a for jb in jobs for a in jb.inputs]
    j_out = [o for jb in jobs for o in jb.outputs]
    aliases, ii, oo, sems = {}, nbi, nbo, []
    for jb in jobs:
        for a, b in jb.aliases.items():
            aliases[ii + a] = oo + b
        ii += len(jb.inputs)
        oo += len(jb.outputs)
        sems += [pltpu.SemaphoreType.DMA((jb.nsem,)), pltpu.SemaphoreType.DMA((jb.nsem,))]

    def wrapped(*refs):
        p = nbi + len(j_in)
        b_in, ji = refs[:nbi], refs[nbi:p]
        b_out, jo = refs[p:p + nbo], refs[p + nbo:p + nbo + len(j_out)]
        p += nbo + len(j_out)
        b_scr, js = refs[p:p + nbs], refs[p + nbs:]
        pids = [pl.program_id(k) for k in range(len(grid))]
        first = functools.reduce(jnp.logical_and, [pid == 0 for pid in pids])
        last = functools.reduce(jnp.logical_and, [pid == g - 1 for pid, g in zip(pids, grid)])

        def each(what):
            a = b = 0
            for n, jb in enumerate(jobs):
                getattr(jb, what)(ji[a:a + len(jb.inputs)], jo[b:b + len(jb.outputs)], js[2 * n], js[2 * n + 1])
                a += len(jb.inputs)
                b += len(jb.outputs)

        @pl.when(first)
        def _():
            each("start")

        body(*b_in, *b_out, *b_scr)

        @pl.when(last)
        def _():
            each("finish")

    res = pl.pallas_call(
        wrapped, name=name, grid=grid,
        in_specs=list(in_specs) + [ANY] * len(j_in), out_specs=list(out_specs) + [ANY] * len(j_out),
        out_shape=list(out_shape) + j_out, input_output_aliases=aliases,
        scratch_shapes=list(scratch_shapes) + sems,
        compiler_params=_params(("arbitrary",) * len(grid)))(*args, *j_in)
    return list(res[:nbo]), list(res[nbo:])


def _rope(a, c, s1, s2):
    half = ROPE_DIM // 2
    return a * c + pltpu.roll(a, HEAD_DIM - half, 1) * s1 + pltpu.roll(a, half, 1) * s2


def _matmul(a, b, mode, out_dtype, name, res=None, rope=None, jobs=(), tm=1024, tn=1024, tk=1024):
    if mode == "nn":
        (M, K), (_, N) = a.shape, b.shape
    elif mode == "nt":
        (M, K), (N, _) = a.shape, b.shape
    else:
        (K, M), (_, N) = a.shape, b.shape
    tm, tk = _tile(M, tm), _tile(K, tk)
    tn = _tile(N, tn) if rope is None else _tile(math.gcd(N, rope[1]), tn)
    nk = K // tk
    dims = {"nn": NN, "nt": NT, "tn": TN}[mode]
    a_spec = (pl.BlockSpec((tk, tm), lambda i, j, k: (k, i)) if mode == "tn"
              else pl.BlockSpec((tm, tk), lambda i, j, k: (i, k)))
    b_spec = (pl.BlockSpec((tn, tk), lambda i, j, k: (j, k)) if mode == "nt"
              else pl.BlockSpec((tk, tn), lambda i, j, k: (k, j)))
    o_spec = pl.BlockSpec((tm, tn), lambda i, j, k: (i, j))
    n_extra = (1 if res is not None else 0) + (3 if rope is not None else 0)
    if rope is not None:
        assert rope[1] % tn == 0

    def body(*refs):
        a_ref, b_ref = refs[:2]
        extra = refs[2:2 + n_extra]
        o_ref, acc_ref = refs[2 + n_extra:]
        k = pl.program_id(2)
        part = _dot(a_ref[...], b_ref[...], dims)

        @pl.when(k == 0)
        def _():
            acc_ref[...] = part

        @pl.when(k > 0)
        def _():
            acc_ref[...] += part

        if rope is None:
            @pl.when(k == nk - 1)
            def _():
                out = acc_ref[...]
                if res is not None:
                    out = out + extra[0][...]
                o_ref[...] = out.astype(out_dtype)
        else:
            roped = pl.program_id(1) * tn < rope[1]

            @pl.when(jnp.logical_and(k == nk - 1, roped))
            def _():
                c, s1, s2 = extra[0][...], extra[1][...], extra[2][...]
                for h in range(tn // HEAD_DIM):
                    cols = slice(h * HEAD_DIM, (h + 1) * HEAD_DIM)
                    o_ref[:, cols] = _rope(acc_ref[:, cols], c, s1, s2).astype(out_dtype)

            @pl.when(jnp.logical_and(k == nk - 1, jnp.logical_not(roped)))
            def _():
                o_ref[...] = acc_ref[...].astype(out_dtype)

    in_specs, args = [a_spec, b_spec], [a, b]
    if res is not None:
        in_specs.append(o_spec)
        args.append(res)
    if rope is not None:
        in_specs += [pl.BlockSpec((tm, HEAD_DIM), lambda i, j, k: (i, 0))] * 3
        args += list(rope[0])
    (out,), extra_out = _hosted_call(
        body, name, (M // tm, N // tn, nk), in_specs, [o_spec], [jax.ShapeDtypeStruct((M, N), out_dtype)],
        [pltpu.VMEM((tm, tn), F32)], args, jobs, ("parallel", "parallel", "arbitrary"))
    return (out, extra_out) if jobs else out


def _gate_up(hn, wg, wu, name):
    M, K = hn.shape
    N = wg.shape[0]
    tm, tn, tk = _tile(M, 1024), _tile(N, 1024), _tile(K, 1024)
    nk = K // tk

    def body(a_ref, g_ref, u_ref, og_ref, ou_ref, oa_ref, accg, accu):
        k = pl.program_id(2)
        a = a_ref[...]
        pg = _dot(a, g_ref[...], NT)
        pu = _dot(a, u_ref[...], NT)

        @pl.when(k == 0)
        def _():
            accg[...] = pg
            accu[...] = pu

        @pl.when(k > 0)
        def _():
            accg[...] += pg
            accu[...] += pu

        @pl.when(k == nk - 1)
        def _():
            g = accg[...]
            u = accu[...]
            og_ref[...] = g.astype(BF16)
            ou_ref[...] = u.astype(BF16)
            oa_ref[...] = (g * jax.nn.sigmoid(g) * u).astype(BF16)

    w_spec = pl.BlockSpec((tn, tk), lambda i, j, k: (j, k))
    o_spec = pl.BlockSpec((tm, tn), lambda i, j, k: (i, j))
    sds = jax.ShapeDtypeStruct((M, N), BF16)
    return pl.pallas_call(
        body, name=name, grid=(M // tm, N // tn, nk),
        in_specs=[pl.BlockSpec((tm, tk), lambda i, j, k: (i, k)), w_spec, w_spec],
        out_specs=[o_spec, o_spec, o_spec], out_shape=[sds, sds, sds],
        scratch_shapes=[pltpu.VMEM((tm, tn), F32), pltpu.VMEM((tm, tn), F32)],
        compiler_params=_params(("parallel", "parallel", "arbitrary")),
    )(hn, wg, wu)


def _swiglu_bwd(d_act, gate, up, name):
    M, N = d_act.shape
    tm, tn = _tile(M, 512, 16), _tile(N, 1024)

    def body(d_ref, g_ref, u_ref, dg_ref, du_ref):
        d = d_ref[...].astype(F32)
        g = g_ref[...].astype(F32)
        u = u_ref[...].astype(F32)
        sig = jax.nn.sigmoid(g)
        du_ref[...] = (d * g * sig).astype(BF16)
        dg_ref[...] = (d * u * sig * (1.0 + g * (1.0 - sig))).astype(BF16)

    spec = pl.BlockSpec((tm, tn), lambda i, j: (i, j))
    sds = jax.ShapeDtypeStruct((M, N), BF16)
    return pl.pallas_call(
        body, name=name, grid=(M // tm, N // tn),
        in_specs=[spec, spec, spec], out_specs=[spec, spec], out_shape=[sds, sds],
        compiler_params=_params(("parallel", "parallel")),
    )(d_act, gate, up)


def _rmsnorm_fwd(x, g, name):
    M, D = x.shape
    tm = _tile(M, 256, 16)

    def body(x_ref, g_ref, o_ref):
        xv = x_ref[...]
        r = lax.rsqrt(jnp.mean(xv * xv, axis=-1, keepdims=True) + RMS_EPS)
        o_ref[...] = (xv * r * g_ref[...]).astype(BF16)

    return pl.pallas_call(
        body, name=name, grid=(M // tm,),
        in_specs=[pl.BlockSpec((tm, D), lambda i: (i, 0)), pl.BlockSpec((1, D), lambda i: (0, 0))],
        out_specs=pl.BlockSpec((tm, D), lambda i: (i, 0)),
        out_shape=jax.ShapeDtypeStruct((M, D), BF16),
        compiler_params=_params(("parallel",)),
    )(x, g)


def _fold8(v):
    tm, D = v.shape
    return jnp.sum(v.reshape(tm // 8, 8, D), axis=0)


def _rmsnorm_bwd(x, dy, g, res, name, want_bf16):
    M, D = x.shape
    tm = _tile(M, 128, 16)

    def body(x_ref, dy_ref, g_ref, r_ref, *outs):
        dx_ref, dg_ref = outs[0], outs[-1]
        xv = x_ref[...]
        dyv = dy_ref[...].astype(F32)
        r = lax.rsqrt(jnp.mean(xv * xv, axis=-1, keepdims=True) + RMS_EPS)
        u = dyv * g_ref[...]
        dot = jnp.sum(xv * u, axis=-1, keepdims=True)
        dx = r * u - xv * (r * r * r * (1.0 / D)) * dot + r_ref[...]
        dx_ref[...] = dx
        if want_bf16:
            outs[1][...] = dx.astype(BF16)
        part = _fold8(dyv * xv * r)

        @pl.when(pl.program_id(0) == 0)
        def _():
            dg_ref[...] = part

        @pl.when(pl.program_id(0) > 0)
        def _():
            dg_ref[...] += part

    row = pl.BlockSpec((tm, D), lambda i: (i, 0))
    out_specs = [row] + ([row] if want_bf16 else []) + [pl.BlockSpec((8, D), lambda i: (0, 0))]
    out_shape = ([jax.ShapeDtypeStruct((M, D), F32)] + ([jax.ShapeDtypeStruct((M, D), BF16)] if want_bf16 else [])
                 + [jax.ShapeDtypeStruct((8, D), F32)])
    return pl.pallas_call(
        body, name=name, grid=(M // tm,),
        in_specs=[row, row, pl.BlockSpec((1, D), lambda i: (0, 0)), row],
        out_specs=out_specs, out_shape=out_shape,
        compiler_params=_params(("arbitrary",)),
    )(x, dy, g, res)


def _loss_head(h, tgt, g, name):
    M, D = h.shape
    tm = _tile(M, 128, 16)

    def body(h_ref, t_ref, g_ref, dh_ref, dhb_ref, l_ref, dg_ref):
        hv = h_ref[...]
        gv = g_ref[...]
        r = lax.rsqrt(jnp.mean(hv * hv, axis=-1, keepdims=True) + RMS_EPS)
        n = hv * r
        e = n * gv - t_ref[...]
        dy = e * (1.0 / D)
        u = dy * gv
        dot = jnp.sum(hv * u, axis=-1, keepdims=True)
        dh = r * u - hv * (r * r * r * (1.0 / D)) * dot
        dh_ref[...] = dh
        dhb_ref[...] = dh.astype(BF16)
        rows = jnp.sum(e * e, axis=-1, keepdims=True)
        lpart = jnp.broadcast_to(jnp.sum(rows, axis=0, keepdims=True) * (0.5 / D), (8, LANES))
        gpart = _fold8(dy * n)

        @pl.when(pl.program_id(0) == 0)
        def _():
            l_ref[...] = lpart
            dg_ref[...] = gpart

        @pl.when(pl.program_id(0) > 0)
        def _():
            l_ref[...] += lpart
            dg_ref[...] += gpart

    row = pl.BlockSpec((tm, D), lambda i: (i, 0))
    return pl.pallas_call(
        body, name=name, grid=(M // tm,),
        in_specs=[row, row, pl.BlockSpec((1, D), lambda i: (0, 0))],
        out_specs=[row, row, pl.BlockSpec((8, LANES), lambda i: (0, 0)), pl.BlockSpec((8, D), lambda i: (0, 0))],
        out_shape=[jax.ShapeDtypeStruct((M, D), F32), jax.ShapeDtypeStruct((M, D), BF16),
                   jax.ShapeDtypeStruct((8, LANES), F32), jax.ShapeDtypeStruct((8, D), F32)],
        compiler_params=_params(("arbitrary",)),
    )(h, tgt, g)


def _headnorm_fwd(o_dil, o_sb, g, name):
    M, W = o_dil.shape
    D = 2 * W
    tm = _tile(M, 256, 16)

    def body(a_ref, b_ref, g_ref, o_ref):
        for h in range(D // HEAD_DIM):
            src = a_ref if h < W // HEAD_DIM else b_ref
            lo = (h * HEAD_DIM) % W
            v = src[:, lo:lo + HEAD_DIM]
            r = lax.rsqrt(jnp.mean(v * v, axis=-1, keepdims=True) + RMS_EPS)
            o_ref[:, h * HEAD_DIM:(h + 1) * HEAD_DIM] = (v * r * g_ref[:, h * HEAD_DIM:(h + 1) * HEAD_DIM]).astype(BF16)

    half = pl.BlockSpec((tm, W), lambda i: (i, 0))
    return pl.pallas_call(
        body, name=name, grid=(M // tm,),
        in_specs=[half, half, pl.BlockSpec((1, D), lambda i: (0, 0))],
        out_specs=pl.BlockSpec((tm, D), lambda i: (i, 0)),
        out_shape=jax.ShapeDtypeStruct((M, D), BF16),
        compiler_params=_params(("parallel",)),
    )(o_dil, o_sb, g)


def _headnorm_bwd(d_mix, o_dil, o_sb, g, name):
    M, W = o_dil.shape
    D = 2 * W
    tm = _tile(M, 128, 16)

    def body(d_ref, a_ref, b_ref, g_ref, do_ref, dg_ref):
        @pl.when(pl.program_id(0) == 0)
        def _():
            dg_ref[...] = jnp.zeros(dg_ref.shape, F32)

        for h in range(D // HEAD_DIM):
            src = a_ref if h < W // HEAD_DIM else b_ref
            lo = (h * HEAD_DIM) % W
            cols = slice(h * HEAD_DIM, (h + 1) * HEAD_DIM)
            v = src[:, lo:lo + HEAD_DIM]
            dy = d_ref[:, cols].astype(F32)
            r = lax.rsqrt(jnp.mean(v * v, axis=-1, keepdims=True) + RMS_EPS)
            u = dy * g_ref[:, cols]
            dot = jnp.sum(v * u, axis=-1, keepdims=True)
            do_ref[:, cols] = r * u - v * (r * r * r * (1.0 / HEAD_DIM)) * dot
            dg_ref[:, cols] += _fold8(dy * v * r)

    half = pl.BlockSpec((tm, W), lambda i: (i, 0))
    row = pl.BlockSpec((tm, D), lambda i: (i, 0))
    return pl.pallas_call(
        body, name=name, grid=(M // tm,),
        in_specs=[row, half, half, pl.BlockSpec((1, D), lambda i: (0, 0))],
        out_specs=[row, pl.BlockSpec((8, D), lambda i: (0, 0))],
        out_shape=[jax.ShapeDtypeStruct((M, D), F32), jax.ShapeDtypeStruct((8, D), F32)],
        compiler_params=_params(("arbitrary",)),
    )(d_mix, o_dil, o_sb, g)


def _rope_tables(positions):
    half = ROPE_DIM // 2
    inv_freq = jnp.power(jnp.float32(ROPE_THETA), -jnp.arange(half, dtype=F32) / half)
    ang = positions.astype(F32)[..., None] * inv_freq
    cos, sin = jnp.cos(ang), jnp.sin(ang)
    rest = HEAD_DIM - ROPE_DIM
    one = jnp.ones(cos.shape[:-1] + (rest,), F32)
    z16 = jnp.zeros_like(sin)
    zr = jnp.zeros_like(one)
    c = jnp.concatenate([cos, cos, one], axis=-1)
    s1 = jnp.concatenate([-sin, z16, zr], axis=-1)
    s2 = jnp.concatenate([z16, sin, zr], axis=-1)
    return c, s1, s2


def _dil_bias(T):
    ne = T // TQ
    e = jnp.arange(ne, dtype=jnp.int32)[:, None, None]
    r = jnp.arange(TQ, dtype=jnp.int32)[None, :, None]
    c = jnp.arange(TK_DIL, dtype=jnp.int32)[None, None, :]
    dist = e * TQ + r - c
    mult = jnp.zeros(dist.shape, F32)
    for d in DILATIONS:
        mult = mult + jnp.where((dist % d == 0) & (dist <= DIL_STEPS * d), 1.0, 0.0)
    return jnp.where((dist >= 0) & (mult > 0), jnp.log(jnp.maximum(mult, 1.0)), NEG)


def _dil_fwd(proj3, bias, Hd, name, jobs=()):
    B, T, N3 = proj3.shape
    nq = T // TQ
    per = TK_DIL // TQ
    scale = HEAD_DIM ** -0.5

    def body(q_ref, k_ref, v_ref, b_ref, o_ref, l_ref):
        i = pl.program_id(2)
        q = q_ref[...]
        last = i // per

        def step(t, carry):
            m, l, acc = carry
            j = last - t
            off = pl.multiple_of(j * TK_DIL, TK_DIL)
            kj = k_ref[pl.ds(off, TK_DIL), :]
            vj = v_ref[pl.ds(off, TK_DIL), :]
            s = _dot(q, kj, NT) * scale + b_ref[i - per * j]
            m_new = jnp.maximum(m, jnp.max(s, axis=-1, keepdims=True))
            p = jnp.exp(s - m_new)
            corr = jnp.exp(m - m_new)
            l = l * corr + jnp.sum(p, axis=-1, keepdims=True)
            acc = acc * corr + _dot(p.astype(BF16), vj, NN)
            return m_new, l, acc

        m, l, acc = lax.fori_loop(0, last + 1, step, (jnp.full((TQ, 1), NEG, F32), jnp.zeros((TQ, 1), F32),
                                                       jnp.zeros((TQ, HEAD_DIM), F32)))
        o_ref[...] = acc / l
        l_ref[...] = jnp.broadcast_to(m + jnp.log(l), (TQ, HEAD_DIM))

    full = lambda base: pl.BlockSpec((None, T, HEAD_DIM), lambda b, h, i: (b, 0, base + h))
    blk = lambda base: pl.BlockSpec((None, TQ, HEAD_DIM), lambda b, h, i: (b, i, base + h))
    sds = jax.ShapeDtypeStruct((B, T, Hd * HEAD_DIM), F32)
    return _hosted_call(
        body, name, (B, Hd, nq),
        [blk(0), full(Hd), full(2 * Hd), pl.BlockSpec(bias.shape, lambda b, h, i: (0, 0, 0))],
        [blk(0), blk(0)], [sds, sds], [], (proj3, proj3, proj3, bias), jobs, ("parallel", "parallel", "arbitrary"))


def _dil_bwd(proj3, bias, tabs, d_o3, o3, lse3, Hd, name, jobs=()):
    B, T, N3 = proj3.shape
    nq = T // TQ
    per = TK_DIL // TQ
    scale = HEAD_DIM ** -0.5

    def body(q_ref, k_ref, v_ref, b_ref, do_ref, o_ref, l_ref, cq, s1q, s2q, ct, s1t, s2t,
             dq_ref, dk_ref, dv_ref, dks, dvs):
        i = pl.program_id(2)

        @pl.when(i == 0)
        def _():
            dks[...] = jnp.zeros(dks.shape, F32)
            dvs[...] = jnp.zeros(dvs.shape, F32)

        q = q_ref[...]
        do = do_ref[...]
        dob = do.astype(BF16)
        dterm = jnp.sum(do * o_ref[...], axis=-1, keepdims=True)
        lse = jnp.concatenate([l_ref[...]] * (TK_DIL // HEAD_DIM), axis=1)
        last = i // per

        def step(j, dq):
            off = pl.multiple_of(j * TK_DIL, TK_DIL)
            kj = k_ref[pl.ds(off, TK_DIL), :]
            vj = v_ref[pl.ds(off, TK_DIL), :]
            s = _dot(q, kj, NT) * scale + b_ref[i - per * j]
            p = jnp.exp(s - lse)
            dp = _dot(dob, vj, NT)
            ds = (p * (dp - dterm) * scale).astype(BF16)
            dks[pl.ds(off, TK_DIL), :] += _dot(ds, q, TN)
            dvs[pl.ds(off, TK_DIL), :] += _dot(p.astype(BF16), dob, TN)
            return dq + _dot(ds, kj, NN)

        dq = lax.fori_loop(0, last + 1, step, jnp.zeros((TQ, HEAD_DIM), F32))
        dq_ref[...] = _rope(dq, cq[...], -s1q[...], -s2q[...]).astype(BF16)

        @pl.when(i == nq - 1)
        def _():
            dk_ref[...] = _rope(dks[...], ct[...], -s1t[...], -s2t[...]).astype(BF16)
            dv_ref[...] = dvs[...].astype(BF16)

    full = lambda base: pl.BlockSpec((None, T, HEAD_DIM), lambda b, h, i: (b, 0, base + h))
    blk = lambda base: pl.BlockSpec((None, TQ, HEAD_DIM), lambda b, h, i: (b, i, base + h))
    tab_q = pl.BlockSpec((None, TQ, HEAD_DIM), lambda b, h, i: (b, i, 0))
    tab_t = pl.BlockSpec((None, T, HEAD_DIM), lambda b, h, i: (b, 0, 0))
    sds = jax.ShapeDtypeStruct((B, T, Hd * HEAD_DIM), BF16)
    return _hosted_call(
        body, name, (B, Hd, nq),
        [blk(0), full(Hd), full(2 * Hd), pl.BlockSpec(bias.shape, lambda b, h, i: (0, 0, 0)),
         blk(0), blk(0), blk(0), tab_q, tab_q, tab_q, tab_t, tab_t, tab_t],
        [blk(0), full(0), full(0)], [sds, sds, sds],
        [pltpu.VMEM((T, HEAD_DIM), F32), pltpu.VMEM((T, HEAD_DIM), F32)],
        (proj3, proj3, proj3, bias, d_o3, o3, lse3, *tabs, *tabs), jobs, ("parallel", "parallel", "arbitrary"))


def _softplus(z):
    return jnp.maximum(z, 0.0) + jnp.log(1.0 + jnp.exp(-jnp.abs(z)))


def _tri_sum(x, m01):
    n = x.shape[0]
    hi = x.astype(BF16)
    lo = (x - hi.astype(F32)).astype(BF16)
    both = _dot(jnp.concatenate([hi, lo], axis=0), m01, NN)
    return both[:n] + both[n:]


def _sb_iota():
    row = lax.broadcasted_iota(jnp.int32, (TQ, TK_SB), 0)
    col = lax.broadcasted_iota(jnp.int32, (TQ, TK_SB), 1)
    return row, col


def _sb_fwd(proj3, Hd, Hs, name, jobs=()):
    B, T, N3 = proj3.shape
    nq = T // TQ
    qb, kb_, vb_ = 3 * Hd, 3 * Hd + Hs, 3 * Hd + 2 * Hs
    scale = HEAD_DIM ** -0.5

    def body(q_ref, k_ref, v_ref, o_ref, lt_ref):
        i = pl.program_id(2)
        q = q_ref[...]
        row, col = _sb_iota()
        after = jnp.where(row > col, 1.0, 0.0).astype(BF16)

        def step(t, carry):
            acc, cs = carry
            off = pl.multiple_of((i - t) * TK_SB, TK_SB)
            kj = k_ref[pl.ds(off, TK_SB), :]
            vj = v_ref[pl.ds(off, TK_SB), :]
            z = _dot(q, kj, NT) * scale
            msk = col < row + jnp.where(t > 0, TK_SB, 0)
            sp = _softplus(z)
            ln = jnp.where(msk, -sp, 0.0)
            excl = _tri_sum(ln, after) + cs
            a = jnp.where(msk, jnp.exp(z - sp + excl), 0.0)
            acc = acc + _dot(a.astype(BF16), vj, NN)
            cs = cs + jnp.sum(ln, axis=-1, keepdims=True)
            return acc, cs

        acc, cs = lax.fori_loop(0, i + 1, step, (jnp.zeros((TQ, HEAD_DIM), F32), jnp.zeros((TQ, 1), F32)))
        o_ref[...] = acc
        lt_ref[...] = jnp.broadcast_to(cs, (TQ, HEAD_DIM))

    full = lambda base: pl.BlockSpec((None, T, HEAD_DIM), lambda b, h, i: (b, 0, base + h))
    blk = lambda base: pl.BlockSpec((None, TQ, HEAD_DIM), lambda b, h, i: (b, i, base + h))
    sds = jax.ShapeDtypeStruct((B, T, Hs * HEAD_DIM), F32)
    return _hosted_call(
        body, name, (B, Hs, nq), [blk(qb), full(kb_), full(vb_)], [blk(0), blk(0)], [sds, sds], [],
        (proj3, proj3, proj3), jobs, ("parallel", "parallel", "arbitrary"))


def _sb_bwd(proj3, d_o3, lt3, Hd, Hs, name, jobs=()):
    B, T, N3 = proj3.shape
    nq = T // TQ
    qb, kb_, vb_ = 3 * Hd, 3 * Hd + Hs, 3 * Hd + 2 * Hs
    scale = HEAD_DIM ** -0.5

    def body(q_ref, k_ref, v_ref, do_ref, lt_ref, dq_ref, dk_ref, dv_ref, dks, dvs):
        i = pl.program_id(2)

        @pl.when(i == 0)
        def _():
            dks[...] = jnp.zeros(dks.shape, F32)
            dvs[...] = jnp.zeros(dvs.shape, F32)

        q = q_ref[...]
        dob = do_ref[...].astype(BF16)
        total = jnp.concatenate([lt_ref[...]] * (TK_SB // HEAD_DIM), axis=1)
        row, col = _sb_iota()
        before = jnp.where(row < col, 1.0, 0.0).astype(BF16)

        def step(j, carry):
            dq, pc, gc = carry
            off = pl.multiple_of(j * TK_SB, TK_SB)
            kj = k_ref[pl.ds(off, TK_SB), :]
            vj = v_ref[pl.ds(off, TK_SB), :]
            z = _dot(q, kj, NT) * scale
            msk = col < row + jnp.where(j < i, TK_SB, 0)
            sp = _softplus(z)
            ln = jnp.where(msk, -sp, 0.0)
            excl = total - (_tri_sum(ln, before) + ln + pc)
            a = jnp.where(msk, jnp.exp(z - sp + excl), 0.0)
            g = a * _dot(dob, vj, NT)
            big = _tri_sum(g, before) + gc
            sig = jnp.exp(z - sp)
            dz = (jnp.where(msk, g * (1.0 - sig) - big * sig, 0.0) * scale).astype(BF16)
            dks[pl.ds(off, TK_SB), :] += _dot(dz, q, TN)
            dvs[pl.ds(off, TK_SB), :] += _dot(a.astype(BF16), dob, TN)
            return (dq + _dot(dz, kj, NN), pc + jnp.sum(ln, axis=-1, keepdims=True),
                    gc + jnp.sum(g, axis=-1, keepdims=True))

        zero1 = jnp.zeros((TQ, 1), F32)
        dq, _, _ = lax.fori_loop(0, i + 1, step, (jnp.zeros((TQ, HEAD_DIM), F32), zero1, zero1))
        dq_ref[...] = dq.astype(BF16)

        @pl.when(i == nq - 1)
        def _():
            dk_ref[...] = dks[...].astype(BF16)
            dv_ref[...] = dvs[...].astype(BF16)

    full = lambda base: pl.BlockSpec((None, T, HEAD_DIM), lambda b, h, i: (b, 0, base + h))
    blk = lambda base: pl.BlockSpec((None, TQ, HEAD_DIM), lambda b, h, i: (b, i, base + h))
    sds = jax.ShapeDtypeStruct((B, T, Hs * HEAD_DIM), BF16)
    return _hosted_call(
        body, name, (B, Hs, nq), [blk(qb), full(kb_), full(vb_), blk(Hd), blk(0)],
        [blk(0), full(0), full(0)], [sds, sds, sds],
        [pltpu.VMEM((T, HEAD_DIM), F32), pltpu.VMEM((T, HEAD_DIM), F32)],
        (proj3, proj3, proj3, d_o3, lt3), jobs, ("parallel", "parallel", "arbitrary"))


def _place():
    x, y, c = lax.axis_index("x"), lax.axis_index("y"), lax.axis_index("c")
    chips = [(1 - x, y), (x, 1 - y), (1 - x, 1 - y)]
    return x, y, c, chips


class _Weight:
    def __init__(self, kind, shard_shape, slot):
        self.kind, self.slot = kind, slot
        self.R, self.C = shard_shape
        if kind == "col":
            assert slot == self.C
            self.full = (self.R, N_CHIPS * slot)
            self.half = (self.R // 2, N_CHIPS * slot)
            self.piece = (self.R // 2, slot)
        else:
            self.full = (N_CHIPS * slot, self.C)
            self.half = (N_CHIPS * slot, self.C // 2)
            self.piece = (self.R, self.C // 2)

    def _rows(self, jj, n):
        return pl.ds(pl.multiple_of(jj * self.slot, 16), n)

    def full_half(self, ref, jj, hc):
        if self.kind == "col":
            return ref.at[pl.ds(hc * (self.R // 2), self.R // 2), pl.ds(pl.multiple_of(jj * self.slot, LANES), self.C)]
        return ref.at[self._rows(jj, self.R), pl.ds(hc * (self.C // 2), self.C // 2)]

    def region_half(self, ref, hc):
        if self.kind == "col":
            return ref.at[pl.ds(hc * (self.R // 2), self.R // 2), :]
        return ref.at[:, pl.ds(hc * (self.C // 2), self.C // 2)]

    def half_piece(self, ref, jj):
        if self.kind == "col":
            return ref.at[:, pl.ds(pl.multiple_of(jj * self.slot, LANES), self.slot)]
        return ref.at[self._rows(jj, self.R), :]


def _cast_into_full(w, geo, j_arr, name):
    R, C = w.shape
    tr = _tile(R, 256, 16)
    if geo.kind == "col":
        o_spec = pl.BlockSpec((tr, C), lambda i, j_ref: (i, j_ref[0]))
    else:
        while geo.slot % tr or R % tr:
            tr -= 16
        spb = geo.slot // tr
        o_spec = pl.BlockSpec((tr, C), lambda i, j_ref: (j_ref[0] * spb + i, 0))

    def body(j_ref, w_ref, o_ref):
        o_ref[...] = w_ref[...].astype(BF16)

    return pl.pallas_call(
        body, name=name,
        grid_spec=pltpu.PrefetchScalarGridSpec(
            num_scalar_prefetch=1, grid=(R // tr,),
            in_specs=[pl.BlockSpec((tr, C), lambda i, j_ref: (i, 0))], out_specs=o_spec),
        out_shape=jax.ShapeDtypeStruct(geo.full, BF16),
        compiler_params=_params(("parallel",)),
    )(j_arr, w)


def _zero_pad(full, geo, name):
    pad = geo.slot - geo.R
    assert geo.kind == "row" and pad > 0 and geo.R % pad == 0 and geo.slot % pad == 0

    def body(f_ref, o_ref):
        o_ref[...] = jnp.zeros(o_ref.shape, BF16)

    return pl.pallas_call(
        body, name=name, grid=(N_CHIPS,), in_specs=[ANY],
        out_specs=pl.BlockSpec((pad, geo.C), lambda jj: ((jj * geo.slot + geo.R) // pad, 0)),
        out_shape=jax.ShapeDtypeStruct(geo.full, BF16), input_output_aliases={0: 0},
        compiler_params=_params(("arbitrary",)),
    )(full)


def _gather_job(full, geo, stage):
    def copies(ins, outs, send, recv):
        x, y, c, chips = _place()
        j = 2 * x + y

        def cp(src, dst, k, to):
            return pltpu.make_async_remote_copy(src_ref=src, dst_ref=dst, send_sem=send.at[k], recv_sem=recv.at[k],
                                                device_id=to, device_id_type=MESH)

        started, landing = [], []
        for k, (cx, cy) in enumerate(chips):
            jk = 2 * cx + cy
            if stage == "ici":
                started.append(cp(geo.full_half(ins[0], j, c), geo.full_half(outs[0], j, c), k, (cx, cy, c)))
                part = geo.full_half(outs[0], jk, c)
            else:
                started.append(cp(geo.full_half(ins[0], jk, c), geo.full_half(outs[0], jk, c), k, (x, y, 1 - c)))
                part = geo.full_half(outs[0], jk, 1 - c)
            landing.append(functools.partial(cp, part, part, k, (x, y, c)))
        return started, landing

    return _Job([full], [jax.ShapeDtypeStruct(geo.full, BF16)], {0: 0}, 3, copies)


def _sibling_job(grad_full, geo):
    def copies(ins, outs, send, recv):
        x, y, c, _ = _place()
        mk = lambda src, to: pltpu.make_async_remote_copy(
            src_ref=src, dst_ref=outs[0], send_sem=send.at[0], recv_sem=recv.at[0], device_id=to, device_id_type=MESH)
        return [mk(geo.region_half(ins[0], 1 - c), (x, y, 1 - c))], [functools.partial(mk, outs[0], (x, y, c))]

    return _Job([grad_full], [jax.ShapeDtypeStruct(geo.half, BF16)], {}, 1, copies)


def _chips_job(chip_sum, geo):
    def copies(ins, outs, send, recv):
        x, y, c, chips = _place()
        started, landing = [], []
        def mk(k, src, to):
            return pltpu.make_async_remote_copy(src_ref=src, dst_ref=outs[0].at[k], send_sem=send.at[k],
                                                recv_sem=recv.at[k], device_id=to, device_id_type=MESH)

        for k, (cx, cy) in enumerate(chips):
            started.append(mk(k, geo.half_piece(ins[0], 2 * cx + cy), (cx, cy, c)))
            landing.append(functools.partial(mk, k, outs[0].at[k], (x, y, c)))
        return started, landing

    return _Job([chip_sum], [jax.ShapeDtypeStruct((3,) + geo.piece, BF16)], {}, 3, copies)


def _final_job(half, geo):
    def copies(ins, outs, send, recv):
        x, y, c, _ = _place()
        mk = lambda src, to: pltpu.make_async_remote_copy(
            src_ref=src, dst_ref=outs[0], send_sem=send.at[0], recv_sem=recv.at[0], device_id=to, device_id_type=MESH)
        return [mk(ins[0], (x, y, 1 - c))], [functools.partial(mk, outs[0], (x, y, c))]

    return _Job([half], [jax.ShapeDtypeStruct(geo.piece, F32)], {}, 1, copies)


def _all_gather_weights(fulls, geo):
    nw = len(fulls)

    def body(*refs):
        own_refs, f_refs = refs[:nw], refs[nw:2 * nw]
        send, recv = refs[2 * nw:]
        x, y, c, chips = _place()
        j = 2 * x + y

        def remote(w, k, jj, hc, to, own=False):
            part = geo[w].full_half(f_refs[w], jj, hc)
            return pltpu.make_async_remote_copy(
                src_ref=geo[w].full_half(own_refs[w], jj, hc) if own else part, dst_ref=part,
                send_sem=send.at[w, k], recv_sem=recv.at[w, k], device_id=to, device_id_type=MESH)

        sends = []
        for w in range(nw):
            for k, chip in enumerate(chips):
                cp = remote(w, k, j, c, (*chip, c), own=True)
                cp.start()
                sends.append(cp)
        for w in range(nw):
            for k, (cx, cy) in enumerate(chips):
                remote(w, k, 2 * cx + cy, c, (x, y, c)).wait_recv()
                cp = remote(w, 3 + k, 2 * cx + cy, c, (x, y, 1 - c))
                cp.start()
                sends.append(cp)
        for w in range(nw):
            for k, (cx, cy) in enumerate(chips):
                remote(w, 3 + k, 2 * cx + cy, 1 - c, (x, y, c)).wait_recv()
        for cp in sends:
            cp.wait_send()

    return pl.pallas_call(
        body, name="all_gather_weights",
        in_specs=[ANY] * nw, out_specs=[ANY] * nw,
        out_shape=[jax.ShapeDtypeStruct(g.full, BF16) for g in geo],
        input_output_aliases={w: w for w in range(nw)},
        scratch_shapes=[pltpu.SemaphoreType.DMA((nw, 6)), pltpu.SemaphoreType.DMA((nw, 6))],
    )(*fulls)


def _chip_sum(g_full, got, geo, c_arr, name):
    Rh, Ch = geo.half
    tr, tc = _tile(Rh, 256, 16), _tile(Ch, 2048)
    nrb, ncb = Rh // tr, Ch // tc

    def body(c_ref, a_ref, b_ref, o_ref):
        o_ref[...] = (a_ref[...].astype(F32) + b_ref[...].astype(F32)).astype(BF16)

    if geo.kind == "col":
        a_spec = pl.BlockSpec((tr, tc), lambda i, j, c_ref: (c_ref[0] * nrb + i, j))
    else:
        a_spec = pl.BlockSpec((tr, tc), lambda i, j, c_ref: (i, c_ref[0] * ncb + j))
    spec = pl.BlockSpec((tr, tc), lambda i, j, c_ref: (i, j))
    return pl.pallas_call(
        body, name=name,
        grid_spec=pltpu.PrefetchScalarGridSpec(num_scalar_prefetch=1, grid=(nrb, ncb),
                                               in_specs=[a_spec, spec], out_specs=spec),
        out_shape=jax.ShapeDtypeStruct(geo.half, BF16),
        compiler_params=_params(("parallel", "parallel")),
    )(c_arr, g_full, got)


def _final_sum(chip_sum, got, geo, j_arr, name):
    Rp, Cp = geo.piece
    if geo.kind == "col":
        tr = _tile(Rp, 128, 16)
        a_spec = pl.BlockSpec((tr, Cp), lambda i, j_ref: (i, j_ref[0]))
    else:
        tr = _tile(Rp, 128, 16)
        while geo.slot % tr:
            tr -= 16
        assert Rp % tr == 0 and geo.slot % tr == 0
        spb = geo.slot // tr
        a_spec = pl.BlockSpec((tr, Cp), lambda i, j_ref: (j_ref[0] * spb + i, 0))
    r_specs = [pl.BlockSpec((None, tr, Cp), functools.partial(lambda i, j_ref, k: (k, i, 0), k=k)) for k in range(3)]
    o_spec = pl.BlockSpec((tr, Cp), lambda i, j_ref: (i, 0))

    def body(j_ref, a_ref, r0, r1, r2, o_ref):
        o_ref[...] = ((a_ref[...].astype(F32) + r0[...].astype(F32)) + r1[...].astype(F32)) + r2[...].astype(F32)

    return pl.pallas_call(
        body, name=name,
        grid_spec=pltpu.PrefetchScalarGridSpec(num_scalar_prefetch=1, grid=(Rp // tr,),
                                               in_specs=[a_spec] + r_specs, out_specs=o_spec),
        out_shape=jax.ShapeDtypeStruct(geo.piece, F32),
        compiler_params=_params(("parallel",)),
    )(j_arr, chip_sum, got, got, got)


def _final_exchange(halves, geo):
    nw = len(halves)

    def body(*refs):
        h_refs, r_refs, send, recv = refs[:nw], refs[nw:2 * nw], refs[2 * nw], refs[2 * nw + 1]
        x, y, c, _ = _place()
        cps = []
        for w in range(nw):
            cp = pltpu.make_async_remote_copy(
                src_ref=h_refs[w], dst_ref=r_refs[w], send_sem=send.at[w], recv_sem=recv.at[w],
                device_id=(x, y, 1 - c), device_id_type=MESH)
            cp.start()
            cps.append(cp)
        for cp in cps:
            cp.wait()

    return pl.pallas_call(
        body, name="grad_final_exchange",
        in_specs=[ANY] * nw, out_specs=[ANY] * nw,
        out_shape=[jax.ShapeDtypeStruct(g.piece, F32) for g in geo],
        scratch_shapes=[pltpu.SemaphoreType.DMA((nw,)), pltpu.SemaphoreType.DMA((nw,))],
    )(*halves)


def _adam_math(w, g, m, v):
    m = ADAM_B1 * m + (1.0 - ADAM_B1) * g
    v = ADAM_B2 * v + (1.0 - ADAM_B2) * (g * g)
    m_hat = m / (1.0 - ADAM_B1 ** ADAM_STEP)
    v_hat = v / (1.0 - ADAM_B2 ** ADAM_STEP)
    delta = -ADAM_LR * (m_hat / (jnp.sqrt(v_hat) + ADAM_EPS) + ADAM_WD * w)
    return delta, m, v


def _adam(ws, mines, theirs, ms, vs, geo, name, jobs=()):
    n = len(ws)
    R, C = ws[0].shape
    tr = _tile(R, 128 if n == 1 else 32, 8)
    nrb = R // tr
    col = geo.kind == "col"
    assert not col or nrb % 2 == 0

    def rows(k, halved):
        def index(w, i):
            r = jnp.where(w == k, i, jnp.where(w < k, 0, nrb - 1))
            return (r % (nrb // 2) if halved else r, 0)
        return index

    def body(*refs):
        ins, outs = refs[:5 * n], refs[5 * n:]
        wi, i = pl.program_id(0), pl.program_id(1)
        c = lax.axis_index("c")
        for k in range(n):
            @pl.when(wi == k)
            def _():
                w_ref, a_ref, b_ref, m_ref, v_ref = ins[5 * k:5 * k + 5]
                if col:
                    gv = jnp.where(i // (nrb // 2) == c, a_ref[...], b_ref[...])
                else:
                    a, b = a_ref[...], b_ref[...]
                    gv = jnp.concatenate([jnp.where(c == 0, a, b), jnp.where(c == 0, b, a)], axis=1)
                delta, mn, vn = _adam_math(w_ref[...], gv, m_ref[...], v_ref[...])
                og, od, om, ov = outs[4 * k:4 * k + 4]
                og[...] = gv
                od[...] = delta
                om[...] = mn
                ov[...] = vn

    in_specs, out_specs, args = [], [], []
    for k in range(n):
        spec = pl.BlockSpec((tr, C), rows(k, False))
        h_spec = pl.BlockSpec((tr, C), rows(k, True)) if col else pl.BlockSpec((tr, C // 2), rows(k, False))
        in_specs += [spec, h_spec, h_spec, spec, spec]
        out_specs += [spec] * 4
        args += [ws[k], mines[k], theirs[k], ms[k], vs[k]]
    outs, job_outs = _hosted_call(body, name, (n, nrb), in_specs, out_specs,
                                  [jax.ShapeDtypeStruct((R, C), F32)] * (4 * n), [], args, jobs,
                                  ("arbitrary", "arbitrary"))
    return [outs[4 * k:4 * k + 4] for k in range(n)], job_outs


def _small_all_reduce(parts, loss_part, D):
    n = len(parts)

    def body(*refs):
        p_refs, l_ref, o_ref, vec, buf, send, recv = refs[:n], refs[n], refs[n + 1], *refs[n + 2:]
        x, y, c, _ = _place()
        me = 4 * x + 2 * y + c
        vec[...] = jnp.zeros(vec.shape, F32)
        for r in range(n):
            vec[r:r + 1, :] = jnp.sum(p_refs[r][...], axis=0, keepdims=True)
        vec[n:n + 1, 0:LANES] = l_ref[0:1, :]
        buf[me] = vec[...]
        cps = []
        for dd in range(1, 8):
            bx, by, bc = (dd >> 2) & 1, (dd >> 1) & 1, dd & 1
            peer = (x + bx - 2 * x * bx, y + by - 2 * y * by, c + bc - 2 * c * bc)
            cp = pltpu.make_async_remote_copy(
                src_ref=vec, dst_ref=buf.at[me], send_sem=send.at[dd - 1], recv_sem=recv.at[dd - 1],
                device_id=peer, device_id_type=MESH)
            cp.start()
            cps.append(cp)
        for cp in cps:
            cp.wait()
        tot = buf[0]
        for s in range(1, 8):
            tot = tot + buf[s]
        o_ref[...] = tot

    vm = pl.BlockSpec(memory_space=pltpu.VMEM)
    return pl.pallas_call(
        body, name="small_all_reduce",
        in_specs=[vm] * (n + 1), out_specs=vm,
        out_shape=jax.ShapeDtypeStruct((8, D), F32),
        scratch_shapes=[pltpu.VMEM((8, D), F32), pltpu.VMEM((8, 8, D), F32),
                        pltpu.SemaphoreType.DMA((7,)), pltpu.SemaphoreType.DMA((7,))],
    )(*parts, loss_part)


def _small_adam(tot, ws, ms, vs, rows):
    n = len(ws)

    def body(*refs):
        t_ref = refs[0]
        w_refs, m_refs, v_refs = refs[1:1 + n], refs[1 + n:1 + 2 * n], refs[1 + 2 * n:1 + 3 * n]
        outs = refs[1 + 3 * n:]
        for i in range(n):
            r, c0 = rows[i]
            width = w_refs[i].shape[1]
            g = t_ref[r:r + 1, c0:c0 + width]
            delta, mn, vn = _adam_math(w_refs[i][...], g, m_refs[i][...], v_refs[i][...])
            outs[4 * i][...] = g
            outs[4 * i + 1][...] = delta
            outs[4 * i + 2][...] = mn
            outs[4 * i + 3][...] = vn

    vm = pl.BlockSpec(memory_space=pltpu.VMEM)
    out_shape = []
    for w in ws:
        out_shape += [jax.ShapeDtypeStruct(w.shape, F32)] * 4
    return pl.pallas_call(
        body, name="small_adam",
        in_specs=[vm] * (1 + 3 * n), out_specs=[vm] * (4 * n), out_shape=out_shape,
    )(tot, *ws, *ms, *vs)


def kernel(x, positions, norm_mix_g, w_in, norm_out_dil_g, norm_out_sb_g, w_out, norm_ffn_g, w_gate, w_up, w_down, norm_final_g, loss_target, m_norm_mix_g, m_w_in, m_norm_out_dil_g, m_norm_out_sb_g, m_w_out, m_norm_ffn_g, m_w_gate, m_w_up, m_w_down, m_norm_final_g, v_norm_mix_g, v_w_in, v_norm_out_dil_g, v_norm_out_sb_g, v_w_out, v_norm_ffn_g, v_w_gate, v_w_up, v_w_down, v_norm_final_g):
    B, T, D = x.shape
    M = B * T
    n_heads = D // HEAD_DIM
    Hd = n_heads // 2
    Hs = n_heads - Hd
    W = Hd * HEAD_DIM
    N3 = 3 * D
    fs = w_gate.shape[2]
    fp = -(-fs // LANES) * LANES
    assert T % TK_DIL == 0 and W == Hs * HEAD_DIM and fp > fs

    x2 = x.reshape(M, D)
    tgt = loss_target.reshape(M, D)
    tr_ = jnp.transpose
    big = [w_in[0], w_out[0], tr_(w_gate[0]), tr_(w_up[0]), w_down[0]]
    big_m = [m_w_in[0], m_w_out[0], tr_(m_w_gate[0]), tr_(m_w_up[0]), m_w_down[0]]
    big_v = [v_w_in[0], v_w_out[0], tr_(v_w_gate[0]), tr_(v_w_up[0]), v_w_down[0]]
    names = ["w_in", "w_out", "w_gate", "w_up", "w_down"]
    c_arr = jnp.reshape(lax.axis_index("c"), (1,)).astype(jnp.int32)
    j_arr = jnp.reshape(2 * lax.axis_index("x") + lax.axis_index("y"), (1,)).astype(jnp.int32)

    ns_in = w_in.shape[2]
    ks_out = w_out.shape[1]
    geo = [_Weight("col", (D, ns_in), ns_in), _Weight("row", (ks_out, D), ks_out),
           _Weight("row", (fs, D), fp), _Weight("row", (fs, D), fp), _Weight("row", (fs, D), fp)]
    g_in, g_out, g_gate, g_up, g_down = geo
    fulls = [_cast_into_full(big[w], geo[w], j_arr, "cast_" + names[w]) for w in range(5)]
    for w in (2, 3, 4):
        fulls[w] = _zero_pad(fulls[w], geo[w], "zero_pad_" + names[w])

    def chip_sum(w, grad_full, from_sibling):
        return _chip_sum(grad_full, from_sibling, geo[w], c_arr, "chip_sum_" + names[w])

    def final_sum(w, cs, from_chips):
        return _final_sum(cs, from_chips, geo[w], j_arr, "final_sum_" + names[w])

    (Win,) = _all_gather_weights([fulls[0]], [g_in])
    tabs3 = _rope_tables(positions)
    tabs2 = [t.reshape(M, HEAD_DIM) for t in tabs3]
    hn = _rmsnorm_fwd(x2, norm_mix_g, "norm_mix")
    proj, (Wout, WgT) = _matmul(hn, Win, "nn", BF16, "in_proj", rope=(tabs2, 2 * W),
                                jobs=[_gather_job(fulls[1], g_out, "ici"), _gather_job(fulls[2], g_gate, "ici")])
    proj3 = proj.reshape(B, T, N3)
    bias = _dil_bias(T)
    (o_dil3, lse3), (Wout, WgT, WuT) = _dil_fwd(
        proj3, bias, Hd, "dil_fwd",
        jobs=[_gather_job(Wout, g_out, "d2d"), _gather_job(WgT, g_gate, "d2d"), _gather_job(fulls[3], g_up, "ici")])
    (o_sb3, lt3), (WuT, Wd) = _sb_fwd(
        proj3, Hd, Hs, "sb_fwd", jobs=[_gather_job(WuT, g_up, "d2d"), _gather_job(fulls[4], g_down, "ici")])
    o_dil, o_sb = o_dil3.reshape(M, W), o_sb3.reshape(M, W)
    g_heads = jnp.concatenate([norm_out_dil_g, norm_out_sb_g], axis=1)
    o_mix = _headnorm_fwd(o_dil, o_sb, g_heads, "head_norm")
    h1, (Wd,) = _matmul(o_mix, Wout, "nn", F32, "out_proj", res=x2, jobs=[_gather_job(Wd, g_down, "d2d")])
    hn2 = _rmsnorm_fwd(h1, norm_ffn_g, "norm_ffn")
    gate, up, act = _gate_up(hn2, WgT, WuT, "gate_up")
    h2 = _matmul(act, Wd, "nn", F32, "down_proj", res=h1)
    dh2, dh2_b, loss_part, dg_final = _loss_head(h2, tgt, norm_final_g.reshape(1, D), "loss_head")

    dWd = _matmul(act, dh2_b, "tn", BF16, "dw_down")
    d_act, (sib_d,) = _matmul(dh2_b, Wd, "nt", BF16, "d_act", jobs=[_sibling_job(dWd, g_down)])
    cs_d = chip_sum(4, dWd, sib_d)
    d_gate, d_up = _swiglu_bwd(d_act, gate, up, "swiglu_bwd")
    dWg, (chips_d,) = _matmul(d_gate, hn2, "tn", BF16, "dw_gate", jobs=[_chips_job(cs_d, g_down)])
    half_d = final_sum(4, cs_d, chips_d)
    dWu, (sib_g, other_d) = _matmul(d_up, hn2, "tn", BF16, "dw_up",
                                    jobs=[_sibling_job(dWg, g_gate), _final_job(half_d, g_down)])
    cs_g = chip_sum(2, dWg, sib_g)
    d_hn2, (chips_g,) = _matmul(d_gate, WgT, "nn", F32, "d_hn2_gate", jobs=[_chips_job(cs_g, g_gate)])
    half_g = final_sum(2, cs_g, chips_g)
    d_hn2, (sib_u, other_g) = _matmul(d_up, WuT, "nn", F32, "d_hn2_up", res=d_hn2,
                                      jobs=[_sibling_job(dWu, g_up), _final_job(half_g, g_gate)])
    cs_u = chip_sum(3, dWu, sib_u)
    dh1, dh1_b, dg_ffn = _rmsnorm_bwd(h1, d_hn2, norm_ffn_g, dh2, "norm_ffn_bwd", True)
    dWo = _matmul(o_mix, dh1_b, "tn", BF16, "dw_out")
    d_mix, (sib_o,) = _matmul(dh1_b, Wout, "nt", BF16, "d_mix", jobs=[_sibling_job(dWo, g_out)])
    cs_o = chip_sum(1, dWo, sib_o)
    d_o, dg_heads = _headnorm_bwd(d_mix, o_dil, o_sb, g_heads, "head_norm_bwd")
    d_o3 = d_o.reshape(B, T, D)
    dqkv_dil, (chips_u,) = _dil_bwd(proj3, bias, tabs3, d_o3, o_dil3, lse3, Hd, "dil_bwd",
                                    jobs=[_chips_job(cs_u, g_up)])
    half_u = final_sum(3, cs_u, chips_u)
    dqkv_sb, (chips_o, other_u) = _sb_bwd(proj3, d_o3, lt3, Hd, Hs, "sb_bwd",
                                          jobs=[_chips_job(cs_o, g_out), _final_job(half_u, g_up)])
    half_o = final_sum(1, cs_o, chips_o)
    dproj = jnp.concatenate([*dqkv_dil, *dqkv_sb], axis=-1).reshape(M, N3)
    dWin, (other_o,) = _matmul(hn, dproj, "tn", BF16, "dw_in", jobs=[_final_job(half_o, g_out)])
    d_hn, (sib_i,) = _matmul(dproj, Win, "nt", F32, "d_hn", jobs=[_sibling_job(dWin, g_in)])
    cs_i = chip_sum(0, dWin, sib_i)
    dx, dg_mix = _rmsnorm_bwd(x2, d_hn, norm_mix_g, dh1, "norm_mix_bwd", False)

    ffn_out, (chips_i,) = _adam([big[w] for w in (2, 3, 4)], [half_g, half_u, half_d], [other_g, other_u, other_d],
                                [big_m[w] for w in (2, 3, 4)], [big_v[w] for w in (2, 3, 4)], g_gate, "adam_ffn",
                                jobs=[_chips_job(cs_i, g_in)])
    half_i = final_sum(0, cs_i, chips_i)
    (other_i,) = _final_exchange([half_i], [g_in])
    (in_out,), _ = _adam([big[0]], [half_i], [other_i], [big_m[0]], [big_v[0]], g_in, "adam_w_in")
    (out_out,), _ = _adam([big[1]], [half_o], [other_o], [big_m[1]], [big_v[1]], g_out, "adam_w_out")
    big_out = [in_out, out_out, [tr_(a) for a in ffn_out[0]], [tr_(a) for a in ffn_out[1]], ffn_out[2]]

    tot = _small_all_reduce([dg_mix, dg_heads, dg_ffn, dg_final], loss_part, D)
    loss = tot[4, 0]
    small_w = [norm_mix_g, norm_out_dil_g, norm_out_sb_g, norm_ffn_g, norm_final_g.reshape(1, D)]
    small_m = [m_norm_mix_g, m_norm_out_dil_g, m_norm_out_sb_g, m_norm_ffn_g, m_norm_final_g.reshape(1, D)]
    small_v = [v_norm_mix_g, v_norm_out_dil_g, v_norm_out_sb_g, v_norm_ffn_g, v_norm_final_g.reshape(1, D)]
    so = _small_adam(tot, small_w, small_m, small_v, [(0, 0), (1, 0), (1, W), (2, 0), (3, 0)])
    small_out = [so[4 * i:4 * i + 4] for i in range(5)]
    small_out[4] = [a.reshape(D) for a in small_out[4]]

    per_weight = [small_out[0], big_out[0], small_out[1], small_out[2], big_out[1], small_out[3],
                  big_out[2], big_out[3], big_out[4], small_out[4]]

    def field(i):
        res = []
        for n_, o in enumerate(per_weight):
            a = o[i]
            res.append(a[None] if n_ in (1, 4, 6, 7, 8) else a)
        return res

    return (loss, dx.reshape(B, T, D), *field(0), *field(1), *field(2), *field(3))
```

```python
import functools
import math

import jax
import jax.numpy as jnp
from jax import lax
from jax.experimental import pallas as pl
from jax.experimental.pallas import tpu as pltpu

F32 = jnp.float32
BF16 = jnp.bfloat16
MESH = pl.DeviceIdType.MESH
ANY = pl.BlockSpec(memory_space=pl.ANY)

LANES = 128
HEAD_DIM = 128
DIL_STEPS = 128
DILATIONS = (1, 4, 16)
TQ = 512
TK_DIL = 512
TK_SB = 256
NEG = -1e30
_PART = TQ
_ROW_PARTS = tuple(slice(p, p + _PART) for p in range(0, TQ, _PART))
ROPE_DIM = 32
ROPE_THETA = 500000.0
RMS_EPS = 1e-5
ADAM_LR, ADAM_B1, ADAM_B2, ADAM_EPS, ADAM_WD, ADAM_STEP = 0.001, 0.9, 0.999, 1e-08, 0.01, 10
N_CHIPS = 4
VMEM_LIMIT = 48 * 1024 * 1024

NN = ((1,), (0,))
NT = ((1,), (1,))
TN = ((0,), (0,))


def _dot(a, b, dims):
    return lax.dot_general(a, b, (dims, ((), ())), preferred_element_type=F32)


def _tile(dim, pref, unit=LANES):
    if dim <= pref:
        return dim
    t = (pref // unit) * unit
    while t > unit and dim % t:
        t -= unit
    assert dim % t == 0, (dim, pref, unit)
    return t


def _params(sem=None):
    return pltpu.CompilerParams(dimension_semantics=sem, vmem_limit_bytes=VMEM_LIMIT)


class _Job:
    def __init__(self, inputs, outputs, aliases, nsem, copies):
        self.inputs, self.outputs, self.aliases, self.nsem, self.copies = inputs, outputs, aliases, nsem, copies

    def start(self, ins, outs, send, recv):
        for cp in self.copies(ins, outs, send, recv)[0]:
            cp.start()

    def finish(self, ins, outs, send, recv):
        started, landing = self.copies(ins, outs, send, recv)
        for make in landing:
            make().wait_recv()
        for cp in started:
            cp.wait_send()


def _hosted_call(body, name, grid, in_specs, out_specs, out_shape, scratch_shapes, args, jobs, semantics):
    nbi, nbo, nbs = len(in_specs), len(out_specs), len(scratch_shapes)
    if not jobs:
        res = pl.pallas_call(
            body, name=name, grid=grid, in_specs=list(in_specs), out_specs=list(out_specs), out_shape=list(out_shape),
            scratch_shapes=list(scratch_shapes), compiler_params=_params(semantics))(*args)
        return list(res), []
    j_in = [a for jb in jobs for a in jb.inputs]
    j_out = [o for jb in jobs for o in jb.outputs]
    aliases, ii, oo, sems = {}, nbi, nbo, []
    for jb in jobs:
        for a, b in jb.aliases.items():
            aliases[ii + a] = oo + b
        ii += len(jb.inputs)
        oo += len(jb.outputs)
        sems += [pltpu.SemaphoreType.DMA((jb.nsem,)), pltpu.SemaphoreType.DMA((jb.nsem,))]

    def wrapped(*refs):
        p = nbi + len(j_in)
        b_in, ji = refs[:nbi], refs[nbi:p]
        b_out, jo = refs[p:p + nbo], refs[p + nbo:p + nbo + len(j_out)]
        p += nbo + len(j_out)
        b_scr, js = refs[p:p + nbs], refs[p + nbs:]
        pids = [pl.program_id(k) for k in range(len(grid))]
        first = functools.reduce(jnp.logical_and, [pid == 0 for pid in pids])
        last = functools.reduce(jnp.logical_and, [pid == g - 1 for pid, g in zip(pids, grid)])

        def each(what):
            a = b = 0
            for n, jb in enumerate(jobs):
                getattr(jb, what)(ji[a:a + len(jb.inputs)], jo[b:b + len(jb.outputs)], js[2 * n], js[2 * n + 1])
                a += len(jb.inputs)
                b += len(jb.outputs)

        @pl.when(first)
        def _():
            each("start")

        body(*b_in, *b_out, *b_scr)

        @pl.when(last)
        def _():
            each("finish")

    res = pl.pallas_call(
        wrapped, name=name, grid=grid,
        in_specs=list(in_specs) + [ANY] * len(j_in), out_specs=list(out_specs) + [ANY] * len(j_out),
        out_shape=list(out_shape) + j_out, input_output_aliases=aliases,
        scratch_shapes=list(scratch_shapes) + sems,
        compiler_params=_params(("arbitrary",) * len(grid)))(*args, *j_in)
    return list(res[:nbo]), list(res[nbo:])


def _rope(a, c, s1, s2):
    half = ROPE_DIM // 2
    return a * c + pltpu.roll(a, HEAD_DIM - half, 1) * s1 + pltpu.roll(a, half, 1) * s2


def _matmul(a, b, mode, out_dtype, name, res=None, rope=None, jobs=(), tm=1024, tn=1024, tk=2048):
    if mode == "nn":
        (M, K), (_, N) = a.shape, b.shape
    elif mode == "nt":
        (M, K), (N, _) = a.shape, b.shape
    else:
        (K, M), (_, N) = a.shape, b.shape
    tm, tk = _tile(M, tm), _tile(K, tk)
    tn = _tile(N, tn) if rope is None else _tile(math.gcd(N, rope[1]), tn)
    nk = K // tk
    dims = {"nn": NN, "nt": NT, "tn": TN}[mode]
    a_spec = (pl.BlockSpec((tk, tm), lambda i, j, k: (k, i)) if mode == "tn"
              else pl.BlockSpec((tm, tk), lambda i, j, k: (i, k)))
    b_spec = (pl.BlockSpec((tn, tk), lambda i, j, k: (j, k)) if mode == "nt"
              else pl.BlockSpec((tk, tn), lambda i, j, k: (k, j)))
    o_spec = pl.BlockSpec((tm, tn), lambda i, j, k: (i, j))
    n_extra = (1 if res is not None else 0) + (3 if rope is not None else 0)
    if rope is not None:
        assert rope[1] % tn == 0

    def body(*refs):
        a_ref, b_ref = refs[:2]
        extra = refs[2:2 + n_extra]
        o_ref, acc_ref = refs[2 + n_extra:]
        k = pl.program_id(2)

        @pl.when(k == 0)
        def _():
            acc_ref[...] = jnp.zeros(acc_ref.shape, F32)

        acc_ref[...] += _dot(a_ref[...], b_ref[...], dims)

        if rope is None:
            @pl.when(k == nk - 1)
            def _():
                out = acc_ref[...]
                if res is not None:
                    out = out + extra[0][...]
                o_ref[...] = out.astype(out_dtype)
        else:
            roped = pl.program_id(1) * tn < rope[1]

            @pl.when(jnp.logical_and(k == nk - 1, roped))
            def _():
                c, s1, s2 = extra[0][...], extra[1][...], extra[2][...]
                for h in range(tn // HEAD_DIM):
                    cols = slice(h * HEAD_DIM, (h + 1) * HEAD_DIM)
                    o_ref[:, cols] = _rope(acc_ref[:, cols], c, s1, s2).astype(out_dtype)

            @pl.when(jnp.logical_and(k == nk - 1, jnp.logical_not(roped)))
            def _():
                o_ref[...] = acc_ref[...].astype(out_dtype)

    in_specs, args = [a_spec, b_spec], [a, b]
    if res is not None:
        in_specs.append(o_spec)
        args.append(res)
    if rope is not None:
        in_specs += [pl.BlockSpec((tm, HEAD_DIM), lambda i, j, k: (i, 0))] * 3
        args += list(rope[0])
    (out,), extra_out = _hosted_call(
        body, name, (M // tm, N // tn, nk), in_specs, [o_spec], [jax.ShapeDtypeStruct((M, N), out_dtype)],
        [pltpu.VMEM((tm, tn), F32)], args, jobs, ("parallel", "parallel", "arbitrary"))
    return (out, extra_out) if jobs else out


def _gate_up(hn, wg, wu, name):
    M, K = hn.shape
    N = wg.shape[0]
    tm, tn, tk = _tile(M, 1024), _tile(N, 512), _tile(K, 2048)
    nk = K // tk

    def body(a_ref, g_ref, u_ref, og_ref, ou_ref, oa_ref, accg, accu):
        k = pl.program_id(2)

        @pl.when(k == 0)
        def _():
            accg[...] = jnp.zeros(accg.shape, F32)
            accu[...] = jnp.zeros(accu.shape, F32)

        a = a_ref[...]
        accg[...] += _dot(a, g_ref[...], NT)
        accu[...] += _dot(a, u_ref[...], NT)

        @pl.when(k == nk - 1)
        def _():
            g = accg[...]
            u = accu[...]
            og_ref[...] = g.astype(BF16)
            ou_ref[...] = u.astype(BF16)
            oa_ref[...] = (g * jax.nn.sigmoid(g) * u).astype(BF16)

    w_spec = pl.BlockSpec((tn, tk), lambda i, j, k: (j, k))
    o_spec = pl.BlockSpec((tm, tn), lambda i, j, k: (i, j))
    sds = jax.ShapeDtypeStruct((M, N), BF16)
    return pl.pallas_call(
        body, name=name, grid=(M // tm, N // tn, nk),
        in_specs=[pl.BlockSpec((tm, tk), lambda i, j, k: (i, k)), w_spec, w_spec],
        out_specs=[o_spec, o_spec, o_spec], out_shape=[sds, sds, sds],
        scratch_shapes=[pltpu.VMEM((tm, tn), F32), pltpu.VMEM((tm, tn), F32)],
        compiler_params=_params(("parallel", "parallel", "arbitrary")),
    )(hn, wg, wu)


def _swiglu_bwd(d_act, gate, up, name):
    M, N = d_act.shape
    tm, tn = _tile(M, 512, 16), _tile(N, 1024)

    def body(d_ref, g_ref, u_ref, dg_ref, du_ref):
        d = d_ref[...].astype(F32)
        g = g_ref[...].astype(F32)
        u = u_ref[...].astype(F32)
        sig = jax.nn.sigmoid(g)
        du_ref[...] = (d * g * sig).astype(BF16)
        dg_ref[...] = (d * u * sig * (1.0 + g * (1.0 - sig))).astype(BF16)

    spec = pl.BlockSpec((tm, tn), lambda i, j: (i, j))
    sds = jax.ShapeDtypeStruct((M, N), BF16)
    return pl.pallas_call(
        body, name=name, grid=(M // tm, N // tn),
        in_specs=[spec, spec, spec], out_specs=[spec, spec], out_shape=[sds, sds],
        compiler_params=_params(("parallel", "parallel")),
    )(d_act, gate, up)


def _rmsnorm_fwd(x, g, name):
    M, D = x.shape
    tm = _tile(M, 256, 16)

    def body(x_ref, g_ref, o_ref):
        xv = x_ref[...]
        r = lax.rsqrt(jnp.mean(xv * xv, axis=-1, keepdims=True) + RMS_EPS)
        o_ref[...] = (xv * r * g_ref[...]).astype(BF16)

    return pl.pallas_call(
        body, name=name, grid=(M // tm,),
        in_specs=[pl.BlockSpec((tm, D), lambda i: (i, 0)), pl.BlockSpec((1, D), lambda i: (0, 0))],
        out_specs=pl.BlockSpec((tm, D), lambda i: (i, 0)),
        out_shape=jax.ShapeDtypeStruct((M, D), BF16),
        compiler_params=_params(("parallel",)),
    )(x, g)


def _fold8(v):
    tm, D = v.shape
    return jnp.sum(v.reshape(tm // 8, 8, D), axis=0)


def _rmsnorm_bwd(x, dy, g, res, name, want_bf16):
    M, D = x.shape
    tm = _tile(M, 128, 16)

    def body(x_ref, dy_ref, g_ref, r_ref, *outs):
        dx_ref, dg_ref = outs[0], outs[-1]
        xv = x_ref[...]
        dyv = dy_ref[...].astype(F32)
        r = lax.rsqrt(jnp.mean(xv * xv, axis=-1, keepdims=True) + RMS_EPS)
        u = dyv * g_ref[...]
        dot = jnp.sum(xv * u, axis=-1, keepdims=True)
        dx = r * u - xv * (r * r * r * (1.0 / D)) * dot + r_ref[...]
        dx_ref[...] = dx
        if want_bf16:
            outs[1][...] = dx.astype(BF16)
        part = _fold8(dyv * xv * r)

        @pl.when(pl.program_id(0) == 0)
        def _():
            dg_ref[...] = part

        @pl.when(pl.program_id(0) > 0)
        def _():
            dg_ref[...] += part

    row = pl.BlockSpec((tm, D), lambda i: (i, 0))
    out_specs = [row] + ([row] if want_bf16 else []) + [pl.BlockSpec((8, D), lambda i: (0, 0))]
    out_shape = ([jax.ShapeDtypeStruct((M, D), F32)] + ([jax.ShapeDtypeStruct((M, D), BF16)] if want_bf16 else [])
                 + [jax.ShapeDtypeStruct((8, D), F32)])
    return pl.pallas_call(
        body, name=name, grid=(M // tm,),
        in_specs=[row, row, pl.BlockSpec((1, D), lambda i: (0, 0)), row],
        out_specs=out_specs, out_shape=out_shape,
        compiler_params=_params(("arbitrary",)),
    )(x, dy, g, res)


def _loss_head(h, tgt, g, name):
    M, D = h.shape
    tm = _tile(M, 128, 16)

    def body(h_ref, t_ref, g_ref, dh_ref, dhb_ref, l_ref, dg_ref):
        hv = h_ref[...]
        gv = g_ref[...]
        r = lax.rsqrt(jnp.mean(hv * hv, axis=-1, keepdims=True) + RMS_EPS)
        n = hv * r
        e = n * gv - t_ref[...]
        dy = e * (1.0 / D)
        u = dy * gv
        dot = jnp.sum(hv * u, axis=-1, keepdims=True)
        dh = r * u - hv * (r * r * r * (1.0 / D)) * dot
        dh_ref[...] = dh
        dhb_ref[...] = dh.astype(BF16)
        rows = jnp.sum(e * e, axis=-1, keepdims=True)
        lpart = jnp.broadcast_to(jnp.sum(rows, axis=0, keepdims=True) * (0.5 / D), (8, LANES))
        gpart = _fold8(dy * n)

        @pl.when(pl.program_id(0) == 0)
        def _():
            l_ref[...] = lpart
            dg_ref[...] = gpart

        @pl.when(pl.program_id(0) > 0)
        def _():
            l_ref[...] += lpart
            dg_ref[...] += gpart

    row = pl.BlockSpec((tm, D), lambda i: (i, 0))
    return pl.pallas_call(
        body, name=name, grid=(M // tm,),
        in_specs=[row, row, pl.BlockSpec((1, D), lambda i: (0, 0))],
        out_specs=[row, row, pl.BlockSpec((8, LANES), lambda i: (0, 0)), pl.BlockSpec((8, D), lambda i: (0, 0))],
        out_shape=[jax.ShapeDtypeStruct((M, D), F32), jax.ShapeDtypeStruct((M, D), BF16),
                   jax.ShapeDtypeStruct((8, LANES), F32), jax.ShapeDtypeStruct((8, D), F32)],
        compiler_params=_params(("arbitrary",)),
    )(h, tgt, g)


def _headnorm_fwd(o_dil, o_sb, g, name):
    M, W = o_dil.shape
    D = 2 * W
    tm = _tile(M, 256, 16)

    def body(a_ref, b_ref, g_ref, o_ref):
        for h in range(D // HEAD_DIM):
            src = a_ref if h < W // HEAD_DIM else b_ref
            lo = (h * HEAD_DIM) % W
            v = src[:, lo:lo + HEAD_DIM]
            r = lax.rsqrt(jnp.mean(v * v, axis=-1, keepdims=True) + RMS_EPS)
            o_ref[:, h * HEAD_DIM:(h + 1) * HEAD_DIM] = (v * r * g_ref[:, h * HEAD_DIM:(h + 1) * HEAD_DIM]).astype(BF16)

    half = pl.BlockSpec((tm, W), lambda i: (i, 0))
    return pl.pallas_call(
        body, name=name, grid=(M // tm,),
        in_specs=[half, half, pl.BlockSpec((1, D), lambda i: (0, 0))],
        out_specs=pl.BlockSpec((tm, D), lambda i: (i, 0)),
        out_shape=jax.ShapeDtypeStruct((M, D), BF16),
        compiler_params=_params(("parallel",)),
    )(o_dil, o_sb, g)


def _headnorm_bwd(d_mix, o_dil, o_sb, g, name):
    M, W = o_dil.shape
    D = 2 * W
    tm = _tile(M, 128, 16)

    def body(d_ref, a_ref, b_ref, g_ref, do_ref, dg_ref):
        @pl.when(pl.program_id(0) == 0)
        def _():
            dg_ref[...] = jnp.zeros(dg_ref.shape, F32)

        for h in range(D // HEAD_DIM):
            src = a_ref if h < W // HEAD_DIM else b_ref
            lo = (h * HEAD_DIM) % W
            cols = slice(h * HEAD_DIM, (h + 1) * HEAD_DIM)
            v = src[:, lo:lo + HEAD_DIM]
            dy = d_ref[:, cols].astype(F32)
            r = lax.rsqrt(jnp.mean(v * v, axis=-1, keepdims=True) + RMS_EPS)
            u = dy * g_ref[:, cols]
            dot = jnp.sum(v * u, axis=-1, keepdims=True)
            do_ref[:, cols] = r * u - v * (r * r * r * (1.0 / HEAD_DIM)) * dot
            dg_ref[:, cols] += _fold8(dy * v * r)

    half = pl.BlockSpec((tm, W), lambda i: (i, 0))
    row = pl.BlockSpec((tm, D), lambda i: (i, 0))
    return pl.pallas_call(
        body, name=name, grid=(M // tm,),
        in_specs=[row, half, half, pl.BlockSpec((1, D), lambda i: (0, 0))],
        out_specs=[row, pl.BlockSpec((8, D), lambda i: (0, 0))],
        out_shape=[jax.ShapeDtypeStruct((M, D), F32), jax.ShapeDtypeStruct((8, D), F32)],
        compiler_params=_params(("arbitrary",)),
    )(d_mix, o_dil, o_sb, g)


def _rope_tables(positions):
    half = ROPE_DIM // 2
    inv_freq = jnp.power(jnp.float32(ROPE_THETA), -jnp.arange(half, dtype=F32) / half)
    ang = positions.astype(F32)[..., None] * inv_freq
    cos, sin = jnp.cos(ang), jnp.sin(ang)
    rest = HEAD_DIM - ROPE_DIM
    one = jnp.ones(cos.shape[:-1] + (rest,), F32)
    z16 = jnp.zeros_like(sin)
    zr = jnp.zeros_like(one)
    c = jnp.concatenate([cos, cos, one], axis=-1)
    s1 = jnp.concatenate([-sin, z16, zr], axis=-1)
    s2 = jnp.concatenate([z16, sin, zr], axis=-1)
    return c, s1, s2


def _dil_bias(T):
    ne = T // TQ
    e = jnp.arange(ne, dtype=jnp.int32)[:, None, None]
    r = jnp.arange(TQ, dtype=jnp.int32)[None, :, None]
    c = jnp.arange(TK_DIL, dtype=jnp.int32)[None, None, :]
    dist = e * TQ + r - c
    mult = jnp.zeros(dist.shape, F32)
    for d in DILATIONS:
        mult = mult + jnp.where((dist % d == 0) & (dist <= DIL_STEPS * d), 1.0, 0.0)
    return jnp.where((dist >= 0) & (mult > 0), jnp.log(jnp.maximum(mult, 1.0)), NEG)


def _dil_fwd(proj3, bias, Hd, name, jobs=()):
    B, T, N3 = proj3.shape
    nq = T // TQ
    per = TK_DIL // TQ
    scale = HEAD_DIM ** -0.5

    def body(q_ref, k_ref, v_ref, b_ref, o_ref, l_ref):
        i = pl.program_id(2)
        qs = [q_ref[r, :] for r in _ROW_PARTS]
        last = i // per

        def step(t, carry):
            j = last - t
            off = pl.multiple_of(j * TK_DIL, TK_DIL)
            kj = k_ref[pl.ds(off, TK_DIL), :]
            vj = v_ref[pl.ds(off, TK_DIL), :]
            new = []
            for n, r in enumerate(_ROW_PARTS):
                m, l, acc = carry[n]
                s = _dot(qs[n], kj, NT) * scale + b_ref[i - per * j, r, :]
                m_new = jnp.maximum(m, jnp.max(s, axis=-1, keepdims=True))
                p = jnp.exp(s - m_new)
                corr = jnp.exp(m - m_new)
                new.append((m_new, l * corr + jnp.sum(p, axis=-1, keepdims=True),
                            acc * corr + _dot(p.astype(BF16), vj, NN)))
            return tuple(new)

        init = (jnp.full((_PART, 1), NEG, F32), jnp.zeros((_PART, 1), F32), jnp.zeros((_PART, HEAD_DIM), F32))
        done = lax.fori_loop(0, last + 1, step, (init,) * len(_ROW_PARTS))
        for n, r in enumerate(_ROW_PARTS):
            m, l, acc = done[n]
            o_ref[r, :] = acc / l
            l_ref[r, :] = jnp.broadcast_to(m + jnp.log(l), (_PART, HEAD_DIM))

    full = lambda base: pl.BlockSpec((None, T, HEAD_DIM), lambda b, h, i: (b, 0, base + h))
    blk = lambda base: pl.BlockSpec((None, TQ, HEAD_DIM), lambda b, h, i: (b, i, base + h))
    sds = jax.ShapeDtypeStruct((B, T, Hd * HEAD_DIM), F32)
    return _hosted_call(
        body, name, (B, Hd, nq),
        [blk(0), full(Hd), full(2 * Hd), pl.BlockSpec(bias.shape, lambda b, h, i: (0, 0, 0))],
        [blk(0), blk(0)], [sds, sds], [], (proj3, proj3, proj3, bias), jobs, ("parallel", "parallel", "arbitrary"))


def _dil_bwd(proj3, bias, tabs, d_o3, o3, lse3, Hd, name, jobs=()):
    B, T, N3 = proj3.shape
    nq = T // TQ
    per = TK_DIL // TQ
    scale = HEAD_DIM ** -0.5

    def body(q_ref, k_ref, v_ref, b_ref, do_ref, o_ref, l_ref, cq, s1q, s2q, ct, s1t, s2t,
             dq_ref, dk_ref, dv_ref, dks, dvs):
        i = pl.program_id(2)

        @pl.when(i == 0)
        def _():
            dks[...] = jnp.zeros(dks.shape, F32)
            dvs[...] = jnp.zeros(dvs.shape, F32)

        q = q_ref[...]
        do = do_ref[...]
        dob = do.astype(BF16)
        dterm = jnp.sum(do * o_ref[...], axis=-1, keepdims=True)
        lse = jnp.concatenate([l_ref[...]] * (TK_DIL // HEAD_DIM), axis=1)
        last = i // per

        def step(j, dqs):
            off = pl.multiple_of(j * TK_DIL, TK_DIL)
            kj = k_ref[pl.ds(off, TK_DIL), :]
            vj = v_ref[pl.ds(off, TK_DIL), :]
            ps, dss, new = [], [], []
            for n, r in enumerate(_ROW_PARTS):
                s = _dot(q[r], kj, NT) * scale + b_ref[i - per * j, r, :]
                p = jnp.exp(s - lse[r])
                dp = _dot(dob[r], vj, NT)
                ds = (p * (dp - dterm[r]) * scale).astype(BF16)
                new.append(dqs[n] + _dot(ds, kj, NN))
                ps.append(p.astype(BF16))
                dss.append(ds)
            dks[pl.ds(off, TK_DIL), :] += _dot(jnp.concatenate(dss, axis=0), q, TN)
            dvs[pl.ds(off, TK_DIL), :] += _dot(jnp.concatenate(ps, axis=0), dob, TN)
            return tuple(new)

        zero = jnp.zeros((_PART, HEAD_DIM), F32)
        dq = jnp.concatenate(lax.fori_loop(0, last + 1, step, (zero,) * len(_ROW_PARTS)), axis=0)
        dq_ref[...] = _rope(dq, cq[...], -s1q[...], -s2q[...]).astype(BF16)

        @pl.when(i == nq - 1)
        def _():
            dk_ref[...] = _rope(dks[...], ct[...], -s1t[...], -s2t[...]).astype(BF16)
            dv_ref[...] = dvs[...].astype(BF16)

    full = lambda base: pl.BlockSpec((None, T, HEAD_DIM), lambda b, h, i: (b, 0, base + h))
    blk = lambda base: pl.BlockSpec((None, TQ, HEAD_DIM), lambda b, h, i: (b, i, base + h))
    tab_q = pl.BlockSpec((None, TQ, HEAD_DIM), lambda b, h, i: (b, i, 0))
    tab_t = pl.BlockSpec((None, T, HEAD_DIM), lambda b, h, i: (b, 0, 0))
    sds = jax.ShapeDtypeStruct((B, T, Hd * HEAD_DIM), BF16)
    return _hosted_call(
        body, name, (B, Hd, nq),
        [blk(0), full(Hd), full(2 * Hd), pl.BlockSpec(bias.shape, lambda b, h, i: (0, 0, 0)),
         blk(0), blk(0), blk(0), tab_q, tab_q, tab_q, tab_t, tab_t, tab_t],
        [blk(0), full(0), full(0)], [sds, sds, sds],
        [pltpu.VMEM((T, HEAD_DIM), F32), pltpu.VMEM((T, HEAD_DIM), F32)],
        (proj3, proj3, proj3, bias, d_o3, o3, lse3, *tabs, *tabs), jobs, ("parallel", "parallel", "arbitrary"))


def _softplus(z):
    return jnp.maximum(z, 0.0) + jnp.log(1.0 + jnp.exp(-jnp.abs(z)))


def _tri_sum(x, m01):
    n = x.shape[0]
    hi = x.astype(BF16)
    lo = (x - hi.astype(F32)).astype(BF16)
    both = _dot(jnp.concatenate([hi, lo], axis=0), m01, NN)
    return both[:n] + both[n:]


def _sb_iota(rows=TQ, first=0):
    row = lax.broadcasted_iota(jnp.int32, (rows, TK_SB), 0) + first
    col = lax.broadcasted_iota(jnp.int32, (rows, TK_SB), 1)
    return row, col


def _sb_fwd(proj3, Hd, Hs, name, jobs=()):
    B, T, N3 = proj3.shape
    nq = T // TQ
    qb, kb_, vb_ = 3 * Hd, 3 * Hd + Hs, 3 * Hd + 2 * Hs
    scale = HEAD_DIM ** -0.5

    def body(q_ref, k_ref, v_ref, o_ref, lt_ref):
        i = pl.program_id(2)
        qs = [q_ref[r, :] for r in _ROW_PARTS]
        row, col = _sb_iota(TK_SB)
        after = jnp.where(row > col, 1.0, 0.0).astype(BF16)
        rows = [_sb_iota(_PART, r.start)[0] for r in _ROW_PARTS]
        colh = _sb_iota(_PART)[1]
        n_chunks = (i + 1) * (TQ // TK_SB)

        def step(t, carry):
            j = n_chunks - 1 - t
            off = pl.multiple_of(j * TK_SB, TK_SB)
            kj = k_ref[pl.ds(off, TK_SB), :]
            vj = v_ref[pl.ds(off, TK_SB), :]
            reach = i * TQ - j * TK_SB
            new = []
            for n in range(len(_ROW_PARTS)):
                acc, cs = carry[n]
                z = _dot(qs[n], kj, NT) * scale
                msk = colh < rows[n] + reach
                sp = _softplus(z)
                ln = jnp.where(msk, -sp, 0.0)
                excl = _tri_sum(ln, after) + cs
                a = jnp.where(msk, jnp.exp(z - sp + excl), 0.0)
                new.append((acc + _dot(a.astype(BF16), vj, NN), cs + jnp.sum(ln, axis=-1, keepdims=True)))
            return tuple(new)

        init = (jnp.zeros((_PART, HEAD_DIM), F32), jnp.zeros((_PART, 1), F32))
        done = lax.fori_loop(0, n_chunks, step, (init,) * len(_ROW_PARTS))
        for n, r in enumerate(_ROW_PARTS):
            o_ref[r, :] = done[n][0]
            lt_ref[r, :] = jnp.broadcast_to(done[n][1], (_PART, HEAD_DIM))

    full = lambda base: pl.BlockSpec((None, T, HEAD_DIM), lambda b, h, i: (b, 0, base + h))
    blk = lambda base: pl.BlockSpec((None, TQ, HEAD_DIM), lambda b, h, i: (b, i, base + h))
    sds = jax.ShapeDtypeStruct((B, T, Hs * HEAD_DIM), F32)
    return _hosted_call(
        body, name, (B, Hs, nq), [blk(qb), full(kb_), full(vb_)], [blk(0), blk(0)], [sds, sds], [],
        (proj3, proj3, proj3), jobs, ("parallel", "parallel", "arbitrary"))


def _sb_bwd(proj3, d_o3, lt3, Hd, Hs, name, jobs=()):
    B, T, N3 = proj3.shape
    nq = T // TQ
    qb, kb_, vb_ = 3 * Hd, 3 * Hd + Hs, 3 * Hd + 2 * Hs
    scale = HEAD_DIM ** -0.5

    def body(q_ref, k_ref, v_ref, do_ref, lt_ref, dq_ref, dk_ref, dv_ref, dks, dvs):
        i = pl.program_id(2)

        @pl.when(i == 0)
        def _():
            dks[...] = jnp.zeros(dks.shape, F32)
            dvs[...] = jnp.zeros(dvs.shape, F32)

        q = q_ref[...]
        dob = do_ref[...].astype(BF16)
        total = jnp.concatenate([lt_ref[...]] * (TK_SB // HEAD_DIM), axis=1)
        row, col = _sb_iota(TK_SB)
        before = jnp.where(row < col, 1.0, 0.0).astype(BF16)
        rows = [_sb_iota(_PART, r.start)[0] for r in _ROW_PARTS]
        colh = _sb_iota(_PART)[1]

        def step(j, carry):
            off = pl.multiple_of(j * TK_SB, TK_SB)
            kj = k_ref[pl.ds(off, TK_SB), :]
            vj = v_ref[pl.ds(off, TK_SB), :]
            reach = i * TQ - j * TK_SB
            a_s, dzs, new = [], [], []
            for n, r in enumerate(_ROW_PARTS):
                dq, pc, gc = carry[n]
                z = _dot(q[r], kj, NT) * scale
                msk = colh < rows[n] + reach
                sp = _softplus(z)
                ln = jnp.where(msk, -sp, 0.0)
                excl = total[r] - (_tri_sum(ln, before) + ln + pc)
                a = jnp.where(msk, jnp.exp(z - sp + excl), 0.0)
                g = a * _dot(dob[r], vj, NT)
                big = _tri_sum(g, before) + gc
                sig = jnp.exp(z - sp)
                dz = (jnp.where(msk, g * (1.0 - sig) - big * sig, 0.0) * scale).astype(BF16)
                new.append((dq + _dot(dz, kj, NN), pc + jnp.sum(ln, axis=-1, keepdims=True),
                            gc + jnp.sum(g, axis=-1, keepdims=True)))
                a_s.append(a.astype(BF16))
                dzs.append(dz)
            dks[pl.ds(off, TK_SB), :] += _dot(jnp.concatenate(dzs, axis=0), q, TN)
            dvs[pl.ds(off, TK_SB), :] += _dot(jnp.concatenate(a_s, axis=0), dob, TN)
            return tuple(new)

        zero1 = jnp.zeros((_PART, 1), F32)
        init = (jnp.zeros((_PART, HEAD_DIM), F32), zero1, zero1)
        done = lax.fori_loop(0, (i + 1) * (TQ // TK_SB), step, (init,) * len(_ROW_PARTS))
        dq_ref[...] = jnp.concatenate([d[0] for d in done], axis=0).astype(BF16)

        @pl.when(i == nq - 1)
        def _():
            dk_ref[...] = dks[...].astype(BF16)
            dv_ref[...] = dvs[...].astype(BF16)

    full = lambda base: pl.BlockSpec((None, T, HEAD_DIM), lambda b, h, i: (b, 0, base + h))
    blk = lambda base: pl.BlockSpec((None, TQ, HEAD_DIM), lambda b, h, i: (b, i, base + h))
    sds = jax.ShapeDtypeStruct((B, T, Hs * HEAD_DIM), BF16)
    return _hosted_call(
        body, name, (B, Hs, nq), [blk(qb), full(kb_), full(vb_), blk(Hd), blk(0)],
        [blk(0), full(0), full(0)], [sds, sds, sds],
        [pltpu.VMEM((T, HEAD_DIM), F32), pltpu.VMEM((T, HEAD_DIM), F32)],
        (proj3, proj3, proj3, d_o3, lt3), jobs, ("parallel", "parallel", "arbitrary"))


def _place():
    x, y, c = lax.axis_index("x"), lax.axis_index("y"), lax.axis_index("c")
    chips = [(1 - x, y), (x, 1 - y), (1 - x, 1 - y)]
    return x, y, c, chips


class _Weight:
    def __init__(self, kind, shard_shape, slot):
        self.kind, self.slot = kind, slot
        self.R, self.C = shard_shape
        if kind == "col":
            assert slot == self.C
            self.full = (self.R, N_CHIPS * slot)
            self.half = (self.R // 2, N_CHIPS * slot)
            self.piece = (self.R // 2, slot)
        else:
            self.full = (N_CHIPS * slot, self.C)
            self.half = (N_CHIPS * slot, self.C // 2)
            self.piece = (self.R, self.C // 2)

    def _rows(self, jj, n):
        return pl.ds(pl.multiple_of(jj * self.slot, 16), n)

    def full_half(self, ref, jj, hc):
        if self.kind == "col":
            return ref.at[pl.ds(hc * (self.R // 2), self.R // 2), pl.ds(pl.multiple_of(jj * self.slot, LANES), self.C)]
        return ref.at[self._rows(jj, self.R), pl.ds(hc * (self.C // 2), self.C // 2)]

    def region_half(self, ref, hc):
        if self.kind == "col":
            return ref.at[pl.ds(hc * (self.R // 2), self.R // 2), :]
        return ref.at[:, pl.ds(hc * (self.C // 2), self.C // 2)]

    def half_piece(self, ref, jj):
        if self.kind == "col":
            return ref.at[:, pl.ds(pl.multiple_of(jj * self.slot, LANES), self.slot)]
        return ref.at[self._rows(jj, self.R), :]


def _cast_into_full(w, geo, j_arr, name):
    R, C = w.shape
    tr = _tile(R, 256, 16)
    if geo.kind == "col":
        o_spec = pl.BlockSpec((tr, C), lambda i, j_ref: (i, j_ref[0]))
    else:
        while geo.slot % tr or R % tr:
            tr -= 16
        spb = geo.slot // tr
        o_spec = pl.BlockSpec((tr, C), lambda i, j_ref: (j_ref[0] * spb + i, 0))

    def body(j_ref, w_ref, o_ref):
        o_ref[...] = w_ref[...].astype(BF16)

    return pl.pallas_call(
        body, name=name,
        grid_spec=pltpu.PrefetchScalarGridSpec(
            num_scalar_prefetch=1, grid=(R // tr,),
            in_specs=[pl.BlockSpec((tr, C), lambda i, j_ref: (i, 0))], out_specs=o_spec),
        out_shape=jax.ShapeDtypeStruct(geo.full, BF16),
        compiler_params=_params(("parallel",)),
    )(j_arr, w)


def _zero_pad(full, geo, name):
    pad = geo.slot - geo.R
    assert geo.kind == "row" and pad > 0 and geo.R % pad == 0 and geo.slot % pad == 0

    def body(f_ref, o_ref):
        o_ref[...] = jnp.zeros(o_ref.shape, BF16)

    return pl.pallas_call(
        body, name=name, grid=(N_CHIPS,), in_specs=[ANY],
        out_specs=pl.BlockSpec((pad, geo.C), lambda jj: ((jj * geo.slot + geo.R) // pad, 0)),
        out_shape=jax.ShapeDtypeStruct(geo.full, BF16), input_output_aliases={0: 0},
        compiler_params=_params(("arbitrary",)),
    )(full)


def _gather_job(full, geo, stage):
    def copies(ins, outs, send, recv):
        x, y, c, chips = _place()
        j = 2 * x + y

        def cp(src, dst, k, to):
            return pltpu.make_async_remote_copy(src_ref=src, dst_ref=dst, send_sem=send.at[k], recv_sem=recv.at[k],
                                                device_id=to, device_id_type=MESH)

        started, landing = [], []
        for k, (cx, cy) in enumerate(chips):
            jk = 2 * cx + cy
            if stage == "ici":
                started.append(cp(geo.full_half(ins[0], j, c), geo.full_half(outs[0], j, c), k, (cx, cy, c)))
                part = geo.full_half(outs[0], jk, c)
            else:
                started.append(cp(geo.full_half(ins[0], jk, c), geo.full_half(outs[0], jk, c), k, (x, y, 1 - c)))
                part = geo.full_half(outs[0], jk, 1 - c)
            landing.append(functools.partial(cp, part, part, k, (x, y, c)))
        return started, landing

    return _Job([full], [jax.ShapeDtypeStruct(geo.full, BF16)], {0: 0}, 3, copies)


def _sibling_job(grad_full, geo):
    def copies(ins, outs, send, recv):
        x, y, c, _ = _place()
        mk = lambda src, to: pltpu.make_async_remote_copy(
            src_ref=src, dst_ref=outs[0], send_sem=send.at[0], recv_sem=recv.at[0], device_id=to, device_id_type=MESH)
        return [mk(geo.region_half(ins[0], 1 - c), (x, y, 1 - c))], [functools.partial(mk, outs[0], (x, y, c))]

    return _Job([grad_full], [jax.ShapeDtypeStruct(geo.half, BF16)], {}, 1, copies)


def _chips_job(chip_sum, geo):
    def copies(ins, outs, send, recv):
        x, y, c, chips = _place()
        started, landing = [], []
        def mk(k, src, to):
            return pltpu.make_async_remote_copy(src_ref=src, dst_ref=outs[0].at[k], send_sem=send.at[k],
                                                recv_sem=recv.at[k], device_id=to, device_id_type=MESH)

        for k, (cx, cy) in enumerate(chips):
            started.append(mk(k, geo.half_piece(ins[0], 2 * cx + cy), (cx, cy, c)))
            landing.append(functools.partial(mk, k, outs[0].at[k], (x, y, c)))
        return started, landing

    return _Job([chip_sum], [jax.ShapeDtypeStruct((3,) + geo.piece, BF16)], {}, 3, copies)


def _final_job(half, geo):
    def copies(ins, outs, send, recv):
        x, y, c, _ = _place()
        mk = lambda src, to: pltpu.make_async_remote_copy(
            src_ref=src, dst_ref=outs[0], send_sem=send.at[0], recv_sem=recv.at[0], device_id=to, device_id_type=MESH)
        return [mk(ins[0], (x, y, 1 - c))], [functools.partial(mk, outs[0], (x, y, c))]

    return _Job([half], [jax.ShapeDtypeStruct(geo.piece, F32)], {}, 1, copies)


def _all_gather_weights(fulls, geo):
    nw = len(fulls)

    def body(*refs):
        own_refs, f_refs = refs[:nw], refs[nw:2 * nw]
        send, recv = refs[2 * nw:]
        x, y, c, chips = _place()
        j = 2 * x + y

        def remote(w, k, jj, hc, to, own=False):
            part = geo[w].full_half(f_refs[w], jj, hc)
            return pltpu.make_async_remote_copy(
                src_ref=geo[w].full_half(own_refs[w], jj, hc) if own else part, dst_ref=part,
                send_sem=send.at[w, k], recv_sem=recv.at[w, k], device_id=to, device_id_type=MESH)

        sends = []
        for w in range(nw):
            for k, chip in enumerate(chips):
                cp = remote(w, k, j, c, (*chip, c), own=True)
                cp.start()
                sends.append(cp)
        for w in range(nw):
            for k, (cx, cy) in enumerate(chips):
                remote(w, k, 2 * cx + cy, c, (x, y, c)).wait_recv()
                cp = remote(w, 3 + k, 2 * cx + cy, c, (x, y, 1 - c))
                cp.start()
                sends.append(cp)
        for w in range(nw):
            for k, (cx, cy) in enumerate(chips):
                remote(w, 3 + k, 2 * cx + cy, 1 - c, (x, y, c)).wait_recv()
        for cp in sends:
            cp.wait_send()

    return pl.pallas_call(
        body, name="all_gather_weights",
        in_specs=[ANY] * nw, out_specs=[ANY] * nw,
        out_shape=[jax.ShapeDtypeStruct(g.full, BF16) for g in geo],
        input_output_aliases={w: w for w in range(nw)},
        scratch_shapes=[pltpu.SemaphoreType.DMA((nw, 6)), pltpu.SemaphoreType.DMA((nw, 6))],
    )(*fulls)


def _chip_sum(g_full, got, geo, c_arr, name):
    Rh, Ch = geo.half
    tr, tc = _tile(Rh, 256, 16), _tile(Ch, 2048)
    nrb, ncb = Rh // tr, Ch // tc

    def body(c_ref, a_ref, b_ref, o_ref):
        o_ref[...] = (a_ref[...].astype(F32) + b_ref[...].astype(F32)).astype(BF16)

    if geo.kind == "col":
        a_spec = pl.BlockSpec((tr, tc), lambda i, j, c_ref: (c_ref[0] * nrb + i, j))
    else:
        a_spec = pl.BlockSpec((tr, tc), lambda i, j, c_ref: (i, c_ref[0] * ncb + j))
    spec = pl.BlockSpec((tr, tc), lambda i, j, c_ref: (i, j))
    return pl.pallas_call(
        body, name=name,
        grid_spec=pltpu.PrefetchScalarGridSpec(num_scalar_prefetch=1, grid=(nrb, ncb),
                                               in_specs=[a_spec, spec], out_specs=spec),
        out_shape=jax.ShapeDtypeStruct(geo.half, BF16),
        compiler_params=_params(("parallel", "parallel")),
    )(c_arr, g_full, got)


def _final_sum(chip_sum, got, geo, j_arr, name):
    Rp, Cp = geo.piece
    if geo.kind == "col":
        tr = _tile(Rp, 128, 16)
        a_spec = pl.BlockSpec((tr, Cp), lambda i, j_ref: (i, j_ref[0]))
    else:
        tr = _tile(Rp, 128, 16)
        while geo.slot % tr:
            tr -= 16
        assert Rp % tr == 0 and geo.slot % tr == 0
        spb = geo.slot // tr
        a_spec = pl.BlockSpec((tr, Cp), lambda i, j_ref: (j_ref[0] * spb + i, 0))
    r_specs = [pl.BlockSpec((None, tr, Cp), functools.partial(lambda i, j_ref, k: (k, i, 0), k=k)) for k in range(3)]
    o_spec = pl.BlockSpec((tr, Cp), lambda i, j_ref: (i, 0))

    def body(j_ref, a_ref, r0, r1, r2, o_ref):
        o_ref[...] = ((a_ref[...].astype(F32) + r0[...].astype(F32)) + r1[...].astype(F32)) + r2[...].astype(F32)

    return pl.pallas_call(
        body, name=name,
        grid_spec=pltpu.PrefetchScalarGridSpec(num_scalar_prefetch=1, grid=(Rp // tr,),
                                               in_specs=[a_spec] + r_specs, out_specs=o_spec),
        out_shape=jax.ShapeDtypeStruct(geo.piece, F32),
        compiler_params=_params(("parallel",)),
    )(j_arr, chip_sum, got, got, got)


def _run_job(job, name):
    ni = len(job.inputs)

    def body(*refs):
        ins, outs, (send, recv) = refs[:ni], refs[ni:-2], refs[-2:]
        job.start(ins, outs, send, recv)
        job.finish(ins, outs, send, recv)

    return pl.pallas_call(
        body, name=name, in_specs=[ANY] * ni, out_specs=[ANY] * len(job.outputs), out_shape=list(job.outputs),
        input_output_aliases=dict(job.aliases),
        scratch_shapes=[pltpu.SemaphoreType.DMA((job.nsem,)), pltpu.SemaphoreType.DMA((job.nsem,))],
    )(*job.inputs)


def _adam_math(w, g, m, v):
    m = ADAM_B1 * m + (1.0 - ADAM_B1) * g
    v = ADAM_B2 * v + (1.0 - ADAM_B2) * (g * g)
    m_hat = m / (1.0 - ADAM_B1 ** ADAM_STEP)
    v_hat = v / (1.0 - ADAM_B2 ** ADAM_STEP)
    delta = -ADAM_LR * (m_hat / (jnp.sqrt(v_hat) + ADAM_EPS) + ADAM_WD * w)
    return delta, m, v


def _adam(ws, mines, theirs, ms, vs, geo, name, jobs=()):
    n = len(ws)
    R, C = ws[0].shape
    tr = _tile(R, 128 if n == 1 else 32, 8)
    nrb = R // tr
    col = geo.kind == "col"
    assert not col or nrb % 2 == 0

    def rows(k, halved):
        def index(w, i):
            r = jnp.where(w == k, i, jnp.where(w < k, 0, nrb - 1))
            return (r % (nrb // 2) if halved else r, 0)
        return index

    def body(*refs):
        ins, outs = refs[:5 * n], refs[5 * n:]
        wi, i = pl.program_id(0), pl.program_id(1)
        c = lax.axis_index("c")
        for k in range(n):
            @pl.when(wi == k)
            def _():
                w_ref, a_ref, b_ref, m_ref, v_ref = ins[5 * k:5 * k + 5]
                if col:
                    gv = jnp.where(i // (nrb // 2) == c, a_ref[...], b_ref[...])
                else:
                    a, b = a_ref[...], b_ref[...]
                    gv = jnp.concatenate([jnp.where(c == 0, a, b), jnp.where(c == 0, b, a)], axis=1)
                delta, mn, vn = _adam_math(w_ref[...], gv, m_ref[...], v_ref[...])
                og, od, om, ov = outs[4 * k:4 * k + 4]
                og[...] = gv
                od[...] = delta
                om[...] = mn
                ov[...] = vn

    in_specs, out_specs, args = [], [], []
    for k in range(n):
        spec = pl.BlockSpec((tr, C), rows(k, False))
        h_spec = pl.BlockSpec((tr, C), rows(k, True)) if col else pl.BlockSpec((tr, C // 2), rows(k, False))
        in_specs += [spec, h_spec, h_spec, spec, spec]
        out_specs += [spec] * 4
        args += [ws[k], mines[k], theirs[k], ms[k], vs[k]]
    outs, job_outs = _hosted_call(body, name, (n, nrb), in_specs, out_specs,
                                  [jax.ShapeDtypeStruct((R, C), F32)] * (4 * n), [], args, jobs,
                                  ("arbitrary", "arbitrary"))
    return [outs[4 * k:4 * k + 4] for k in range(n)], job_outs


def _small_all_reduce(parts, loss_part, D):
    n = len(parts)

    def body(*refs):
        p_refs, l_ref, o_ref, vec, buf, send, recv = refs[:n], refs[n], refs[n + 1], *refs[n + 2:]
        x, y, c, _ = _place()
        me = 4 * x + 2 * y + c
        vec[...] = jnp.zeros(vec.shape, F32)
        for r in range(n):
            vec[r:r + 1, :] = jnp.sum(p_refs[r][...], axis=0, keepdims=True)
        vec[n:n + 1, 0:LANES] = l_ref[0:1, :]
        buf[me] = vec[...]
        cps = []
        for dd in range(1, 8):
            bx, by, bc = (dd >> 2) & 1, (dd >> 1) & 1, dd & 1
            peer = (x + bx - 2 * x * bx, y + by - 2 * y * by, c + bc - 2 * c * bc)
            cp = pltpu.make_async_remote_copy(
                src_ref=vec, dst_ref=buf.at[me], send_sem=send.at[dd - 1], recv_sem=recv.at[dd - 1],
                device_id=peer, device_id_type=MESH)
            cp.start()
            cps.append(cp)
        for cp in cps:
            cp.wait()
        tot = buf[0]
        for s in range(1, 8):
            tot = tot + buf[s]
        o_ref[...] = tot

    vm = pl.BlockSpec(memory_space=pltpu.VMEM)
    return pl.pallas_call(
        body, name="small_all_reduce",
        in_specs=[vm] * (n + 1), out_specs=vm,
        out_shape=jax.ShapeDtypeStruct((8, D), F32),
        scratch_shapes=[pltpu.VMEM((8, D), F32), pltpu.VMEM((8, 8, D), F32),
                        pltpu.SemaphoreType.DMA((7,)), pltpu.SemaphoreType.DMA((7,))],
    )(*parts, loss_part)


def _small_adam(tot, ws, ms, vs, rows):
    n = len(ws)

    def body(*refs):
        t_ref = refs[0]
        w_refs, m_refs, v_refs = refs[1:1 + n], refs[1 + n:1 + 2 * n], refs[1 + 2 * n:1 + 3 * n]
        outs = refs[1 + 3 * n:]
        for i in range(n):
            r, c0 = rows[i]
            width = w_refs[i].shape[1]
            g = t_ref[r:r + 1, c0:c0 + width]
            delta, mn, vn = _adam_math(w_refs[i][...], g, m_refs[i][...], v_refs[i][...])
            outs[4 * i][...] = g
            outs[4 * i + 1][...] = delta
            outs[4 * i + 2][...] = mn
            outs[4 * i + 3][...] = vn

    vm = pl.BlockSpec(memory_space=pltpu.VMEM)
    out_shape = []
    for w in ws:
        out_shape += [jax.ShapeDtypeStruct(w.shape, F32)] * 4
    return pl.pallas_call(
        body, name="small_adam",
        in_specs=[vm] * (1 + 3 * n), out_specs=[vm] * (4 * n), out_shape=out_shape,
    )(tot, *ws, *ms, *vs)


def kernel(x, positions, norm_mix_g, w_in, norm_out_dil_g, norm_out_sb_g, w_out, norm_ffn_g, w_gate, w_up, w_down, norm_final_g, loss_target, m_norm_mix_g, m_w_in, m_norm_out_dil_g, m_norm_out_sb_g, m_w_out, m_norm_ffn_g, m_w_gate, m_w_up, m_w_down, m_norm_final_g, v_norm_mix_g, v_w_in, v_norm_out_dil_g, v_norm_out_sb_g, v_w_out, v_norm_ffn_g, v_w_gate, v_w_up, v_w_down, v_norm_final_g):
    B, T, D = x.shape
    M = B * T
    n_heads = D // HEAD_DIM
    Hd = n_heads // 2
    Hs = n_heads - Hd
    W = Hd * HEAD_DIM
    N3 = 3 * D
    fs = w_gate.shape[2]
    fp = -(-fs // LANES) * LANES
    assert T % TK_DIL == 0 and W == Hs * HEAD_DIM and fp > fs

    x2 = x.reshape(M, D)
    tgt = loss_target.reshape(M, D)
    tr_ = jnp.transpose
    big = [w_in[0], w_out[0], tr_(w_gate[0]), tr_(w_up[0]), w_down[0]]
    big_m = [m_w_in[0], m_w_out[0], tr_(m_w_gate[0]), tr_(m_w_up[0]), m_w_down[0]]
    big_v = [v_w_in[0], v_w_out[0], tr_(v_w_gate[0]), tr_(v_w_up[0]), v_w_down[0]]
    names = ["w_in", "w_out", "w_gate", "w_up", "w_down"]
    c_arr = jnp.reshape(lax.axis_index("c"), (1,)).astype(jnp.int32)
    j_arr = jnp.reshape(2 * lax.axis_index("x") + lax.axis_index("y"), (1,)).astype(jnp.int32)

    ns_in = w_in.shape[2]
    ks_out = w_out.shape[1]
    geo = [_Weight("col", (D, ns_in), ns_in), _Weight("row", (ks_out, D), ks_out),
           _Weight("row", (fs, D), fp), _Weight("row", (fs, D), fp), _Weight("row", (fs, D), fp)]
    g_in, g_out, g_gate, g_up, g_down = geo
    fulls = [_cast_into_full(big[w], geo[w], j_arr, "cast_" + names[w]) for w in range(5)]
    for w in (2, 3, 4):
        fulls[w] = _zero_pad(fulls[w], geo[w], "zero_pad_" + names[w])

    def chip_sum(w, grad_full, from_sibling):
        return _chip_sum(grad_full, from_sibling, geo[w], c_arr, "chip_sum_" + names[w])

    def final_sum(w, cs, from_chips):
        return _final_sum(cs, from_chips, geo[w], j_arr, "final_sum_" + names[w])

    (Win,) = _all_gather_weights([fulls[0]], [g_in])
    tabs3 = _rope_tables(positions)
    tabs2 = [t.reshape(M, HEAD_DIM) for t in tabs3]
    hn = _rmsnorm_fwd(x2, norm_mix_g, "norm_mix")
    proj, (Wout, WgT) = _matmul(hn, Win, "nn", BF16, "in_proj", rope=(tabs2, 2 * W),
                                jobs=[_gather_job(fulls[1], g_out, "ici"), _gather_job(fulls[2], g_gate, "ici")])
    proj3 = proj.reshape(B, T, N3)
    bias = _dil_bias(T)
    (o_dil3, lse3), (Wout, WgT, WuT) = _dil_fwd(
        proj3, bias, Hd, "dil_fwd",
        jobs=[_gather_job(Wout, g_out, "d2d"), _gather_job(WgT, g_gate, "d2d"), _gather_job(fulls[3], g_up, "ici")])
    (o_sb3, lt3), (WuT, Wd) = _sb_fwd(
        proj3, Hd, Hs, "sb_fwd", jobs=[_gather_job(WuT, g_up, "d2d"), _gather_job(fulls[4], g_down, "ici")])
    o_dil, o_sb = o_dil3.reshape(M, W), o_sb3.reshape(M, W)
    g_heads = jnp.concatenate([norm_out_dil_g, norm_out_sb_g], axis=1)
    o_mix = _headnorm_fwd(o_dil, o_sb, g_heads, "head_norm")
    h1, (Wd,) = _matmul(o_mix, Wout, "nn", F32, "out_proj", res=x2, jobs=[_gather_job(Wd, g_down, "d2d")])
    hn2 = _rmsnorm_fwd(h1, norm_ffn_g, "norm_ffn")
    gate, up, act = _gate_up(hn2, WgT, WuT, "gate_up")
    h2 = _matmul(act, Wd, "nn", F32, "down_proj", res=h1)
    dh2, dh2_b, loss_part, dg_final = _loss_head(h2, tgt, norm_final_g.reshape(1, D), "loss_head")

    dWd = _matmul(act, dh2_b, "tn", BF16, "dw_down")
    d_act, (sib_d,) = _matmul(dh2_b, Wd, "nt", BF16, "d_act", jobs=[_sibling_job(dWd, g_down)])
    cs_d = chip_sum(4, dWd, sib_d)
    d_gate, d_up = _swiglu_bwd(d_act, gate, up, "swiglu_bwd")
    dWg, (chips_d,) = _matmul(d_gate, hn2, "tn", BF16, "dw_gate", jobs=[_chips_job(cs_d, g_down)])
    half_d = final_sum(4, cs_d, chips_d)
    dWu, (sib_g, other_d) = _matmul(d_up, hn2, "tn", BF16, "dw_up",
                                    jobs=[_sibling_job(dWg, g_gate), _final_job(half_d, g_down)])
    cs_g = chip_sum(2, dWg, sib_g)
    d_hn2, (chips_g,) = _matmul(d_gate, WgT, "nn", F32, "d_hn2_gate", jobs=[_chips_job(cs_g, g_gate)])
    half_g = final_sum(2, cs_g, chips_g)
    d_hn2, (sib_u, other_g) = _matmul(d_up, WuT, "nn", F32, "d_hn2_up", res=d_hn2,
                                      jobs=[_sibling_job(dWu, g_up), _final_job(half_g, g_gate)])
    cs_u = chip_sum(3, dWu, sib_u)
    dh1, dh1_b, dg_ffn = _rmsnorm_bwd(h1, d_hn2, norm_ffn_g, dh2, "norm_ffn_bwd", True)
    dWo = _matmul(o_mix, dh1_b, "tn", BF16, "dw_out")
    d_mix, (sib_o,) = _matmul(dh1_b, Wout, "nt", BF16, "d_mix", jobs=[_sibling_job(dWo, g_out)])
    cs_o = chip_sum(1, dWo, sib_o)
    d_o, dg_heads = _headnorm_bwd(d_mix, o_dil, o_sb, g_heads, "head_norm_bwd")
    d_o3 = d_o.reshape(B, T, D)
    dqkv_dil, (chips_u,) = _dil_bwd(proj3, bias, tabs3, d_o3, o_dil3, lse3, Hd, "dil_bwd",
                                    jobs=[_chips_job(cs_u, g_up)])
    half_u = final_sum(3, cs_u, chips_u)
    dqkv_sb, (chips_o, other_u) = _sb_bwd(proj3, d_o3, lt3, Hd, Hs, "sb_bwd",
                                          jobs=[_chips_job(cs_o, g_out), _final_job(half_u, g_up)])
    half_o = final_sum(1, cs_o, chips_o)
    dproj = jnp.concatenate([*dqkv_dil, *dqkv_sb], axis=-1).reshape(M, N3)
    dWin, (other_o,) = _matmul(hn, dproj, "tn", BF16, "dw_in", jobs=[_final_job(half_o, g_out)])
    (sib_i,) = _run_job(_sibling_job(dWin, g_in), "sibling_w_in")
    cs_i = chip_sum(0, dWin, sib_i)
    d_hn, (chips_i,) = _matmul(dproj, Win, "nt", F32, "d_hn", jobs=[_chips_job(cs_i, g_in)])
    half_i = final_sum(0, cs_i, chips_i)
    dx, dg_mix = _rmsnorm_bwd(x2, d_hn, norm_mix_g, dh1, "norm_mix_bwd", False)
    (other_i,) = _run_job(_final_job(half_i, g_in), "final_w_in")

    mine = [half_i, half_o, half_g, half_u, half_d]
    theirs = [other_i, other_o, other_g, other_u, other_d]
    big_out = [_adam([big[w]], [mine[w]], [theirs[w]], [big_m[w]], [big_v[w]], geo[w], "adam_" + names[w])[0][0]
               for w in range(5)]
    for w in (2, 3):
        big_out[w] = [tr_(a) for a in big_out[w]]

    tot = _small_all_reduce([dg_mix, dg_heads, dg_ffn, dg_final], loss_part, D)
    loss = tot[4, 0]
    small_w = [norm_mix_g, norm_out_dil_g, norm_out_sb_g, norm_ffn_g, norm_final_g.reshape(1, D)]
    small_m = [m_norm_mix_g, m_norm_out_dil_g, m_norm_out_sb_g, m_norm_ffn_g, m_norm_final_g.reshape(1, D)]
    small_v = [v_norm_mix_g, v_norm_out_dil_g, v_norm_out_sb_g, v_norm_ffn_g, v_norm_final_g.reshape(1, D)]
    so = _small_adam(tot, small_w, small_m, small_v, [(0, 0), (1, 0), (1, W), (2, 0), (3, 0)])
    small_out = [so[4 * i:4 * i + 4] for i in range(5)]
    small_out[4] = [a.reshape(D) for a in small_out[4]]

    per_weight = [small_out[0], big_out[0], small_out[1], small_out[2], big_out[1], small_out[3],
                  big_out[2], big_out[3], big_out[4], small_out[4]]

    def field(i):
        res = []
        for n_, o in enumerate(per_weight):
            a = o[i]
            res.append(a[None] if n_ in (1, 4, 6, 7, 8) else a)
        return res

    return (loss, dx.reshape(B, T, D), *field(0), *field(1), *field(2), *field(3))
```

```python
import functools
import math

import jax
import jax.numpy as jnp
from jax import lax
from jax.experimental import pallas as pl
from jax.experimental.pallas import tpu as pltpu

F32 = jnp.float32
BF16 = jnp.bfloat16
MESH = pl.DeviceIdType.MESH
ANY = pl.BlockSpec(memory_space=pl.ANY)

LANES = 128
HEAD_DIM = 128
DIL_STEPS = 128
DILATIONS = (1, 4, 16)
TQ = 512
TK_DIL = 512
TK_SB = 256
NEG = -1e30
_PART = TQ
_ROW_PARTS = tuple(slice(p, p + _PART) for p in range(0, TQ, _PART))
ROPE_DIM = 32
ROPE_THETA = 500000.0
RMS_EPS = 1e-5
ADAM_LR, ADAM_B1, ADAM_B2, ADAM_EPS, ADAM_WD, ADAM_STEP = 0.001, 0.9, 0.999, 1e-08, 0.01, 10
N_CHIPS = 4
VMEM_LIMIT = 56 * 1024 * 1024
MATMUL_VMEM = 52 * 1024 * 1024

NN = ((1,), (0,))
NT = ((1,), (1,))
TN = ((0,), (0,))


def _dot(a, b, dims):
    return lax.dot_general(a, b, (dims, ((), ())), preferred_element_type=F32)


def _tile(dim, pref, unit=LANES):
    if dim <= pref:
        return dim
    t = (pref // unit) * unit
    while t > unit and dim % t:
        t -= unit
    assert dim % t == 0, (dim, pref, unit)
    return t


def _params(sem=None):
    return pltpu.CompilerParams(dimension_semantics=sem, vmem_limit_bytes=VMEM_LIMIT)


class _Job:
    def __init__(self, inputs, outputs, aliases, nsem, copies):
        self.inputs, self.outputs, self.aliases, self.nsem, self.copies = inputs, outputs, aliases, nsem, copies

    def start(self, ins, outs, send, recv):
        for cp in self.copies(ins, outs, send, recv)[0]:
            cp.start()

    def finish(self, ins, outs, send, recv):
        started, landing = self.copies(ins, outs, send, recv)
        for make in landing:
            make().wait_recv()
        for cp in started:
            cp.wait_send()


def _hosted_call(body, name, grid, in_specs, out_specs, out_shape, scratch_shapes, args, jobs, semantics):
    nbi, nbo, nbs = len(in_specs), len(out_specs), len(scratch_shapes)
    if not jobs:
        res = pl.pallas_call(
            body, name=name, grid=grid, in_specs=list(in_specs), out_specs=list(out_specs), out_shape=list(out_shape),
            scratch_shapes=list(scratch_shapes), compiler_params=_params(semantics))(*args)
        return list(res), []
    j_in = [a for jb in jobs for a in jb.inputs]
    j_out = [o for jb in jobs for o in jb.outputs]
    aliases, ii, oo, sems = {}, nbi, nbo, []
    for jb in jobs:
        for a, b in jb.aliases.items():
            aliases[ii + a] = oo + b
        ii += len(jb.inputs)
        oo += len(jb.outputs)
        sems += [pltpu.SemaphoreType.DMA((jb.nsem,)), pltpu.SemaphoreType.DMA((jb.nsem,))]

    def wrapped(*refs):
        p = nbi + len(j_in)
        b_in, ji = refs[:nbi], refs[nbi:p]
        b_out, jo = refs[p:p + nbo], refs[p + nbo:p + nbo + len(j_out)]
        p += nbo + len(j_out)
        b_scr, js = refs[p:p + nbs], refs[p + nbs:]
        pids = [pl.program_id(k) for k in range(len(grid))]
        first = functools.reduce(jnp.logical_and, [pid == 0 for pid in pids])
        last = functools.reduce(jnp.logical_and, [pid == g - 1 for pid, g in zip(pids, grid)])

        def each(what):
            a = b = 0
            for n, jb in enumerate(jobs):
                getattr(jb, what)(ji[a:a + len(jb.inputs)], jo[b:b + len(jb.outputs)], js[2 * n], js[2 * n + 1])
                a += len(jb.inputs)
                b += len(jb.outputs)

        @pl.when(first)
        def _():
            each("start")

        body(*b_in, *b_out, *b_scr)

        @pl.when(last)
        def _():
            each("finish")

    res = pl.pallas_call(
        wrapped, name=name, grid=grid,
        in_specs=list(in_specs) + [ANY] * len(j_in), out_specs=list(out_specs) + [ANY] * len(j_out),
        out_shape=list(out_shape) + j_out, input_output_aliases=aliases,
        scratch_shapes=list(scratch_shapes) + sems,
        compiler_params=_params(("arbitrary",) * len(grid)))(*args, *j_in)
    return list(res[:nbo]), list(res[nbo:])


def _rope(a, c, s1, s2):
    half = ROPE_DIM // 2
    return a * c + pltpu.roll(a, HEAD_DIM - half, 1) * s1 + pltpu.roll(a, half, 1) * s2


def _matmul(a, b, mode, out_dtype, name, res=None, rope=None, swiglu=None, jobs=(), tm=1024, tn=1024, tk=2816):
    if mode == "nn":
        (M, K), (_, N) = a.shape, b.shape
    elif mode == "nt":
        (M, K), (N, _) = a.shape, b.shape
    else:
        (K, M), (_, N) = a.shape, b.shape
    tm = _tile(M, tm)
    tn = _tile(N, tn) if rope is None else _tile(math.gcd(N, rope[1]), tn)
    n_out = 2 if swiglu is not None else 1

    def vmem_bytes(t):
        osz = jnp.dtype(out_dtype).itemsize
        return (4 * t * (tm + tn) + 4 * tm * tn + (4 * tm * tn if t < K else 0) + 2 * n_out * tm * tn * osz
                + (8 * tm * tn if res is not None else 0) + (8 * tm * tn if swiglu is not None else 0)
                + (4 * tm * tn + 24 * tm * HEAD_DIM if rope is not None else 0))

    whole_k = () if swiglu is not None else (K,)
    tk = next(t for t in whole_k + (_tile(K, tk), _tile(K, 2048), _tile(K, 1024)) if vmem_bytes(t) <= MATMUL_VMEM)
    nk = K // tk
    dims = {"nn": NN, "nt": NT, "tn": TN}[mode]
    a_spec = (pl.BlockSpec((tk, tm), lambda i, j, k: (k, i)) if mode == "tn"
              else pl.BlockSpec((tm, tk), lambda i, j, k: (i, k)))
    b_spec = (pl.BlockSpec((tn, tk), lambda i, j, k: (j, k)) if mode == "nt"
              else pl.BlockSpec((tk, tn), lambda i, j, k: (k, j)))
    o_spec = pl.BlockSpec((tm, tn), lambda i, j, k: (i, j))
    n_extra = (1 if res is not None else 0) + (3 if rope is not None else 0) + (2 if swiglu is not None else 0)
    use_acc = nk > 1 or rope is not None
    if rope is not None:
        assert rope[1] % tn == 0

    def body(*refs):
        a_ref, b_ref = refs[:2]
        extra = refs[2:2 + n_extra]
        o_ref = refs[2 + n_extra]
        k = pl.program_id(2)
        prod = _dot(a_ref[...], b_ref[...], dims)
        if use_acc:
            acc_ref = refs[-1]
            if nk > 1:
                @pl.when(k == 0)
                def _():
                    acc_ref[...] = jnp.zeros(acc_ref.shape, F32)

                acc_ref[...] += prod
            else:
                acc_ref[...] = prod
            total = lambda: acc_ref[...]
        else:
            total = lambda: prod
        at_end = pl.when(k == nk - 1) if nk > 1 else (lambda f: f())

        if swiglu is not None:
            @at_end
            def _():
                d = total()
                g = extra[0][...].astype(F32)
                u = extra[1][...].astype(F32)
                sig = jax.nn.sigmoid(g)
                o_ref[...] = (d * u * sig * (1.0 + g * (1.0 - sig))).astype(out_dtype)
                refs[3 + n_extra][...] = (d * g * sig).astype(out_dtype)
        elif rope is None:
            @at_end
            def _():
                out = total()
                if res is not None:
                    out = out + extra[0][...]
                o_ref[...] = out.astype(out_dtype)
        else:
            roped = pl.program_id(1) * tn < rope[1]

            @pl.when(jnp.logical_and(k == nk - 1, roped))
            def _():
                c, s1, s2 = extra[0][...], extra[1][...], extra[2][...]
                for h in range(tn // HEAD_DIM):
                    cols = slice(h * HEAD_DIM, (h + 1) * HEAD_DIM)
                    o_ref[:, cols] = _rope(acc_ref[:, cols], c, s1, s2).astype(out_dtype)

            @pl.when(jnp.logical_and(k == nk - 1, jnp.logical_not(roped)))
            def _():
                o_ref[...] = acc_ref[...].astype(out_dtype)

    in_specs, args = [a_spec, b_spec], [a, b]
    if res is not None:
        in_specs.append(o_spec)
        args.append(res)
    if rope is not None:
        in_specs += [pl.BlockSpec((tm, HEAD_DIM), lambda i, j, k: (i, 0))] * 3
        args += list(rope[0])
    if swiglu is not None:
        in_specs += [o_spec, o_spec]
        args += list(swiglu)
    outs, extra_out = _hosted_call(
        body, name, (M // tm, N // tn, nk), in_specs, [o_spec] * n_out,
        [jax.ShapeDtypeStruct((M, N), out_dtype)] * n_out,
        [pltpu.VMEM((tm, tn), F32)] if use_acc else [], args, jobs, ("parallel", "parallel", "arbitrary"))
    out = outs[0] if n_out == 1 else tuple(outs)
    return (out, extra_out) if jobs else out


def _gate_up(hn, wg, wu, name, jobs=()):
    M, K = hn.shape
    N = wg.shape[0]
    tm, tn, tk = _tile(M, 1024), _tile(N, 512), _tile(K, 4096)
    nk = K // tk

    def body(a_ref, g_ref, u_ref, og_ref, ou_ref, oa_ref, *accs):
        k = pl.program_id(2)
        a = a_ref[...]
        pg = _dot(a, g_ref[...], NT)
        pu = _dot(a, u_ref[...], NT)

        def finish(g, u):
            og_ref[...] = g.astype(BF16)
            ou_ref[...] = u.astype(BF16)
            oa_ref[...] = (g * jax.nn.sigmoid(g) * u).astype(BF16)

        if nk == 1:
            finish(pg, pu)
        else:
            accg, accu = accs

            @pl.when(k == 0)
            def _():
                accg[...] = jnp.zeros(accg.shape, F32)
                accu[...] = jnp.zeros(accu.shape, F32)

            accg[...] += pg
            accu[...] += pu

            @pl.when(k == nk - 1)
            def _():
                finish(accg[...], accu[...])

    w_spec = pl.BlockSpec((tn, tk), lambda i, j, k: (j, k))
    o_spec = pl.BlockSpec((tm, tn), lambda i, j, k: (i, j))
    sds = jax.ShapeDtypeStruct((M, N), BF16)
    return _hosted_call(
        body, name, (M // tm, N // tn, nk), [pl.BlockSpec((tm, tk), lambda i, j, k: (i, k)), w_spec, w_spec],
        [o_spec, o_spec, o_spec], [sds, sds, sds],
        [pltpu.VMEM((tm, tn), F32), pltpu.VMEM((tm, tn), F32)] if nk > 1 else [],
        (hn, wg, wu), jobs, ("parallel", "parallel", "arbitrary"))


def _rmsnorm_fwd(x, g, name):
    M, D = x.shape
    tm = _tile(M, 256, 16)

    def body(x_ref, g_ref, o_ref):
        xv = x_ref[...]
        r = lax.rsqrt(jnp.mean(xv * xv, axis=-1, keepdims=True) + RMS_EPS)
        o_ref[...] = (xv * r * g_ref[...]).astype(BF16)

    return pl.pallas_call(
        body, name=name, grid=(M // tm,),
        in_specs=[pl.BlockSpec((tm, D), lambda i: (i, 0)), pl.BlockSpec((1, D), lambda i: (0, 0))],
        out_specs=pl.BlockSpec((tm, D), lambda i: (i, 0)),
        out_shape=jax.ShapeDtypeStruct((M, D), BF16),
        compiler_params=_params(("parallel",)),
    )(x, g)


def _fold8(v):
    tm, D = v.shape
    return jnp.sum(v.reshape(tm // 8, 8, D), axis=0)


def _rmsnorm_bwd(x, dy, g, res, name, want_bf16):
    M, D = x.shape
    tm = _tile(M, 128, 16)

    def body(x_ref, dy_ref, g_ref, r_ref, *outs):
        dx_ref, dg_ref = outs[0], outs[-1]
        xv = x_ref[...]
        dyv = dy_ref[...].astype(F32)
        r = lax.rsqrt(jnp.mean(xv * xv, axis=-1, keepdims=True) + RMS_EPS)
        u = dyv * g_ref[...]
        dot = jnp.sum(xv * u, axis=-1, keepdims=True)
        dx = r * u - xv * (r * r * r * (1.0 / D)) * dot + r_ref[...]
        dx_ref[...] = dx
        if want_bf16:
            outs[1][...] = dx.astype(BF16)
        part = _fold8(dyv * xv * r)

        @pl.when(pl.program_id(0) == 0)
        def _():
            dg_ref[...] = part

        @pl.when(pl.program_id(0) > 0)
        def _():
            dg_ref[...] += part

    row = pl.BlockSpec((tm, D), lambda i: (i, 0))
    out_specs = [row] + ([row] if want_bf16 else []) + [pl.BlockSpec((8, D), lambda i: (0, 0))]
    out_shape = ([jax.ShapeDtypeStruct((M, D), F32)] + ([jax.ShapeDtypeStruct((M, D), BF16)] if want_bf16 else [])
                 + [jax.ShapeDtypeStruct((8, D), F32)])
    return pl.pallas_call(
        body, name=name, grid=(M // tm,),
        in_specs=[row, row, pl.BlockSpec((1, D), lambda i: (0, 0)), row],
        out_specs=out_specs, out_shape=out_shape,
        compiler_params=_params(("arbitrary",)),
    )(x, dy, g, res)


def _loss_head(h, tgt, g, name):
    M, D = h.shape
    tm = _tile(M, 128, 16)

    def body(h_ref, t_ref, g_ref, dh_ref, dhb_ref, l_ref, dg_ref):
        hv = h_ref[...]
        gv = g_ref[...]
        r = lax.rsqrt(jnp.mean(hv * hv, axis=-1, keepdims=True) + RMS_EPS)
        n = hv * r
        e = n * gv - t_ref[...]
        dy = e * (1.0 / D)
        u = dy * gv
        dot = jnp.sum(hv * u, axis=-1, keepdims=True)
        dh = r * u - hv * (r * r * r * (1.0 / D)) * dot
        dh_ref[...] = dh
        dhb_ref[...] = dh.astype(BF16)
        rows = jnp.sum(e * e, axis=-1, keepdims=True)
        lpart = jnp.broadcast_to(jnp.sum(rows, axis=0, keepdims=True) * (0.5 / D), (8, LANES))
        gpart = _fold8(dy * n)

        @pl.when(pl.program_id(0) == 0)
        def _():
            l_ref[...] = lpart
            dg_ref[...] = gpart

        @pl.when(pl.program_id(0) > 0)
        def _():
            l_ref[...] += lpart
            dg_ref[...] += gpart

    row = pl.BlockSpec((tm, D), lambda i: (i, 0))
    return pl.pallas_call(
        body, name=name, grid=(M // tm,),
        in_specs=[row, row, pl.BlockSpec((1, D), lambda i: (0, 0))],
        out_specs=[row, row, pl.BlockSpec((8, LANES), lambda i: (0, 0)), pl.BlockSpec((8, D), lambda i: (0, 0))],
        out_shape=[jax.ShapeDtypeStruct((M, D), F32), jax.ShapeDtypeStruct((M, D), BF16),
                   jax.ShapeDtypeStruct((8, LANES), F32), jax.ShapeDtypeStruct((8, D), F32)],
        compiler_params=_params(("arbitrary",)),
    )(h, tgt, g)


def _headnorm_fwd(o_dil, o_sb, g, name):
    M, W = o_dil.shape
    D = 2 * W
    tm = _tile(M, 256, 16)

    def body(a_ref, b_ref, g_ref, o_ref):
        for h in range(D // HEAD_DIM):
            src = a_ref if h < W // HEAD_DIM else b_ref
            lo = (h * HEAD_DIM) % W
            v = src[:, lo:lo + HEAD_DIM]
            r = lax.rsqrt(jnp.mean(v * v, axis=-1, keepdims=True) + RMS_EPS)
            o_ref[:, h * HEAD_DIM:(h + 1) * HEAD_DIM] = (v * r * g_ref[:, h * HEAD_DIM:(h + 1) * HEAD_DIM]).astype(BF16)

    half = pl.BlockSpec((tm, W), lambda i: (i, 0))
    return pl.pallas_call(
        body, name=name, grid=(M // tm,),
        in_specs=[half, half, pl.BlockSpec((1, D), lambda i: (0, 0))],
        out_specs=pl.BlockSpec((tm, D), lambda i: (i, 0)),
        out_shape=jax.ShapeDtypeStruct((M, D), BF16),
        compiler_params=_params(("parallel",)),
    )(o_dil, o_sb, g)


def _headnorm_bwd(d_mix, o_dil, o_sb, g, name):
    M, W = o_dil.shape
    D = 2 * W
    tm = _tile(M, 128, 16)

    def body(d_ref, a_ref, b_ref, g_ref, do_ref, dg_ref):
        @pl.when(pl.program_id(0) == 0)
        def _():
            dg_ref[...] = jnp.zeros(dg_ref.shape, F32)

        for h in range(D // HEAD_DIM):
            src = a_ref if h < W // HEAD_DIM else b_ref
            lo = (h * HEAD_DIM) % W
            cols = slice(h * HEAD_DIM, (h + 1) * HEAD_DIM)
            v = src[:, lo:lo + HEAD_DIM]
            dy = d_ref[:, cols].astype(F32)
            r = lax.rsqrt(jnp.mean(v * v, axis=-1, keepdims=True) + RMS_EPS)
            u = dy * g_ref[:, cols]
            dot = jnp.sum(v * u, axis=-1, keepdims=True)
            do_ref[:, cols] = r * u - v * (r * r * r * (1.0 / HEAD_DIM)) * dot
            dg_ref[:, cols] += _fold8(dy * v * r)

    half = pl.BlockSpec((tm, W), lambda i: (i, 0))
    row = pl.BlockSpec((tm, D), lambda i: (i, 0))
    return pl.pallas_call(
        body, name=name, grid=(M // tm,),
        in_specs=[row, half, half, pl.BlockSpec((1, D), lambda i: (0, 0))],
        out_specs=[row, pl.BlockSpec((8, D), lambda i: (0, 0))],
        out_shape=[jax.ShapeDtypeStruct((M, D), F32), jax.ShapeDtypeStruct((8, D), F32)],
        compiler_params=_params(("arbitrary",)),
    )(d_mix, o_dil, o_sb, g)


def _rope_tables(positions):
    half = ROPE_DIM // 2
    inv_freq = jnp.power(jnp.float32(ROPE_THETA), -jnp.arange(half, dtype=F32) / half)
    ang = positions.astype(F32)[..., None] * inv_freq
    cos, sin = jnp.cos(ang), jnp.sin(ang)
    rest = HEAD_DIM - ROPE_DIM
    one = jnp.ones(cos.shape[:-1] + (rest,), F32)
    z16 = jnp.zeros_like(sin)
    zr = jnp.zeros_like(one)
    c = jnp.concatenate([cos, cos, one], axis=-1)
    s1 = jnp.concatenate([-sin, z16, zr], axis=-1)
    s2 = jnp.concatenate([z16, sin, zr], axis=-1)
    return c, s1, s2


def _dil_bias(T):
    ne = T // TQ
    e = jnp.arange(ne, dtype=jnp.int32)[:, None, None]
    r = jnp.arange(TQ, dtype=jnp.int32)[None, :, None]
    c = jnp.arange(TK_DIL, dtype=jnp.int32)[None, None, :]
    dist = e * TQ + r - c
    mult = jnp.zeros(dist.shape, F32)
    for d in DILATIONS:
        mult = mult + jnp.where((dist % d == 0) & (dist <= DIL_STEPS * d), 1.0, 0.0)
    return jnp.where((dist >= 0) & (mult > 0), jnp.log(jnp.maximum(mult, 1.0)), NEG)


def _dil_fwd(proj3, bias, Hd, name, jobs=()):
    B, T, N3 = proj3.shape
    nq = T // TQ
    per = TK_DIL // TQ
    scale = HEAD_DIM ** -0.5

    def body(q_ref, k_ref, v_ref, b_ref, o_ref, l_ref):
        i = pl.program_id(2)
        qs = [q_ref[r, :] for r in _ROW_PARTS]
        last = i // per

        def step(t, carry):
            j = last - t
            off = pl.multiple_of(j * TK_DIL, TK_DIL)
            kj = k_ref[pl.ds(off, TK_DIL), :]
            vj = v_ref[pl.ds(off, TK_DIL), :]
            new = []
            for n, r in enumerate(_ROW_PARTS):
                m, l, acc = carry[n]
                s = _dot(qs[n], kj, NT) * scale + b_ref[i - per * j, r, :]
                m_new = jnp.maximum(m, jnp.max(s, axis=-1, keepdims=True))
                p = jnp.exp(s - m_new)
                corr = jnp.exp(m - m_new)
                new.append((m_new, l * corr + jnp.sum(p, axis=-1, keepdims=True),
                            acc * corr + _dot(p.astype(BF16), vj, NN)))
            return tuple(new)

        init = (jnp.full((_PART, 1), NEG, F32), jnp.zeros((_PART, 1), F32), jnp.zeros((_PART, HEAD_DIM), F32))
        done = lax.fori_loop(0, last + 1, step, (init,) * len(_ROW_PARTS))
        for n, r in enumerate(_ROW_PARTS):
            m, l, acc = done[n]
            o_ref[r, :] = acc / l
            l_ref[r, :] = jnp.broadcast_to(m + jnp.log(l), (_PART, HEAD_DIM))

    full = lambda base: pl.BlockSpec((None, T, HEAD_DIM), lambda b, h, i: (b, 0, base + h))
    blk = lambda base: pl.BlockSpec((None, TQ, HEAD_DIM), lambda b, h, i: (b, i, base + h))
    sds = jax.ShapeDtypeStruct((B, T, Hd * HEAD_DIM), F32)
    return _hosted_call(
        body, name, (B, Hd, nq),
        [blk(0), full(Hd), full(2 * Hd), pl.BlockSpec(bias.shape, lambda b, h, i: (0, 0, 0))],
        [blk(0), blk(0)], [sds, sds], [], (proj3, proj3, proj3, bias), jobs, ("parallel", "parallel", "arbitrary"))


def _dil_bwd(proj3, bias, tabs, d_o3, o3, lse3, Hd, name, jobs=()):
    B, T, N3 = proj3.shape
    nq = T // TQ
    per = TK_DIL // TQ
    scale = HEAD_DIM ** -0.5

    def body(q_ref, k_ref, v_ref, b_ref, do_ref, o_ref, l_ref, cq, s1q, s2q, ct, s1t, s2t,
             dq_ref, dk_ref, dv_ref, dks, dvs):
        i = pl.program_id(2)

        @pl.when(i == 0)
        def _():
            dks[...] = jnp.zeros(dks.shape, F32)
            dvs[...] = jnp.zeros(dvs.shape, F32)

        q = q_ref[...]
        do = do_ref[...]
        dob = do.astype(BF16)
        dterm = jnp.sum(do * o_ref[...], axis=-1, keepdims=True)
        lse = jnp.concatenate([l_ref[...]] * (TK_DIL // HEAD_DIM), axis=1)
        last = i // per

        def step(j, dqs):
            off = pl.multiple_of(j * TK_DIL, TK_DIL)
            kj = k_ref[pl.ds(off, TK_DIL), :]
            vj = v_ref[pl.ds(off, TK_DIL), :]
            ps, dss, new = [], [], []
            for n, r in enumerate(_ROW_PARTS):
                s = _dot(q[r], kj, NT) * scale + b_ref[i - per * j, r, :]
                p = jnp.exp(s - lse[r])
                dp = _dot(dob[r], vj, NT)
                ds = (p * (dp - dterm[r]) * scale).astype(BF16)
                new.append(dqs[n] + _dot(ds, kj, NN))
                ps.append(p.astype(BF16))
                dss.append(ds)
            dks[pl.ds(off, TK_DIL), :] += _dot(jnp.concatenate(dss, axis=0), q, TN)
            dvs[pl.ds(off, TK_DIL), :] += _dot(jnp.concatenate(ps, axis=0), dob, TN)
            return tuple(new)

        zero = jnp.zeros((_PART, HEAD_DIM), F32)
        dq = jnp.concatenate(lax.fori_loop(0, last + 1, step, (zero,) * len(_ROW_PARTS)), axis=0)
        dq_ref[...] = _rope(dq, cq[...], -s1q[...], -s2q[...]).astype(BF16)

        @pl.when(i == nq - 1)
        def _():
            dk_ref[...] = _rope(dks[...], ct[...], -s1t[...], -s2t[...]).astype(BF16)
            dv_ref[...] = dvs[...].astype(BF16)

    full = lambda base: pl.BlockSpec((None, T, HEAD_DIM), lambda b, h, i: (b, 0, base + h))
    blk = lambda base: pl.BlockSpec((None, TQ, HEAD_DIM), lambda b, h, i: (b, i, base + h))
    tab_q = pl.BlockSpec((None, TQ, HEAD_DIM), lambda b, h, i: (b, i, 0))
    tab_t = pl.BlockSpec((None, T, HEAD_DIM), lambda b, h, i: (b, 0, 0))
    sds = jax.ShapeDtypeStruct((B, T, Hd * HEAD_DIM), BF16)
    return _hosted_call(
        body, name, (B, Hd, nq),
        [blk(0), full(Hd), full(2 * Hd), pl.BlockSpec(bias.shape, lambda b, h, i: (0, 0, 0)),
         blk(0), blk(0), blk(0), tab_q, tab_q, tab_q, tab_t, tab_t, tab_t],
        [blk(0), full(0), full(0)], [sds, sds, sds],
        [pltpu.VMEM((T, HEAD_DIM), F32), pltpu.VMEM((T, HEAD_DIM), F32)],
        (proj3, proj3, proj3, bias, d_o3, o3, lse3, *tabs, *tabs), jobs, ("parallel", "parallel", "arbitrary"))


def _softplus(z):
    return jnp.maximum(z, 0.0) + jnp.log(1.0 + jnp.exp(-jnp.abs(z)))


def _tri_sum(x, m01):
    n = x.shape[0]
    hi = x.astype(BF16)
    lo = (x - hi.astype(F32)).astype(BF16)
    both = _dot(jnp.concatenate([hi, lo], axis=0), m01, NN)
    return both[:n] + both[n:]


def _sb_iota(rows=TQ, first=0):
    row = lax.broadcasted_iota(jnp.int32, (rows, TK_SB), 0) + first
    col = lax.broadcasted_iota(jnp.int32, (rows, TK_SB), 1)
    return row, col


def _sb_fwd(proj3, Hd, Hs, name, jobs=()):
    B, T, N3 = proj3.shape
    nq = T // TQ
    qb, kb_, vb_ = 3 * Hd, 3 * Hd + Hs, 3 * Hd + 2 * Hs
    scale = HEAD_DIM ** -0.5

    def body(q_ref, k_ref, v_ref, o_ref, lt_ref):
        i = pl.program_id(2)
        qs = [q_ref[r, :] for r in _ROW_PARTS]
        row, col = _sb_iota(TK_SB)
        after = jnp.where(row > col, 1.0, 0.0).astype(BF16)
        rows = [_sb_iota(_PART, r.start)[0] for r in _ROW_PARTS]
        colh = _sb_iota(_PART)[1]
        n_chunks = (i + 1) * (TQ // TK_SB)

        def step(t, carry):
            j = n_chunks - 1 - t
            off = pl.multiple_of(j * TK_SB, TK_SB)
            kj = k_ref[pl.ds(off, TK_SB), :]
            vj = v_ref[pl.ds(off, TK_SB), :]
            reach = i * TQ - j * TK_SB
            new = []
            for n in range(len(_ROW_PARTS)):
                acc, cs = carry[n]
                z = _dot(qs[n], kj, NT) * scale
                msk = colh < rows[n] + reach
                sp = _softplus(z)
                ln = jnp.where(msk, -sp, 0.0)
                excl = _tri_sum(ln, after) + cs
                a = jnp.where(msk, jnp.exp(z - sp + excl), 0.0)
                new.append((acc + _dot(a.astype(BF16), vj, NN), cs + jnp.sum(ln, axis=-1, keepdims=True)))
            return tuple(new)

        init = (jnp.zeros((_PART, HEAD_DIM), F32), jnp.zeros((_PART, 1), F32))
        done = lax.fori_loop(0, n_chunks, step, (init,) * len(_ROW_PARTS))
        for n, r in enumerate(_ROW_PARTS):
            o_ref[r, :] = done[n][0]
            lt_ref[r, :] = jnp.broadcast_to(done[n][1], (_PART, HEAD_DIM))

    full = lambda base: pl.BlockSpec((None, T, HEAD_DIM), lambda b, h, i: (b, 0, base + h))
    blk = lambda base: pl.BlockSpec((None, TQ, HEAD_DIM), lambda b, h, i: (b, i, base + h))
    sds = jax.ShapeDtypeStruct((B, T, Hs * HEAD_DIM), F32)
    return _hosted_call(
        body, name, (B, Hs, nq), [blk(qb), full(kb_), full(vb_)], [blk(0), blk(0)], [sds, sds], [],
        (proj3, proj3, proj3), jobs, ("parallel", "parallel", "arbitrary"))


def _sb_bwd(proj3, d_o3, lt3, Hd, Hs, name, jobs=()):
    B, T, N3 = proj3.shape
    nq = T // TQ
    qb, kb_, vb_ = 3 * Hd, 3 * Hd + Hs, 3 * Hd + 2 * Hs
    scale = HEAD_DIM ** -0.5

    def body(q_ref, k_ref, v_ref, do_ref, lt_ref, dq_ref, dk_ref, dv_ref, dks, dvs):
        i = pl.program_id(2)

        @pl.when(i == 0)
        def _():
            dks[...] = jnp.zeros(dks.shape, F32)
            dvs[...] = jnp.zeros(dvs.shape, F32)

        q = q_ref[...]
        dob = do_ref[...].astype(BF16)
        total = jnp.concatenate([lt_ref[...]] * (TK_SB // HEAD_DIM), axis=1)
        row, col = _sb_iota(TK_SB)
        before = jnp.where(row < col, 1.0, 0.0).astype(BF16)
        rows = [_sb_iota(_PART, r.start)[0] for r in _ROW_PARTS]
        colh = _sb_iota(_PART)[1]

        def step(j, carry):
            off = pl.multiple_of(j * TK_SB, TK_SB)
            kj = k_ref[pl.ds(off, TK_SB), :]
            vj = v_ref[pl.ds(off, TK_SB), :]
            reach = i * TQ - j * TK_SB
            a_s, dzs, new = [], [], []
            for n, r in enumerate(_ROW_PARTS):
                dq, pc, gc = carry[n]
                z = _dot(q[r], kj, NT) * scale
                msk = colh < rows[n] + reach
                sp = _softplus(z)
                ln = jnp.where(msk, -sp, 0.0)
                excl = total[r] - (_tri_sum(ln, before) + ln + pc)
                a = jnp.where(msk, jnp.exp(z - sp + excl), 0.0)
                g = a * _dot(dob[r], vj, NT)
                big = _tri_sum(g, before) + gc
                sig = jnp.exp(z - sp)
                dz = (jnp.where(msk, g * (1.0 - sig) - big * sig, 0.0) * scale).astype(BF16)
                new.append((dq + _dot(dz, kj, NN), pc + jnp.sum(ln, axis=-1, keepdims=True),
                            gc + jnp.sum(g, axis=-1, keepdims=True)))
                a_s.append(a.astype(BF16))
                dzs.append(dz)
            dks[pl.ds(off, TK_SB), :] += _dot(jnp.concatenate(dzs, axis=0), q, TN)
            dvs[pl.ds(off, TK_SB), :] += _dot(jnp.concatenate(a_s, axis=0), dob, TN)
            return tuple(new)

        zero1 = jnp.zeros((_PART, 1), F32)
        init = (jnp.zeros((_PART, HEAD_DIM), F32), zero1, zero1)
        done = lax.fori_loop(0, (i + 1) * (TQ // TK_SB), step, (init,) * len(_ROW_PARTS))
        dq_ref[...] = jnp.concatenate([d[0] for d in done], axis=0).astype(BF16)

        @pl.when(i == nq - 1)
        def _():
            dk_ref[...] = dks[...].astype(BF16)
            dv_ref[...] = dvs[...].astype(BF16)

    full = lambda base: pl.BlockSpec((None, T, HEAD_DIM), lambda b, h, i: (b, 0, base + h))
    blk = lambda base: pl.BlockSpec((None, TQ, HEAD_DIM), lambda b, h, i: (b, i, base + h))
    sds = jax.ShapeDtypeStruct((B, T, Hs * HEAD_DIM), BF16)
    return _hosted_call(
        body, name, (B, Hs, nq), [blk(qb), full(kb_), full(vb_), blk(Hd), blk(0)],
        [blk(0), full(0), full(0)], [sds, sds, sds],
        [pltpu.VMEM((T, HEAD_DIM), F32), pltpu.VMEM((T, HEAD_DIM), F32)],
        (proj3, proj3, proj3, d_o3, lt3), jobs, ("parallel", "parallel", "arbitrary"))


def _place():
    x, y, c = lax.axis_index("x"), lax.axis_index("y"), lax.axis_index("c")
    chips = [(1 - x, y), (x, 1 - y), (1 - x, 1 - y)]
    return x, y, c, chips


class _Weight:
    def __init__(self, kind, shard_shape, slot):
        self.kind, self.slot = kind, slot
        self.R, self.C = shard_shape
        if kind == "col":
            assert slot == self.C
            self.full = (self.R, N_CHIPS * slot)
            self.half = (self.R // 2, N_CHIPS * slot)
            self.piece = (self.R // 2, slot)
        else:
            self.full = (N_CHIPS * slot, self.C)
            self.half = (N_CHIPS * slot, self.C // 2)
            self.piece = (self.R, self.C // 2)

    def _rows(self, jj, n):
        return pl.ds(pl.multiple_of(jj * self.slot, 16), n)

    def full_half(self, ref, jj, hc):
        if self.kind == "col":
            return ref.at[pl.ds(hc * (self.R // 2), self.R // 2), pl.ds(pl.multiple_of(jj * self.slot, LANES), self.C)]
        return ref.at[self._rows(jj, self.R), pl.ds(hc * (self.C // 2), self.C // 2)]

    def region_half(self, ref, hc):
        if self.kind == "col":
            return ref.at[pl.ds(hc * (self.R // 2), self.R // 2), :]
        return ref.at[:, pl.ds(hc * (self.C // 2), self.C // 2)]

    def half_piece(self, ref, jj):
        if self.kind == "col":
            return ref.at[:, pl.ds(pl.multiple_of(jj * self.slot, LANES), self.slot)]
        return ref.at[self._rows(jj, self.R), :]


def _cast_into_full(w, geo, j_arr, name):
    R, C = w.shape
    tr = _tile(R, 256, 16)
    if geo.kind == "col":
        o_spec = pl.BlockSpec((tr, C), lambda i, j_ref: (i, j_ref[0]))
    else:
        while geo.slot % tr or R % tr:
            tr -= 16
        spb = geo.slot // tr
        o_spec = pl.BlockSpec((tr, C), lambda i, j_ref: (j_ref[0] * spb + i, 0))

    def body(j_ref, w_ref, o_ref):
        o_ref[...] = w_ref[...].astype(BF16)

    return pl.pallas_call(
        body, name=name,
        grid_spec=pltpu.PrefetchScalarGridSpec(
            num_scalar_prefetch=1, grid=(R // tr,),
            in_specs=[pl.BlockSpec((tr, C), lambda i, j_ref: (i, 0))], out_specs=o_spec),
        out_shape=jax.ShapeDtypeStruct(geo.full, BF16),
        compiler_params=_params(("parallel",)),
    )(j_arr, w)


def _zero_pad(full, geo, name):
    pad = geo.slot - geo.R
    assert geo.kind == "row" and pad > 0 and geo.R % pad == 0 and geo.slot % pad == 0

    def body(f_ref, o_ref):
        o_ref[...] = jnp.zeros(o_ref.shape, BF16)

    return pl.pallas_call(
        body, name=name, grid=(N_CHIPS,), in_specs=[ANY],
        out_specs=pl.BlockSpec((pad, geo.C), lambda jj: ((jj * geo.slot + geo.R) // pad, 0)),
        out_shape=jax.ShapeDtypeStruct(geo.full, BF16), input_output_aliases={0: 0},
        compiler_params=_params(("arbitrary",)),
    )(full)


def _gather_job(full, geo, stage):
    def copies(ins, outs, send, recv):
        x, y, c, chips = _place()
        j = 2 * x + y

        def cp(src, dst, k, to):
            return pltpu.make_async_remote_copy(src_ref=src, dst_ref=dst, send_sem=send.at[k], recv_sem=recv.at[k],
                                                device_id=to, device_id_type=MESH)

        started, landing = [], []
        for k, (cx, cy) in enumerate(chips):
            jk = 2 * cx + cy
            if stage == "ici":
                started.append(cp(geo.full_half(ins[0], j, c), geo.full_half(outs[0], j, c), k, (cx, cy, c)))
                part = geo.full_half(outs[0], jk, c)
            else:
                started.append(cp(geo.full_half(ins[0], jk, c), geo.full_half(outs[0], jk, c), k, (x, y, 1 - c)))
                part = geo.full_half(outs[0], jk, 1 - c)
            landing.append(functools.partial(cp, part, part, k, (x, y, c)))
        return started, landing

    return _Job([full], [jax.ShapeDtypeStruct(geo.full, BF16)], {0: 0}, 3, copies)


def _sibling_job(grad_full, geo):
    def copies(ins, outs, send, recv):
        x, y, c, _ = _place()
        mk = lambda src, to: pltpu.make_async_remote_copy(
            src_ref=src, dst_ref=outs[0], send_sem=send.at[0], recv_sem=recv.at[0], device_id=to, device_id_type=MESH)
        return [mk(geo.region_half(ins[0], 1 - c), (x, y, 1 - c))], [functools.partial(mk, outs[0], (x, y, c))]

    return _Job([grad_full], [jax.ShapeDtypeStruct(geo.half, BF16)], {}, 1, copies)


def _chips_job(chip_sum, geo):
    def copies(ins, outs, send, recv):
        x, y, c, chips = _place()
        started, landing = [], []
        def mk(k, src, to):
            return pltpu.make_async_remote_copy(src_ref=src, dst_ref=outs[0].at[k], send_sem=send.at[k],
                                                recv_sem=recv.at[k], device_id=to, device_id_type=MESH)

        for k, (cx, cy) in enumerate(chips):
            started.append(mk(k, geo.half_piece(ins[0], 2 * cx + cy), (cx, cy, c)))
            landing.append(functools.partial(mk, k, outs[0].at[k], (x, y, c)))
        return started, landing

    return _Job([chip_sum], [jax.ShapeDtypeStruct((3,) + geo.piece, BF16)], {}, 3, copies)


def _final_job(half, geo):
    def copies(ins, outs, send, recv):
        x, y, c, _ = _place()
        mk = lambda src, to: pltpu.make_async_remote_copy(
            src_ref=src, dst_ref=outs[0], send_sem=send.at[0], recv_sem=recv.at[0], device_id=to, device_id_type=MESH)
        return [mk(ins[0], (x, y, 1 - c))], [functools.partial(mk, outs[0], (x, y, c))]

    return _Job([half], [jax.ShapeDtypeStruct(geo.piece, F32)], {}, 1, copies)


def _all_gather_weights(fulls, geo):
    nw = len(fulls)

    def body(*refs):
        own_refs, f_refs = refs[:nw], refs[nw:2 * nw]
        send, recv = refs[2 * nw:]
        x, y, c, chips = _place()
        j = 2 * x + y

        def remote(w, k, jj, hc, to, own=False):
            part = geo[w].full_half(f_refs[w], jj, hc)
            return pltpu.make_async_remote_copy(
                src_ref=geo[w].full_half(own_refs[w], jj, hc) if own else part, dst_ref=part,
                send_sem=send.at[w, k], recv_sem=recv.at[w, k], device_id=to, device_id_type=MESH)

        sends = []
        for w in range(nw):
            for k, chip in enumerate(chips):
                cp = remote(w, k, j, c, (*chip, c), own=True)
                cp.start()
                sends.append(cp)
        for w in range(nw):
            for k, (cx, cy) in enumerate(chips):
                remote(w, k, 2 * cx + cy, c, (x, y, c)).wait_recv()
                cp = remote(w, 3 + k, 2 * cx + cy, c, (x, y, 1 - c))
                cp.start()
                sends.append(cp)
        for w in range(nw):
            for k, (cx, cy) in enumerate(chips):
                remote(w, 3 + k, 2 * cx + cy, 1 - c, (x, y, c)).wait_recv()
        for cp in sends:
            cp.wait_send()

    return pl.pallas_call(
        body, name="all_gather_weights",
        in_specs=[ANY] * nw, out_specs=[ANY] * nw,
        out_shape=[jax.ShapeDtypeStruct(g.full, BF16) for g in geo],
        input_output_aliases={w: w for w in range(nw)},
        scratch_shapes=[pltpu.SemaphoreType.DMA((nw, 6)), pltpu.SemaphoreType.DMA((nw, 6))],
    )(*fulls)


def _chip_sum(g_full, got, geo, c_arr, name):
    Rh, Ch = geo.half
    tr, tc = _tile(Rh, 256, 16), _tile(Ch, 2048)
    nrb, ncb = Rh // tr, Ch // tc

    def body(c_ref, a_ref, b_ref, o_ref):
        o_ref[...] = (a_ref[...].astype(F32) + b_ref[...].astype(F32)).astype(BF16)

    if geo.kind == "col":
        a_spec = pl.BlockSpec((tr, tc), lambda i, j, c_ref: (c_ref[0] * nrb + i, j))
    else:
        a_spec = pl.BlockSpec((tr, tc), lambda i, j, c_ref: (i, c_ref[0] * ncb + j))
    spec = pl.BlockSpec((tr, tc), lambda i, j, c_ref: (i, j))
    return pl.pallas_call(
        body, name=name,
        grid_spec=pltpu.PrefetchScalarGridSpec(num_scalar_prefetch=1, grid=(nrb, ncb),
                                               in_specs=[a_spec, spec], out_specs=spec),
        out_shape=jax.ShapeDtypeStruct(geo.half, BF16),
        compiler_params=_params(("parallel", "parallel")),
    )(c_arr, g_full, got)


def _final_sum(chip_sum, got, geo, j_arr, name):
    Rp, Cp = geo.piece
    if geo.kind == "col":
        tr = _tile(Rp, 128, 16)
        a_spec = pl.BlockSpec((tr, Cp), lambda i, j_ref: (i, j_ref[0]))
    else:
        tr = _tile(Rp, 128, 16)
        while geo.slot % tr:
            tr -= 16
        assert Rp % tr == 0 and geo.slot % tr == 0
        spb = geo.slot // tr
        a_spec = pl.BlockSpec((tr, Cp), lambda i, j_ref: (j_ref[0] * spb + i, 0))
    r_specs = [pl.BlockSpec((None, tr, Cp), functools.partial(lambda i, j_ref, k: (k, i, 0), k=k)) for k in range(3)]
    o_spec = pl.BlockSpec((tr, Cp), lambda i, j_ref: (i, 0))

    def body(j_ref, a_ref, r0, r1, r2, o_ref):
        o_ref[...] = ((a_ref[...].astype(F32) + r0[...].astype(F32)) + r1[...].astype(F32)) + r2[...].astype(F32)

    return pl.pallas_call(
        body, name=name,
        grid_spec=pltpu.PrefetchScalarGridSpec(num_scalar_prefetch=1, grid=(Rp // tr,),
                                               in_specs=[a_spec] + r_specs, out_specs=o_spec),
        out_shape=jax.ShapeDtypeStruct(geo.piece, F32),
        compiler_params=_params(("parallel",)),
    )(j_arr, chip_sum, got, got, got)


def _run_job(job, name):
    ni = len(job.inputs)

    def body(*refs):
        ins, outs, (send, recv) = refs[:ni], refs[ni:-2], refs[-2:]
        job.start(ins, outs, send, recv)
        job.finish(ins, outs, send, recv)

    return pl.pallas_call(
        body, name=name, in_specs=[ANY] * ni, out_specs=[ANY] * len(job.outputs), out_shape=list(job.outputs),
        input_output_aliases=dict(job.aliases),
        scratch_shapes=[pltpu.SemaphoreType.DMA((job.nsem,)), pltpu.SemaphoreType.DMA((job.nsem,))],
    )(*job.inputs)


def _adam_math(w, g, m, v):
    m = ADAM_B1 * m + (1.0 - ADAM_B1) * g
    v = ADAM_B2 * v + (1.0 - ADAM_B2) * (g * g)
    m_hat = m / (1.0 - ADAM_B1 ** ADAM_STEP)
    v_hat = v / (1.0 - ADAM_B2 ** ADAM_STEP)
    delta = -ADAM_LR * (m_hat / (jnp.sqrt(v_hat) + ADAM_EPS) + ADAM_WD * w)
    return delta, m, v


def _adam(ws, mines, theirs, ms, vs, geo, name, jobs=()):
    n = len(ws)
    R, C = ws[0].shape
    tr = _tile(R, 128 if n == 1 else 32, 8)
    nrb = R // tr
    col = geo.kind == "col"
    assert not col or nrb % 2 == 0

    def rows(k, halved):
        def index(w, i):
            r = jnp.where(w == k, i, jnp.where(w < k, 0, nrb - 1))
            return (r % (nrb // 2) if halved else r, 0)
        return index

    def body(*refs):
        ins, outs = refs[:5 * n], refs[5 * n:]
        wi, i = pl.program_id(0), pl.program_id(1)
        c = lax.axis_index("c")
        for k in range(n):
            @pl.when(wi == k)
            def _():
                w_ref, a_ref, b_ref, m_ref, v_ref = ins[5 * k:5 * k + 5]
                if col:
                    gv = jnp.where(i // (nrb // 2) == c, a_ref[...], b_ref[...])
                else:
                    a, b = a_ref[...], b_ref[...]
                    gv = jnp.concatenate([jnp.where(c == 0, a, b), jnp.where(c == 0, b, a)], axis=1)
                delta, mn, vn = _adam_math(w_ref[...], gv, m_ref[...], v_ref[...])
                og, od, om, ov = outs[4 * k:4 * k + 4]
                og[...] = gv
                od[...] = delta
                om[...] = mn
                ov[...] = vn

    in_specs, out_specs, args = [], [], []
    for k in range(n):
        spec = pl.BlockSpec((tr, C), rows(k, False))
        h_spec = pl.BlockSpec((tr, C), rows(k, True)) if col else pl.BlockSpec((tr, C // 2), rows(k, False))
        in_specs += [spec, h_spec, h_spec, spec, spec]
        out_specs += [spec] * 4
        args += [ws[k], mines[k], theirs[k], ms[k], vs[k]]
    outs, job_outs = _hosted_call(body, name, (n, nrb), in_specs, out_specs,
                                  [jax.ShapeDtypeStruct((R, C), F32)] * (4 * n), [], args, jobs,
                                  ("arbitrary", "arbitrary"))
    return [outs[4 * k:4 * k + 4] for k in range(n)], job_outs


def _small_all_reduce(parts, loss_part, D):
    n = len(parts)

    def body(*refs):
        p_refs, l_ref, o_ref, vec, buf, send, recv = refs[:n], refs[n], refs[n + 1], *refs[n + 2:]
        x, y, c, _ = _place()
        me = 4 * x + 2 * y + c
        vec[...] = jnp.zeros(vec.shape, F32)
        for r in range(n):
            vec[r:r + 1, :] = jnp.sum(p_refs[r][...], axis=0, keepdims=True)
        vec[n:n + 1, 0:LANES] = l_ref[0:1, :]
        buf[me] = vec[...]
        cps = []
        for dd in range(1, 8):
            bx, by, bc = (dd >> 2) & 1, (dd >> 1) & 1, dd & 1
            peer = (x + bx - 2 * x * bx, y + by - 2 * y * by, c + bc - 2 * c * bc)
            cp = pltpu.make_async_remote_copy(
                src_ref=vec, dst_ref=buf.at[me], send_sem=send.at[dd - 1], recv_sem=recv.at[dd - 1],
                device_id=peer, device_id_type=MESH)
            cp.start()
            cps.append(cp)
        for cp in cps:
            cp.wait()
        tot = buf[0]
        for s in range(1, 8):
            tot = tot + buf[s]
        o_ref[...] = tot

    vm = pl.BlockSpec(memory_space=pltpu.VMEM)
    return pl.pallas_call(
        body, name="small_all_reduce",
        in_specs=[vm] * (n + 1), out_specs=vm,
        out_shape=jax.ShapeDtypeStruct((8, D), F32),
        scratch_shapes=[pltpu.VMEM((8, D), F32), pltpu.VMEM((8, 8, D), F32),
                        pltpu.SemaphoreType.DMA((7,)), pltpu.SemaphoreType.DMA((7,))],
    )(*parts, loss_part)


def _small_adam(tot, ws, ms, vs, rows):
    n = len(ws)

    def body(*refs):
        t_ref = refs[0]
        w_refs, m_refs, v_refs = refs[1:1 + n], refs[1 + n:1 + 2 * n], refs[1 + 2 * n:1 + 3 * n]
        outs = refs[1 + 3 * n:]
        for i in range(n):
            r, c0 = rows[i]
            width = w_refs[i].shape[1]
            g = t_ref[r:r + 1, c0:c0 + width]
            delta, mn, vn = _adam_math(w_refs[i][...], g, m_refs[i][...], v_refs[i][...])
            outs[4 * i][...] = g
            outs[4 * i + 1][...] = delta
            outs[4 * i + 2][...] = mn
            outs[4 * i + 3][...] = vn

    vm = pl.BlockSpec(memory_space=pltpu.VMEM)
    out_shape = []
    for w in ws:
        out_shape += [jax.ShapeDtypeStruct(w.shape, F32)] * 4
    return pl.pallas_call(
        body, name="small_adam",
        in_specs=[vm] * (1 + 3 * n), out_specs=[vm] * (4 * n), out_shape=out_shape,
    )(tot, *ws, *ms, *vs)


def kernel(x, positions, norm_mix_g, w_in, norm_out_dil_g, norm_out_sb_g, w_out, norm_ffn_g, w_gate, w_up, w_down, norm_final_g, loss_target, m_norm_mix_g, m_w_in, m_norm_out_dil_g, m_norm_out_sb_g, m_w_out, m_norm_ffn_g, m_w_gate, m_w_up, m_w_down, m_norm_final_g, v_norm_mix_g, v_w_in, v_norm_out_dil_g, v_norm_out_sb_g, v_w_out, v_norm_ffn_g, v_w_gate, v_w_up, v_w_down, v_norm_final_g):
    B, T, D = x.shape
    M = B * T
    n_heads = D // HEAD_DIM
    Hd = n_heads // 2
    Hs = n_heads - Hd
    W = Hd * HEAD_DIM
    N3 = 3 * D
    fs = w_gate.shape[2]
    fp = -(-fs // LANES) * LANES
    assert T % TK_DIL == 0 and W == Hs * HEAD_DIM and fp > fs

    x2 = x.reshape(M, D)
    tgt = loss_target.reshape(M, D)
    tr_ = jnp.transpose
    big = [w_in[0], w_out[0], tr_(w_gate[0]), tr_(w_up[0]), w_down[0]]
    big_m = [m_w_in[0], m_w_out[0], tr_(m_w_gate[0]), tr_(m_w_up[0]), m_w_down[0]]
    big_v = [v_w_in[0], v_w_out[0], tr_(v_w_gate[0]), tr_(v_w_up[0]), v_w_down[0]]
    names = ["w_in", "w_out", "w_gate", "w_up", "w_down"]
    c_arr = jnp.reshape(lax.axis_index("c"), (1,)).astype(jnp.int32)
    j_arr = jnp.reshape(2 * lax.axis_index("x") + lax.axis_index("y"), (1,)).astype(jnp.int32)

    ns_in = w_in.shape[2]
    ks_out = w_out.shape[1]
    geo = [_Weight("col", (D, ns_in), ns_in), _Weight("row", (ks_out, D), ks_out),
           _Weight("row", (fs, D), fp), _Weight("row", (fs, D), fp), _Weight("row", (fs, D), fp)]
    g_in, g_out, g_gate, g_up, g_down = geo
    fulls = [_cast_into_full(big[w], geo[w], j_arr, "cast_" + names[w]) for w in range(5)]
    for w in (2, 3, 4):
        fulls[w] = _zero_pad(fulls[w], geo[w], "zero_pad_" + names[w])

    def chip_sum(w, grad_full, from_sibling):
        return _chip_sum(grad_full, from_sibling, geo[w], c_arr, "chip_sum_" + names[w])

    def final_sum(w, cs, from_chips):
        return _final_sum(cs, from_chips, geo[w], j_arr, "final_sum_" + names[w])

    (Win,) = _all_gather_weights([fulls[0]], [g_in])
    tabs3 = _rope_tables(positions)
    tabs2 = [t.reshape(M, HEAD_DIM) for t in tabs3]
    hn = _rmsnorm_fwd(x2, norm_mix_g, "norm_mix")
    proj, (WgT,) = _matmul(hn, Win, "nn", BF16, "in_proj", rope=(tabs2, 2 * W),
                           jobs=[_gather_job(fulls[2], g_gate, "ici")])
    proj3 = proj.reshape(B, T, N3)
    bias = _dil_bias(T)
    (o_dil3, lse3), (WgT, Wout) = _dil_fwd(
        proj3, bias, Hd, "dil_fwd", jobs=[_gather_job(WgT, g_gate, "d2d"), _gather_job(fulls[1], g_out, "ici")])
    (o_sb3, lt3), (Wout, WuT) = _sb_fwd(
        proj3, Hd, Hs, "sb_fwd", jobs=[_gather_job(Wout, g_out, "d2d"), _gather_job(fulls[3], g_up, "ici")])
    o_dil, o_sb = o_dil3.reshape(M, W), o_sb3.reshape(M, W)
    g_heads = jnp.concatenate([norm_out_dil_g, norm_out_sb_g], axis=1)
    o_mix = _headnorm_fwd(o_dil, o_sb, g_heads, "head_norm")
    h1, (WuT,) = _matmul(o_mix, Wout, "nn", F32, "out_proj", res=x2, jobs=[_gather_job(WuT, g_up, "d2d")])
    hn2 = _rmsnorm_fwd(h1, norm_ffn_g, "norm_ffn")
    (gate, up, act), (Wd,) = _gate_up(hn2, WgT, WuT, "gate_up", jobs=[_gather_job(fulls[4], g_down, "ici")])
    (Wd,) = _run_job(_gather_job(Wd, g_down, "d2d"), "gather_w_down_d2d")
    h2 = _matmul(act, Wd, "nn", F32, "down_proj", res=h1)
    dh2, dh2_b, loss_part, dg_final = _loss_head(h2, tgt, norm_final_g.reshape(1, D), "loss_head")

    dWd = _matmul(act, dh2_b, "tn", BF16, "dw_down")
    (d_gate, d_up), (sib_d,) = _matmul(dh2_b, Wd, "nt", BF16, "d_act", swiglu=(gate, up),
                                       jobs=[_sibling_job(dWd, g_down)])
    cs_d = chip_sum(4, dWd, sib_d)
    dWg, (chips_d,) = _matmul(d_gate, hn2, "tn", BF16, "dw_gate", jobs=[_chips_job(cs_d, g_down)])
    half_d = final_sum(4, cs_d, chips_d)
    dWu, (sib_g, other_d) = _matmul(d_up, hn2, "tn", BF16, "dw_up",
                                    jobs=[_sibling_job(dWg, g_gate), _final_job(half_d, g_down)])
    cs_g = chip_sum(2, dWg, sib_g)
    d_hn2, (chips_g,) = _matmul(d_gate, WgT, "nn", F32, "d_hn2_gate", jobs=[_chips_job(cs_g, g_gate)])
    half_g = final_sum(2, cs_g, chips_g)
    d_hn2, (sib_u, other_g) = _matmul(d_up, WuT, "nn", F32, "d_hn2_up", res=d_hn2,
                                      jobs=[_sibling_job(dWu, g_up), _final_job(half_g, g_gate)])
    cs_u = chip_sum(3, dWu, sib_u)
    dh1, dh1_b, dg_ffn = _rmsnorm_bwd(h1, d_hn2, norm_ffn_g, dh2, "norm_ffn_bwd", True)
    dWo = _matmul(o_mix, dh1_b, "tn", BF16, "dw_out")
    d_mix, (sib_o,) = _matmul(dh1_b, Wout, "nt", BF16, "d_mix", jobs=[_sibling_job(dWo, g_out)])
    cs_o = chip_sum(1, dWo, sib_o)
    d_o, dg_heads = _headnorm_bwd(d_mix, o_dil, o_sb, g_heads, "head_norm_bwd")
    d_o3 = d_o.reshape(B, T, D)
    dqkv_dil, (chips_u,) = _dil_bwd(proj3, bias, tabs3, d_o3, o_dil3, lse3, Hd, "dil_bwd",
                                    jobs=[_chips_job(cs_u, g_up)])
    half_u = final_sum(3, cs_u, chips_u)
    dqkv_sb, (chips_o, other_u) = _sb_bwd(proj3, d_o3, lt3, Hd, Hs, "sb_bwd",
                                          jobs=[_chips_job(cs_o, g_out), _final_job(half_u, g_up)])
    half_o = final_sum(1, cs_o, chips_o)
    dproj = jnp.concatenate([*dqkv_dil, *dqkv_sb], axis=-1).reshape(M, N3)
    dWin, (other_o,) = _matmul(hn, dproj, "tn", BF16, "dw_in", jobs=[_final_job(half_o, g_out)])
    (sib_i,) = _run_job(_sibling_job(dWin, g_in), "sibling_w_in")
    cs_i = chip_sum(0, dWin, sib_i)
    d_hn, (chips_i,) = _matmul(dproj, Win, "nt", F32, "d_hn", jobs=[_chips_job(cs_i, g_in)])
    half_i = final_sum(0, cs_i, chips_i)
    dx, dg_mix = _rmsnorm_bwd(x2, d_hn, norm_mix_g, dh1, "norm_mix_bwd", False)
    (other_i,) = _run_job(_final_job(half_i, g_in), "final_w_in")

    mine = [half_i, half_o, half_g, half_u, half_d]
    theirs = [other_i, other_o, other_g, other_u, other_d]
    big_out = [_adam([big[w]], [mine[w]], [theirs[w]], [big_m[w]], [big_v[w]], geo[w], "adam_" + names[w])[0][0]
               for w in range(5)]
    for w in (2, 3):
        big_out[w] = [tr_(a) for a in big_out[w]]

    tot = _small_all_reduce([dg_mix, dg_heads, dg_ffn, dg_final], loss_part, D)
    loss = tot[4, 0]
    small_w = [norm_mix_g, norm_out_dil_g, norm_out_sb_g, norm_ffn_g, norm_final_g.reshape(1, D)]
    small_m = [m_norm_mix_g, m_norm_out_dil_g, m_norm_out_sb_g, m_norm_ffn_g, m_norm_final_g.reshape(1, D)]
    small_v = [v_norm_mix_g, v_norm_out_dil_g, v_norm_out_sb_g, v_norm_ffn_g, v_norm_final_g.reshape(1, D)]
    so = _small_adam(tot, small_w, small_m, small_v, [(0, 0), (1, 0), (1, W), (2, 0), (3, 0)])
    small_out = [so[4 * i:4 * i + 4] for i in range(5)]
    small_out[4] = [a.reshape(D) for a in small_out[4]]

    per_weight = [small_out[0], big_out[0], small_out[1], small_out[2], big_out[1], small_out[3],
                  big_out[2], big_out[3], big_out[4], small_out[4]]

    def field(i):
        res = []
        for n_, o in enumerate(per_weight):
            a = o[i]
            res.append(a[None] if n_ in (1, 4, 6, 7, 8) else a)
        return res

    return (loss, dx.reshape(B, T, D), *field(0), *field(1), *field(2), *field(3))
```

```python
import functools
import math

import jax
import jax.numpy as jnp
from jax import lax
from jax.experimental import pallas as pl
from jax.experimental.pallas import tpu as pltpu

F32 = jnp.float32
BF16 = jnp.bfloat16
MESH = pl.DeviceIdType.MESH
ANY = pl.BlockSpec(memory_space=pl.ANY)

LANES = 128
HEAD_DIM = 128
DIL_STEPS = 128
DILATIONS = (1, 4, 16)
TQ = 512
TK_DIL = 512
TK_SB = 256
NEG = -1e30
_PART = TQ
_ROW_PARTS = tuple(slice(p, p + _PART) for p in range(0, TQ, _PART))
ROPE_DIM = 32
ROPE_THETA = 500000.0
RMS_EPS = 1e-5
ADAM_LR, ADAM_B1, ADAM_B2, ADAM_EPS, ADAM_WD, ADAM_STEP = 0.001, 0.9, 0.999, 1e-08, 0.01, 10
N_CHIPS = 4
VMEM_LIMIT = 56 * 1024 * 1024
MATMUL_VMEM = 52 * 1024 * 1024

NN = ((1,), (0,))
NT = ((1,), (1,))
TN = ((0,), (0,))


def _dot(a, b, dims):
    return lax.dot_general(a, b, (dims, ((), ())), preferred_element_type=F32)


def _tile(dim, pref, unit=LANES):
    if dim <= pref:
        return dim
    t = (pref // unit) * unit
    while t > unit and dim % t:
        t -= unit
    assert dim % t == 0, (dim, pref, unit)
    return t


def _params(sem=None):
    return pltpu.CompilerParams(dimension_semantics=sem, vmem_limit_bytes=VMEM_LIMIT)


class _Job:
    def __init__(self, inputs, outputs, aliases, nsem, copies):
        self.inputs, self.outputs, self.aliases, self.nsem, self.copies = inputs, outputs, aliases, nsem, copies

    def start(self, ins, outs, send, recv):
        for cp in self.copies(ins, outs, send, recv)[0]:
            cp.start()

    def finish(self, ins, outs, send, recv):
        started, landing = self.copies(ins, outs, send, recv)
        for make in landing:
            make().wait_recv()
        for cp in started:
            cp.wait_send()


def _hosted_call(body, name, grid, in_specs, out_specs, out_shape, scratch_shapes, args, jobs, semantics):
    nbi, nbo, nbs = len(in_specs), len(out_specs), len(scratch_shapes)
    if not jobs:
        res = pl.pallas_call(
            body, name=name, grid=grid, in_specs=list(in_specs), out_specs=list(out_specs), out_shape=list(out_shape),
            scratch_shapes=list(scratch_shapes), compiler_params=_params(semantics))(*args)
        return list(res), []
    j_in = [a for jb in jobs for a in jb.inputs]
    j_out = [o for jb in jobs for o in jb.outputs]
    aliases, ii, oo, sems = {}, nbi, nbo, []
    for jb in jobs:
        for a, b in jb.aliases.items():
            aliases[ii + a] = oo + b
        ii += len(jb.inputs)
        oo += len(jb.outputs)
        sems += [pltpu.SemaphoreType.DMA((jb.nsem,)), pltpu.SemaphoreType.DMA((jb.nsem,))]

    def wrapped(*refs):
        p = nbi + len(j_in)
        b_in, ji = refs[:nbi], refs[nbi:p]
        b_out, jo = refs[p:p + nbo], refs[p + nbo:p + nbo + len(j_out)]
        p += nbo + len(j_out)
        b_scr, js = refs[p:p + nbs], refs[p + nbs:]
        pids = [pl.program_id(k) for k in range(len(grid))]
        first = functools.reduce(jnp.logical_and, [pid == 0 for pid in pids])
        last = functools.reduce(jnp.logical_and, [pid == g - 1 for pid, g in zip(pids, grid)])

        def each(what):
            a = b = 0
            for n, jb in enumerate(jobs):
                getattr(jb, what)(ji[a:a + len(jb.inputs)], jo[b:b + len(jb.outputs)], js[2 * n], js[2 * n + 1])
                a += len(jb.inputs)
                b += len(jb.outputs)

        @pl.when(first)
        def _():
            each("start")

        body(*b_in, *b_out, *b_scr)

        @pl.when(last)
        def _():
            each("finish")

    res = pl.pallas_call(
        wrapped, name=name, grid=grid,
        in_specs=list(in_specs) + [ANY] * len(j_in), out_specs=list(out_specs) + [ANY] * len(j_out),
        out_shape=list(out_shape) + j_out, input_output_aliases=aliases,
        scratch_shapes=list(scratch_shapes) + sems,
        compiler_params=_params(("arbitrary",) * len(grid)))(*args, *j_in)
    return list(res[:nbo]), list(res[nbo:])


def _rope(a, c, s1, s2):
    half = ROPE_DIM // 2
    return a * c + pltpu.roll(a, HEAD_DIM - half, 1) * s1 + pltpu.roll(a, half, 1) * s2


def _matmul(a, b, mode, out_dtype, name, res=None, rope=None, swiglu=None, jobs=(), tm=1024, tn=1024, tk=2816):
    if mode == "nn":
        (M, K), (_, N) = a.shape, b.shape
    elif mode == "nt":
        (M, K), (N, _) = a.shape, b.shape
    else:
        (K, M), (_, N) = a.shape, b.shape
    tm = _tile(M, tm)
    tn = _tile(N, tn) if rope is None else _tile(math.gcd(N, rope[1]), tn)
    n_out = 2 if swiglu is not None else 1

    def vmem_bytes(t):
        osz = jnp.dtype(out_dtype).itemsize
        return (4 * t * (tm + tn) + 4 * tm * tn + (4 * tm * tn if t < K else 0) + 2 * n_out * tm * tn * osz
                + (8 * tm * tn if res is not None else 0) + (8 * tm * tn if swiglu is not None else 0)
                + (4 * tm * tn + 24 * tm * HEAD_DIM if rope is not None else 0))

    whole_k = () if swiglu is not None else (K,)
    tk = next(t for t in whole_k + (_tile(K, tk), _tile(K, 2048), _tile(K, 1024)) if vmem_bytes(t) <= MATMUL_VMEM)
    nk = K // tk
    dims = {"nn": NN, "nt": NT, "tn": TN}[mode]
    a_spec = (pl.BlockSpec((tk, tm), lambda i, j, k: (k, i)) if mode == "tn"
              else pl.BlockSpec((tm, tk), lambda i, j, k: (i, k)))
    b_spec = (pl.BlockSpec((tn, tk), lambda i, j, k: (j, k)) if mode == "nt"
              else pl.BlockSpec((tk, tn), lambda i, j, k: (k, j)))
    o_spec = pl.BlockSpec((tm, tn), lambda i, j, k: (i, j))
    n_extra = (1 if res is not None else 0) + (3 if rope is not None else 0) + (2 if swiglu is not None else 0)
    use_acc = nk > 1 or rope is not None
    if rope is not None:
        assert rope[1] % tn == 0

    def body(*refs):
        a_ref, b_ref = refs[:2]
        extra = refs[2:2 + n_extra]
        o_ref = refs[2 + n_extra]
        k = pl.program_id(2)
        prod = _dot(a_ref[...], b_ref[...], dims)
        if use_acc:
            acc_ref = refs[-1]
            if nk > 1:
                @pl.when(k == 0)
                def _():
                    acc_ref[...] = jnp.zeros(acc_ref.shape, F32)

                acc_ref[...] += prod
            else:
                acc_ref[...] = prod
            total = lambda: acc_ref[...]
        else:
            total = lambda: prod
        at_end = pl.when(k == nk - 1) if nk > 1 else (lambda f: f())

        if swiglu is not None:
            @at_end
            def _():
                d = total()
                g = extra[0][...].astype(F32)
                u = extra[1][...].astype(F32)
                sig = jax.nn.sigmoid(g)
                o_ref[...] = (d * u * sig * (1.0 + g * (1.0 - sig))).astype(out_dtype)
                refs[3 + n_extra][...] = (d * g * sig).astype(out_dtype)
        elif rope is None:
            @at_end
            def _():
                out = total()
                if res is not None:
                    out = out + extra[0][...]
                o_ref[...] = out.astype(out_dtype)
        else:
            roped = pl.program_id(1) * tn < rope[1]

            @pl.when(jnp.logical_and(k == nk - 1, roped))
            def _():
                c, s1, s2 = extra[0][...], extra[1][...], extra[2][...]
                for h in range(tn // HEAD_DIM):
                    cols = slice(h * HEAD_DIM, (h + 1) * HEAD_DIM)
                    o_ref[:, cols] = _rope(acc_ref[:, cols], c, s1, s2).astype(out_dtype)

            @pl.when(jnp.logical_and(k == nk - 1, jnp.logical_not(roped)))
            def _():
                o_ref[...] = acc_ref[...].astype(out_dtype)

    in_specs, args = [a_spec, b_spec], [a, b]
    if res is not None:
        in_specs.append(o_spec)
        args.append(res)
    if rope is not None:
        in_specs += [pl.BlockSpec((tm, HEAD_DIM), lambda i, j, k: (i, 0))] * 3
        args += list(rope[0])
    if swiglu is not None:
        in_specs += [o_spec, o_spec]
        args += list(swiglu)
    outs, extra_out = _hosted_call(
        body, name, (M // tm, N // tn, nk), in_specs, [o_spec] * n_out,
        [jax.ShapeDtypeStruct((M, N), out_dtype)] * n_out,
        [pltpu.VMEM((tm, tn), F32)] if use_acc else [], args, jobs, ("parallel", "parallel", "arbitrary"))
    out = outs[0] if n_out == 1 else tuple(outs)
    return (out, extra_out) if jobs else out


def _gate_up(hn, wg, wu, name, jobs=()):
    M, K = hn.shape
    N = wg.shape[0]
    tm, tn, tk = _tile(M, 1024), _tile(N, 512), _tile(K, 4096)
    nk = K // tk

    def body(a_ref, g_ref, u_ref, og_ref, ou_ref, oa_ref, *accs):
        k = pl.program_id(2)
        a = a_ref[...]
        pg = _dot(a, g_ref[...], NT)
        pu = _dot(a, u_ref[...], NT)

        def finish(g, u):
            og_ref[...] = g.astype(BF16)
            ou_ref[...] = u.astype(BF16)
            oa_ref[...] = (g * jax.nn.sigmoid(g) * u).astype(BF16)

        if nk == 1:
            finish(pg, pu)
        else:
            accg, accu = accs

            @pl.when(k == 0)
            def _():
                accg[...] = jnp.zeros(accg.shape, F32)
                accu[...] = jnp.zeros(accu.shape, F32)

            accg[...] += pg
            accu[...] += pu

            @pl.when(k == nk - 1)
            def _():
                finish(accg[...], accu[...])

    w_spec = pl.BlockSpec((tn, tk), lambda i, j, k: (j, k))
    o_spec = pl.BlockSpec((tm, tn), lambda i, j, k: (i, j))
    sds = jax.ShapeDtypeStruct((M, N), BF16)
    return _hosted_call(
        body, name, (M // tm, N // tn, nk), [pl.BlockSpec((tm, tk), lambda i, j, k: (i, k)), w_spec, w_spec],
        [o_spec, o_spec, o_spec], [sds, sds, sds],
        [pltpu.VMEM((tm, tn), F32), pltpu.VMEM((tm, tn), F32)] if nk > 1 else [],
        (hn, wg, wu), jobs, ("parallel", "parallel", "arbitrary"))


def _rmsnorm_fwd(x, g, name):
    M, D = x.shape
    tm = _tile(M, 256, 16)

    def body(x_ref, g_ref, o_ref):
        xv = x_ref[...]
        r = lax.rsqrt(jnp.mean(xv * xv, axis=-1, keepdims=True) + RMS_EPS)
        o_ref[...] = (xv * r * g_ref[...]).astype(BF16)

    return pl.pallas_call(
        body, name=name, grid=(M // tm,),
        in_specs=[pl.BlockSpec((tm, D), lambda i: (i, 0)), pl.BlockSpec((1, D), lambda i: (0, 0))],
        out_specs=pl.BlockSpec((tm, D), lambda i: (i, 0)),
        out_shape=jax.ShapeDtypeStruct((M, D), BF16),
        compiler_params=_params(("parallel",)),
    )(x, g)


def _fold8(v):
    tm, D = v.shape
    return jnp.sum(v.reshape(tm // 8, 8, D), axis=0)


def _rmsnorm_bwd(x, dy, g, res, name, want_bf16):
    M, D = x.shape
    tm = _tile(M, 128, 16)

    def body(x_ref, dy_ref, g_ref, r_ref, *outs):
        dx_ref, dg_ref = outs[0], outs[-1]
        xv = x_ref[...]
        dyv = dy_ref[...].astype(F32)
        r = lax.rsqrt(jnp.mean(xv * xv, axis=-1, keepdims=True) + RMS_EPS)
        u = dyv * g_ref[...]
        dot = jnp.sum(xv * u, axis=-1, keepdims=True)
        dx = r * u - xv * (r * r * r * (1.0 / D)) * dot + r_ref[...]
        dx_ref[...] = dx
        if want_bf16:
            outs[1][...] = dx.astype(BF16)
        part = _fold8(dyv * xv * r)

        @pl.when(pl.program_id(0) == 0)
        def _():
            dg_ref[...] = part

        @pl.when(pl.program_id(0) > 0)
        def _():
            dg_ref[...] += part

    row = pl.BlockSpec((tm, D), lambda i: (i, 0))
    out_specs = [row] + ([row] if want_bf16 else []) + [pl.BlockSpec((8, D), lambda i: (0, 0))]
    out_shape = ([jax.ShapeDtypeStruct((M, D), F32)] + ([jax.ShapeDtypeStruct((M, D), BF16)] if want_bf16 else [])
                 + [jax.ShapeDtypeStruct((8, D), F32)])
    return pl.pallas_call(
        body, name=name, grid=(M // tm,),
        in_specs=[row, row, pl.BlockSpec((1, D), lambda i: (0, 0)), row],
        out_specs=out_specs, out_shape=out_shape,
        compiler_params=_params(("arbitrary",)),
    )(x, dy, g, res)


def _loss_head(h, tgt, g, name):
    M, D = h.shape
    tm = _tile(M, 128, 16)

    def body(h_ref, t_ref, g_ref, dh_ref, dhb_ref, l_ref, dg_ref):
        hv = h_ref[...]
        gv = g_ref[...]
        r = lax.rsqrt(jnp.mean(hv * hv, axis=-1, keepdims=True) + RMS_EPS)
        n = hv * r
        e = n * gv - t_ref[...]
        dy = e * (1.0 / D)
        u = dy * gv
        dot = jnp.sum(hv * u, axis=-1, keepdims=True)
        dh = r * u - hv * (r * r * r * (1.0 / D)) * dot
        dh_ref[...] = dh
        dhb_ref[...] = dh.astype(BF16)
        rows = jnp.sum(e * e, axis=-1, keepdims=True)
        lpart = jnp.broadcast_to(jnp.sum(rows, axis=0, keepdims=True) * (0.5 / D), (8, LANES))
        gpart = _fold8(dy * n)

        @pl.when(pl.program_id(0) == 0)
        def _():
            l_ref[...] = lpart
            dg_ref[...] = gpart

        @pl.when(pl.program_id(0) > 0)
        def _():
            l_ref[...] += lpart
            dg_ref[...] += gpart

    row = pl.BlockSpec((tm, D), lambda i: (i, 0))
    return pl.pallas_call(
        body, name=name, grid=(M // tm,),
        in_specs=[row, row, pl.BlockSpec((1, D), lambda i: (0, 0))],
        out_specs=[row, row, pl.BlockSpec((8, LANES), lambda i: (0, 0)), pl.BlockSpec((8, D), lambda i: (0, 0))],
        out_shape=[jax.ShapeDtypeStruct((M, D), F32), jax.ShapeDtypeStruct((M, D), BF16),
                   jax.ShapeDtypeStruct((8, LANES), F32), jax.ShapeDtypeStruct((8, D), F32)],
        compiler_params=_params(("arbitrary",)),
    )(h, tgt, g)


def _headnorm_fwd(o_dil, o_sb, g, name):
    M, W = o_dil.shape
    D = 2 * W
    tm = _tile(M, 256, 16)

    def body(a_ref, b_ref, g_ref, o_ref):
        for h in range(D // HEAD_DIM):
            src = a_ref if h < W // HEAD_DIM else b_ref
            lo = (h * HEAD_DIM) % W
            v = src[:, lo:lo + HEAD_DIM]
            r = lax.rsqrt(jnp.mean(v * v, axis=-1, keepdims=True) + RMS_EPS)
            o_ref[:, h * HEAD_DIM:(h + 1) * HEAD_DIM] = (v * r * g_ref[:, h * HEAD_DIM:(h + 1) * HEAD_DIM]).astype(BF16)

    half = pl.BlockSpec((tm, W), lambda i: (i, 0))
    return pl.pallas_call(
        body, name=name, grid=(M // tm,),
        in_specs=[half, half, pl.BlockSpec((1, D), lambda i: (0, 0))],
        out_specs=pl.BlockSpec((tm, D), lambda i: (i, 0)),
        out_shape=jax.ShapeDtypeStruct((M, D), BF16),
        compiler_params=_params(("parallel",)),
    )(o_dil, o_sb, g)


def _headnorm_bwd(d_mix, o_dil, o_sb, g, name):
    M, W = o_dil.shape
    D = 2 * W
    tm = _tile(M, 128, 16)

    def body(d_ref, a_ref, b_ref, g_ref, do_ref, dg_ref):
        @pl.when(pl.program_id(0) == 0)
        def _():
            dg_ref[...] = jnp.zeros(dg_ref.shape, F32)

        for h in range(D // HEAD_DIM):
            src = a_ref if h < W // HEAD_DIM else b_ref
            lo = (h * HEAD_DIM) % W
            cols = slice(h * HEAD_DIM, (h + 1) * HEAD_DIM)
            v = src[:, lo:lo + HEAD_DIM]
            dy = d_ref[:, cols].astype(F32)
            r = lax.rsqrt(jnp.mean(v * v, axis=-1, keepdims=True) + RMS_EPS)
            u = dy * g_ref[:, cols]
            dot = jnp.sum(v * u, axis=-1, keepdims=True)
            do_ref[:, cols] = r * u - v * (r * r * r * (1.0 / HEAD_DIM)) * dot
            dg_ref[:, cols] += _fold8(dy * v * r)

    half = pl.BlockSpec((tm, W), lambda i: (i, 0))
    row = pl.BlockSpec((tm, D), lambda i: (i, 0))
    return pl.pallas_call(
        body, name=name, grid=(M // tm,),
        in_specs=[row, half, half, pl.BlockSpec((1, D), lambda i: (0, 0))],
        out_specs=[row, pl.BlockSpec((8, D), lambda i: (0, 0))],
        out_shape=[jax.ShapeDtypeStruct((M, D), F32), jax.ShapeDtypeStruct((8, D), F32)],
        compiler_params=_params(("arbitrary",)),
    )(d_mix, o_dil, o_sb, g)


def _rope_tables(positions):
    half = ROPE_DIM // 2
    inv_freq = jnp.power(jnp.float32(ROPE_THETA), -jnp.arange(half, dtype=F32) / half)
    ang = positions.astype(F32)[..., None] * inv_freq
    cos, sin = jnp.cos(ang), jnp.sin(ang)
    rest = HEAD_DIM - ROPE_DIM
    one = jnp.ones(cos.shape[:-1] + (rest,), F32)
    z16 = jnp.zeros_like(sin)
    zr = jnp.zeros_like(one)
    c = jnp.concatenate([cos, cos, one], axis=-1)
    s1 = jnp.concatenate([-sin, z16, zr], axis=-1)
    s2 = jnp.concatenate([z16, sin, zr], axis=-1)
    return c, s1, s2


def _dil_bias(T):
    ne = T // TQ
    e = jnp.arange(ne, dtype=jnp.int32)[:, None, None]
    r = jnp.arange(TQ, dtype=jnp.int32)[None, :, None]
    c = jnp.arange(TK_DIL, dtype=jnp.int32)[None, None, :]
    dist = e * TQ + r - c
    mult = jnp.zeros(dist.shape, F32)
    for d in DILATIONS:
        mult = mult + jnp.where((dist % d == 0) & (dist <= DIL_STEPS * d), 1.0, 0.0)
    return jnp.where((dist >= 0) & (mult > 0), jnp.log(jnp.maximum(mult, 1.0)), NEG)


def _dil_fwd(proj3, bias, Hd, name, jobs=()):
    B, T, N3 = proj3.shape
    nq = T // TQ
    per = TK_DIL // TQ
    scale = HEAD_DIM ** -0.5

    def body(q_ref, k_ref, v_ref, b_ref, o_ref, l_ref):
        i = pl.program_id(2)
        qs = [q_ref[r, :] for r in _ROW_PARTS]
        last = i // per

        def step(t, carry):
            j = last - t
            off = pl.multiple_of(j * TK_DIL, TK_DIL)
            kj = k_ref[pl.ds(off, TK_DIL), :]
            vj = v_ref[pl.ds(off, TK_DIL), :]
            new = []
            for n, r in enumerate(_ROW_PARTS):
                m, l, acc = carry[n]
                s = _dot(qs[n], kj, NT) * scale + b_ref[i - per * j, r, :]
                m_new = jnp.maximum(m, jnp.max(s, axis=-1, keepdims=True))
                p = jnp.exp(s - m_new)
                corr = jnp.exp(m - m_new)
                new.append((m_new, l * corr + jnp.sum(p, axis=-1, keepdims=True),
                            acc * corr + _dot(p.astype(BF16), vj, NN)))
            return tuple(new)

        init = (jnp.full((_PART, 1), NEG, F32), jnp.zeros((_PART, 1), F32), jnp.zeros((_PART, HEAD_DIM), F32))
        done = lax.fori_loop(0, last + 1, step, (init,) * len(_ROW_PARTS))
        for n, r in enumerate(_ROW_PARTS):
            m, l, acc = done[n]
            o_ref[r, :] = acc / l
            l_ref[r, :] = jnp.broadcast_to(m + jnp.log(l), (_PART, HEAD_DIM))

    full = lambda base: pl.BlockSpec((None, T, HEAD_DIM), lambda b, h, i: (b, 0, base + h))
    blk = lambda base: pl.BlockSpec((None, TQ, HEAD_DIM), lambda b, h, i: (b, i, base + h))
    sds = jax.ShapeDtypeStruct((B, T, Hd * HEAD_DIM), F32)
    return _hosted_call(
        body, name, (B, Hd, nq),
        [blk(0), full(Hd), full(2 * Hd), pl.BlockSpec(bias.shape, lambda b, h, i: (0, 0, 0))],
        [blk(0), blk(0)], [sds, sds], [], (proj3, proj3, proj3, bias), jobs, ("parallel", "parallel", "arbitrary"))


def _dil_bwd(proj3, bias, tabs, d_o3, o3, lse3, Hd, name, jobs=()):
    B, T, N3 = proj3.shape
    nq = T // TQ
    per = TK_DIL // TQ
    scale = HEAD_DIM ** -0.5

    def body(q_ref, k_ref, v_ref, b_ref, do_ref, o_ref, l_ref, cq, s1q, s2q, ct, s1t, s2t,
             dq_ref, dk_ref, dv_ref, dks, dvs):
        i = pl.program_id(2)

        @pl.when(i == 0)
        def _():
            dks[...] = jnp.zeros(dks.shape, F32)
            dvs[...] = jnp.zeros(dvs.shape, F32)

        q = q_ref[...]
        do = do_ref[...]
        dob = do.astype(BF16)
        dterm = jnp.sum(do * o_ref[...], axis=-1, keepdims=True)
        lse = jnp.concatenate([l_ref[...]] * (TK_DIL // HEAD_DIM), axis=1)
        last = i // per

        def step(j, dqs):
            off = pl.multiple_of(j * TK_DIL, TK_DIL)
            kj = k_ref[pl.ds(off, TK_DIL), :]
            vj = v_ref[pl.ds(off, TK_DIL), :]
            ps, dss, new = [], [], []
            for n, r in enumerate(_ROW_PARTS):
                s = _dot(q[r], kj, NT) * scale + b_ref[i - per * j, r, :]
                p = jnp.exp(s - lse[r])
                dp = _dot(dob[r], vj, NT)
                ds = (p * (dp - dterm[r]) * scale).astype(BF16)
                new.append(dqs[n] + _dot(ds, kj, NN))
                ps.append(p.astype(BF16))
                dss.append(ds)
            dks[pl.ds(off, TK_DIL), :] += _dot(jnp.concatenate(dss, axis=0), q, TN)
            dvs[pl.ds(off, TK_DIL), :] += _dot(jnp.concatenate(ps, axis=0), dob, TN)
            return tuple(new)

        zero = jnp.zeros((_PART, HEAD_DIM), F32)
        dq = jnp.concatenate(lax.fori_loop(0, last + 1, step, (zero,) * len(_ROW_PARTS)), axis=0)
        dq_ref[...] = _rope(dq, cq[...], -s1q[...], -s2q[...]).astype(BF16)

        @pl.when(i == nq - 1)
        def _():
            dk_ref[...] = _rope(dks[...], ct[...], -s1t[...], -s2t[...]).astype(BF16)
            dv_ref[...] = dvs[...].astype(BF16)

    full = lambda base: pl.BlockSpec((None, T, HEAD_DIM), lambda b, h, i: (b, 0, base + h))
    blk = lambda base: pl.BlockSpec((None, TQ, HEAD_DIM), lambda b, h, i: (b, i, base + h))
    tab_q = pl.BlockSpec((None, TQ, HEAD_DIM), lambda b, h, i: (b, i, 0))
    tab_t = pl.BlockSpec((None, T, HEAD_DIM), lambda b, h, i: (b, 0, 0))
    sds = jax.ShapeDtypeStruct((B, T, Hd * HEAD_DIM), BF16)
    return _hosted_call(
        body, name, (B, Hd, nq),
        [blk(0), full(Hd), full(2 * Hd), pl.BlockSpec(bias.shape, lambda b, h, i: (0, 0, 0)),
         blk(0), blk(0), blk(0), tab_q, tab_q, tab_q, tab_t, tab_t, tab_t],
        [blk(0), full(0), full(0)], [sds, sds, sds],
        [pltpu.VMEM((T, HEAD_DIM), F32), pltpu.VMEM((T, HEAD_DIM), F32)],
        (proj3, proj3, proj3, bias, d_o3, o3, lse3, *tabs, *tabs), jobs, ("parallel", "parallel", "arbitrary"))


def _softplus(z):
    return jnp.maximum(z, 0.0) + jnp.log(1.0 + jnp.exp(-jnp.abs(z)))


def _tri_sum(x, m01):
    n = x.shape[0]
    hi = x.astype(BF16)
    lo = (x - hi.astype(F32)).astype(BF16)
    both = _dot(jnp.concatenate([hi, lo], axis=0), m01, NN)
    return both[:n] + both[n:]


def _sb_iota(rows=TQ, first=0):
    row = lax.broadcasted_iota(jnp.int32, (rows, TK_SB), 0) + first
    col = lax.broadcasted_iota(jnp.int32, (rows, TK_SB), 1)
    return row, col


def _sb_fwd(proj3, Hd, Hs, name, jobs=()):
    B, T, N3 = proj3.shape
    nq = T // TQ
    qb, kb_, vb_ = 3 * Hd, 3 * Hd + Hs, 3 * Hd + 2 * Hs
    scale = HEAD_DIM ** -0.5

    def body(q_ref, k_ref, v_ref, o_ref, lt_ref):
        i = pl.program_id(2)
        qs = [q_ref[r, :] for r in _ROW_PARTS]
        row, col = _sb_iota(TK_SB)
        after = jnp.where(row > col, 1.0, 0.0).astype(BF16)
        rows = [_sb_iota(_PART, r.start)[0] for r in _ROW_PARTS]
        colh = _sb_iota(_PART)[1]
        n_chunks = (i + 1) * (TQ // TK_SB)

        def step(t, carry):
            j = n_chunks - 1 - t
            off = pl.multiple_of(j * TK_SB, TK_SB)
            kj = k_ref[pl.ds(off, TK_SB), :]
            vj = v_ref[pl.ds(off, TK_SB), :]
            reach = i * TQ - j * TK_SB
            new = []
            for n in range(len(_ROW_PARTS)):
                acc, cs = carry[n]
                z = _dot(qs[n], kj, NT) * scale
                msk = colh < rows[n] + reach
                sp = _softplus(z)
                ln = jnp.where(msk, -sp, 0.0)
                excl = _tri_sum(ln, after) + cs
                a = jnp.where(msk, jnp.exp(z - sp + excl), 0.0)
                new.append((acc + _dot(a.astype(BF16), vj, NN), cs + jnp.sum(ln, axis=-1, keepdims=True)))
            return tuple(new)

        init = (jnp.zeros((_PART, HEAD_DIM), F32), jnp.zeros((_PART, 1), F32))
        done = lax.fori_loop(0, n_chunks, step, (init,) * len(_ROW_PARTS))
        for n, r in enumerate(_ROW_PARTS):
            o_ref[r, :] = done[n][0]
            lt_ref[r, :] = jnp.broadcast_to(done[n][1], (_PART, HEAD_DIM))

    full = lambda base: pl.BlockSpec((None, T, HEAD_DIM), lambda b, h, i: (b, 0, base + h))
    blk = lambda base: pl.BlockSpec((None, TQ, HEAD_DIM), lambda b, h, i: (b, i, base + h))
    sds = jax.ShapeDtypeStruct((B, T, Hs * HEAD_DIM), F32)
    return _hosted_call(
        body, name, (B, Hs, nq), [blk(qb), full(kb_), full(vb_)], [blk(0), blk(0)], [sds, sds], [],
        (proj3, proj3, proj3), jobs, ("parallel", "parallel", "arbitrary"))


def _sb_bwd(proj3, d_o3, lt3, Hd, Hs, name, jobs=()):
    B, T, N3 = proj3.shape
    nq = T // TQ
    qb, kb_, vb_ = 3 * Hd, 3 * Hd + Hs, 3 * Hd + 2 * Hs
    scale = HEAD_DIM ** -0.5

    def body(q_ref, k_ref, v_ref, do_ref, lt_ref, dq_ref, dk_ref, dv_ref, dks, dvs):
        i = pl.program_id(2)

        @pl.when(i == 0)
        def _():
            dks[...] = jnp.zeros(dks.shape, F32)
            dvs[...] = jnp.zeros(dvs.shape, F32)

        q = q_ref[...]
        dob = do_ref[...].astype(BF16)
        total = jnp.concatenate([lt_ref[...]] * (TK_SB // HEAD_DIM), axis=1)
        row, col = _sb_iota(TK_SB)
        before = jnp.where(row < col, 1.0, 0.0).astype(BF16)
        rows = [_sb_iota(_PART, r.start)[0] for r in _ROW_PARTS]
        colh = _sb_iota(_PART)[1]

        def step(j, carry):
            off = pl.multiple_of(j * TK_SB, TK_SB)
            kj = k_ref[pl.ds(off, TK_SB), :]
            vj = v_ref[pl.ds(off, TK_SB), :]
            reach = i * TQ - j * TK_SB
            a_s, dzs, new = [], [], []
            for n, r in enumerate(_ROW_PARTS):
                dq, pc, gc = carry[n]
                z = _dot(q[r], kj, NT) * scale
                msk = colh < rows[n] + reach
                sp = _softplus(z)
                ln = jnp.where(msk, -sp, 0.0)
                excl = total[r] - (_tri_sum(ln, before) + ln + pc)
                a = jnp.where(msk, jnp.exp(z - sp + excl), 0.0)
                g = a * _dot(dob[r], vj, NT)
                big = _tri_sum(g, before) + gc
                sig = jnp.exp(z - sp)
                dz = (jnp.where(msk, g * (1.0 - sig) - big * sig, 0.0) * scale).astype(BF16)
                new.append((dq + _dot(dz, kj, NN), pc + jnp.sum(ln, axis=-1, keepdims=True),
                            gc + jnp.sum(g, axis=-1, keepdims=True)))
                a_s.append(a.astype(BF16))
                dzs.append(dz)
            dks[pl.ds(off, TK_SB), :] += _dot(jnp.concatenate(dzs, axis=0), q, TN)
            dvs[pl.ds(off, TK_SB), :] += _dot(jnp.concatenate(a_s, axis=0), dob, TN)
            return tuple(new)

        zero1 = jnp.zeros((_PART, 1), F32)
        init = (jnp.zeros((_PART, HEAD_DIM), F32), zero1, zero1)
        done = lax.fori_loop(0, (i + 1) * (TQ // TK_SB), step, (init,) * len(_ROW_PARTS))
        dq_ref[...] = jnp.concatenate([d[0] for d in done], axis=0).astype(BF16)

        @pl.when(i == nq - 1)
        def _():
            dk_ref[...] = dks[...].astype(BF16)
            dv_ref[...] = dvs[...].astype(BF16)

    full = lambda base: pl.BlockSpec((None, T, HEAD_DIM), lambda b, h, i: (b, 0, base + h))
    blk = lambda base: pl.BlockSpec((None, TQ, HEAD_DIM), lambda b, h, i: (b, i, base + h))
    sds = jax.ShapeDtypeStruct((B, T, Hs * HEAD_DIM), BF16)
    return _hosted_call(
        body, name, (B, Hs, nq), [blk(qb), full(kb_), full(vb_), blk(Hd), blk(0)],
        [blk(0), full(0), full(0)], [sds, sds, sds],
        [pltpu.VMEM((T, HEAD_DIM), F32), pltpu.VMEM((T, HEAD_DIM), F32)],
        (proj3, proj3, proj3, d_o3, lt3), jobs, ("parallel", "parallel", "arbitrary"))


def _place():
    x, y, c = lax.axis_index("x"), lax.axis_index("y"), lax.axis_index("c")
    chips = [(1 - x, y), (x, 1 - y), (1 - x, 1 - y)]
    return x, y, c, chips


class _Weight:
    def __init__(self, kind, shard_shape, slot):
        self.kind, self.slot = kind, slot
        self.R, self.C = shard_shape
        if kind == "col":
            assert slot == self.C
            self.full = (self.R, N_CHIPS * slot)
            self.half = (self.R // 2, N_CHIPS * slot)
            self.piece = (self.R // 2, slot)
        else:
            self.full = (N_CHIPS * slot, self.C)
            self.half = (N_CHIPS * slot, self.C // 2)
            self.piece = (self.R, self.C // 2)

    def _rows(self, jj, n):
        return pl.ds(pl.multiple_of(jj * self.slot, 16), n)

    def full_half(self, ref, jj, hc):
        if self.kind == "col":
            return ref.at[pl.ds(hc * (self.R // 2), self.R // 2), pl.ds(pl.multiple_of(jj * self.slot, LANES), self.C)]
        return ref.at[self._rows(jj, self.R), pl.ds(hc * (self.C // 2), self.C // 2)]

    def region_half(self, ref, hc):
        if self.kind == "col":
            return ref.at[pl.ds(hc * (self.R // 2), self.R // 2), :]
        return ref.at[:, pl.ds(hc * (self.C // 2), self.C // 2)]

    def half_piece(self, ref, jj):
        if self.kind == "col":
            return ref.at[:, pl.ds(pl.multiple_of(jj * self.slot, LANES), self.slot)]
        return ref.at[self._rows(jj, self.R), :]


def _cast_into_full(w, geo, j_arr, name):
    R, C = w.shape
    tr = _tile(R, 256, 16)
    if geo.kind == "col":
        o_spec = pl.BlockSpec((tr, C), lambda i, j_ref: (i, j_ref[0]))
    else:
        while geo.slot % tr or R % tr:
            tr -= 16
        spb = geo.slot // tr
        o_spec = pl.BlockSpec((tr, C), lambda i, j_ref: (j_ref[0] * spb + i, 0))

    def body(j_ref, w_ref, o_ref):
        o_ref[...] = w_ref[...].astype(BF16)

    return pl.pallas_call(
        body, name=name,
        grid_spec=pltpu.PrefetchScalarGridSpec(
            num_scalar_prefetch=1, grid=(R // tr,),
            in_specs=[pl.BlockSpec((tr, C), lambda i, j_ref: (i, 0))], out_specs=o_spec),
        out_shape=jax.ShapeDtypeStruct(geo.full, BF16),
        compiler_params=_params(("parallel",)),
    )(j_arr, w)


def _zero_pad(full, geo, name):
    pad = geo.slot - geo.R
    assert geo.kind == "row" and pad > 0 and geo.R % pad == 0 and geo.slot % pad == 0

    def body(f_ref, o_ref):
        o_ref[...] = jnp.zeros(o_ref.shape, BF16)

    return pl.pallas_call(
        body, name=name, grid=(N_CHIPS,), in_specs=[ANY],
        out_specs=pl.BlockSpec((pad, geo.C), lambda jj: ((jj * geo.slot + geo.R) // pad, 0)),
        out_shape=jax.ShapeDtypeStruct(geo.full, BF16), input_output_aliases={0: 0},
        compiler_params=_params(("arbitrary",)),
    )(full)


def _gather_job(full, geo, stage):
    def copies(ins, outs, send, recv):
        x, y, c, chips = _place()
        j = 2 * x + y

        def cp(src, dst, k, to):
            return pltpu.make_async_remote_copy(src_ref=src, dst_ref=dst, send_sem=send.at[k], recv_sem=recv.at[k],
                                                device_id=to, device_id_type=MESH)

        started, landing = [], []
        for k, (cx, cy) in enumerate(chips):
            jk = 2 * cx + cy
            if stage == "ici":
                started.append(cp(geo.full_half(ins[0], j, c), geo.full_half(outs[0], j, c), k, (cx, cy, c)))
                part = geo.full_half(outs[0], jk, c)
            else:
                started.append(cp(geo.full_half(ins[0], jk, c), geo.full_half(outs[0], jk, c), k, (x, y, 1 - c)))
                part = geo.full_half(outs[0], jk, 1 - c)
            landing.append(functools.partial(cp, part, part, k, (x, y, c)))
        return started, landing

    return _Job([full], [jax.ShapeDtypeStruct(geo.full, BF16)], {0: 0}, 3, copies)


def _sibling_job(grad_full, geo):
    def copies(ins, outs, send, recv):
        x, y, c, _ = _place()
        mk = lambda src, to: pltpu.make_async_remote_copy(
            src_ref=src, dst_ref=outs[0], send_sem=send.at[0], recv_sem=recv.at[0], device_id=to, device_id_type=MESH)
        return [mk(geo.region_half(ins[0], 1 - c), (x, y, 1 - c))], [functools.partial(mk, outs[0], (x, y, c))]

    return _Job([grad_full], [jax.ShapeDtypeStruct(geo.half, BF16)], {}, 1, copies)


def _chips_job(chip_sum, geo, part=(0, 1), so_far=None):
    p, n = part
    rows = geo.piece[0] // n
    assert geo.piece[0] % n == 0 and rows % 16 == 0

    def copies(ins, outs, send, recv):
        x, y, c, chips = _place()
        started, landing = [], []

        def mk(k, src, to):
            dst = outs[0].at[k, pl.ds(p * rows, rows), :]
            return pltpu.make_async_remote_copy(src_ref=dst if src is None else src, dst_ref=dst, send_sem=send.at[k],
                                                recv_sem=recv.at[k], device_id=to, device_id_type=MESH)

        for k, (cx, cy) in enumerate(chips):
            started.append(mk(k, geo.half_piece(ins[0], 2 * cx + cy).at[pl.ds(p * rows, rows), :], (cx, cy, c)))
            landing.append(functools.partial(mk, k, None, (x, y, c)))
        return started, landing

    inputs = [chip_sum] + ([] if so_far is None else [so_far])
    return _Job(inputs, [jax.ShapeDtypeStruct((3,) + geo.piece, BF16)], {} if so_far is None else {1: 0}, 3, copies)


def _final_job(half, geo):
    def copies(ins, outs, send, recv):
        x, y, c, _ = _place()
        mk = lambda src, to: pltpu.make_async_remote_copy(
            src_ref=src, dst_ref=outs[0], send_sem=send.at[0], recv_sem=recv.at[0], device_id=to, device_id_type=MESH)
        return [mk(ins[0], (x, y, 1 - c))], [functools.partial(mk, outs[0], (x, y, c))]

    return _Job([half], [jax.ShapeDtypeStruct(geo.piece, F32)], {}, 1, copies)


def _gather_two_step(full, geo, name):
    R = geo.R
    assert geo.kind == "col" and R % 64 == 0

    def body(own_ref, f_ref, send, recv):
        x, y, c, _ = _place()
        j, jx, jy, jd = 2 * x + y, 2 * (1 - x) + y, 2 * x + (1 - y), 2 * (1 - x) + (1 - y)
        me, xn, yn, sib = (x, y, c), (1 - x, y, c), (x, 1 - y, c), (x, y, 1 - c)

        def half(ref, jj, hc):
            return geo.full_half(ref, jj, hc)

        def quarter(ref, jj, hc, q):
            return ref.at[pl.ds(hc * (R // 2) + q * (R // 4), R // 4), pl.ds(pl.multiple_of(jj * geo.slot, LANES), geo.C)]

        def cp(k, src, dst, to):
            return pltpu.make_async_remote_copy(src_ref=src, dst_ref=dst, send_sem=send.at[k], recv_sem=recv.at[k],
                                                device_id=to, device_id_type=MESH)

        def arrived(k, part):
            cp(k, part, part, me).wait_recv()

        def pass_on(k, part, to):
            d = cp(k, part, part, to)
            d.start()
            return d

        sent = [cp(0, half(own_ref, j, c), half(f_ref, j, c), xn), cp(1, half(own_ref, j, c), half(f_ref, j, c), yn)]
        for d in sent:
            d.start()
        arrived(0, half(f_ref, jx, c))
        sent.append(pass_on(2, quarter(f_ref, jx, c, 0), yn))
        sent.append(pass_on(4, half(f_ref, jx, c), sib))
        arrived(1, half(f_ref, jy, c))
        sent.append(pass_on(3, quarter(f_ref, jy, c, 1), xn))
        sent.append(pass_on(5, half(f_ref, jy, c), sib))
        arrived(2, quarter(f_ref, jd, c, 0))
        sent.append(pass_on(6, quarter(f_ref, jd, c, 0), sib))
        arrived(3, quarter(f_ref, jd, c, 1))
        sent.append(pass_on(7, quarter(f_ref, jd, c, 1), sib))
        arrived(4, half(f_ref, jx, 1 - c))
        arrived(5, half(f_ref, jy, 1 - c))
        arrived(6, quarter(f_ref, jd, 1 - c, 0))
        arrived(7, quarter(f_ref, jd, 1 - c, 1))
        for d in sent:
            d.wait_send()

    return pl.pallas_call(
        body, name=name, in_specs=[ANY], out_specs=ANY, out_shape=jax.ShapeDtypeStruct(geo.full, BF16),
        input_output_aliases={0: 0},
        scratch_shapes=[pltpu.SemaphoreType.DMA((8,)), pltpu.SemaphoreType.DMA((8,))],
    )(full)


def _chip_sum(g_full, got, geo, c_arr, name):
    Rh, Ch = geo.half
    tr, tc = _tile(Rh, 256, 16), _tile(Ch, 2048)
    nrb, ncb = Rh // tr, Ch // tc

    def body(c_ref, a_ref, b_ref, o_ref):
        o_ref[...] = (a_ref[...].astype(F32) + b_ref[...].astype(F32)).astype(BF16)

    if geo.kind == "col":
        a_spec = pl.BlockSpec((tr, tc), lambda i, j, c_ref: (c_ref[0] * nrb + i, j))
    else:
        a_spec = pl.BlockSpec((tr, tc), lambda i, j, c_ref: (i, c_ref[0] * ncb + j))
    spec = pl.BlockSpec((tr, tc), lambda i, j, c_ref: (i, j))
    return pl.pallas_call(
        body, name=name,
        grid_spec=pltpu.PrefetchScalarGridSpec(num_scalar_prefetch=1, grid=(nrb, ncb),
                                               in_specs=[a_spec, spec], out_specs=spec),
        out_shape=jax.ShapeDtypeStruct(geo.half, BF16),
        compiler_params=_params(("parallel", "parallel")),
    )(c_arr, g_full, got)


def _final_sum(chip_sum, got, geo, j_arr, name):
    Rp, Cp = geo.piece
    if geo.kind == "col":
        tr = _tile(Rp, 128, 16)
        a_spec = pl.BlockSpec((tr, Cp), lambda i, j_ref: (i, j_ref[0]))
    else:
        tr = _tile(Rp, 128, 16)
        while geo.slot % tr:
            tr -= 16
        assert Rp % tr == 0 and geo.slot % tr == 0
        spb = geo.slot // tr
        a_spec = pl.BlockSpec((tr, Cp), lambda i, j_ref: (j_ref[0] * spb + i, 0))
    r_specs = [pl.BlockSpec((None, tr, Cp), functools.partial(lambda i, j_ref, k: (k, i, 0), k=k)) for k in range(3)]
    o_spec = pl.BlockSpec((tr, Cp), lambda i, j_ref: (i, 0))

    def body(j_ref, a_ref, r0, r1, r2, o_ref):
        o_ref[...] = ((a_ref[...].astype(F32) + r0[...].astype(F32)) + r1[...].astype(F32)) + r2[...].astype(F32)

    return pl.pallas_call(
        body, name=name,
        grid_spec=pltpu.PrefetchScalarGridSpec(num_scalar_prefetch=1, grid=(Rp // tr,),
                                               in_specs=[a_spec] + r_specs, out_specs=o_spec),
        out_shape=jax.ShapeDtypeStruct(geo.piece, F32),
        compiler_params=_params(("parallel",)),
    )(j_arr, chip_sum, got, got, got)


def _run_job(job, name):
    ni = len(job.inputs)

    def body(*refs):
        ins, outs, (send, recv) = refs[:ni], refs[ni:-2], refs[-2:]
        job.start(ins, outs, send, recv)
        job.finish(ins, outs, send, recv)

    return pl.pallas_call(
        body, name=name, in_specs=[ANY] * ni, out_specs=[ANY] * len(job.outputs), out_shape=list(job.outputs),
        input_output_aliases=dict(job.aliases),
        scratch_shapes=[pltpu.SemaphoreType.DMA((job.nsem,)), pltpu.SemaphoreType.DMA((job.nsem,))],
    )(*job.inputs)


def _adam_math(w, g, m, v):
    m = ADAM_B1 * m + (1.0 - ADAM_B1) * g
    v = ADAM_B2 * v + (1.0 - ADAM_B2) * (g * g)
    m_hat = m / (1.0 - ADAM_B1 ** ADAM_STEP)
    v_hat = v / (1.0 - ADAM_B2 ** ADAM_STEP)
    delta = -ADAM_LR * (m_hat / (jnp.sqrt(v_hat) + ADAM_EPS) + ADAM_WD * w)
    return delta, m, v


def _adam(ws, mines, theirs, ms, vs, geo, name, jobs=()):
    n = len(ws)
    R, C = ws[0].shape
    tr = _tile(R, 128 if n == 1 else 32, 8)
    nrb = R // tr
    col = geo.kind == "col"
    assert not col or nrb % 2 == 0

    def rows(k, halved):
        def index(w, i):
            r = jnp.where(w == k, i, jnp.where(w < k, 0, nrb - 1))
            return (r % (nrb // 2) if halved else r, 0)
        return index

    def body(*refs):
        ins, outs = refs[:5 * n], refs[5 * n:]
        wi, i = pl.program_id(0), pl.program_id(1)
        c = lax.axis_index("c")
        for k in range(n):
            @pl.when(wi == k)
            def _():
                w_ref, a_ref, b_ref, m_ref, v_ref = ins[5 * k:5 * k + 5]
                if col:
                    gv = jnp.where(i // (nrb // 2) == c, a_ref[...], b_ref[...])
                else:
                    a, b = a_ref[...], b_ref[...]
                    gv = jnp.concatenate([jnp.where(c == 0, a, b), jnp.where(c == 0, b, a)], axis=1)
                delta, mn, vn = _adam_math(w_ref[...], gv, m_ref[...], v_ref[...])
                og, od, om, ov = outs[4 * k:4 * k + 4]
                og[...] = gv
                od[...] = delta
                om[...] = mn
                ov[...] = vn

    in_specs, out_specs, args = [], [], []
    for k in range(n):
        spec = pl.BlockSpec((tr, C), rows(k, False))
        h_spec = pl.BlockSpec((tr, C), rows(k, True)) if col else pl.BlockSpec((tr, C // 2), rows(k, False))
        in_specs += [spec, h_spec, h_spec, spec, spec]
        out_specs += [spec] * 4
        args += [ws[k], mines[k], theirs[k], ms[k], vs[k]]
    outs, job_outs = _hosted_call(body, name, (n, nrb), in_specs, out_specs,
                                  [jax.ShapeDtypeStruct((R, C), F32)] * (4 * n), [], args, jobs,
                                  ("arbitrary", "arbitrary"))
    return [outs[4 * k:4 * k + 4] for k in range(n)], job_outs


def _small_all_reduce(parts, loss_part, D):
    n = len(parts)

    def body(*refs):
        p_refs, l_ref, o_ref, vec, buf, send, recv = refs[:n], refs[n], refs[n + 1], *refs[n + 2:]
        x, y, c, _ = _place()
        me = 4 * x + 2 * y + c
        vec[...] = jnp.zeros(vec.shape, F32)
        for r in range(n):
            vec[r:r + 1, :] = jnp.sum(p_refs[r][...], axis=0, keepdims=True)
        vec[n:n + 1, 0:LANES] = l_ref[0:1, :]
        buf[me] = vec[...]
        cps = []
        for dd in range(1, 8):
            bx, by, bc = (dd >> 2) & 1, (dd >> 1) & 1, dd & 1
            peer = (x + bx - 2 * x * bx, y + by - 2 * y * by, c + bc - 2 * c * bc)
            cp = pltpu.make_async_remote_copy(
                src_ref=vec, dst_ref=buf.at[me], send_sem=send.at[dd - 1], recv_sem=recv.at[dd - 1],
                device_id=peer, device_id_type=MESH)
            cp.start()
            cps.append(cp)
        for cp in cps:
            cp.wait()
        tot = buf[0]
        for s in range(1, 8):
            tot = tot + buf[s]
        o_ref[...] = tot

    vm = pl.BlockSpec(memory_space=pltpu.VMEM)
    return pl.pallas_call(
        body, name="small_all_reduce",
        in_specs=[vm] * (n + 1), out_specs=vm,
        out_shape=jax.ShapeDtypeStruct((8, D), F32),
        scratch_shapes=[pltpu.VMEM((8, D), F32), pltpu.VMEM((8, 8, D), F32),
                        pltpu.SemaphoreType.DMA((7,)), pltpu.SemaphoreType.DMA((7,))],
    )(*parts, loss_part)


def _small_adam(tot, ws, ms, vs, rows):
    n = len(ws)

    def body(*refs):
        t_ref = refs[0]
        w_refs, m_refs, v_refs = refs[1:1 + n], refs[1 + n:1 + 2 * n], refs[1 + 2 * n:1 + 3 * n]
        outs = refs[1 + 3 * n:]
        for i in range(n):
            r, c0 = rows[i]
            width = w_refs[i].shape[1]
            g = t_ref[r:r + 1, c0:c0 + width]
            delta, mn, vn = _adam_math(w_refs[i][...], g, m_refs[i][...], v_refs[i][...])
            outs[4 * i][...] = g
            outs[4 * i + 1][...] = delta
            outs[4 * i + 2][...] = mn
            outs[4 * i + 3][...] = vn

    vm = pl.BlockSpec(memory_space=pltpu.VMEM)
    out_shape = []
    for w in ws:
        out_shape += [jax.ShapeDtypeStruct(w.shape, F32)] * 4
    return pl.pallas_call(
        body, name="small_adam",
        in_specs=[vm] * (1 + 3 * n), out_specs=[vm] * (4 * n), out_shape=out_shape,
    )(tot, *ws, *ms, *vs)


def kernel(x, positions, norm_mix_g, w_in, norm_out_dil_g, norm_out_sb_g, w_out, norm_ffn_g, w_gate, w_up, w_down, norm_final_g, loss_target, m_norm_mix_g, m_w_in, m_norm_out_dil_g, m_norm_out_sb_g, m_w_out, m_norm_ffn_g, m_w_gate, m_w_up, m_w_down, m_norm_final_g, v_norm_mix_g, v_w_in, v_norm_out_dil_g, v_norm_out_sb_g, v_w_out, v_norm_ffn_g, v_w_gate, v_w_up, v_w_down, v_norm_final_g):
    B, T, D = x.shape
    M = B * T
    n_heads = D // HEAD_DIM
    Hd = n_heads // 2
    Hs = n_heads - Hd
    W = Hd * HEAD_DIM
    N3 = 3 * D
    fs = w_gate.shape[2]
    fp = -(-fs // LANES) * LANES
    assert T % TK_DIL == 0 and W == Hs * HEAD_DIM and fp > fs

    x2 = x.reshape(M, D)
    tgt = loss_target.reshape(M, D)
    tr_ = jnp.transpose
    big = [w_in[0], w_out[0], tr_(w_gate[0]), tr_(w_up[0]), w_down[0]]
    big_m = [m_w_in[0], m_w_out[0], tr_(m_w_gate[0]), tr_(m_w_up[0]), m_w_down[0]]
    big_v = [v_w_in[0], v_w_out[0], tr_(v_w_gate[0]), tr_(v_w_up[0]), v_w_down[0]]
    names = ["w_in", "w_out", "w_gate", "w_up", "w_down"]
    c_arr = jnp.reshape(lax.axis_index("c"), (1,)).astype(jnp.int32)
    j_arr = jnp.reshape(2 * lax.axis_index("x") + lax.axis_index("y"), (1,)).astype(jnp.int32)

    ns_in = w_in.shape[2]
    ks_out = w_out.shape[1]
    geo = [_Weight("col", (D, ns_in), ns_in), _Weight("row", (ks_out, D), ks_out),
           _Weight("row", (fs, D), fp), _Weight("row", (fs, D), fp), _Weight("row", (fs, D), fp)]
    g_in, g_out, g_gate, g_up, g_down = geo
    fulls = [_cast_into_full(big[w], geo[w], j_arr, "cast_" + names[w]) for w in range(5)]
    for w in (2, 3, 4):
        fulls[w] = _zero_pad(fulls[w], geo[w], "zero_pad_" + names[w])

    def chip_sum(w, grad_full, from_sibling):
        return _chip_sum(grad_full, from_sibling, geo[w], c_arr, "chip_sum_" + names[w])

    def final_sum(w, cs, from_chips):
        return _final_sum(cs, from_chips, geo[w], j_arr, "final_sum_" + names[w])

    Win = _gather_two_step(fulls[0], g_in, "gather_w_in")
    tabs3 = _rope_tables(positions)
    tabs2 = [t.reshape(M, HEAD_DIM) for t in tabs3]
    hn = _rmsnorm_fwd(x2, norm_mix_g, "norm_mix")
    proj, (WgT,) = _matmul(hn, Win, "nn", BF16, "in_proj", rope=(tabs2, 2 * W),
                           jobs=[_gather_job(fulls[2], g_gate, "ici")])
    proj3 = proj.reshape(B, T, N3)
    bias = _dil_bias(T)
    (o_dil3, lse3), (WgT, Wout) = _dil_fwd(
        proj3, bias, Hd, "dil_fwd", jobs=[_gather_job(WgT, g_gate, "d2d"), _gather_job(fulls[1], g_out, "ici")])
    (o_sb3, lt3), (Wout, WuT) = _sb_fwd(
        proj3, Hd, Hs, "sb_fwd", jobs=[_gather_job(Wout, g_out, "d2d"), _gather_job(fulls[3], g_up, "ici")])
    o_dil, o_sb = o_dil3.reshape(M, W), o_sb3.reshape(M, W)
    g_heads = jnp.concatenate([norm_out_dil_g, norm_out_sb_g], axis=1)
    o_mix = _headnorm_fwd(o_dil, o_sb, g_heads, "head_norm")
    h1, (WuT,) = _matmul(o_mix, Wout, "nn", F32, "out_proj", res=x2, jobs=[_gather_job(WuT, g_up, "d2d")])
    hn2 = _rmsnorm_fwd(h1, norm_ffn_g, "norm_ffn")
    (gate, up, act), (Wd,) = _gate_up(hn2, WgT, WuT, "gate_up", jobs=[_gather_job(fulls[4], g_down, "ici")])
    (Wd,) = _run_job(_gather_job(Wd, g_down, "d2d"), "gather_w_down_d2d")
    h2 = _matmul(act, Wd, "nn", F32, "down_proj", res=h1)
    dh2, dh2_b, loss_part, dg_final = _loss_head(h2, tgt, norm_final_g.reshape(1, D), "loss_head")

    dWd = _matmul(act, dh2_b, "tn", BF16, "dw_down")
    (d_gate, d_up), (sib_d,) = _matmul(dh2_b, Wd, "nt", BF16, "d_act", swiglu=(gate, up),
                                       jobs=[_sibling_job(dWd, g_down)])
    cs_d = chip_sum(4, dWd, sib_d)
    dWg, (chips_d,) = _matmul(d_gate, hn2, "tn", BF16, "dw_gate", jobs=[_chips_job(cs_d, g_down, (0, 2))])
    dWu, (chips_d, sib_g) = _matmul(d_up, hn2, "tn", BF16, "dw_up",
                                    jobs=[_chips_job(cs_d, g_down, (1, 2), chips_d), _sibling_job(dWg, g_gate)])
    half_d = final_sum(4, cs_d, chips_d)
    cs_g = chip_sum(2, dWg, sib_g)
    d_hn2, (chips_g, other_d) = _matmul(d_gate, WgT, "nn", F32, "d_hn2_gate",
                                        jobs=[_chips_job(cs_g, g_gate, (0, 2)), _final_job(half_d, g_down)])
    d_hn2, (chips_g, sib_u) = _matmul(d_up, WuT, "nn", F32, "d_hn2_up", res=d_hn2,
                                      jobs=[_chips_job(cs_g, g_gate, (1, 2), chips_g), _sibling_job(dWu, g_up)])
    half_g = final_sum(2, cs_g, chips_g)
    cs_u = chip_sum(3, dWu, sib_u)
    dh1, dh1_b, dg_ffn = _rmsnorm_bwd(h1, d_hn2, norm_ffn_g, dh2, "norm_ffn_bwd", True)
    dWo, (other_g,) = _matmul(o_mix, dh1_b, "tn", BF16, "dw_out", jobs=[_final_job(half_g, g_gate)])
    d_mix, (sib_o,) = _matmul(dh1_b, Wout, "nt", BF16, "d_mix", jobs=[_sibling_job(dWo, g_out)])
    cs_o = chip_sum(1, dWo, sib_o)
    d_o, dg_heads = _headnorm_bwd(d_mix, o_dil, o_sb, g_heads, "head_norm_bwd")
    d_o3 = d_o.reshape(B, T, D)
    dqkv_dil, (chips_u,) = _dil_bwd(proj3, bias, tabs3, d_o3, o_dil3, lse3, Hd, "dil_bwd",
                                    jobs=[_chips_job(cs_u, g_up, (0, 2))])
    dqkv_sb, (chips_u, chips_o) = _sb_bwd(proj3, d_o3, lt3, Hd, Hs, "sb_bwd",
                                          jobs=[_chips_job(cs_u, g_up, (1, 2), chips_u), _chips_job(cs_o, g_out)])
    half_u = final_sum(3, cs_u, chips_u)
    half_o = final_sum(1, cs_o, chips_o)
    dproj = jnp.concatenate([*dqkv_dil, *dqkv_sb], axis=-1).reshape(M, N3)
    dWin, (other_u, other_o) = _matmul(hn, dproj, "tn", BF16, "dw_in",
                                       jobs=[_final_job(half_u, g_up), _final_job(half_o, g_out)])
    (sib_i,) = _run_job(_sibling_job(dWin, g_in), "sibling_w_in")
    cs_i = chip_sum(0, dWin, sib_i)
    d_hn, (chips_i,) = _matmul(dproj, Win, "nt", F32, "d_hn", jobs=[_chips_job(cs_i, g_in)])
    half_i = final_sum(0, cs_i, chips_i)
    dx, dg_mix = _rmsnorm_bwd(x2, d_hn, norm_mix_g, dh1, "norm_mix_bwd", False)
    (other_i,) = _run_job(_final_job(half_i, g_in), "final_w_in")

    mine = [half_i, half_o, half_g, half_u, half_d]
    theirs = [other_i, other_o, other_g, other_u, other_d]
    big_out = [_adam([big[w]], [mine[w]], [theirs[w]], [big_m[w]], [big_v[w]], geo[w], "adam_" + names[w])[0][0]
               for w in range(5)]
    for w in (2, 3):
        big_out[w] = [tr_(a) for a in big_out[w]]

    tot = _small_all_reduce([dg_mix, dg_heads, dg_ffn, dg_final], loss_part, D)
    loss = tot[4, 0]
    small_w = [norm_mix_g, norm_out_dil_g, norm_out_sb_g, norm_ffn_g, norm_final_g.reshape(1, D)]
    small_m = [m_norm_mix_g, m_norm_out_dil_g, m_norm_out_sb_g, m_norm_ffn_g, m_norm_final_g.reshape(1, D)]
    small_v = [v_norm_mix_g, v_norm_out_dil_g, v_norm_out_sb_g, v_norm_ffn_g, v_norm_final_g.reshape(1, D)]
    so = _small_adam(tot, small_w, small_m, small_v, [(0, 0), (1, 0), (1, W), (2, 0), (3, 0)])
    small_out = [so[4 * i:4 * i + 4] for i in range(5)]
    small_out[4] = [a.reshape(D) for a in small_out[4]]

    per_weight = [small_out[0], big_out[0], small_out[1], small_out[2], big_out[1], small_out[3],
                  big_out[2], big_out[3], big_out[4], small_out[4]]

    def field(i):
        res = []
        for n_, o in enumerate(per_weight):
            a = o[i]
            res.append(a[None] if n_ in (1, 4, 6, 7, 8) else a)
        return res

    return (loss, dx.reshape(B, T, D), *field(0), *field(1), *field(2), *field(3))
```

```python
import functools
import math

import jax
import jax.numpy as jnp
from jax import lax
from jax.experimental import pallas as pl
from jax.experimental.pallas import tpu as pltpu

F32 = jnp.float32
BF16 = jnp.bfloat16
MESH = pl.DeviceIdType.MESH
ANY = pl.BlockSpec(memory_space=pl.ANY)

LANES = 128
HEAD_DIM = 128
DIL_STEPS = 128
DILATIONS = (1, 4, 16)
TQ = 512
TK_DIL = 512
TK_SB = 256
NEG = -1e30
_PART = TQ
_ROW_PARTS = tuple(slice(p, p + _PART) for p in range(0, TQ, _PART))
ROPE_DIM = 32
ROPE_THETA = 500000.0
RMS_EPS = 1e-5
ADAM_LR, ADAM_B1, ADAM_B2, ADAM_EPS, ADAM_WD, ADAM_STEP = 0.001, 0.9, 0.999, 1e-08, 0.01, 10
N_CHIPS = 4
VMEM_LIMIT = 56 * 1024 * 1024
MATMUL_VMEM = 52 * 1024 * 1024

NN = ((1,), (0,))
NT = ((1,), (1,))
TN = ((0,), (0,))


def _dot(a, b, dims):
    return lax.dot_general(a, b, (dims, ((), ())), preferred_element_type=F32)


def _tile(dim, pref, unit=LANES):
    if dim <= pref:
        return dim
    t = (pref // unit) * unit
    while t > unit and dim % t:
        t -= unit
    assert dim % t == 0, (dim, pref, unit)
    return t


def _params(sem=None):
    return pltpu.CompilerParams(dimension_semantics=sem, vmem_limit_bytes=VMEM_LIMIT)


class _Job:
    def __init__(self, inputs, outputs, aliases, nsem, copies):
        self.inputs, self.outputs, self.aliases, self.nsem, self.copies = inputs, outputs, aliases, nsem, copies

    def start(self, ins, outs, send, recv):
        for cp in self.copies(ins, outs, send, recv)[0]:
            cp.start()

    def finish(self, ins, outs, send, recv):
        started, landing = self.copies(ins, outs, send, recv)
        for make in landing:
            make().wait_recv()
        for cp in started:
            cp.wait_send()


def _hosted_call(body, name, grid, in_specs, out_specs, out_shape, scratch_shapes, args, jobs, semantics):
    nbi, nbo, nbs = len(in_specs), len(out_specs), len(scratch_shapes)
    if not jobs:
        res = pl.pallas_call(
            body, name=name, grid=grid, in_specs=list(in_specs), out_specs=list(out_specs), out_shape=list(out_shape),
            scratch_shapes=list(scratch_shapes), compiler_params=_params(semantics))(*args)
        return list(res), []
    j_in = [a for jb in jobs for a in jb.inputs]
    j_out = [o for jb in jobs for o in jb.outputs]
    aliases, ii, oo, sems = {}, nbi, nbo, []
    for jb in jobs:
        for a, b in jb.aliases.items():
            aliases[ii + a] = oo + b
        ii += len(jb.inputs)
        oo += len(jb.outputs)
        sems += [pltpu.SemaphoreType.DMA((jb.nsem,)), pltpu.SemaphoreType.DMA((jb.nsem,))]

    def wrapped(*refs):
        p = nbi + len(j_in)
        b_in, ji = refs[:nbi], refs[nbi:p]
        b_out, jo = refs[p:p + nbo], refs[p + nbo:p + nbo + len(j_out)]
        p += nbo + len(j_out)
        b_scr, js = refs[p:p + nbs], refs[p + nbs:]
        pids = [pl.program_id(k) for k in range(len(grid))]
        first = functools.reduce(jnp.logical_and, [pid == 0 for pid in pids])
        last = functools.reduce(jnp.logical_and, [pid == g - 1 for pid, g in zip(pids, grid)])

        def each(what):
            a = b = 0
            for n, jb in enumerate(jobs):
                getattr(jb, what)(ji[a:a + len(jb.inputs)], jo[b:b + len(jb.outputs)], js[2 * n], js[2 * n + 1])
                a += len(jb.inputs)
                b += len(jb.outputs)

        @pl.when(first)
        def _():
            each("start")

        body(*b_in, *b_out, *b_scr)

        @pl.when(last)
        def _():
            each("finish")

    res = pl.pallas_call(
        wrapped, name=name, grid=grid,
        in_specs=list(in_specs) + [ANY] * len(j_in), out_specs=list(out_specs) + [ANY] * len(j_out),
        out_shape=list(out_shape) + j_out, input_output_aliases=aliases,
        scratch_shapes=list(scratch_shapes) + sems,
        compiler_params=_params(("arbitrary",) * len(grid)))(*args, *j_in)
    return list(res[:nbo]), list(res[nbo:])


def _rope(a, c, s1, s2):
    half = ROPE_DIM // 2
    return a * c + pltpu.roll(a, HEAD_DIM - half, 1) * s1 + pltpu.roll(a, half, 1) * s2


def _matmul(a, b, mode, out_dtype, name, res=None, rope=None, swiglu=None, jobs=(), tm=1024, tn=1024, tk=2816):
    if mode == "nn":
        (M, K), (_, N) = a.shape, b.shape
    elif mode == "nt":
        (M, K), (N, _) = a.shape, b.shape
    else:
        (K, M), (_, N) = a.shape, b.shape
    tm = _tile(M, tm)
    tn = _tile(N, tn) if rope is None else _tile(math.gcd(N, rope[1]), tn)
    n_out = 2 if swiglu is not None else 1

    def vmem_bytes(t):
        osz = jnp.dtype(out_dtype).itemsize
        return (4 * t * (tm + tn) + 4 * tm * tn + (4 * tm * tn if t < K else 0) + 2 * n_out * tm * tn * osz
                + (8 * tm * tn if res is not None else 0) + (8 * tm * tn if swiglu is not None else 0)
                + (4 * tm * tn + 24 * tm * HEAD_DIM if rope is not None else 0))

    whole_k = () if swiglu is not None else (K,)
    tk = next(t for t in whole_k + (_tile(K, tk), _tile(K, 2048), _tile(K, 1024)) if vmem_bytes(t) <= MATMUL_VMEM)
    nk = K // tk
    dims = {"nn": NN, "nt": NT, "tn": TN}[mode]
    a_spec = (pl.BlockSpec((tk, tm), lambda i, j, k: (k, i)) if mode == "tn"
              else pl.BlockSpec((tm, tk), lambda i, j, k: (i, k)))
    b_spec = (pl.BlockSpec((tn, tk), lambda i, j, k: (j, k)) if mode == "nt"
              else pl.BlockSpec((tk, tn), lambda i, j, k: (k, j)))
    o_spec = pl.BlockSpec((tm, tn), lambda i, j, k: (i, j))
    n_extra = (1 if res is not None else 0) + (3 if rope is not None else 0) + (2 if swiglu is not None else 0)
    use_acc = nk > 1 or rope is not None
    if rope is not None:
        assert rope[1] % tn == 0

    def body(*refs):
        a_ref, b_ref = refs[:2]
        extra = refs[2:2 + n_extra]
        o_ref = refs[2 + n_extra]
        k = pl.program_id(2)
        prod = _dot(a_ref[...], b_ref[...], dims)
        if use_acc:
            acc_ref = refs[-1]
            if nk > 1:
                @pl.when(k == 0)
                def _():
                    acc_ref[...] = jnp.zeros(acc_ref.shape, F32)

                acc_ref[...] += prod
            else:
                acc_ref[...] = prod
            total = lambda: acc_ref[...]
        else:
            total = lambda: prod
        at_end = pl.when(k == nk - 1) if nk > 1 else (lambda f: f())

        if swiglu is not None:
            @at_end
            def _():
                d = total()
                g = extra[0][...].astype(F32)
                u = extra[1][...].astype(F32)
                sig = jax.nn.sigmoid(g)
                o_ref[...] = (d * u * sig * (1.0 + g * (1.0 - sig))).astype(out_dtype)
                refs[3 + n_extra][...] = (d * g * sig).astype(out_dtype)
        elif rope is None:
            @at_end
            def _():
                out = total()
                if res is not None:
                    out = out + extra[0][...]
                o_ref[...] = out.astype(out_dtype)
        else:
            roped = pl.program_id(1) * tn < rope[1]

            @pl.when(jnp.logical_and(k == nk - 1, roped))
            def _():
                c, s1, s2 = extra[0][...], extra[1][...], extra[2][...]
                for h in range(tn // HEAD_DIM):
                    cols = slice(h * HEAD_DIM, (h + 1) * HEAD_DIM)
                    o_ref[:, cols] = _rope(acc_ref[:, cols], c, s1, s2).astype(out_dtype)

            @pl.when(jnp.logical_and(k == nk - 1, jnp.logical_not(roped)))
            def _():
                o_ref[...] = acc_ref[...].astype(out_dtype)

    in_specs, args = [a_spec, b_spec], [a, b]
    if res is not None:
        in_specs.append(o_spec)
        args.append(res)
    if rope is not None:
        in_specs += [pl.BlockSpec((tm, HEAD_DIM), lambda i, j, k: (i, 0))] * 3
        args += list(rope[0])
    if swiglu is not None:
        in_specs += [o_spec, o_spec]
        args += list(swiglu)
    outs, extra_out = _hosted_call(
        body, name, (M // tm, N // tn, nk), in_specs, [o_spec] * n_out,
        [jax.ShapeDtypeStruct((M, N), out_dtype)] * n_out,
        [pltpu.VMEM((tm, tn), F32)] if use_acc else [], args, jobs, ("parallel", "parallel", "arbitrary"))
    out = outs[0] if n_out == 1 else tuple(outs)
    return (out, extra_out) if jobs else out


def _gate_up(hn, wg, wu, name, jobs=()):
    M, K = hn.shape
    N = wg.shape[0]
    tm, tn, tk = _tile(M, 1024), _tile(N, 512), _tile(K, 4096)
    nk = K // tk

    def body(a_ref, g_ref, u_ref, og_ref, ou_ref, oa_ref, *accs):
        k = pl.program_id(2)
        a = a_ref[...]
        pg = _dot(a, g_ref[...], NT)
        pu = _dot(a, u_ref[...], NT)

        def finish(g, u):
            og_ref[...] = g.astype(BF16)
            ou_ref[...] = u.astype(BF16)
            oa_ref[...] = (g * jax.nn.sigmoid(g) * u).astype(BF16)

        if nk == 1:
            finish(pg, pu)
        else:
            accg, accu = accs

            @pl.when(k == 0)
            def _():
                accg[...] = jnp.zeros(accg.shape, F32)
                accu[...] = jnp.zeros(accu.shape, F32)

            accg[...] += pg
            accu[...] += pu

            @pl.when(k == nk - 1)
            def _():
                finish(accg[...], accu[...])

    w_spec = pl.BlockSpec((tn, tk), lambda i, j, k: (j, k))
    o_spec = pl.BlockSpec((tm, tn), lambda i, j, k: (i, j))
    sds = jax.ShapeDtypeStruct((M, N), BF16)
    return _hosted_call(
        body, name, (M // tm, N // tn, nk), [pl.BlockSpec((tm, tk), lambda i, j, k: (i, k)), w_spec, w_spec],
        [o_spec, o_spec, o_spec], [sds, sds, sds],
        [pltpu.VMEM((tm, tn), F32), pltpu.VMEM((tm, tn), F32)] if nk > 1 else [],
        (hn, wg, wu), jobs, ("parallel", "parallel", "arbitrary"))


def _rmsnorm_fwd(x, g, name):
    M, D = x.shape
    tm = _tile(M, 256, 16)

    def body(x_ref, g_ref, o_ref):
        xv = x_ref[...]
        r = lax.rsqrt(jnp.mean(xv * xv, axis=-1, keepdims=True) + RMS_EPS)
        o_ref[...] = (xv * r * g_ref[...]).astype(BF16)

    return pl.pallas_call(
        body, name=name, grid=(M // tm,),
        in_specs=[pl.BlockSpec((tm, D), lambda i: (i, 0)), pl.BlockSpec((1, D), lambda i: (0, 0))],
        out_specs=pl.BlockSpec((tm, D), lambda i: (i, 0)),
        out_shape=jax.ShapeDtypeStruct((M, D), BF16),
        compiler_params=_params(("parallel",)),
    )(x, g)


def _fold8(v):
    tm, D = v.shape
    return jnp.sum(v.reshape(tm // 8, 8, D), axis=0)


def _rmsnorm_bwd(x, dy, g, res, name, want_bf16):
    M, D = x.shape
    tm = _tile(M, 128, 16)

    def body(x_ref, dy_ref, g_ref, r_ref, *outs):
        dx_ref, dg_ref = outs[0], outs[-1]
        xv = x_ref[...]
        dyv = dy_ref[...].astype(F32)
        r = lax.rsqrt(jnp.mean(xv * xv, axis=-1, keepdims=True) + RMS_EPS)
        u = dyv * g_ref[...]
        dot = jnp.sum(xv * u, axis=-1, keepdims=True)
        dx = r * u - xv * (r * r * r * (1.0 / D)) * dot + r_ref[...]
        dx_ref[...] = dx
        if want_bf16:
            outs[1][...] = dx.astype(BF16)
        part = _fold8(dyv * xv * r)

        @pl.when(pl.program_id(0) == 0)
        def _():
            dg_ref[...] = part

        @pl.when(pl.program_id(0) > 0)
        def _():
            dg_ref[...] += part

    row = pl.BlockSpec((tm, D), lambda i: (i, 0))
    out_specs = [row] + ([row] if want_bf16 else []) + [pl.BlockSpec((8, D), lambda i: (0, 0))]
    out_shape = ([jax.ShapeDtypeStruct((M, D), F32)] + ([jax.ShapeDtypeStruct((M, D), BF16)] if want_bf16 else [])
                 + [jax.ShapeDtypeStruct((8, D), F32)])
    return pl.pallas_call(
        body, name=name, grid=(M // tm,),
        in_specs=[row, row, pl.BlockSpec((1, D), lambda i: (0, 0)), row],
        out_specs=out_specs, out_shape=out_shape,
        compiler_params=_params(("arbitrary",)),
    )(x, dy, g, res)


def _loss_head(h, tgt, g, name):
    M, D = h.shape
    tm = _tile(M, 128, 16)

    def body(h_ref, t_ref, g_ref, dh_ref, dhb_ref, l_ref, dg_ref):
        hv = h_ref[...]
        gv = g_ref[...]
        r = lax.rsqrt(jnp.mean(hv * hv, axis=-1, keepdims=True) + RMS_EPS)
        n = hv * r
        e = n * gv - t_ref[...]
        dy = e * (1.0 / D)
        u = dy * gv
        dot = jnp.sum(hv * u, axis=-1, keepdims=True)
        dh = r * u - hv * (r * r * r * (1.0 / D)) * dot
        dh_ref[...] = dh
        dhb_ref[...] = dh.astype(BF16)
        rows = jnp.sum(e * e, axis=-1, keepdims=True)
        lpart = jnp.broadcast_to(jnp.sum(rows, axis=0, keepdims=True) * (0.5 / D), (8, LANES))
        gpart = _fold8(dy * n)

        @pl.when(pl.program_id(0) == 0)
        def _():
            l_ref[...] = lpart
            dg_ref[...] = gpart

        @pl.when(pl.program_id(0) > 0)
        def _():
            l_ref[...] += lpart
            dg_ref[...] += gpart

    row = pl.BlockSpec((tm, D), lambda i: (i, 0))
    return pl.pallas_call(
        body, name=name, grid=(M // tm,),
        in_specs=[row, row, pl.BlockSpec((1, D), lambda i: (0, 0))],
        out_specs=[row, row, pl.BlockSpec((8, LANES), lambda i: (0, 0)), pl.BlockSpec((8, D), lambda i: (0, 0))],
        out_shape=[jax.ShapeDtypeStruct((M, D), F32), jax.ShapeDtypeStruct((M, D), BF16),
                   jax.ShapeDtypeStruct((8, LANES), F32), jax.ShapeDtypeStruct((8, D), F32)],
        compiler_params=_params(("arbitrary",)),
    )(h, tgt, g)


def _headnorm_fwd(o_dil, o_sb, g, name):
    M, W = o_dil.shape
    D = 2 * W
    tm = _tile(M, 256, 16)

    def body(a_ref, b_ref, g_ref, o_ref):
        for h in range(D // HEAD_DIM):
            src = a_ref if h < W // HEAD_DIM else b_ref
            lo = (h * HEAD_DIM) % W
            v = src[:, lo:lo + HEAD_DIM]
            r = lax.rsqrt(jnp.mean(v * v, axis=-1, keepdims=True) + RMS_EPS)
            o_ref[:, h * HEAD_DIM:(h + 1) * HEAD_DIM] = (v * r * g_ref[:, h * HEAD_DIM:(h + 1) * HEAD_DIM]).astype(BF16)

    half = pl.BlockSpec((tm, W), lambda i: (i, 0))
    return pl.pallas_call(
        body, name=name, grid=(M // tm,),
        in_specs=[half, half, pl.BlockSpec((1, D), lambda i: (0, 0))],
        out_specs=pl.BlockSpec((tm, D), lambda i: (i, 0)),
        out_shape=jax.ShapeDtypeStruct((M, D), BF16),
        compiler_params=_params(("parallel",)),
    )(o_dil, o_sb, g)


def _headnorm_bwd(d_mix, o_dil, o_sb, g, name):
    M, W = o_dil.shape
    D = 2 * W
    tm = _tile(M, 128, 16)

    def body(d_ref, a_ref, b_ref, g_ref, do_ref, dg_ref):
        @pl.when(pl.program_id(0) == 0)
        def _():
            dg_ref[...] = jnp.zeros(dg_ref.shape, F32)

        for h in range(D // HEAD_DIM):
            src = a_ref if h < W // HEAD_DIM else b_ref
            lo = (h * HEAD_DIM) % W
            cols = slice(h * HEAD_DIM, (h + 1) * HEAD_DIM)
            v = src[:, lo:lo + HEAD_DIM]
            dy = d_ref[:, cols].astype(F32)
            r = lax.rsqrt(jnp.mean(v * v, axis=-1, keepdims=True) + RMS_EPS)
            u = dy * g_ref[:, cols]
            dot = jnp.sum(v * u, axis=-1, keepdims=True)
            do_ref[:, cols] = r * u - v * (r * r * r * (1.0 / HEAD_DIM)) * dot
            dg_ref[:, cols] += _fold8(dy * v * r)

    half = pl.BlockSpec((tm, W), lambda i: (i, 0))
    row = pl.BlockSpec((tm, D), lambda i: (i, 0))
    return pl.pallas_call(
        body, name=name, grid=(M // tm,),
        in_specs=[row, half, half, pl.BlockSpec((1, D), lambda i: (0, 0))],
        out_specs=[row, pl.BlockSpec((8, D), lambda i: (0, 0))],
        out_shape=[jax.ShapeDtypeStruct((M, D), F32), jax.ShapeDtypeStruct((8, D), F32)],
        compiler_params=_params(("arbitrary",)),
    )(d_mix, o_dil, o_sb, g)


def _rope_tables(positions):
    half = ROPE_DIM // 2
    inv_freq = jnp.power(jnp.float32(ROPE_THETA), -jnp.arange(half, dtype=F32) / half)
    ang = positions.astype(F32)[..., None] * inv_freq
    cos, sin = jnp.cos(ang), jnp.sin(ang)
    rest = HEAD_DIM - ROPE_DIM
    one = jnp.ones(cos.shape[:-1] + (rest,), F32)
    z16 = jnp.zeros_like(sin)
    zr = jnp.zeros_like(one)
    c = jnp.concatenate([cos, cos, one], axis=-1)
    s1 = jnp.concatenate([-sin, z16, zr], axis=-1)
    s2 = jnp.concatenate([z16, sin, zr], axis=-1)
    return c, s1, s2


def _dil_bias(T):
    ne = T // TQ
    e = jnp.arange(ne, dtype=jnp.int32)[:, None, None]
    r = jnp.arange(TQ, dtype=jnp.int32)[None, :, None]
    c = jnp.arange(TK_DIL, dtype=jnp.int32)[None, None, :]
    dist = e * TQ + r - c
    mult = jnp.zeros(dist.shape, F32)
    for d in DILATIONS:
        mult = mult + jnp.where((dist % d == 0) & (dist <= DIL_STEPS * d), 1.0, 0.0)
    return jnp.where((dist >= 0) & (mult > 0), jnp.log(jnp.maximum(mult, 1.0)), NEG)


def _dil_fwd(proj3, bias, Hd, name, jobs=()):
    B, T, N3 = proj3.shape
    nq = T // TQ
    per = TK_DIL // TQ
    scale = HEAD_DIM ** -0.5

    def body(q_ref, k_ref, v_ref, b_ref, o_ref, l_ref):
        i = pl.program_id(2)
        qs = [q_ref[r, :] for r in _ROW_PARTS]
        last = i // per

        def step(t, carry):
            j = last - t
            off = pl.multiple_of(j * TK_DIL, TK_DIL)
            kj = k_ref[pl.ds(off, TK_DIL), :]
            vj = v_ref[pl.ds(off, TK_DIL), :]
            new = []
            for n, r in enumerate(_ROW_PARTS):
                m, l, acc = carry[n]
                s = _dot(qs[n], kj, NT) * scale + b_ref[i - per * j, r, :]
                m_new = jnp.maximum(m, jnp.max(s, axis=-1, keepdims=True))
                p = jnp.exp(s - m_new)
                corr = jnp.exp(m - m_new)
                new.append((m_new, l * corr + jnp.sum(p, axis=-1, keepdims=True),
                            acc * corr + _dot(p.astype(BF16), vj, NN)))
            return tuple(new)

        init = (jnp.full((_PART, 1), NEG, F32), jnp.zeros((_PART, 1), F32), jnp.zeros((_PART, HEAD_DIM), F32))
        done = lax.fori_loop(0, last + 1, step, (init,) * len(_ROW_PARTS))
        for n, r in enumerate(_ROW_PARTS):
            m, l, acc = done[n]
            o_ref[r, :] = acc / l
            l_ref[r, :] = jnp.broadcast_to(m + jnp.log(l), (_PART, HEAD_DIM))

    full = lambda base: pl.BlockSpec((None, T, HEAD_DIM), lambda b, h, i: (b, 0, base + h))
    blk = lambda base: pl.BlockSpec((None, TQ, HEAD_DIM), lambda b, h, i: (b, i, base + h))
    sds = jax.ShapeDtypeStruct((B, T, Hd * HEAD_DIM), F32)
    return _hosted_call(
        body, name, (B, Hd, nq),
        [blk(0), full(Hd), full(2 * Hd), pl.BlockSpec(bias.shape, lambda b, h, i: (0, 0, 0))],
        [blk(0), blk(0)], [sds, sds], [], (proj3, proj3, proj3, bias), jobs, ("parallel", "parallel", "arbitrary"))


def _dil_bwd(proj3, bias, tabs, d_o3, o3, lse3, Hd, name, jobs=()):
    B, T, N3 = proj3.shape
    nq = T // TQ
    per = TK_DIL // TQ
    scale = HEAD_DIM ** -0.5

    def body(q_ref, k_ref, v_ref, b_ref, do_ref, o_ref, l_ref, cq, s1q, s2q, ct, s1t, s2t,
             dq_ref, dk_ref, dv_ref, dks, dvs):
        i = pl.program_id(2)

        @pl.when(i == 0)
        def _():
            dks[...] = jnp.zeros(dks.shape, F32)
            dvs[...] = jnp.zeros(dvs.shape, F32)

        q = q_ref[...]
        do = do_ref[...]
        dob = do.astype(BF16)
        dterm = jnp.sum(do * o_ref[...], axis=-1, keepdims=True)
        lse = jnp.concatenate([l_ref[...]] * (TK_DIL // HEAD_DIM), axis=1)
        last = i // per

        def step(j, dqs):
            off = pl.multiple_of(j * TK_DIL, TK_DIL)
            kj = k_ref[pl.ds(off, TK_DIL), :]
            vj = v_ref[pl.ds(off, TK_DIL), :]
            ps, dss, new = [], [], []
            for n, r in enumerate(_ROW_PARTS):
                s = _dot(q[r], kj, NT) * scale + b_ref[i - per * j, r, :]
                p = jnp.exp(s - lse[r])
                dp = _dot(dob[r], vj, NT)
                ds = (p * (dp - dterm[r]) * scale).astype(BF16)
                new.append(dqs[n] + _dot(ds, kj, NN))
                ps.append(p.astype(BF16))
                dss.append(ds)
            dks[pl.ds(off, TK_DIL), :] += _dot(jnp.concatenate(dss, axis=0), q, TN)
            dvs[pl.ds(off, TK_DIL), :] += _dot(jnp.concatenate(ps, axis=0), dob, TN)
            return tuple(new)

        zero = jnp.zeros((_PART, HEAD_DIM), F32)
        dq = jnp.concatenate(lax.fori_loop(0, last + 1, step, (zero,) * len(_ROW_PARTS)), axis=0)
        dq_ref[...] = _rope(dq, cq[...], -s1q[...], -s2q[...]).astype(BF16)

        @pl.when(i == nq - 1)
        def _():
            dk_ref[...] = _rope(dks[...], ct[...], -s1t[...], -s2t[...]).astype(BF16)
            dv_ref[...] = dvs[...].astype(BF16)

    full = lambda base: pl.BlockSpec((None, T, HEAD_DIM), lambda b, h, i: (b, 0, base + h))
    blk = lambda base: pl.BlockSpec((None, TQ, HEAD_DIM), lambda b, h, i: (b, i, base + h))
    tab_q = pl.BlockSpec((None, TQ, HEAD_DIM), lambda b, h, i: (b, i, 0))
    tab_t = pl.BlockSpec((None, T, HEAD_DIM), lambda b, h, i: (b, 0, 0))
    sds = jax.ShapeDtypeStruct((B, T, Hd * HEAD_DIM), BF16)
    return _hosted_call(
        body, name, (B, Hd, nq),
        [blk(0), full(Hd), full(2 * Hd), pl.BlockSpec(bias.shape, lambda b, h, i: (0, 0, 0)),
         blk(0), blk(0), blk(0), tab_q, tab_q, tab_q, tab_t, tab_t, tab_t],
        [blk(0), full(0), full(0)], [sds, sds, sds],
        [pltpu.VMEM((T, HEAD_DIM), F32), pltpu.VMEM((T, HEAD_DIM), F32)],
        (proj3, proj3, proj3, bias, d_o3, o3, lse3, *tabs, *tabs), jobs, ("parallel", "parallel", "arbitrary"))


def _softplus(z):
    return jnp.maximum(z, 0.0) + jnp.log(1.0 + jnp.exp(-jnp.abs(z)))


def _tri_sum(x, m01):
    n = x.shape[0]
    hi = x.astype(BF16)
    lo = (x - hi.astype(F32)).astype(BF16)
    both = _dot(jnp.concatenate([hi, lo], axis=0), m01, NN)
    return both[:n] + both[n:]


def _sb_iota(rows=TQ, first=0):
    row = lax.broadcasted_iota(jnp.int32, (rows, TK_SB), 0) + first
    col = lax.broadcasted_iota(jnp.int32, (rows, TK_SB), 1)
    return row, col


def _sb_fwd(proj3, Hd, Hs, name, jobs=()):
    B, T, N3 = proj3.shape
    nq = T // TQ
    qb, kb_, vb_ = 3 * Hd, 3 * Hd + Hs, 3 * Hd + 2 * Hs
    scale = HEAD_DIM ** -0.5

    assert TQ == 2 * TK_SB
    low = slice(TK_SB, TQ)

    def body(q_ref, k_ref, v_ref, o_ref, lt_ref):
        i = pl.program_id(2)
        q = q_ref[...]
        row, col = _sb_iota(TK_SB)
        after = jnp.where(row > col, 1.0, 0.0).astype(BF16)

        def chunk(j, qr, rows, first_row, acc, cs):
            off = pl.multiple_of(j * TK_SB, TK_SB)
            kj = k_ref[pl.ds(off, TK_SB), :]
            vj = v_ref[pl.ds(off, TK_SB), :]
            r_i, c_i = _sb_iota(rows, first_row)
            msk = c_i < r_i + (i * TQ - j * TK_SB)
            z = _dot(qr, kj, NT) * scale
            sp = _softplus(z)
            ln = jnp.where(msk, -sp, 0.0)
            excl = _tri_sum(ln, after) + cs
            a = jnp.where(msk, jnp.exp(z - sp + excl), 0.0)
            return acc + _dot(a.astype(BF16), vj, NN), cs + jnp.sum(ln, axis=-1, keepdims=True)

        acc_l, cs_l = chunk(2 * i + 1, q[low], TK_SB, TK_SB, jnp.zeros((TK_SB, HEAD_DIM), F32),
                            jnp.zeros((TK_SB, 1), F32))
        init = (jnp.concatenate([jnp.zeros((TK_SB, HEAD_DIM), F32), acc_l], axis=0),
                jnp.concatenate([jnp.zeros((TK_SB, 1), F32), cs_l], axis=0))
        acc, cs = lax.fori_loop(0, 2 * i + 1, lambda t, carry: chunk(2 * i - t, q, TQ, 0, *carry), init)
        o_ref[...] = acc
        lt_ref[...] = jnp.broadcast_to(cs, (TQ, HEAD_DIM))

    full = lambda base: pl.BlockSpec((None, T, HEAD_DIM), lambda b, h, i: (b, 0, base + h))
    blk = lambda base: pl.BlockSpec((None, TQ, HEAD_DIM), lambda b, h, i: (b, i, base + h))
    sds = jax.ShapeDtypeStruct((B, T, Hs * HEAD_DIM), F32)
    return _hosted_call(
        body, name, (B, Hs, nq), [blk(qb), full(kb_), full(vb_)], [blk(0), blk(0)], [sds, sds], [],
        (proj3, proj3, proj3), jobs, ("parallel", "parallel", "arbitrary"))


def _sb_bwd(proj3, d_o3, lt3, Hd, Hs, name, jobs=()):
    B, T, N3 = proj3.shape
    nq = T // TQ
    qb, kb_, vb_ = 3 * Hd, 3 * Hd + Hs, 3 * Hd + 2 * Hs
    scale = HEAD_DIM ** -0.5

    assert TQ == 2 * TK_SB
    low = slice(TK_SB, TQ)

    def body(q_ref, k_ref, v_ref, do_ref, lt_ref, dq_ref, dk_ref, dv_ref, dks, dvs):
        i = pl.program_id(2)

        @pl.when(i == 0)
        def _():
            dks[...] = jnp.zeros(dks.shape, F32)
            dvs[...] = jnp.zeros(dvs.shape, F32)

        q = q_ref[...]
        dob = do_ref[...].astype(BF16)
        total = jnp.concatenate([lt_ref[...]] * (TK_SB // HEAD_DIM), axis=1)
        row, col = _sb_iota(TK_SB)
        before = jnp.where(row < col, 1.0, 0.0).astype(BF16)

        def chunk(j, qr, dor, tot, rows, first_row, dq, pc, gc):
            off = pl.multiple_of(j * TK_SB, TK_SB)
            kj = k_ref[pl.ds(off, TK_SB), :]
            vj = v_ref[pl.ds(off, TK_SB), :]
            r_i, c_i = _sb_iota(rows, first_row)
            msk = c_i < r_i + (i * TQ - j * TK_SB)
            z = _dot(qr, kj, NT) * scale
            sp = _softplus(z)
            ln = jnp.where(msk, -sp, 0.0)
            excl = tot - (_tri_sum(ln, before) + ln + pc)
            a = jnp.where(msk, jnp.exp(z - sp + excl), 0.0)
            g = a * _dot(dor, vj, NT)
            big = _tri_sum(g, before) + gc
            sig = jnp.exp(z - sp)
            dz = (jnp.where(msk, g * (1.0 - sig) - big * sig, 0.0) * scale).astype(BF16)
            dks[pl.ds(off, TK_SB), :] += _dot(dz, qr, TN)
            dvs[pl.ds(off, TK_SB), :] += _dot(a.astype(BF16), dor, TN)
            return (dq + _dot(dz, kj, NN), pc + jnp.sum(ln, axis=-1, keepdims=True),
                    gc + jnp.sum(g, axis=-1, keepdims=True))

        zero1 = jnp.zeros((TQ, 1), F32)
        dq, pc, gc = lax.fori_loop(0, 2 * i + 1, lambda j, carry: chunk(j, q, dob, total, TQ, 0, *carry),
                                   (jnp.zeros((TQ, HEAD_DIM), F32), zero1, zero1))
        dq_l, _, _ = chunk(2 * i + 1, q[low], dob[low], total[low], TK_SB, TK_SB, dq[low], pc[low], gc[low])
        dq_ref[...] = jnp.concatenate([dq[:TK_SB], dq_l], axis=0).astype(BF16)

        @pl.when(i == nq - 1)
        def _():
            dk_ref[...] = dks[...].astype(BF16)
            dv_ref[...] = dvs[...].astype(BF16)

    full = lambda base: pl.BlockSpec((None, T, HEAD_DIM), lambda b, h, i: (b, 0, base + h))
    blk = lambda base: pl.BlockSpec((None, TQ, HEAD_DIM), lambda b, h, i: (b, i, base + h))
    sds = jax.ShapeDtypeStruct((B, T, Hs * HEAD_DIM), BF16)
    return _hosted_call(
        body, name, (B, Hs, nq), [blk(qb), full(kb_), full(vb_), blk(Hd), blk(0)],
        [blk(0), full(0), full(0)], [sds, sds, sds],
        [pltpu.VMEM((T, HEAD_DIM), F32), pltpu.VMEM((T, HEAD_DIM), F32)],
        (proj3, proj3, proj3, d_o3, lt3), jobs, ("parallel", "parallel", "arbitrary"))


def _place():
    x, y, c = lax.axis_index("x"), lax.axis_index("y"), lax.axis_index("c")
    chips = [(1 - x, y), (x, 1 - y), (1 - x, 1 - y)]
    return x, y, c, chips


class _Weight:
    def __init__(self, kind, shard_shape, slot):
        self.kind, self.slot = kind, slot
        self.R, self.C = shard_shape
        if kind == "col":
            assert slot == self.C
            self.full = (self.R, N_CHIPS * slot)
            self.half = (self.R // 2, N_CHIPS * slot)
            self.piece = (self.R // 2, slot)
        else:
            self.full = (N_CHIPS * slot, self.C)
            self.half = (N_CHIPS * slot, self.C // 2)
            self.piece = (self.R, self.C // 2)

    def _rows(self, jj, n):
        return pl.ds(pl.multiple_of(jj * self.slot, 16), n)

    def full_half(self, ref, jj, hc):
        if self.kind == "col":
            return ref.at[pl.ds(hc * (self.R // 2), self.R // 2), pl.ds(pl.multiple_of(jj * self.slot, LANES), self.C)]
        return ref.at[self._rows(jj, self.R), pl.ds(hc * (self.C // 2), self.C // 2)]

    def region_half(self, ref, hc):
        if self.kind == "col":
            return ref.at[pl.ds(hc * (self.R // 2), self.R // 2), :]
        return ref.at[:, pl.ds(hc * (self.C // 2), self.C // 2)]

    def half_piece(self, ref, jj):
        if self.kind == "col":
            return ref.at[:, pl.ds(pl.multiple_of(jj * self.slot, LANES), self.slot)]
        return ref.at[self._rows(jj, self.R), :]


def _cast_into_full(w, geo, j_arr, name):
    R, C = w.shape
    tr = _tile(R, 256, 16)
    if geo.kind == "col":
        o_spec = pl.BlockSpec((tr, C), lambda i, j_ref: (i, j_ref[0]))
    else:
        while geo.slot % tr or R % tr:
            tr -= 16
        spb = geo.slot // tr
        o_spec = pl.BlockSpec((tr, C), lambda i, j_ref: (j_ref[0] * spb + i, 0))

    def body(j_ref, w_ref, o_ref):
        o_ref[...] = w_ref[...].astype(BF16)

    return pl.pallas_call(
        body, name=name,
        grid_spec=pltpu.PrefetchScalarGridSpec(
            num_scalar_prefetch=1, grid=(R // tr,),
            in_specs=[pl.BlockSpec((tr, C), lambda i, j_ref: (i, 0))], out_specs=o_spec),
        out_shape=jax.ShapeDtypeStruct(geo.full, BF16),
        compiler_params=_params(("parallel",)),
    )(j_arr, w)


def _zero_pad(full, geo, name):
    pad = geo.slot - geo.R
    assert geo.kind == "row" and pad > 0 and geo.R % pad == 0 and geo.slot % pad == 0

    def body(f_ref, o_ref):
        o_ref[...] = jnp.zeros(o_ref.shape, BF16)

    return pl.pallas_call(
        body, name=name, grid=(N_CHIPS,), in_specs=[ANY],
        out_specs=pl.BlockSpec((pad, geo.C), lambda jj: ((jj * geo.slot + geo.R) // pad, 0)),
        out_shape=jax.ShapeDtypeStruct(geo.full, BF16), input_output_aliases={0: 0},
        compiler_params=_params(("arbitrary",)),
    )(full)


def _gather_job(full, geo, stage):
    def copies(ins, outs, send, recv):
        x, y, c, chips = _place()
        j = 2 * x + y

        def cp(src, dst, k, to):
            return pltpu.make_async_remote_copy(src_ref=src, dst_ref=dst, send_sem=send.at[k], recv_sem=recv.at[k],
                                                device_id=to, device_id_type=MESH)

        started, landing = [], []
        for k, (cx, cy) in enumerate(chips):
            jk = 2 * cx + cy
            if stage == "ici":
                started.append(cp(geo.full_half(ins[0], j, c), geo.full_half(outs[0], j, c), k, (cx, cy, c)))
                part = geo.full_half(outs[0], jk, c)
            else:
                started.append(cp(geo.full_half(ins[0], jk, c), geo.full_half(outs[0], jk, c), k, (x, y, 1 - c)))
                part = geo.full_half(outs[0], jk, 1 - c)
            landing.append(functools.partial(cp, part, part, k, (x, y, c)))
        return started, landing

    return _Job([full], [jax.ShapeDtypeStruct(geo.full, BF16)], {0: 0}, 3, copies)


def _sibling_job(grad_full, geo):
    def copies(ins, outs, send, recv):
        x, y, c, _ = _place()
        mk = lambda src, to: pltpu.make_async_remote_copy(
            src_ref=src, dst_ref=outs[0], send_sem=send.at[0], recv_sem=recv.at[0], device_id=to, device_id_type=MESH)
        return [mk(geo.region_half(ins[0], 1 - c), (x, y, 1 - c))], [functools.partial(mk, outs[0], (x, y, c))]

    return _Job([grad_full], [jax.ShapeDtypeStruct(geo.half, BF16)], {}, 1, copies)


def _chips_job(chip_sum, geo, part=(0, 1), so_far=None):
    p, n = part
    rows = geo.piece[0] // n
    assert geo.piece[0] % n == 0 and rows % 16 == 0

    def copies(ins, outs, send, recv):
        x, y, c, chips = _place()
        started, landing = [], []

        def mk(k, src, to):
            dst = outs[0].at[k, pl.ds(p * rows, rows), :]
            return pltpu.make_async_remote_copy(src_ref=dst if src is None else src, dst_ref=dst, send_sem=send.at[k],
                                                recv_sem=recv.at[k], device_id=to, device_id_type=MESH)

        for k, (cx, cy) in enumerate(chips):
            started.append(mk(k, geo.half_piece(ins[0], 2 * cx + cy).at[pl.ds(p * rows, rows), :], (cx, cy, c)))
            landing.append(functools.partial(mk, k, None, (x, y, c)))
        return started, landing

    inputs = [chip_sum] + ([] if so_far is None else [so_far])
    return _Job(inputs, [jax.ShapeDtypeStruct((3,) + geo.piece, BF16)], {} if so_far is None else {1: 0}, 3, copies)


def _final_job(half, geo):
    def copies(ins, outs, send, recv):
        x, y, c, _ = _place()
        mk = lambda src, to: pltpu.make_async_remote_copy(
            src_ref=src, dst_ref=outs[0], send_sem=send.at[0], recv_sem=recv.at[0], device_id=to, device_id_type=MESH)
        return [mk(ins[0], (x, y, 1 - c))], [functools.partial(mk, outs[0], (x, y, c))]

    return _Job([half], [jax.ShapeDtypeStruct(geo.piece, F32)], {}, 1, copies)


def _gather_two_step(full, geo, name):
    R = geo.R
    assert geo.kind == "col" and R % 64 == 0

    def body(own_ref, f_ref, send, recv):
        x, y, c, _ = _place()
        j, jx, jy, jd = 2 * x + y, 2 * (1 - x) + y, 2 * x + (1 - y), 2 * (1 - x) + (1 - y)
        me, xn, yn, sib = (x, y, c), (1 - x, y, c), (x, 1 - y, c), (x, y, 1 - c)

        def half(ref, jj, hc):
            return geo.full_half(ref, jj, hc)

        def quarter(ref, jj, hc, q):
            return ref.at[pl.ds(hc * (R // 2) + q * (R // 4), R // 4), pl.ds(pl.multiple_of(jj * geo.slot, LANES), geo.C)]

        def cp(k, src, dst, to):
            return pltpu.make_async_remote_copy(src_ref=src, dst_ref=dst, send_sem=send.at[k], recv_sem=recv.at[k],
                                                device_id=to, device_id_type=MESH)

        def arrived(k, part):
            cp(k, part, part, me).wait_recv()

        def pass_on(k, part, to):
            d = cp(k, part, part, to)
            d.start()
            return d

        sent = [cp(0, half(own_ref, j, c), half(f_ref, j, c), xn), cp(1, half(own_ref, j, c), half(f_ref, j, c), yn)]
        for d in sent:
            d.start()
        arrived(0, half(f_ref, jx, c))
        sent.append(pass_on(2, quarter(f_ref, jx, c, 0), yn))
        sent.append(pass_on(4, half(f_ref, jx, c), sib))
        arrived(1, half(f_ref, jy, c))
        sent.append(pass_on(3, quarter(f_ref, jy, c, 1), xn))
        sent.append(pass_on(5, half(f_ref, jy, c), sib))
        arrived(2, quarter(f_ref, jd, c, 0))
        sent.append(pass_on(6, quarter(f_ref, jd, c, 0), sib))
        arrived(3, quarter(f_ref, jd, c, 1))
        sent.append(pass_on(7, quarter(f_ref, jd, c, 1), sib))
        arrived(4, half(f_ref, jx, 1 - c))
        arrived(5, half(f_ref, jy, 1 - c))
        arrived(6, quarter(f_ref, jd, 1 - c, 0))
        arrived(7, quarter(f_ref, jd, 1 - c, 1))
        for d in sent:
            d.wait_send()

    return pl.pallas_call(
        body, name=name, in_specs=[ANY], out_specs=ANY, out_shape=jax.ShapeDtypeStruct(geo.full, BF16),
        input_output_aliases={0: 0},
        scratch_shapes=[pltpu.SemaphoreType.DMA((8,)), pltpu.SemaphoreType.DMA((8,))],
    )(full)


def _chip_sum(g_full, got, geo, c_arr, name):
    Rh, Ch = geo.half
    tr, tc = _tile(Rh, 256, 16), _tile(Ch, 2048)
    nrb, ncb = Rh // tr, Ch // tc

    def body(c_ref, a_ref, b_ref, o_ref):
        o_ref[...] = (a_ref[...].astype(F32) + b_ref[...].astype(F32)).astype(BF16)

    if geo.kind == "col":
        a_spec = pl.BlockSpec((tr, tc), lambda i, j, c_ref: (c_ref[0] * nrb + i, j))
    else:
        a_spec = pl.BlockSpec((tr, tc), lambda i, j, c_ref: (i, c_ref[0] * ncb + j))
    spec = pl.BlockSpec((tr, tc), lambda i, j, c_ref: (i, j))
    return pl.pallas_call(
        body, name=name,
        grid_spec=pltpu.PrefetchScalarGridSpec(num_scalar_prefetch=1, grid=(nrb, ncb),
                                               in_specs=[a_spec, spec], out_specs=spec),
        out_shape=jax.ShapeDtypeStruct(geo.half, BF16),
        compiler_params=_params(("parallel", "parallel")),
    )(c_arr, g_full, got)


def _final_sum(chip_sum, got, geo, j_arr, name):
    Rp, Cp = geo.piece
    if geo.kind == "col":
        tr = _tile(Rp, 128, 16)
        a_spec = pl.BlockSpec((tr, Cp), lambda i, j_ref: (i, j_ref[0]))
    else:
        tr = _tile(Rp, 128, 16)
        while geo.slot % tr:
            tr -= 16
        assert Rp % tr == 0 and geo.slot % tr == 0
        spb = geo.slot // tr
        a_spec = pl.BlockSpec((tr, Cp), lambda i, j_ref: (j_ref[0] * spb + i, 0))
    r_specs = [pl.BlockSpec((None, tr, Cp), functools.partial(lambda i, j_ref, k: (k, i, 0), k=k)) for k in range(3)]
    o_spec = pl.BlockSpec((tr, Cp), lambda i, j_ref: (i, 0))

    def body(j_ref, a_ref, r0, r1, r2, o_ref):
        o_ref[...] = ((a_ref[...].astype(F32) + r0[...].astype(F32)) + r1[...].astype(F32)) + r2[...].astype(F32)

    return pl.pallas_call(
        body, name=name,
        grid_spec=pltpu.PrefetchScalarGridSpec(num_scalar_prefetch=1, grid=(Rp // tr,),
                                               in_specs=[a_spec] + r_specs, out_specs=o_spec),
        out_shape=jax.ShapeDtypeStruct(geo.piece, F32),
        compiler_params=_params(("parallel",)),
    )(j_arr, chip_sum, got, got, got)


def _run_job(job, name):
    ni = len(job.inputs)

    def body(*refs):
        ins, outs, (send, recv) = refs[:ni], refs[ni:-2], refs[-2:]
        job.start(ins, outs, send, recv)
        job.finish(ins, outs, send, recv)

    return pl.pallas_call(
        body, name=name, in_specs=[ANY] * ni, out_specs=[ANY] * len(job.outputs), out_shape=list(job.outputs),
        input_output_aliases=dict(job.aliases),
        scratch_shapes=[pltpu.SemaphoreType.DMA((job.nsem,)), pltpu.SemaphoreType.DMA((job.nsem,))],
    )(*job.inputs)


def _adam_math(w, g, m, v):
    m = ADAM_B1 * m + (1.0 - ADAM_B1) * g
    v = ADAM_B2 * v + (1.0 - ADAM_B2) * (g * g)
    m_hat = m / (1.0 - ADAM_B1 ** ADAM_STEP)
    v_hat = v / (1.0 - ADAM_B2 ** ADAM_STEP)
    delta = -ADAM_LR * (m_hat / (jnp.sqrt(v_hat) + ADAM_EPS) + ADAM_WD * w)
    return delta, m, v


def _adam(ws, mines, theirs, ms, vs, geo, name, jobs=()):
    n = len(ws)
    R, C = ws[0].shape
    tr = _tile(R, 128 if n == 1 else 32, 8)
    nrb = R // tr
    col = geo.kind == "col"
    assert not col or nrb % 2 == 0

    def rows(k, halved):
        def index(w, i):
            r = jnp.where(w == k, i, jnp.where(w < k, 0, nrb - 1))
            return (r % (nrb // 2) if halved else r, 0)
        return index

    def body(*refs):
        ins, outs = refs[:5 * n], refs[5 * n:]
        wi, i = pl.program_id(0), pl.program_id(1)
        c = lax.axis_index("c")
        for k in range(n):
            @pl.when(wi == k)
            def _():
                w_ref, a_ref, b_ref, m_ref, v_ref = ins[5 * k:5 * k + 5]
                if col:
                    gv = jnp.where(i // (nrb // 2) == c, a_ref[...], b_ref[...])
                else:
                    a, b = a_ref[...], b_ref[...]
                    gv = jnp.concatenate([jnp.where(c == 0, a, b), jnp.where(c == 0, b, a)], axis=1)
                delta, mn, vn = _adam_math(w_ref[...], gv, m_ref[...], v_ref[...])
                og, od, om, ov = outs[4 * k:4 * k + 4]
                og[...] = gv
                od[...] = delta
                om[...] = mn
                ov[...] = vn

    in_specs, out_specs, args = [], [], []
    for k in range(n):
        spec = pl.BlockSpec((tr, C), rows(k, False))
        h_spec = pl.BlockSpec((tr, C), rows(k, True)) if col else pl.BlockSpec((tr, C // 2), rows(k, False))
        in_specs += [spec, h_spec, h_spec, spec, spec]
        out_specs += [spec] * 4
        args += [ws[k], mines[k], theirs[k], ms[k], vs[k]]
    outs, job_outs = _hosted_call(body, name, (n, nrb), in_specs, out_specs,
                                  [jax.ShapeDtypeStruct((R, C), F32)] * (4 * n), [], args, jobs,
                                  ("arbitrary", "arbitrary"))
    return [outs[4 * k:4 * k + 4] for k in range(n)], job_outs


def _small_all_reduce(parts, loss_part, D):
    n = len(parts)

    def body(*refs):
        p_refs, l_ref, o_ref, vec, buf, send, recv = refs[:n], refs[n], refs[n + 1], *refs[n + 2:]
        x, y, c, _ = _place()
        me = 4 * x + 2 * y + c
        vec[...] = jnp.zeros(vec.shape, F32)
        for r in range(n):
            vec[r:r + 1, :] = jnp.sum(p_refs[r][...], axis=0, keepdims=True)
        vec[n:n + 1, 0:LANES] = l_ref[0:1, :]
        buf[me] = vec[...]
        cps = []
        for dd in range(1, 8):
            bx, by, bc = (dd >> 2) & 1, (dd >> 1) & 1, dd & 1
            peer = (x + bx - 2 * x * bx, y + by - 2 * y * by, c + bc - 2 * c * bc)
            cp = pltpu.make_async_remote_copy(
                src_ref=vec, dst_ref=buf.at[me], send_sem=send.at[dd - 1], recv_sem=recv.at[dd - 1],
                device_id=peer, device_id_type=MESH)
            cp.start()
            cps.append(cp)
        for cp in cps:
            cp.wait()
        tot = buf[0]
        for s in range(1, 8):
            tot = tot + buf[s]
        o_ref[...] = tot

    vm = pl.BlockSpec(memory_space=pltpu.VMEM)
    return pl.pallas_call(
        body, name="small_all_reduce",
        in_specs=[vm] * (n + 1), out_specs=vm,
        out_shape=jax.ShapeDtypeStruct((8, D), F32),
        scratch_shapes=[pltpu.VMEM((8, D), F32), pltpu.VMEM((8, 8, D), F32),
                        pltpu.SemaphoreType.DMA((7,)), pltpu.SemaphoreType.DMA((7,))],
    )(*parts, loss_part)


def _small_adam(tot, ws, ms, vs, rows):
    n = len(ws)

    def body(*refs):
        t_ref = refs[0]
        w_refs, m_refs, v_refs = refs[1:1 + n], refs[1 + n:1 + 2 * n], refs[1 + 2 * n:1 + 3 * n]
        outs = refs[1 + 3 * n:]
        for i in range(n):
            r, c0 = rows[i]
            width = w_refs[i].shape[1]
            g = t_ref[r:r + 1, c0:c0 + width]
            delta, mn, vn = _adam_math(w_refs[i][...], g, m_refs[i][...], v_refs[i][...])
            outs[4 * i][...] = g
            outs[4 * i + 1][...] = delta
            outs[4 * i + 2][...] = mn
            outs[4 * i + 3][...] = vn

    vm = pl.BlockSpec(memory_space=pltpu.VMEM)
    out_shape = []
    for w in ws:
        out_shape += [jax.ShapeDtypeStruct(w.shape, F32)] * 4
    return pl.pallas_call(
        body, name="small_adam",
        in_specs=[vm] * (1 + 3 * n), out_specs=[vm] * (4 * n), out_shape=out_shape,
    )(tot, *ws, *ms, *vs)


def kernel(x, positions, norm_mix_g, w_in, norm_out_dil_g, norm_out_sb_g, w_out, norm_ffn_g, w_gate, w_up, w_down, norm_final_g, loss_target, m_norm_mix_g, m_w_in, m_norm_out_dil_g, m_norm_out_sb_g, m_w_out, m_norm_ffn_g, m_w_gate, m_w_up, m_w_down, m_norm_final_g, v_norm_mix_g, v_w_in, v_norm_out_dil_g, v_norm_out_sb_g, v_w_out, v_norm_ffn_g, v_w_gate, v_w_up, v_w_down, v_norm_final_g):
    B, T, D = x.shape
    M = B * T
    n_heads = D // HEAD_DIM
    Hd = n_heads // 2
    Hs = n_heads - Hd
    W = Hd * HEAD_DIM
    N3 = 3 * D
    fs = w_gate.shape[2]
    fp = -(-fs // LANES) * LANES
    assert T % TK_DIL == 0 and W == Hs * HEAD_DIM and fp > fs

    x2 = x.reshape(M, D)
    tgt = loss_target.reshape(M, D)
    tr_ = jnp.transpose
    big = [w_in[0], w_out[0], tr_(w_gate[0]), tr_(w_up[0]), w_down[0]]
    big_m = [m_w_in[0], m_w_out[0], tr_(m_w_gate[0]), tr_(m_w_up[0]), m_w_down[0]]
    big_v = [v_w_in[0], v_w_out[0], tr_(v_w_gate[0]), tr_(v_w_up[0]), v_w_down[0]]
    names = ["w_in", "w_out", "w_gate", "w_up", "w_down"]
    c_arr = jnp.reshape(lax.axis_index("c"), (1,)).astype(jnp.int32)
    j_arr = jnp.reshape(2 * lax.axis_index("x") + lax.axis_index("y"), (1,)).astype(jnp.int32)

    ns_in = w_in.shape[2]
    ks_out = w_out.shape[1]
    geo = [_Weight("col", (D, ns_in), ns_in), _Weight("row", (ks_out, D), ks_out),
           _Weight("row", (fs, D), fp), _Weight("row", (fs, D), fp), _Weight("row", (fs, D), fp)]
    g_in, g_out, g_gate, g_up, g_down = geo
    fulls = [_cast_into_full(big[w], geo[w], j_arr, "cast_" + names[w]) for w in range(5)]
    for w in (2, 3, 4):
        fulls[w] = _zero_pad(fulls[w], geo[w], "zero_pad_" + names[w])

    def chip_sum(w, grad_full, from_sibling):
        return _chip_sum(grad_full, from_sibling, geo[w], c_arr, "chip_sum_" + names[w])

    def final_sum(w, cs, from_chips):
        return _final_sum(cs, from_chips, geo[w], j_arr, "final_sum_" + names[w])

    Win = _gather_two_step(fulls[0], g_in, "gather_w_in")
    tabs3 = _rope_tables(positions)
    tabs2 = [t.reshape(M, HEAD_DIM) for t in tabs3]
    hn = _rmsnorm_fwd(x2, norm_mix_g, "norm_mix")
    proj, (WgT,) = _matmul(hn, Win, "nn", BF16, "in_proj", rope=(tabs2, 2 * W),
                           jobs=[_gather_job(fulls[2], g_gate, "ici")])
    proj3 = proj.reshape(B, T, N3)
    bias = _dil_bias(T)
    (o_dil3, lse3), (WgT, Wout) = _dil_fwd(
        proj3, bias, Hd, "dil_fwd", jobs=[_gather_job(WgT, g_gate, "d2d"), _gather_job(fulls[1], g_out, "ici")])
    (o_sb3, lt3), (Wout, WuT) = _sb_fwd(
        proj3, Hd, Hs, "sb_fwd", jobs=[_gather_job(Wout, g_out, "d2d"), _gather_job(fulls[3], g_up, "ici")])
    o_dil, o_sb = o_dil3.reshape(M, W), o_sb3.reshape(M, W)
    g_heads = jnp.concatenate([norm_out_dil_g, norm_out_sb_g], axis=1)
    o_mix = _headnorm_fwd(o_dil, o_sb, g_heads, "head_norm")
    h1, (WuT,) = _matmul(o_mix, Wout, "nn", F32, "out_proj", res=x2, jobs=[_gather_job(WuT, g_up, "d2d")])
    hn2 = _rmsnorm_fwd(h1, norm_ffn_g, "norm_ffn")
    (gate, up, act), (Wd,) = _gate_up(hn2, WgT, WuT, "gate_up", jobs=[_gather_job(fulls[4], g_down, "ici")])
    (Wd,) = _run_job(_gather_job(Wd, g_down, "d2d"), "gather_w_down_d2d")
    ffn_k = dict(tn=512, tk=fp * N_CHIPS // 2)
    h2 = _matmul(act, Wd, "nn", F32, "down_proj", res=h1, **ffn_k)
    dh2, dh2_b, loss_part, dg_final = _loss_head(h2, tgt, norm_final_g.reshape(1, D), "loss_head")

    dWd = _matmul(act, dh2_b, "tn", BF16, "dw_down")
    (d_gate, d_up), (sib_d,) = _matmul(dh2_b, Wd, "nt", BF16, "d_act", swiglu=(gate, up),
                                       jobs=[_sibling_job(dWd, g_down)])
    cs_d = chip_sum(4, dWd, sib_d)
    dWg, (chips_d,) = _matmul(d_gate, hn2, "tn", BF16, "dw_gate", jobs=[_chips_job(cs_d, g_down, (0, 2))])
    dWu, (chips_d, sib_g) = _matmul(d_up, hn2, "tn", BF16, "dw_up",
                                    jobs=[_chips_job(cs_d, g_down, (1, 2), chips_d), _sibling_job(dWg, g_gate)])
    half_d = final_sum(4, cs_d, chips_d)
    cs_g = chip_sum(2, dWg, sib_g)
    d_hn2, (chips_g, other_d) = _matmul(d_gate, WgT, "nn", F32, "d_hn2_gate", **ffn_k,
                                        jobs=[_chips_job(cs_g, g_gate, (0, 2)), _final_job(half_d, g_down)])
    d_hn2, (chips_g, sib_u) = _matmul(d_up, WuT, "nn", F32, "d_hn2_up", res=d_hn2, **ffn_k,
                                      jobs=[_chips_job(cs_g, g_gate, (1, 2), chips_g), _sibling_job(dWu, g_up)])
    half_g = final_sum(2, cs_g, chips_g)
    cs_u = chip_sum(3, dWu, sib_u)
    dh1, dh1_b, dg_ffn = _rmsnorm_bwd(h1, d_hn2, norm_ffn_g, dh2, "norm_ffn_bwd", True)
    dWo, (other_g,) = _matmul(o_mix, dh1_b, "tn", BF16, "dw_out", jobs=[_final_job(half_g, g_gate)])
    d_mix, (sib_o,) = _matmul(dh1_b, Wout, "nt", BF16, "d_mix", jobs=[_sibling_job(dWo, g_out)])
    cs_o = chip_sum(1, dWo, sib_o)
    d_o, dg_heads = _headnorm_bwd(d_mix, o_dil, o_sb, g_heads, "head_norm_bwd")
    d_o3 = d_o.reshape(B, T, D)
    dqkv_dil, (chips_u,) = _dil_bwd(proj3, bias, tabs3, d_o3, o_dil3, lse3, Hd, "dil_bwd",
                                    jobs=[_chips_job(cs_u, g_up, (0, 2))])
    dqkv_sb, (chips_u, chips_o) = _sb_bwd(proj3, d_o3, lt3, Hd, Hs, "sb_bwd",
                                          jobs=[_chips_job(cs_u, g_up, (1, 2), chips_u), _chips_job(cs_o, g_out)])
    half_u = final_sum(3, cs_u, chips_u)
    half_o = final_sum(1, cs_o, chips_o)
    dproj = jnp.concatenate([*dqkv_dil, *dqkv_sb], axis=-1).reshape(M, N3)
    dWin, (other_u, other_o) = _matmul(hn, dproj, "tn", BF16, "dw_in",
                                       jobs=[_final_job(half_u, g_up), _final_job(half_o, g_out)])
    (sib_i,) = _run_job(_sibling_job(dWin, g_in), "sibling_w_in")
    cs_i = chip_sum(0, dWin, sib_i)
    d_hn, (chips_i,) = _matmul(dproj, Win, "nt", F32, "d_hn", jobs=[_chips_job(cs_i, g_in)])
    half_i = final_sum(0, cs_i, chips_i)
    dx, dg_mix = _rmsnorm_bwd(x2, d_hn, norm_mix_g, dh1, "norm_mix_bwd", False)
    (other_i,) = _run_job(_final_job(half_i, g_in), "final_w_in")

    mine = [half_i, half_o, half_g, half_u, half_d]
    theirs = [other_i, other_o, other_g, other_u, other_d]
    big_out = [_adam([big[w]], [mine[w]], [theirs[w]], [big_m[w]], [big_v[w]], geo[w], "adam_" + names[w])[0][0]
               for w in range(5)]
    for w in (2, 3):
        big_out[w] = [tr_(a) for a in big_out[w]]

    tot = _small_all_reduce([dg_mix, dg_heads, dg_ffn, dg_final], loss_part, D)
    loss = tot[4, 0]
    small_w = [norm_mix_g, norm_out_dil_g, norm_out_sb_g, norm_ffn_g, norm_final_g.reshape(1, D)]
    small_m = [m_norm_mix_g, m_norm_out_dil_g, m_norm_out_sb_g, m_norm_ffn_g, m_norm_final_g.reshape(1, D)]
    small_v = [v_norm_mix_g, v_norm_out_dil_g, v_norm_out_sb_g, v_norm_ffn_g, v_norm_final_g.reshape(1, D)]
    so = _small_adam(tot, small_w, small_m, small_v, [(0, 0), (1, 0), (1, W), (2, 0), (3, 0)])
    small_out = [so[4 * i:4 * i + 4] for i in range(5)]
    small_out[4] = [a.reshape(D) for a in small_out[4]]

    per_weight = [small_out[0], big_out[0], small_out[1], small_out[2], big_out[1], small_out[3],
                  big_out[2], big_out[3], big_out[4], small_out[4]]

    def field(i):
        res = []
        for n_, o in enumerate(per_weight):
            a = o[i]
            res.append(a[None] if n_ in (1, 4, 6, 7, 8) else a)
        return res

    return (loss, dx.reshape(B, T, D), *field(0), *field(1), *field(2), *field(3))
```

```python
import functools
import math

import jax
import jax.numpy as jnp
from jax import lax
from jax.experimental import pallas as pl
from jax.experimental.pallas import tpu as pltpu

F32 = jnp.float32
BF16 = jnp.bfloat16
MESH = pl.DeviceIdType.MESH
ANY = pl.BlockSpec(memory_space=pl.ANY)

LANES = 128
HEAD_DIM = 128
DIL_STEPS = 128
DILATIONS = (1, 4, 16)
TQ = 512
TK_DIL = 512
TK_SB = 256
NEG = -1e30
_PART = TQ
_ROW_PARTS = tuple(slice(p, p + _PART) for p in range(0, TQ, _PART))
ROPE_DIM = 32
ROPE_THETA = 500000.0
RMS_EPS = 1e-5
ADAM_LR, ADAM_B1, ADAM_B2, ADAM_EPS, ADAM_WD, ADAM_STEP = 0.001, 0.9, 0.999, 1e-08, 0.01, 10
N_CHIPS = 4
VMEM_LIMIT = 56 * 1024 * 1024
MATMUL_VMEM = 52 * 1024 * 1024

NN = ((1,), (0,))
NT = ((1,), (1,))
TN = ((0,), (0,))


def _dot(a, b, dims):
    return lax.dot_general(a, b, (dims, ((), ())), preferred_element_type=F32)


def _tile(dim, pref, unit=LANES):
    if dim <= pref:
        return dim
    t = (pref // unit) * unit
    while t > unit and dim % t:
        t -= unit
    assert dim % t == 0, (dim, pref, unit)
    return t


def _params(sem=None):
    return pltpu.CompilerParams(dimension_semantics=sem, vmem_limit_bytes=VMEM_LIMIT)


class _Job:
    def __init__(self, inputs, outputs, aliases, nsem, copies):
        self.inputs, self.outputs, self.aliases, self.nsem, self.copies = inputs, outputs, aliases, nsem, copies

    def start(self, ins, outs, send, recv):
        for cp in self.copies(ins, outs, send, recv)[0]:
            cp.start()

    def finish(self, ins, outs, send, recv):
        started, landing = self.copies(ins, outs, send, recv)
        for make in landing:
            make().wait_recv()
        for cp in started:
            cp.wait_send()


def _hosted_call(body, name, grid, in_specs, out_specs, out_shape, scratch_shapes, args, jobs, semantics):
    nbi, nbo, nbs = len(in_specs), len(out_specs), len(scratch_shapes)
    if not jobs:
        res = pl.pallas_call(
            body, name=name, grid=grid, in_specs=list(in_specs), out_specs=list(out_specs), out_shape=list(out_shape),
            scratch_shapes=list(scratch_shapes), compiler_params=_params(semantics))(*args)
        return list(res), []
    j_in = [a for jb in jobs for a in jb.inputs]
    j_out = [o for jb in jobs for o in jb.outputs]
    aliases, ii, oo, sems = {}, nbi, nbo, []
    for jb in jobs:
        for a, b in jb.aliases.items():
            aliases[ii + a] = oo + b
        ii += len(jb.inputs)
        oo += len(jb.outputs)
        sems += [pltpu.SemaphoreType.DMA((jb.nsem,)), pltpu.SemaphoreType.DMA((jb.nsem,))]

    def wrapped(*refs):
        p = nbi + len(j_in)
        b_in, ji = refs[:nbi], refs[nbi:p]
        b_out, jo = refs[p:p + nbo], refs[p + nbo:p + nbo + len(j_out)]
        p += nbo + len(j_out)
        b_scr, js = refs[p:p + nbs], refs[p + nbs:]
        pids = [pl.program_id(k) for k in range(len(grid))]
        first = functools.reduce(jnp.logical_and, [pid == 0 for pid in pids])
        last = functools.reduce(jnp.logical_and, [pid == g - 1 for pid, g in zip(pids, grid)])

        def each(what):
            a = b = 0
            for n, jb in enumerate(jobs):
                getattr(jb, what)(ji[a:a + len(jb.inputs)], jo[b:b + len(jb.outputs)], js[2 * n], js[2 * n + 1])
                a += len(jb.inputs)
                b += len(jb.outputs)

        @pl.when(first)
        def _():
            each("start")

        body(*b_in, *b_out, *b_scr)

        @pl.when(last)
        def _():
            each("finish")

    res = pl.pallas_call(
        wrapped, name=name, grid=grid,
        in_specs=list(in_specs) + [ANY] * len(j_in), out_specs=list(out_specs) + [ANY] * len(j_out),
        out_shape=list(out_shape) + j_out, input_output_aliases=aliases,
        scratch_shapes=list(scratch_shapes) + sems,
        compiler_params=_params(("arbitrary",) * len(grid)))(*args, *j_in)
    return list(res[:nbo]), list(res[nbo:])


def _rope(a, c, s1, s2):
    half = ROPE_DIM // 2
    return a * c + pltpu.roll(a, HEAD_DIM - half, 1) * s1 + pltpu.roll(a, half, 1) * s2


def _matmul(a, b, mode, out_dtype, name, res=None, rope=None, swiglu=None, jobs=(), tm=1024, tn=1024, tk=2816):
    if mode == "nn":
        (M, K), (_, N) = a.shape, b.shape
    elif mode == "nt":
        (M, K), (N, _) = a.shape, b.shape
    else:
        (K, M), (_, N) = a.shape, b.shape
    tm = _tile(M, tm)
    tn = _tile(N, tn) if rope is None else _tile(math.gcd(N, rope[1]), tn)
    n_out = 2 if swiglu is not None else 1

    def vmem_bytes(t):
        osz = jnp.dtype(out_dtype).itemsize
        return (4 * t * (tm + tn) + 4 * tm * tn + (4 * tm * tn if t < K else 0) + 2 * n_out * tm * tn * osz
                + (8 * tm * tn if res is not None else 0) + (8 * tm * tn if swiglu is not None else 0)
                + (4 * tm * tn + 24 * tm * HEAD_DIM if rope is not None else 0))

    whole_k = () if swiglu is not None else (K,)
    tk = next(t for t in whole_k + (_tile(K, tk), _tile(K, 2048), _tile(K, 1024)) if vmem_bytes(t) <= MATMUL_VMEM)
    nk = K // tk
    dims = {"nn": NN, "nt": NT, "tn": TN}[mode]
    a_spec = (pl.BlockSpec((tk, tm), lambda i, j, k: (k, i)) if mode == "tn"
              else pl.BlockSpec((tm, tk), lambda i, j, k: (i, k)))
    b_spec = (pl.BlockSpec((tn, tk), lambda i, j, k: (j, k)) if mode == "nt"
              else pl.BlockSpec((tk, tn), lambda i, j, k: (k, j)))
    o_spec = pl.BlockSpec((tm, tn), lambda i, j, k: (i, j))
    n_extra = (1 if res is not None else 0) + (3 if rope is not None else 0) + (2 if swiglu is not None else 0)
    use_acc = nk > 1 or rope is not None
    if rope is not None:
        assert rope[1] % tn == 0

    def body(*refs):
        a_ref, b_ref = refs[:2]
        extra = refs[2:2 + n_extra]
        o_ref = refs[2 + n_extra]
        k = pl.program_id(2)
        prod = _dot(a_ref[...], b_ref[...], dims)
        if use_acc:
            acc_ref = refs[-1]
            if nk > 1:
                @pl.when(k == 0)
                def _():
                    acc_ref[...] = jnp.zeros(acc_ref.shape, F32)

                acc_ref[...] += prod
            else:
                acc_ref[...] = prod
            total = lambda: acc_ref[...]
        else:
            total = lambda: prod
        at_end = pl.when(k == nk - 1) if nk > 1 else (lambda f: f())

        if swiglu is not None:
            @at_end
            def _():
                d = total()
                g = extra[0][...].astype(F32)
                u = extra[1][...].astype(F32)
                sig = jax.nn.sigmoid(g)
                o_ref[...] = (d * u * sig * (1.0 + g * (1.0 - sig))).astype(out_dtype)
                refs[3 + n_extra][...] = (d * g * sig).astype(out_dtype)
        elif rope is None:
            @at_end
            def _():
                out = total()
                if res is not None:
                    out = out + extra[0][...]
                o_ref[...] = out.astype(out_dtype)
        else:
            roped = pl.program_id(1) * tn < rope[1]

            @pl.when(jnp.logical_and(k == nk - 1, roped))
            def _():
                c, s1, s2 = extra[0][...], extra[1][...], extra[2][...]
                for h in range(tn // HEAD_DIM):
                    cols = slice(h * HEAD_DIM, (h + 1) * HEAD_DIM)
                    o_ref[:, cols] = _rope(acc_ref[:, cols], c, s1, s2).astype(out_dtype)

            @pl.when(jnp.logical_and(k == nk - 1, jnp.logical_not(roped)))
            def _():
                o_ref[...] = acc_ref[...].astype(out_dtype)

    in_specs, args = [a_spec, b_spec], [a, b]
    if res is not None:
        in_specs.append(o_spec)
        args.append(res)
    if rope is not None:
        in_specs += [pl.BlockSpec((tm, HEAD_DIM), lambda i, j, k: (i, 0))] * 3
        args += list(rope[0])
    if swiglu is not None:
        in_specs += [o_spec, o_spec]
        args += list(swiglu)
    outs, extra_out = _hosted_call(
        body, name, (M // tm, N // tn, nk), in_specs, [o_spec] * n_out,
        [jax.ShapeDtypeStruct((M, N), out_dtype)] * n_out,
        [pltpu.VMEM((tm, tn), F32)] if use_acc else [], args, jobs, ("parallel", "parallel", "arbitrary"))
    out = outs[0] if n_out == 1 else tuple(outs)
    return (out, extra_out) if jobs else out


def _gate_up(hn, wg, wu, name, jobs=()):
    M, K = hn.shape
    N = wg.shape[0]
    tm, tn, tk = _tile(M, 1024), _tile(N, 512), _tile(K, 4096)
    nk = K // tk

    def body(a_ref, g_ref, u_ref, og_ref, ou_ref, oa_ref, *accs):
        k = pl.program_id(2)
        a = a_ref[...]
        pg = _dot(a, g_ref[...], NT)
        pu = _dot(a, u_ref[...], NT)

        def finish(g, u):
            og_ref[...] = g.astype(BF16)
            ou_ref[...] = u.astype(BF16)
            oa_ref[...] = (g * jax.nn.sigmoid(g) * u).astype(BF16)

        if nk == 1:
            finish(pg, pu)
        else:
            accg, accu = accs

            @pl.when(k == 0)
            def _():
                accg[...] = jnp.zeros(accg.shape, F32)
                accu[...] = jnp.zeros(accu.shape, F32)

            accg[...] += pg
            accu[...] += pu

            @pl.when(k == nk - 1)
            def _():
                finish(accg[...], accu[...])

    w_spec = pl.BlockSpec((tn, tk), lambda i, j, k: (j, k))
    o_spec = pl.BlockSpec((tm, tn), lambda i, j, k: (i, j))
    sds = jax.ShapeDtypeStruct((M, N), BF16)
    return _hosted_call(
        body, name, (M // tm, N // tn, nk), [pl.BlockSpec((tm, tk), lambda i, j, k: (i, k)), w_spec, w_spec],
        [o_spec, o_spec, o_spec], [sds, sds, sds],
        [pltpu.VMEM((tm, tn), F32), pltpu.VMEM((tm, tn), F32)] if nk > 1 else [],
        (hn, wg, wu), jobs, ("parallel", "parallel", "arbitrary"))


def _rmsnorm_fwd(x, g, name):
    M, D = x.shape
    tm = _tile(M, 256, 16)

    def body(x_ref, g_ref, o_ref):
        xv = x_ref[...]
        r = lax.rsqrt(jnp.mean(xv * xv, axis=-1, keepdims=True) + RMS_EPS)
        o_ref[...] = (xv * r * g_ref[...]).astype(BF16)

    return pl.pallas_call(
        body, name=name, grid=(M // tm,),
        in_specs=[pl.BlockSpec((tm, D), lambda i: (i, 0)), pl.BlockSpec((1, D), lambda i: (0, 0))],
        out_specs=pl.BlockSpec((tm, D), lambda i: (i, 0)),
        out_shape=jax.ShapeDtypeStruct((M, D), BF16),
        compiler_params=_params(("parallel",)),
    )(x, g)


def _fold8(v):
    tm, D = v.shape
    return jnp.sum(v.reshape(tm // 8, 8, D), axis=0)


def _rmsnorm_bwd(x, dy, g, res, name, want_bf16):
    M, D = x.shape
    tm = _tile(M, 128, 16)

    def body(x_ref, dy_ref, g_ref, r_ref, *outs):
        dx_ref, dg_ref = outs[0], outs[-1]
        xv = x_ref[...]
        dyv = dy_ref[...].astype(F32)
        r = lax.rsqrt(jnp.mean(xv * xv, axis=-1, keepdims=True) + RMS_EPS)
        u = dyv * g_ref[...]
        dot = jnp.sum(xv * u, axis=-1, keepdims=True)
        dx = r * u - xv * (r * r * r * (1.0 / D)) * dot + r_ref[...]
        dx_ref[...] = dx
        if want_bf16:
            outs[1][...] = dx.astype(BF16)
        part = _fold8(dyv * xv * r)

        @pl.when(pl.program_id(0) == 0)
        def _():
            dg_ref[...] = part

        @pl.when(pl.program_id(0) > 0)
        def _():
            dg_ref[...] += part

    row = pl.BlockSpec((tm, D), lambda i: (i, 0))
    out_specs = [row] + ([row] if want_bf16 else []) + [pl.BlockSpec((8, D), lambda i: (0, 0))]
    out_shape = ([jax.ShapeDtypeStruct((M, D), F32)] + ([jax.ShapeDtypeStruct((M, D), BF16)] if want_bf16 else [])
                 + [jax.ShapeDtypeStruct((8, D), F32)])
    return pl.pallas_call(
        body, name=name, grid=(M // tm,),
        in_specs=[row, row, pl.BlockSpec((1, D), lambda i: (0, 0)), row],
        out_specs=out_specs, out_shape=out_shape,
        compiler_params=_params(("arbitrary",)),
    )(x, dy, g, res)


def _loss_head(h, tgt, g, name):
    M, D = h.shape
    tm = _tile(M, 128, 16)

    def body(h_ref, t_ref, g_ref, dh_ref, dhb_ref, l_ref, dg_ref):
        hv = h_ref[...]
        gv = g_ref[...]
        r = lax.rsqrt(jnp.mean(hv * hv, axis=-1, keepdims=True) + RMS_EPS)
        n = hv * r
        e = n * gv - t_ref[...]
        dy = e * (1.0 / D)
        u = dy * gv
        dot = jnp.sum(hv * u, axis=-1, keepdims=True)
        dh = r * u - hv * (r * r * r * (1.0 / D)) * dot
        dh_ref[...] = dh
        dhb_ref[...] = dh.astype(BF16)
        rows = jnp.sum(e * e, axis=-1, keepdims=True)
        lpart = jnp.broadcast_to(jnp.sum(rows, axis=0, keepdims=True) * (0.5 / D), (8, LANES))
        gpart = _fold8(dy * n)

        @pl.when(pl.program_id(0) == 0)
        def _():
            l_ref[...] = lpart
            dg_ref[...] = gpart

        @pl.when(pl.program_id(0) > 0)
        def _():
            l_ref[...] += lpart
            dg_ref[...] += gpart

    row = pl.BlockSpec((tm, D), lambda i: (i, 0))
    return pl.pallas_call(
        body, name=name, grid=(M // tm,),
        in_specs=[row, row, pl.BlockSpec((1, D), lambda i: (0, 0))],
        out_specs=[row, row, pl.BlockSpec((8, LANES), lambda i: (0, 0)), pl.BlockSpec((8, D), lambda i: (0, 0))],
        out_shape=[jax.ShapeDtypeStruct((M, D), F32), jax.ShapeDtypeStruct((M, D), BF16),
                   jax.ShapeDtypeStruct((8, LANES), F32), jax.ShapeDtypeStruct((8, D), F32)],
        compiler_params=_params(("arbitrary",)),
    )(h, tgt, g)


def _headnorm_fwd(o_dil, o_sb, g, name):
    M, W = o_dil.shape
    D = 2 * W
    tm = _tile(M, 256, 16)

    def body(a_ref, b_ref, g_ref, o_ref):
        for h in range(D // HEAD_DIM):
            src = a_ref if h < W // HEAD_DIM else b_ref
            lo = (h * HEAD_DIM) % W
            v = src[:, lo:lo + HEAD_DIM]
            r = lax.rsqrt(jnp.mean(v * v, axis=-1, keepdims=True) + RMS_EPS)
            o_ref[:, h * HEAD_DIM:(h + 1) * HEAD_DIM] = (v * r * g_ref[:, h * HEAD_DIM:(h + 1) * HEAD_DIM]).astype(BF16)

    half = pl.BlockSpec((tm, W), lambda i: (i, 0))
    return pl.pallas_call(
        body, name=name, grid=(M // tm,),
        in_specs=[half, half, pl.BlockSpec((1, D), lambda i: (0, 0))],
        out_specs=pl.BlockSpec((tm, D), lambda i: (i, 0)),
        out_shape=jax.ShapeDtypeStruct((M, D), BF16),
        compiler_params=_params(("parallel",)),
    )(o_dil, o_sb, g)


def _headnorm_bwd(d_mix, o_dil, o_sb, g, name):
    M, W = o_dil.shape
    D = 2 * W
    tm = _tile(M, 128, 16)

    def body(d_ref, a_ref, b_ref, g_ref, do_ref, dg_ref):
        @pl.when(pl.program_id(0) == 0)
        def _():
            dg_ref[...] = jnp.zeros(dg_ref.shape, F32)

        for h in range(D // HEAD_DIM):
            src = a_ref if h < W // HEAD_DIM else b_ref
            lo = (h * HEAD_DIM) % W
            cols = slice(h * HEAD_DIM, (h + 1) * HEAD_DIM)
            v = src[:, lo:lo + HEAD_DIM]
            dy = d_ref[:, cols].astype(F32)
            r = lax.rsqrt(jnp.mean(v * v, axis=-1, keepdims=True) + RMS_EPS)
            u = dy * g_ref[:, cols]
            dot = jnp.sum(v * u, axis=-1, keepdims=True)
            do_ref[:, cols] = r * u - v * (r * r * r * (1.0 / HEAD_DIM)) * dot
            dg_ref[:, cols] += _fold8(dy * v * r)

    half = pl.BlockSpec((tm, W), lambda i: (i, 0))
    row = pl.BlockSpec((tm, D), lambda i: (i, 0))
    return pl.pallas_call(
        body, name=name, grid=(M // tm,),
        in_specs=[row, half, half, pl.BlockSpec((1, D), lambda i: (0, 0))],
        out_specs=[row, pl.BlockSpec((8, D), lambda i: (0, 0))],
        out_shape=[jax.ShapeDtypeStruct((M, D), F32), jax.ShapeDtypeStruct((8, D), F32)],
        compiler_params=_params(("arbitrary",)),
    )(d_mix, o_dil, o_sb, g)


def _rope_tables(positions):
    half = ROPE_DIM // 2
    inv_freq = jnp.power(jnp.float32(ROPE_THETA), -jnp.arange(half, dtype=F32) / half)
    ang = positions.astype(F32)[..., None] * inv_freq
    cos, sin = jnp.cos(ang), jnp.sin(ang)
    rest = HEAD_DIM - ROPE_DIM
    one = jnp.ones(cos.shape[:-1] + (rest,), F32)
    z16 = jnp.zeros_like(sin)
    zr = jnp.zeros_like(one)
    c = jnp.concatenate([cos, cos, one], axis=-1)
    s1 = jnp.concatenate([-sin, z16, zr], axis=-1)
    s2 = jnp.concatenate([z16, sin, zr], axis=-1)
    return c, s1, s2


def _dil_bias(T):
    ne = T // TQ
    e = jnp.arange(ne, dtype=jnp.int32)[:, None, None]
    r = jnp.arange(TQ, dtype=jnp.int32)[None, :, None]
    c = jnp.arange(TK_DIL, dtype=jnp.int32)[None, None, :]
    dist = e * TQ + r - c
    mult = jnp.zeros(dist.shape, F32)
    for d in DILATIONS:
        mult = mult + jnp.where((dist % d == 0) & (dist <= DIL_STEPS * d), 1.0, 0.0)
    return jnp.where((dist >= 0) & (mult > 0), jnp.log(jnp.maximum(mult, 1.0)), NEG)


def _dil_fwd(proj3, bias, Hd, name, jobs=()):
    B, T, N3 = proj3.shape
    nq = T // TQ
    per = TK_DIL // TQ
    scale = HEAD_DIM ** -0.5

    def body(q_ref, k_ref, v_ref, b_ref, o_ref, l_ref):
        i = pl.program_id(2)
        qs = [q_ref[r, :] for r in _ROW_PARTS]
        last = i // per

        def step(t, carry):
            j = last - t
            off = pl.multiple_of(j * TK_DIL, TK_DIL)
            kj = k_ref[pl.ds(off, TK_DIL), :]
            vj = v_ref[pl.ds(off, TK_DIL), :]
            new = []
            for n, r in enumerate(_ROW_PARTS):
                m, l, acc = carry[n]
                s = _dot(qs[n], kj, NT) * scale + b_ref[i - per * j, r, :]
                m_new = jnp.maximum(m, jnp.max(s, axis=-1, keepdims=True))
                p = jnp.exp(s - m_new)
                corr = jnp.exp(m - m_new)
                new.append((m_new, l * corr + jnp.sum(p, axis=-1, keepdims=True),
                            acc * corr + _dot(p.astype(BF16), vj, NN)))
            return tuple(new)

        init = (jnp.full((_PART, 1), NEG, F32), jnp.zeros((_PART, 1), F32), jnp.zeros((_PART, HEAD_DIM), F32))
        done = lax.fori_loop(0, last + 1, step, (init,) * len(_ROW_PARTS))
        for n, r in enumerate(_ROW_PARTS):
            m, l, acc = done[n]
            o_ref[r, :] = acc / l
            l_ref[r, :] = jnp.broadcast_to(m + jnp.log(l), (_PART, HEAD_DIM))

    full = lambda base: pl.BlockSpec((None, T, HEAD_DIM), lambda b, h, i: (b, 0, base + h))
    blk = lambda base: pl.BlockSpec((None, TQ, HEAD_DIM), lambda b, h, i: (b, i, base + h))
    sds = jax.ShapeDtypeStruct((B, T, Hd * HEAD_DIM), F32)
    return _hosted_call(
        body, name, (B, Hd, nq),
        [blk(0), full(Hd), full(2 * Hd), pl.BlockSpec(bias.shape, lambda b, h, i: (0, 0, 0))],
        [blk(0), blk(0)], [sds, sds], [], (proj3, proj3, proj3, bias), jobs, ("parallel", "parallel", "arbitrary"))


def _dil_bwd(proj3, bias, tabs, d_o3, o3, lse3, Hd, name, jobs=()):
    B, T, N3 = proj3.shape
    nq = T // TQ
    per = TK_DIL // TQ
    scale = HEAD_DIM ** -0.5

    def body(q_ref, k_ref, v_ref, b_ref, do_ref, o_ref, l_ref, cq, s1q, s2q, ct, s1t, s2t,
             dq_ref, dk_ref, dv_ref, dks, dvs):
        i = pl.program_id(2)

        @pl.when(i == 0)
        def _():
            dks[...] = jnp.zeros(dks.shape, F32)
            dvs[...] = jnp.zeros(dvs.shape, F32)

        q = q_ref[...]
        do = do_ref[...]
        dob = do.astype(BF16)
        dterm = jnp.sum(do * o_ref[...], axis=-1, keepdims=True)
        lse = jnp.concatenate([l_ref[...]] * (TK_DIL // HEAD_DIM), axis=1)
        last = i // per

        def step(j, dqs):
            off = pl.multiple_of(j * TK_DIL, TK_DIL)
            kj = k_ref[pl.ds(off, TK_DIL), :]
            vj = v_ref[pl.ds(off, TK_DIL), :]
            ps, dss, new = [], [], []
            for n, r in enumerate(_ROW_PARTS):
                s = _dot(q[r], kj, NT) * scale + b_ref[i - per * j, r, :]
                p = jnp.exp(s - lse[r])
                dp = _dot(dob[r], vj, NT)
                ds = (p * (dp - dterm[r]) * scale).astype(BF16)
                new.append(dqs[n] + _dot(ds, kj, NN))
                ps.append(p.astype(BF16))
                dss.append(ds)
            dks[pl.ds(off, TK_DIL), :] += _dot(jnp.concatenate(dss, axis=0), q, TN)
            dvs[pl.ds(off, TK_DIL), :] += _dot(jnp.concatenate(ps, axis=0), dob, TN)
            return tuple(new)

        zero = jnp.zeros((_PART, HEAD_DIM), F32)
        dq = jnp.concatenate(lax.fori_loop(0, last + 1, step, (zero,) * len(_ROW_PARTS)), axis=0)
        dq_ref[...] = _rope(dq, cq[...], -s1q[...], -s2q[...]).astype(BF16)

        @pl.when(i == nq - 1)
        def _():
            dk_ref[...] = _rope(dks[...], ct[...], -s1t[...], -s2t[...]).astype(BF16)
            dv_ref[...] = dvs[...].astype(BF16)

    full = lambda base: pl.BlockSpec((None, T, HEAD_DIM), lambda b, h, i: (b, 0, base + h))
    blk = lambda base: pl.BlockSpec((None, TQ, HEAD_DIM), lambda b, h, i: (b, i, base + h))
    tab_q = pl.BlockSpec((None, TQ, HEAD_DIM), lambda b, h, i: (b, i, 0))
    tab_t = pl.BlockSpec((None, T, HEAD_DIM), lambda b, h, i: (b, 0, 0))
    sds = jax.ShapeDtypeStruct((B, T, Hd * HEAD_DIM), BF16)
    return _hosted_call(
        body, name, (B, Hd, nq),
        [blk(0), full(Hd), full(2 * Hd), pl.BlockSpec(bias.shape, lambda b, h, i: (0, 0, 0)),
         blk(0), blk(0), blk(0), tab_q, tab_q, tab_q, tab_t, tab_t, tab_t],
        [blk(0), full(0), full(0)], [sds, sds, sds],
        [pltpu.VMEM((T, HEAD_DIM), F32), pltpu.VMEM((T, HEAD_DIM), F32)],
        (proj3, proj3, proj3, bias, d_o3, o3, lse3, *tabs, *tabs), jobs, ("parallel", "parallel", "arbitrary"))


def _softplus(z):
    return jnp.maximum(z, 0.0) + jnp.log(1.0 + jnp.exp(-jnp.abs(z)))


def _tri_sum(x, m01):
    n = x.shape[0]
    hi = x.astype(BF16)
    lo = (x - hi.astype(F32)).astype(BF16)
    both = _dot(jnp.concatenate([hi, lo], axis=0), m01, NN)
    return both[:n] + both[n:]


def _sb_iota(rows=TQ, first=0):
    row = lax.broadcasted_iota(jnp.int32, (rows, TK_SB), 0) + first
    col = lax.broadcasted_iota(jnp.int32, (rows, TK_SB), 1)
    return row, col


def _sb_fwd(proj3, Hd, Hs, name, jobs=()):
    B, T, N3 = proj3.shape
    nq = T // TQ
    qb, kb_, vb_ = 3 * Hd, 3 * Hd + Hs, 3 * Hd + 2 * Hs
    scale = HEAD_DIM ** -0.5

    assert TQ == 2 * TK_SB
    low = slice(TK_SB, TQ)

    def body(q_ref, k_ref, v_ref, o_ref, lt_ref):
        i = pl.program_id(2)
        q = q_ref[...]
        row, col = _sb_iota(TK_SB)
        after = jnp.where(row > col, 1.0, 0.0).astype(BF16)

        def chunk(j, qr, rows, first_row, acc, cs):
            off = pl.multiple_of(j * TK_SB, TK_SB)
            kj = k_ref[pl.ds(off, TK_SB), :]
            vj = v_ref[pl.ds(off, TK_SB), :]
            z = _dot(qr, kj, NT) * scale
            sp = _softplus(z)
            ln = -sp
            if first_row is not None:
                r_i, c_i = _sb_iota(rows, first_row)
                msk = c_i < r_i + (i * TQ - j * TK_SB)
                ln = jnp.where(msk, ln, 0.0)
            a = jnp.exp(z - sp + _tri_sum(ln, after) + cs)
            if first_row is not None:
                a = jnp.where(msk, a, 0.0)
            return acc + _dot(a.astype(BF16), vj, NN), cs + jnp.sum(ln, axis=-1, keepdims=True)

        acc_l, cs_l = chunk(2 * i + 1, q[low], TK_SB, TK_SB, jnp.zeros((TK_SB, HEAD_DIM), F32),
                            jnp.zeros((TK_SB, 1), F32))
        carry = chunk(2 * i, q, TQ, 0, jnp.concatenate([jnp.zeros((TK_SB, HEAD_DIM), F32), acc_l], axis=0),
                      jnp.concatenate([jnp.zeros((TK_SB, 1), F32), cs_l], axis=0))
        acc, cs = lax.fori_loop(0, 2 * i, lambda t, carry: chunk(2 * i - 1 - t, q, TQ, None, *carry), carry)
        o_ref[...] = acc
        lt_ref[...] = jnp.broadcast_to(cs, (TQ, HEAD_DIM))

    full = lambda base: pl.BlockSpec((None, T, HEAD_DIM), lambda b, h, i: (b, 0, base + h))
    blk = lambda base: pl.BlockSpec((None, TQ, HEAD_DIM), lambda b, h, i: (b, i, base + h))
    sds = jax.ShapeDtypeStruct((B, T, Hs * HEAD_DIM), F32)
    return _hosted_call(
        body, name, (B, Hs, nq), [blk(qb), full(kb_), full(vb_)], [blk(0), blk(0)], [sds, sds], [],
        (proj3, proj3, proj3), jobs, ("parallel", "parallel", "arbitrary"))


def _sb_bwd(proj3, d_o3, lt3, Hd, Hs, name, jobs=()):
    B, T, N3 = proj3.shape
    nq = T // TQ
    qb, kb_, vb_ = 3 * Hd, 3 * Hd + Hs, 3 * Hd + 2 * Hs
    scale = HEAD_DIM ** -0.5

    assert TQ == 2 * TK_SB
    low = slice(TK_SB, TQ)

    def body(q_ref, k_ref, v_ref, do_ref, lt_ref, dq_ref, dk_ref, dv_ref, dks, dvs):
        i = pl.program_id(2)

        @pl.when(i == 0)
        def _():
            dks[...] = jnp.zeros(dks.shape, F32)
            dvs[...] = jnp.zeros(dvs.shape, F32)

        q = q_ref[...]
        dob = do_ref[...].astype(BF16)
        total = jnp.concatenate([lt_ref[...]] * (TK_SB // HEAD_DIM), axis=1)
        row, col = _sb_iota(TK_SB)
        before = jnp.where(row < col, 1.0, 0.0).astype(BF16)

        def chunk(j, qr, dor, tot, rows, first_row, dq, pc, gc):
            off = pl.multiple_of(j * TK_SB, TK_SB)
            kj = k_ref[pl.ds(off, TK_SB), :]
            vj = v_ref[pl.ds(off, TK_SB), :]
            z = _dot(qr, kj, NT) * scale
            sp = _softplus(z)
            ln = -sp
            if first_row is not None:
                r_i, c_i = _sb_iota(rows, first_row)
                msk = c_i < r_i + (i * TQ - j * TK_SB)
                ln = jnp.where(msk, ln, 0.0)
            excl = tot - (_tri_sum(ln, before) + ln + pc)
            a = jnp.exp(z - sp + excl)
            if first_row is not None:
                a = jnp.where(msk, a, 0.0)
            g = a * _dot(dor, vj, NT)
            big = _tri_sum(g, before) + gc
            dz = g - (g + big) * jnp.exp(z - sp)
            if first_row is not None:
                dz = jnp.where(msk, dz, 0.0)
            dz = dz.astype(BF16)
            dks[pl.ds(off, TK_SB), :] += _dot(dz, qr, TN)
            dvs[pl.ds(off, TK_SB), :] += _dot(a.astype(BF16), dor, TN)
            return (dq + _dot(dz, kj, NN), pc + jnp.sum(ln, axis=-1, keepdims=True),
                    gc + jnp.sum(g, axis=-1, keepdims=True))

        zero1 = jnp.zeros((TQ, 1), F32)
        carry = lax.fori_loop(0, 2 * i, lambda j, carry: chunk(j, q, dob, total, TQ, None, *carry),
                              (jnp.zeros((TQ, HEAD_DIM), F32), zero1, zero1))
        dq, pc, gc = chunk(2 * i, q, dob, total, TQ, 0, *carry)
        dq_l, _, _ = chunk(2 * i + 1, q[low], dob[low], total[low], TK_SB, TK_SB, dq[low], pc[low], gc[low])
        dq_ref[...] = (jnp.concatenate([dq[:TK_SB], dq_l], axis=0) * scale).astype(BF16)

        @pl.when(i == nq - 1)
        def _():
            dk_ref[...] = (dks[...] * scale).astype(BF16)
            dv_ref[...] = dvs[...].astype(BF16)

    full = lambda base: pl.BlockSpec((None, T, HEAD_DIM), lambda b, h, i: (b, 0, base + h))
    blk = lambda base: pl.BlockSpec((None, TQ, HEAD_DIM), lambda b, h, i: (b, i, base + h))
    sds = jax.ShapeDtypeStruct((B, T, Hs * HEAD_DIM), BF16)
    return _hosted_call(
        body, name, (B, Hs, nq), [blk(qb), full(kb_), full(vb_), blk(Hd), blk(0)],
        [blk(0), full(0), full(0)], [sds, sds, sds],
        [pltpu.VMEM((T, HEAD_DIM), F32), pltpu.VMEM((T, HEAD_DIM), F32)],
        (proj3, proj3, proj3, d_o3, lt3), jobs, ("parallel", "parallel", "arbitrary"))


def _place():
    x, y, c = lax.axis_index("x"), lax.axis_index("y"), lax.axis_index("c")
    chips = [(1 - x, y), (x, 1 - y), (1 - x, 1 - y)]
    return x, y, c, chips


class _Weight:
    def __init__(self, kind, shard_shape, slot):
        self.kind, self.slot = kind, slot
        self.R, self.C = shard_shape
        if kind == "col":
            assert slot == self.C
            self.full = (self.R, N_CHIPS * slot)
            self.half = (self.R // 2, N_CHIPS * slot)
            self.piece = (self.R // 2, slot)
        else:
            self.full = (N_CHIPS * slot, self.C)
            self.half = (N_CHIPS * slot, self.C // 2)
            self.piece = (self.R, self.C // 2)

    def _rows(self, jj, n):
        return pl.ds(pl.multiple_of(jj * self.slot, 16), n)

    def full_half(self, ref, jj, hc):
        if self.kind == "col":
            return ref.at[pl.ds(hc * (self.R // 2), self.R // 2), pl.ds(pl.multiple_of(jj * self.slot, LANES), self.C)]
        return ref.at[self._rows(jj, self.R), pl.ds(hc * (self.C // 2), self.C // 2)]

    def region_half(self, ref, hc):
        if self.kind == "col":
            return ref.at[pl.ds(hc * (self.R // 2), self.R // 2), :]
        return ref.at[:, pl.ds(hc * (self.C // 2), self.C // 2)]

    def half_piece(self, ref, jj):
        if self.kind == "col":
            return ref.at[:, pl.ds(pl.multiple_of(jj * self.slot, LANES), self.slot)]
        return ref.at[self._rows(jj, self.R), :]


def _cast_into_full(w, geo, j_arr, name):
    R, C = w.shape
    tr = _tile(R, 256, 16)
    if geo.kind == "col":
        o_spec = pl.BlockSpec((tr, C), lambda i, j_ref: (i, j_ref[0]))
    else:
        while geo.slot % tr or R % tr:
            tr -= 16
        spb = geo.slot // tr
        o_spec = pl.BlockSpec((tr, C), lambda i, j_ref: (j_ref[0] * spb + i, 0))

    def body(j_ref, w_ref, o_ref):
        o_ref[...] = w_ref[...].astype(BF16)

    return pl.pallas_call(
        body, name=name,
        grid_spec=pltpu.PrefetchScalarGridSpec(
            num_scalar_prefetch=1, grid=(R // tr,),
            in_specs=[pl.BlockSpec((tr, C), lambda i, j_ref: (i, 0))], out_specs=o_spec),
        out_shape=jax.ShapeDtypeStruct(geo.full, BF16),
        compiler_params=_params(("parallel",)),
    )(j_arr, w)


def _zero_pad(full, geo, name):
    pad = geo.slot - geo.R
    assert geo.kind == "row" and pad > 0 and geo.R % pad == 0 and geo.slot % pad == 0

    def body(f_ref, o_ref):
        o_ref[...] = jnp.zeros(o_ref.shape, BF16)

    return pl.pallas_call(
        body, name=name, grid=(N_CHIPS,), in_specs=[ANY],
        out_specs=pl.BlockSpec((pad, geo.C), lambda jj: ((jj * geo.slot + geo.R) // pad, 0)),
        out_shape=jax.ShapeDtypeStruct(geo.full, BF16), input_output_aliases={0: 0},
        compiler_params=_params(("arbitrary",)),
    )(full)


def _gather_job(full, geo, stage):
    def copies(ins, outs, send, recv):
        x, y, c, chips = _place()
        j = 2 * x + y

        def cp(src, dst, k, to):
            return pltpu.make_async_remote_copy(src_ref=src, dst_ref=dst, send_sem=send.at[k], recv_sem=recv.at[k],
                                                device_id=to, device_id_type=MESH)

        started, landing = [], []
        for k, (cx, cy) in enumerate(chips):
            jk = 2 * cx + cy
            if stage == "ici":
                started.append(cp(geo.full_half(ins[0], j, c), geo.full_half(outs[0], j, c), k, (cx, cy, c)))
                part = geo.full_half(outs[0], jk, c)
            else:
                started.append(cp(geo.full_half(ins[0], jk, c), geo.full_half(outs[0], jk, c), k, (x, y, 1 - c)))
                part = geo.full_half(outs[0], jk, 1 - c)
            landing.append(functools.partial(cp, part, part, k, (x, y, c)))
        return started, landing

    return _Job([full], [jax.ShapeDtypeStruct(geo.full, BF16)], {0: 0}, 3, copies)


def _sibling_job(grad_full, geo):
    def copies(ins, outs, send, recv):
        x, y, c, _ = _place()
        mk = lambda src, to: pltpu.make_async_remote_copy(
            src_ref=src, dst_ref=outs[0], send_sem=send.at[0], recv_sem=recv.at[0], device_id=to, device_id_type=MESH)
        return [mk(geo.region_half(ins[0], 1 - c), (x, y, 1 - c))], [functools.partial(mk, outs[0], (x, y, c))]

    return _Job([grad_full], [jax.ShapeDtypeStruct(geo.half, BF16)], {}, 1, copies)


def _chips_job(chip_sum, geo, part=(0, 1), so_far=None):
    p, n = part
    rows = geo.piece[0] // n
    assert geo.piece[0] % n == 0 and rows % 16 == 0

    def copies(ins, outs, send, recv):
        x, y, c, chips = _place()
        started, landing = [], []

        def mk(k, src, to):
            dst = outs[0].at[k, pl.ds(p * rows, rows), :]
            return pltpu.make_async_remote_copy(src_ref=dst if src is None else src, dst_ref=dst, send_sem=send.at[k],
                                                recv_sem=recv.at[k], device_id=to, device_id_type=MESH)

        for k, (cx, cy) in enumerate(chips):
            started.append(mk(k, geo.half_piece(ins[0], 2 * cx + cy).at[pl.ds(p * rows, rows), :], (cx, cy, c)))
            landing.append(functools.partial(mk, k, None, (x, y, c)))
        return started, landing

    inputs = [chip_sum] + ([] if so_far is None else [so_far])
    return _Job(inputs, [jax.ShapeDtypeStruct((3,) + geo.piece, BF16)], {} if so_far is None else {1: 0}, 3, copies)


def _final_job(half, geo):
    def copies(ins, outs, send, recv):
        x, y, c, _ = _place()
        mk = lambda src, to: pltpu.make_async_remote_copy(
            src_ref=src, dst_ref=outs[0], send_sem=send.at[0], recv_sem=recv.at[0], device_id=to, device_id_type=MESH)
        return [mk(ins[0], (x, y, 1 - c))], [functools.partial(mk, outs[0], (x, y, c))]

    return _Job([half], [jax.ShapeDtypeStruct(geo.piece, F32)], {}, 1, copies)


def _gather_two_step(full, geo, name):
    R = geo.R
    assert geo.kind == "col" and R % 64 == 0

    def body(own_ref, f_ref, send, recv):
        x, y, c, _ = _place()
        j, jx, jy, jd = 2 * x + y, 2 * (1 - x) + y, 2 * x + (1 - y), 2 * (1 - x) + (1 - y)
        me, xn, yn, sib = (x, y, c), (1 - x, y, c), (x, 1 - y, c), (x, y, 1 - c)

        def half(ref, jj, hc):
            return geo.full_half(ref, jj, hc)

        def quarter(ref, jj, hc, q):
            return ref.at[pl.ds(hc * (R // 2) + q * (R // 4), R // 4), pl.ds(pl.multiple_of(jj * geo.slot, LANES), geo.C)]

        def cp(k, src, dst, to):
            return pltpu.make_async_remote_copy(src_ref=src, dst_ref=dst, send_sem=send.at[k], recv_sem=recv.at[k],
                                                device_id=to, device_id_type=MESH)

        def arrived(k, part):
            cp(k, part, part, me).wait_recv()

        def pass_on(k, part, to):
            d = cp(k, part, part, to)
            d.start()
            return d

        sent = [cp(0, half(own_ref, j, c), half(f_ref, j, c), xn), cp(1, half(own_ref, j, c), half(f_ref, j, c), yn)]
        for d in sent:
            d.start()
        arrived(0, half(f_ref, jx, c))
        sent.append(pass_on(2, quarter(f_ref, jx, c, 0), yn))
        sent.append(pass_on(4, half(f_ref, jx, c), sib))
        arrived(1, half(f_ref, jy, c))
        sent.append(pass_on(3, quarter(f_ref, jy, c, 1), xn))
        sent.append(pass_on(5, half(f_ref, jy, c), sib))
        arrived(2, quarter(f_ref, jd, c, 0))
        sent.append(pass_on(6, quarter(f_ref, jd, c, 0), sib))
        arrived(3, quarter(f_ref, jd, c, 1))
        sent.append(pass_on(7, quarter(f_ref, jd, c, 1), sib))
        arrived(4, half(f_ref, jx, 1 - c))
        arrived(5, half(f_ref, jy, 1 - c))
        arrived(6, quarter(f_ref, jd, 1 - c, 0))
        arrived(7, quarter(f_ref, jd, 1 - c, 1))
        for d in sent:
            d.wait_send()

    return pl.pallas_call(
        body, name=name, in_specs=[ANY], out_specs=ANY, out_shape=jax.ShapeDtypeStruct(geo.full, BF16),
        input_output_aliases={0: 0},
        scratch_shapes=[pltpu.SemaphoreType.DMA((8,)), pltpu.SemaphoreType.DMA((8,))],
    )(full)


def _chip_sum(g_full, got, geo, c_arr, name):
    Rh, Ch = geo.half
    tr, tc = _tile(Rh, 256, 16), _tile(Ch, 2048)
    nrb, ncb = Rh // tr, Ch // tc

    def body(c_ref, a_ref, b_ref, o_ref):
        o_ref[...] = (a_ref[...].astype(F32) + b_ref[...].astype(F32)).astype(BF16)

    if geo.kind == "col":
        a_spec = pl.BlockSpec((tr, tc), lambda i, j, c_ref: (c_ref[0] * nrb + i, j))
    else:
        a_spec = pl.BlockSpec((tr, tc), lambda i, j, c_ref: (i, c_ref[0] * ncb + j))
    spec = pl.BlockSpec((tr, tc), lambda i, j, c_ref: (i, j))
    return pl.pallas_call(
        body, name=name,
        grid_spec=pltpu.PrefetchScalarGridSpec(num_scalar_prefetch=1, grid=(nrb, ncb),
                                               in_specs=[a_spec, spec], out_specs=spec),
        out_shape=jax.ShapeDtypeStruct(geo.half, BF16),
        compiler_params=_params(("parallel", "parallel")),
    )(c_arr, g_full, got)


def _final_sum(chip_sum, got, geo, j_arr, name):
    Rp, Cp = geo.piece
    if geo.kind == "col":
        tr = _tile(Rp, 128, 16)
        a_spec = pl.BlockSpec((tr, Cp), lambda i, j_ref: (i, j_ref[0]))
    else:
        tr = _tile(Rp, 128, 16)
        while geo.slot % tr:
            tr -= 16
        assert Rp % tr == 0 and geo.slot % tr == 0
        spb = geo.slot // tr
        a_spec = pl.BlockSpec((tr, Cp), lambda i, j_ref: (j_ref[0] * spb + i, 0))
    r_specs = [pl.BlockSpec((None, tr, Cp), functools.partial(lambda i, j_ref, k: (k, i, 0), k=k)) for k in range(3)]
    o_spec = pl.BlockSpec((tr, Cp), lambda i, j_ref: (i, 0))

    def body(j_ref, a_ref, r0, r1, r2, o_ref):
        o_ref[...] = ((a_ref[...].astype(F32) + r0[...].astype(F32)) + r1[...].astype(F32)) + r2[...].astype(F32)

    return pl.pallas_call(
        body, name=name,
        grid_spec=pltpu.PrefetchScalarGridSpec(num_scalar_prefetch=1, grid=(Rp // tr,),
                                               in_specs=[a_spec] + r_specs, out_specs=o_spec),
        out_shape=jax.ShapeDtypeStruct(geo.piece, F32),
        compiler_params=_params(("parallel",)),
    )(j_arr, chip_sum, got, got, got)


def _run_job(job, name):
    ni = len(job.inputs)

    def body(*refs):
        ins, outs, (send, recv) = refs[:ni], refs[ni:-2], refs[-2:]
        job.start(ins, outs, send, recv)
        job.finish(ins, outs, send, recv)

    return pl.pallas_call(
        body, name=name, in_specs=[ANY] * ni, out_specs=[ANY] * len(job.outputs), out_shape=list(job.outputs),
        input_output_aliases=dict(job.aliases),
        scratch_shapes=[pltpu.SemaphoreType.DMA((job.nsem,)), pltpu.SemaphoreType.DMA((job.nsem,))],
    )(*job.inputs)


def _adam_math(w, g, m, v):
    m = ADAM_B1 * m + (1.0 - ADAM_B1) * g
    v = ADAM_B2 * v + (1.0 - ADAM_B2) * (g * g)
    m_hat = m / (1.0 - ADAM_B1 ** ADAM_STEP)
    v_hat = v / (1.0 - ADAM_B2 ** ADAM_STEP)
    delta = -ADAM_LR * (m_hat / (jnp.sqrt(v_hat) + ADAM_EPS) + ADAM_WD * w)
    return delta, m, v


def _adam(ws, mines, theirs, ms, vs, geo, name, jobs=()):
    n = len(ws)
    R, C = ws[0].shape
    tr = _tile(R, 128 if n == 1 else 32, 8)
    nrb = R // tr
    col = geo.kind == "col"
    assert not col or nrb % 2 == 0

    def rows(k, halved):
        def index(w, i):
            r = jnp.where(w == k, i, jnp.where(w < k, 0, nrb - 1))
            return (r % (nrb // 2) if halved else r, 0)
        return index

    def body(*refs):
        ins, outs = refs[:5 * n], refs[5 * n:]
        wi, i = pl.program_id(0), pl.program_id(1)
        c = lax.axis_index("c")
        for k in range(n):
            @pl.when(wi == k)
            def _():
                w_ref, a_ref, b_ref, m_ref, v_ref = ins[5 * k:5 * k + 5]
                if col:
                    gv = jnp.where(i // (nrb // 2) == c, a_ref[...], b_ref[...])
                else:
                    a, b = a_ref[...], b_ref[...]
                    gv = jnp.concatenate([jnp.where(c == 0, a, b), jnp.where(c == 0, b, a)], axis=1)
                delta, mn, vn = _adam_math(w_ref[...], gv, m_ref[...], v_ref[...])
                og, od, om, ov = outs[4 * k:4 * k + 4]
                og[...] = gv
                od[...] = delta
                om[...] = mn
                ov[...] = vn

    in_specs, out_specs, args = [], [], []
    for k in range(n):
        spec = pl.BlockSpec((tr, C), rows(k, False))
        h_spec = pl.BlockSpec((tr, C), rows(k, True)) if col else pl.BlockSpec((tr, C // 2), rows(k, False))
        in_specs += [spec, h_spec, h_spec, spec, spec]
        out_specs += [spec] * 4
        args += [ws[k], mines[k], theirs[k], ms[k], vs[k]]
    outs, job_outs = _hosted_call(body, name, (n, nrb), in_specs, out_specs,
                                  [jax.ShapeDtypeStruct((R, C), F32)] * (4 * n), [], args, jobs,
                                  ("arbitrary", "arbitrary"))
    return [outs[4 * k:4 * k + 4] for k in range(n)], job_outs


def _small_all_reduce(parts, loss_part, D):
    n = len(parts)

    def body(*refs):
        p_refs, l_ref, o_ref, vec, buf, send, recv = refs[:n], refs[n], refs[n + 1], *refs[n + 2:]
        x, y, c, _ = _place()
        me = 4 * x + 2 * y + c
        vec[...] = jnp.zeros(vec.shape, F32)
        for r in range(n):
            vec[r:r + 1, :] = jnp.sum(p_refs[r][...], axis=0, keepdims=True)
        vec[n:n + 1, 0:LANES] = l_ref[0:1, :]
        buf[me] = vec[...]
        cps = []
        for dd in range(1, 8):
            bx, by, bc = (dd >> 2) & 1, (dd >> 1) & 1, dd & 1
            peer = (x + bx - 2 * x * bx, y + by - 2 * y * by, c + bc - 2 * c * bc)
            cp = pltpu.make_async_remote_copy(
                src_ref=vec, dst_ref=buf.at[me], send_sem=send.at[dd - 1], recv_sem=recv.at[dd - 1],
                device_id=peer, device_id_type=MESH)
            cp.start()
            cps.append(cp)
        for cp in cps:
            cp.wait()
        tot = buf[0]
        for s in range(1, 8):
            tot = tot + buf[s]
        o_ref[...] = tot

    vm = pl.BlockSpec(memory_space=pltpu.VMEM)
    return pl.pallas_call(
        body, name="small_all_reduce",
        in_specs=[vm] * (n + 1), out_specs=vm,
        out_shape=jax.ShapeDtypeStruct((8, D), F32),
        scratch_shapes=[pltpu.VMEM((8, D), F32), pltpu.VMEM((8, 8, D), F32),
                        pltpu.SemaphoreType.DMA((7,)), pltpu.SemaphoreType.DMA((7,))],
    )(*parts, loss_part)


def _small_adam(tot, ws, ms, vs, rows):
    n = len(ws)

    def body(*refs):
        t_ref = refs[0]
        w_refs, m_refs, v_refs = refs[1:1 + n], refs[1 + n:1 + 2 * n], refs[1 + 2 * n:1 + 3 * n]
        outs = refs[1 + 3 * n:]
        for i in range(n):
            r, c0 = rows[i]
            width = w_refs[i].shape[1]
            g = t_ref[r:r + 1, c0:c0 + width]
            delta, mn, vn = _adam_math(w_refs[i][...], g, m_refs[i][...], v_refs[i][...])
            outs[4 * i][...] = g
            outs[4 * i + 1][...] = delta
            outs[4 * i + 2][...] = mn
            outs[4 * i + 3][...] = vn

    vm = pl.BlockSpec(memory_space=pltpu.VMEM)
    out_shape = []
    for w in ws:
        out_shape += [jax.ShapeDtypeStruct(w.shape, F32)] * 4
    return pl.pallas_call(
        body, name="small_adam",
        in_specs=[vm] * (1 + 3 * n), out_specs=[vm] * (4 * n), out_shape=out_shape,
    )(tot, *ws, *ms, *vs)


def kernel(x, positions, norm_mix_g, w_in, norm_out_dil_g, norm_out_sb_g, w_out, norm_ffn_g, w_gate, w_up, w_down, norm_final_g, loss_target, m_norm_mix_g, m_w_in, m_norm_out_dil_g, m_norm_out_sb_g, m_w_out, m_norm_ffn_g, m_w_gate, m_w_up, m_w_down, m_norm_final_g, v_norm_mix_g, v_w_in, v_norm_out_dil_g, v_norm_out_sb_g, v_w_out, v_norm_ffn_g, v_w_gate, v_w_up, v_w_down, v_norm_final_g):
    B, T, D = x.shape
    M = B * T
    n_heads = D // HEAD_DIM
    Hd = n_heads // 2
    Hs = n_heads - Hd
    W = Hd * HEAD_DIM
    N3 = 3 * D
    fs = w_gate.shape[2]
    fp = -(-fs // LANES) * LANES
    assert T % TK_DIL == 0 and W == Hs * HEAD_DIM and fp > fs

    x2 = x.reshape(M, D)
    tgt = loss_target.reshape(M, D)
    tr_ = jnp.transpose
    big = [w_in[0], w_out[0], tr_(w_gate[0]), tr_(w_up[0]), w_down[0]]
    big_m = [m_w_in[0], m_w_out[0], tr_(m_w_gate[0]), tr_(m_w_up[0]), m_w_down[0]]
    big_v = [v_w_in[0], v_w_out[0], tr_(v_w_gate[0]), tr_(v_w_up[0]), v_w_down[0]]
    names = ["w_in", "w_out", "w_gate", "w_up", "w_down"]
    c_arr = jnp.reshape(lax.axis_index("c"), (1,)).astype(jnp.int32)
    j_arr = jnp.reshape(2 * lax.axis_index("x") + lax.axis_index("y"), (1,)).astype(jnp.int32)

    ns_in = w_in.shape[2]
    ks_out = w_out.shape[1]
    geo = [_Weight("col", (D, ns_in), ns_in), _Weight("row", (ks_out, D), ks_out),
           _Weight("row", (fs, D), fp), _Weight("row", (fs, D), fp), _Weight("row", (fs, D), fp)]
    g_in, g_out, g_gate, g_up, g_down = geo
    fulls = [_cast_into_full(big[w], geo[w], j_arr, "cast_" + names[w]) for w in range(5)]
    for w in (2, 3, 4):
        fulls[w] = _zero_pad(fulls[w], geo[w], "zero_pad_" + names[w])

    def chip_sum(w, grad_full, from_sibling):
        return _chip_sum(grad_full, from_sibling, geo[w], c_arr, "chip_sum_" + names[w])

    def final_sum(w, cs, from_chips):
        return _final_sum(cs, from_chips, geo[w], j_arr, "final_sum_" + names[w])

    Win = _gather_two_step(fulls[0], g_in, "gather_w_in")
    tabs3 = _rope_tables(positions)
    tabs2 = [t.reshape(M, HEAD_DIM) for t in tabs3]
    hn = _rmsnorm_fwd(x2, norm_mix_g, "norm_mix")
    proj, (WgT,) = _matmul(hn, Win, "nn", BF16, "in_proj", rope=(tabs2, 2 * W),
                           jobs=[_gather_job(fulls[2], g_gate, "ici")])
    proj3 = proj.reshape(B, T, N3)
    bias = _dil_bias(T)
    (o_dil3, lse3), (WgT, Wout) = _dil_fwd(
        proj3, bias, Hd, "dil_fwd", jobs=[_gather_job(WgT, g_gate, "d2d"), _gather_job(fulls[1], g_out, "ici")])
    (o_sb3, lt3), (Wout, WuT) = _sb_fwd(
        proj3, Hd, Hs, "sb_fwd", jobs=[_gather_job(Wout, g_out, "d2d"), _gather_job(fulls[3], g_up, "ici")])
    o_dil, o_sb = o_dil3.reshape(M, W), o_sb3.reshape(M, W)
    g_heads = jnp.concatenate([norm_out_dil_g, norm_out_sb_g], axis=1)
    o_mix = _headnorm_fwd(o_dil, o_sb, g_heads, "head_norm")
    h1, (WuT,) = _matmul(o_mix, Wout, "nn", F32, "out_proj", res=x2, jobs=[_gather_job(WuT, g_up, "d2d")])
    hn2 = _rmsnorm_fwd(h1, norm_ffn_g, "norm_ffn")
    (gate, up, act), (Wd,) = _gate_up(hn2, WgT, WuT, "gate_up", jobs=[_gather_job(fulls[4], g_down, "ici")])
    (Wd,) = _run_job(_gather_job(Wd, g_down, "d2d"), "gather_w_down_d2d")
    h2 = _matmul(act, Wd, "nn", F32, "down_proj", res=h1)
    dh2, dh2_b, loss_part, dg_final = _loss_head(h2, tgt, norm_final_g.reshape(1, D), "loss_head")

    dWd = _matmul(act, dh2_b, "tn", BF16, "dw_down")
    (d_gate, d_up), (sib_d,) = _matmul(dh2_b, Wd, "nt", BF16, "d_act", swiglu=(gate, up),
                                       jobs=[_sibling_job(dWd, g_down)])
    cs_d = chip_sum(4, dWd, sib_d)
    dWg, (chips_d,) = _matmul(d_gate, hn2, "tn", BF16, "dw_gate", jobs=[_chips_job(cs_d, g_down, (0, 2))])
    dWu, (chips_d, sib_g) = _matmul(d_up, hn2, "tn", BF16, "dw_up",
                                    jobs=[_chips_job(cs_d, g_down, (1, 2), chips_d), _sibling_job(dWg, g_gate)])
    half_d = final_sum(4, cs_d, chips_d)
    cs_g = chip_sum(2, dWg, sib_g)
    d_hn2, (chips_g, other_d) = _matmul(d_gate, WgT, "nn", F32, "d_hn2_gate",
                                        jobs=[_chips_job(cs_g, g_gate, (0, 2)), _final_job(half_d, g_down)])
    d_hn2, (chips_g, sib_u) = _matmul(d_up, WuT, "nn", F32, "d_hn2_up", res=d_hn2,
                                      jobs=[_chips_job(cs_g, g_gate, (1, 2), chips_g), _sibling_job(dWu, g_up)])
    half_g = final_sum(2, cs_g, chips_g)
    cs_u = chip_sum(3, dWu, sib_u)
    dh1, dh1_b, dg_ffn = _rmsnorm_bwd(h1, d_hn2, norm_ffn_g, dh2, "norm_ffn_bwd", True)
    dWo, (other_g,) = _matmul(o_mix, dh1_b, "tn", BF16, "dw_out", jobs=[_final_job(half_g, g_gate)])
    d_mix, (sib_o,) = _matmul(dh1_b, Wout, "nt", BF16, "d_mix", jobs=[_sibling_job(dWo, g_out)])
    cs_o = chip_sum(1, dWo, sib_o)
    d_o, dg_heads = _headnorm_bwd(d_mix, o_dil, o_sb, g_heads, "head_norm_bwd")
    d_o3 = d_o.reshape(B, T, D)
    dqkv_dil, (chips_u,) = _dil_bwd(proj3, bias, tabs3, d_o3, o_dil3, lse3, Hd, "dil_bwd",
                                    jobs=[_chips_job(cs_u, g_up, (0, 2))])
    dqkv_sb, (chips_u, chips_o) = _sb_bwd(proj3, d_o3, lt3, Hd, Hs, "sb_bwd",
                                          jobs=[_chips_job(cs_u, g_up, (1, 2), chips_u), _chips_job(cs_o, g_out)])
    half_u = final_sum(3, cs_u, chips_u)
    half_o = final_sum(1, cs_o, chips_o)
    dproj = jnp.concatenate([*dqkv_dil, *dqkv_sb], axis=-1).reshape(M, N3)
    dWin, (other_u, other_o) = _matmul(hn, dproj, "tn", BF16, "dw_in",
                                       jobs=[_final_job(half_u, g_up), _final_job(half_o, g_out)])
    (sib_i,) = _run_job(_sibling_job(dWin, g_in), "sibling_w_in")
    cs_i = chip_sum(0, dWin, sib_i)
    d_hn, (chips_i,) = _matmul(dproj, Win, "nt", F32, "d_hn", jobs=[_chips_job(cs_i, g_in)])
    half_i = final_sum(0, cs_i, chips_i)
    dx, dg_mix = _rmsnorm_bwd(x2, d_hn, norm_mix_g, dh1, "norm_mix_bwd", False)
    (other_i,) = _run_job(_final_job(half_i, g_in), "final_w_in")

    mine = [half_i, half_o, half_g, half_u, half_d]
    theirs = [other_i, other_o, other_g, other_u, other_d]
    big_out = [_adam([big[w]], [mine[w]], [theirs[w]], [big_m[w]], [big_v[w]], geo[w], "adam_" + names[w])[0][0]
               for w in range(5)]
    for w in (2, 3):
        big_out[w] = [tr_(a) for a in big_out[w]]

    tot = _small_all_reduce([dg_mix, dg_heads, dg_ffn, dg_final], loss_part, D)
    loss = tot[4, 0]
    small_w = [norm_mix_g, norm_out_dil_g, norm_out_sb_g, norm_ffn_g, norm_final_g.reshape(1, D)]
    small_m = [m_norm_mix_g, m_norm_out_dil_g, m_norm_out_sb_g, m_norm_ffn_g, m_norm_final_g.reshape(1, D)]
    small_v = [v_norm_mix_g, v_norm_out_dil_g, v_norm_out_sb_g, v_norm_ffn_g, v_norm_final_g.reshape(1, D)]
    so = _small_adam(tot, small_w, small_m, small_v, [(0, 0), (1, 0), (1, W), (2, 0), (3, 0)])
    small_out = [so[4 * i:4 * i + 4] for i in range(5)]
    small_out[4] = [a.reshape(D) for a in small_out[4]]

    per_weight = [small_out[0], big_out[0], small_out[1], small_out[2], big_out[1], small_out[3],
                  big_out[2], big_out[3], big_out[4], small_out[4]]

    def field(i):
        res = []
        for n_, o in enumerate(per_weight):
            a = o[i]
            res.append(a[None] if n_ in (1, 4, 6, 7, 8) else a)
        return res

    return (loss, dx.reshape(B, T, D), *field(0), *field(1), *field(2), *field(3))
```

```python
import functools
import math

import jax
import jax.numpy as jnp
from jax import lax
from jax.experimental import pallas as pl
from jax.experimental.pallas import tpu as pltpu

F32 = jnp.float32
BF16 = jnp.bfloat16
MESH = pl.DeviceIdType.MESH
ANY = pl.BlockSpec(memory_space=pl.ANY)

LANES = 128
HEAD_DIM = 128
DIL_STEPS = 128
DILATIONS = (1, 4, 16)
TQ = 512
TK_DIL = 512
TK_SB = 256
NEG = -1e30
_PART = TQ
_ROW_PARTS = tuple(slice(p, p + _PART) for p in range(0, TQ, _PART))
ROPE_DIM = 32
ROPE_THETA = 500000.0
RMS_EPS = 1e-5
ADAM_LR, ADAM_B1, ADAM_B2, ADAM_EPS, ADAM_WD, ADAM_STEP = 0.001, 0.9, 0.999, 1e-08, 0.01, 10
N_CHIPS = 4
VMEM_LIMIT = 56 * 1024 * 1024
MATMUL_VMEM = 52 * 1024 * 1024

NN = ((1,), (0,))
NT = ((1,), (1,))
TN = ((0,), (0,))


def _dot(a, b, dims):
    return lax.dot_general(a, b, (dims, ((), ())), preferred_element_type=F32)


def _tile(dim, pref, unit=LANES):
    if dim <= pref:
        return dim
    t = (pref // unit) * unit
    while t > unit and dim % t:
        t -= unit
    assert dim % t == 0, (dim, pref, unit)
    return t


def _params(sem=None):
    return pltpu.CompilerParams(dimension_semantics=sem, vmem_limit_bytes=VMEM_LIMIT)


class _Job:
    def __init__(self, inputs, outputs, aliases, nsem, copies):
        self.inputs, self.outputs, self.aliases, self.nsem, self.copies = inputs, outputs, aliases, nsem, copies

    def start(self, ins, outs, send, recv):
        for cp in self.copies(ins, outs, send, recv)[0]:
            cp.start()

    def finish(self, ins, outs, send, recv):
        started, landing = self.copies(ins, outs, send, recv)
        for make in landing:
            make().wait_recv()
        for cp in started:
            cp.wait_send()


def _hosted_call(body, name, grid, in_specs, out_specs, out_shape, scratch_shapes, args, jobs, semantics):
    nbi, nbo, nbs = len(in_specs), len(out_specs), len(scratch_shapes)
    if not jobs:
        res = pl.pallas_call(
            body, name=name, grid=grid, in_specs=list(in_specs), out_specs=list(out_specs), out_shape=list(out_shape),
            scratch_shapes=list(scratch_shapes), compiler_params=_params(semantics))(*args)
        return list(res), []
    j_in = [a for jb in jobs for a in jb.inputs]
    j_out = [o for jb in jobs for o in jb.outputs]
    aliases, ii, oo, sems = {}, nbi, nbo, []
    for jb in jobs:
        for a, b in jb.aliases.items():
            aliases[ii + a] = oo + b
        ii += len(jb.inputs)
        oo += len(jb.outputs)
        sems += [pltpu.SemaphoreType.DMA((jb.nsem,)), pltpu.SemaphoreType.DMA((jb.nsem,))]

    def wrapped(*refs):
        p = nbi + len(j_in)
        b_in, ji = refs[:nbi], refs[nbi:p]
        b_out, jo = refs[p:p + nbo], refs[p + nbo:p + nbo + len(j_out)]
        p += nbo + len(j_out)
        b_scr, js = refs[p:p + nbs], refs[p + nbs:]
        pids = [pl.program_id(k) for k in range(len(grid))]
        first = functools.reduce(jnp.logical_and, [pid == 0 for pid in pids])
        last = functools.reduce(jnp.logical_and, [pid == g - 1 for pid, g in zip(pids, grid)])

        def each(what):
            a = b = 0
            for n, jb in enumerate(jobs):
                getattr(jb, what)(ji[a:a + len(jb.inputs)], jo[b:b + len(jb.outputs)], js[2 * n], js[2 * n + 1])
                a += len(jb.inputs)
                b += len(jb.outputs)

        @pl.when(first)
        def _():
            each("start")

        body(*b_in, *b_out, *b_scr)

        @pl.when(last)
        def _():
            each("finish")

    res = pl.pallas_call(
        wrapped, name=name, grid=grid,
        in_specs=list(in_specs) + [ANY] * len(j_in), out_specs=list(out_specs) + [ANY] * len(j_out),
        out_shape=list(out_shape) + j_out, input_output_aliases=aliases,
        scratch_shapes=list(scratch_shapes) + sems,
        compiler_params=_params(("arbitrary",) * len(grid)))(*args, *j_in)
    return list(res[:nbo]), list(res[nbo:])


def _rope(a, c, s1, s2):
    half = ROPE_DIM // 2
    return a * c + pltpu.roll(a, HEAD_DIM - half, 1) * s1 + pltpu.roll(a, half, 1) * s2


def _matmul(a, b, mode, out_dtype, name, res=None, rope=None, swiglu=None, jobs=(), tm=1024, tn=1024, tk=2816):
    if mode == "nn":
        (M, K), (_, N) = a.shape, b.shape
    elif mode == "nt":
        (M, K), (N, _) = a.shape, b.shape
    else:
        (K, M), (_, N) = a.shape, b.shape
    tm = _tile(M, tm)
    tn = _tile(N, tn) if rope is None else _tile(math.gcd(N, rope[1]), tn)
    n_out = 2 if swiglu is not None else 1

    def vmem_bytes(t):
        osz = jnp.dtype(out_dtype).itemsize
        return (4 * t * (tm + tn) + 4 * tm * tn + (4 * tm * tn if t < K else 0) + 2 * n_out * tm * tn * osz
                + (8 * tm * tn if res is not None else 0) + (24 * tm * tn if swiglu is not None else 0)
                + (4 * tm * tn + 24 * tm * HEAD_DIM if rope is not None else 0))

    tk = next(t for t in (K, _tile(K, tk), _tile(K, 2048), _tile(K, 1024)) if vmem_bytes(t) <= MATMUL_VMEM)
    nk = K // tk
    dims = {"nn": NN, "nt": NT, "tn": TN}[mode]
    a_spec = (pl.BlockSpec((tk, tm), lambda i, j, k: (k, i)) if mode == "tn"
              else pl.BlockSpec((tm, tk), lambda i, j, k: (i, k)))
    b_spec = (pl.BlockSpec((tn, tk), lambda i, j, k: (j, k)) if mode == "nt"
              else pl.BlockSpec((tk, tn), lambda i, j, k: (k, j)))
    o_spec = pl.BlockSpec((tm, tn), lambda i, j, k: (i, j))
    n_extra = (1 if res is not None else 0) + (3 if rope is not None else 0) + (2 if swiglu is not None else 0)
    use_acc = nk > 1 or rope is not None
    if rope is not None:
        assert rope[1] % tn == 0

    def body(*refs):
        a_ref, b_ref = refs[:2]
        extra = refs[2:2 + n_extra]
        o_ref = refs[2 + n_extra]
        k = pl.program_id(2)
        prod = _dot(a_ref[...], b_ref[...], dims)
        if use_acc:
            acc_ref = refs[-1]
            if nk > 1:
                @pl.when(k == 0)
                def _():
                    acc_ref[...] = jnp.zeros(acc_ref.shape, F32)

                acc_ref[...] += prod
            else:
                acc_ref[...] = prod
            total = lambda: acc_ref[...]
        else:
            total = lambda: prod
        at_end = pl.when(k == nk - 1) if nk > 1 else (lambda f: f())

        if swiglu is not None:
            @at_end
            def _():
                d = total()
                g = extra[0][...].astype(F32)
                u = extra[1][...].astype(F32)
                sig = jax.nn.sigmoid(g)
                o_ref[...] = (d * u * sig * (1.0 + g * (1.0 - sig))).astype(out_dtype)
                refs[3 + n_extra][...] = (d * g * sig).astype(out_dtype)
        elif rope is None:
            @at_end
            def _():
                out = total()
                if res is not None:
                    out = out + extra[0][...]
                o_ref[...] = out.astype(out_dtype)
        else:
            roped = pl.program_id(1) * tn < rope[1]

            @pl.when(jnp.logical_and(k == nk - 1, roped))
            def _():
                c, s1, s2 = extra[0][...], extra[1][...], extra[2][...]
                for h in range(tn // HEAD_DIM):
                    cols = slice(h * HEAD_DIM, (h + 1) * HEAD_DIM)
                    o_ref[:, cols] = _rope(acc_ref[:, cols], c, s1, s2).astype(out_dtype)

            @pl.when(jnp.logical_and(k == nk - 1, jnp.logical_not(roped)))
            def _():
                o_ref[...] = acc_ref[...].astype(out_dtype)

    in_specs, args = [a_spec, b_spec], [a, b]
    if res is not None:
        in_specs.append(o_spec)
        args.append(res)
    if rope is not None:
        in_specs += [pl.BlockSpec((tm, HEAD_DIM), lambda i, j, k: (i, 0))] * 3
        args += list(rope[0])
    if swiglu is not None:
        in_specs += [o_spec, o_spec]
        args += list(swiglu)
    outs, extra_out = _hosted_call(
        body, name, (M // tm, N // tn, nk), in_specs, [o_spec] * n_out,
        [jax.ShapeDtypeStruct((M, N), out_dtype)] * n_out,
        [pltpu.VMEM((tm, tn), F32)] if use_acc else [], args, jobs, ("parallel", "parallel", "arbitrary"))
    out = outs[0] if n_out == 1 else tuple(outs)
    return (out, extra_out) if jobs else out


def _gate_up(hn, wg, wu, name, jobs=()):
    M, K = hn.shape
    N = wg.shape[0]
    tm, tn, tk = _tile(M, 1024), _tile(N, 512), _tile(K, 4096)
    nk = K // tk

    def body(a_ref, g_ref, u_ref, og_ref, ou_ref, oa_ref, *accs):
        k = pl.program_id(2)
        a = a_ref[...]
        pg = _dot(a, g_ref[...], NT)
        pu = _dot(a, u_ref[...], NT)

        def finish(g, u):
            og_ref[...] = g.astype(BF16)
            ou_ref[...] = u.astype(BF16)
            oa_ref[...] = (g * jax.nn.sigmoid(g) * u).astype(BF16)

        if nk == 1:
            finish(pg, pu)
        else:
            accg, accu = accs

            @pl.when(k == 0)
            def _():
                accg[...] = jnp.zeros(accg.shape, F32)
                accu[...] = jnp.zeros(accu.shape, F32)

            accg[...] += pg
            accu[...] += pu

            @pl.when(k == nk - 1)
            def _():
                finish(accg[...], accu[...])

    w_spec = pl.BlockSpec((tn, tk), lambda i, j, k: (j, k))
    o_spec = pl.BlockSpec((tm, tn), lambda i, j, k: (i, j))
    sds = jax.ShapeDtypeStruct((M, N), BF16)
    return _hosted_call(
        body, name, (M // tm, N // tn, nk), [pl.BlockSpec((tm, tk), lambda i, j, k: (i, k)), w_spec, w_spec],
        [o_spec, o_spec, o_spec], [sds, sds, sds],
        [pltpu.VMEM((tm, tn), F32), pltpu.VMEM((tm, tn), F32)] if nk > 1 else [],
        (hn, wg, wu), jobs, ("parallel", "parallel", "arbitrary"))


def _rmsnorm_fwd(x, g, name):
    M, D = x.shape
    tm = _tile(M, 256, 16)

    def body(x_ref, g_ref, o_ref):
        xv = x_ref[...]
        r = lax.rsqrt(jnp.mean(xv * xv, axis=-1, keepdims=True) + RMS_EPS)
        o_ref[...] = (xv * r * g_ref[...]).astype(BF16)

    return pl.pallas_call(
        body, name=name, grid=(M // tm,),
        in_specs=[pl.BlockSpec((tm, D), lambda i: (i, 0)), pl.BlockSpec((1, D), lambda i: (0, 0))],
        out_specs=pl.BlockSpec((tm, D), lambda i: (i, 0)),
        out_shape=jax.ShapeDtypeStruct((M, D), BF16),
        compiler_params=_params(("parallel",)),
    )(x, g)


def _fold8(v):
    tm, D = v.shape
    return jnp.sum(v.reshape(tm // 8, 8, D), axis=0)


def _rmsnorm_bwd(x, dy, g, res, name, want_bf16):
    M, D = x.shape
    tm = _tile(M, 128, 16)

    def body(x_ref, dy_ref, g_ref, r_ref, *outs):
        dx_ref, dg_ref = outs[0], outs[-1]
        xv = x_ref[...]
        dyv = dy_ref[...].astype(F32)
        r = lax.rsqrt(jnp.mean(xv * xv, axis=-1, keepdims=True) + RMS_EPS)
        u = dyv * g_ref[...]
        dot = jnp.sum(xv * u, axis=-1, keepdims=True)
        dx = r * u - xv * (r * r * r * (1.0 / D)) * dot + r_ref[...]
        dx_ref[...] = dx
        if want_bf16:
            outs[1][...] = dx.astype(BF16)
        part = _fold8(dyv * xv * r)

        @pl.when(pl.program_id(0) == 0)
        def _():
            dg_ref[...] = part

        @pl.when(pl.program_id(0) > 0)
        def _():
            dg_ref[...] += part

    row = pl.BlockSpec((tm, D), lambda i: (i, 0))
    out_specs = [row] + ([row] if want_bf16 else []) + [pl.BlockSpec((8, D), lambda i: (0, 0))]
    out_shape = ([jax.ShapeDtypeStruct((M, D), F32)] + ([jax.ShapeDtypeStruct((M, D), BF16)] if want_bf16 else [])
                 + [jax.ShapeDtypeStruct((8, D), F32)])
    return pl.pallas_call(
        body, name=name, grid=(M // tm,),
        in_specs=[row, row, pl.BlockSpec((1, D), lambda i: (0, 0)), row],
        out_specs=out_specs, out_shape=out_shape,
        compiler_params=_params(("arbitrary",)),
    )(x, dy, g, res)


def _loss_head(h, tgt, g, name):
    M, D = h.shape
    tm = _tile(M, 128, 16)

    def body(h_ref, t_ref, g_ref, dh_ref, dhb_ref, l_ref, dg_ref):
        hv = h_ref[...]
        gv = g_ref[...]
        r = lax.rsqrt(jnp.mean(hv * hv, axis=-1, keepdims=True) + RMS_EPS)
        n = hv * r
        e = n * gv - t_ref[...]
        dy = e * (1.0 / D)
        u = dy * gv
        dot = jnp.sum(hv * u, axis=-1, keepdims=True)
        dh = r * u - hv * (r * r * r * (1.0 / D)) * dot
        dh_ref[...] = dh
        dhb_ref[...] = dh.astype(BF16)
        rows = jnp.sum(e * e, axis=-1, keepdims=True)
        lpart = jnp.broadcast_to(jnp.sum(rows, axis=0, keepdims=True) * (0.5 / D), (8, LANES))
        gpart = _fold8(dy * n)

        @pl.when(pl.program_id(0) == 0)
        def _():
            l_ref[...] = lpart
            dg_ref[...] = gpart

        @pl.when(pl.program_id(0) > 0)
        def _():
            l_ref[...] += lpart
            dg_ref[...] += gpart

    row = pl.BlockSpec((tm, D), lambda i: (i, 0))
    return pl.pallas_call(
        body, name=name, grid=(M // tm,),
        in_specs=[row, row, pl.BlockSpec((1, D), lambda i: (0, 0))],
        out_specs=[row, row, pl.BlockSpec((8, LANES), lambda i: (0, 0)), pl.BlockSpec((8, D), lambda i: (0, 0))],
        out_shape=[jax.ShapeDtypeStruct((M, D), F32), jax.ShapeDtypeStruct((M, D), BF16),
                   jax.ShapeDtypeStruct((8, LANES), F32), jax.ShapeDtypeStruct((8, D), F32)],
        compiler_params=_params(("arbitrary",)),
    )(h, tgt, g)


def _headnorm_fwd(o_dil, o_sb, g, name):
    M, W = o_dil.shape
    D = 2 * W
    tm = _tile(M, 256, 16)

    def body(a_ref, b_ref, g_ref, o_ref):
        for h in range(D // HEAD_DIM):
            src = a_ref if h < W // HEAD_DIM else b_ref
            lo = (h * HEAD_DIM) % W
            v = src[:, lo:lo + HEAD_DIM]
            r = lax.rsqrt(jnp.mean(v * v, axis=-1, keepdims=True) + RMS_EPS)
            o_ref[:, h * HEAD_DIM:(h + 1) * HEAD_DIM] = (v * r * g_ref[:, h * HEAD_DIM:(h + 1) * HEAD_DIM]).astype(BF16)

    half = pl.BlockSpec((tm, W), lambda i: (i, 0))
    return pl.pallas_call(
        body, name=name, grid=(M // tm,),
        in_specs=[half, half, pl.BlockSpec((1, D), lambda i: (0, 0))],
        out_specs=pl.BlockSpec((tm, D), lambda i: (i, 0)),
        out_shape=jax.ShapeDtypeStruct((M, D), BF16),
        compiler_params=_params(("parallel",)),
    )(o_dil, o_sb, g)


def _headnorm_bwd(d_mix, o_dil, o_sb, g, name):
    M, W = o_dil.shape
    D = 2 * W
    tm = _tile(M, 128, 16)

    def body(d_ref, a_ref, b_ref, g_ref, do_ref, dg_ref):
        @pl.when(pl.program_id(0) == 0)
        def _():
            dg_ref[...] = jnp.zeros(dg_ref.shape, F32)

        for h in range(D // HEAD_DIM):
            src = a_ref if h < W // HEAD_DIM else b_ref
            lo = (h * HEAD_DIM) % W
            cols = slice(h * HEAD_DIM, (h + 1) * HEAD_DIM)
            v = src[:, lo:lo + HEAD_DIM]
            dy = d_ref[:, cols].astype(F32)
            r = lax.rsqrt(jnp.mean(v * v, axis=-1, keepdims=True) + RMS_EPS)
            u = dy * g_ref[:, cols]
            dot = jnp.sum(v * u, axis=-1, keepdims=True)
            do_ref[:, cols] = r * u - v * (r * r * r * (1.0 / HEAD_DIM)) * dot
            dg_ref[:, cols] += _fold8(dy * v * r)

    half = pl.BlockSpec((tm, W), lambda i: (i, 0))
    row = pl.BlockSpec((tm, D), lambda i: (i, 0))
    return pl.pallas_call(
        body, name=name, grid=(M // tm,),
        in_specs=[row, half, half, pl.BlockSpec((1, D), lambda i: (0, 0))],
        out_specs=[row, pl.BlockSpec((8, D), lambda i: (0, 0))],
        out_shape=[jax.ShapeDtypeStruct((M, D), F32), jax.ShapeDtypeStruct((8, D), F32)],
        compiler_params=_params(("arbitrary",)),
    )(d_mix, o_dil, o_sb, g)


def _rope_tables(positions):
    half = ROPE_DIM // 2
    inv_freq = jnp.power(jnp.float32(ROPE_THETA), -jnp.arange(half, dtype=F32) / half)
    ang = positions.astype(F32)[..., None] * inv_freq
    cos, sin = jnp.cos(ang), jnp.sin(ang)
    rest = HEAD_DIM - ROPE_DIM
    one = jnp.ones(cos.shape[:-1] + (rest,), F32)
    z16 = jnp.zeros_like(sin)
    zr = jnp.zeros_like(one)
    c = jnp.concatenate([cos, cos, one], axis=-1)
    s1 = jnp.concatenate([-sin, z16, zr], axis=-1)
    s2 = jnp.concatenate([z16, sin, zr], axis=-1)
    return c, s1, s2


def _dil_bias(T):
    ne = T // TQ
    e = jnp.arange(ne, dtype=jnp.int32)[:, None, None]
    r = jnp.arange(TQ, dtype=jnp.int32)[None, :, None]
    c = jnp.arange(TK_DIL, dtype=jnp.int32)[None, None, :]
    dist = e * TQ + r - c
    mult = jnp.zeros(dist.shape, F32)
    for d in DILATIONS:
        mult = mult + jnp.where((dist % d == 0) & (dist <= DIL_STEPS * d), 1.0, 0.0)
    return jnp.where((dist >= 0) & (mult > 0), jnp.log(jnp.maximum(mult, 1.0)), NEG)


def _dil_fwd(proj3, bias, Hd, name, jobs=()):
    B, T, N3 = proj3.shape
    nq = T // TQ
    per = TK_DIL // TQ
    scale = HEAD_DIM ** -0.5

    def body(q_ref, k_ref, v_ref, b_ref, o_ref, l_ref):
        i = pl.program_id(2)
        qs = [q_ref[r, :] for r in _ROW_PARTS]
        last = i // per

        def step(t, carry):
            j = last - t
            off = pl.multiple_of(j * TK_DIL, TK_DIL)
            kj = k_ref[pl.ds(off, TK_DIL), :]
            vj = v_ref[pl.ds(off, TK_DIL), :]
            new = []
            for n, r in enumerate(_ROW_PARTS):
                m, l, acc = carry[n]
                s = _dot(qs[n], kj, NT) * scale + b_ref[i - per * j, r, :]
                m_new = jnp.maximum(m, jnp.max(s, axis=-1, keepdims=True))
                p = jnp.exp(s - m_new)
                corr = jnp.exp(m - m_new)
                new.append((m_new, l * corr + jnp.sum(p, axis=-1, keepdims=True),
                            acc * corr + _dot(p.astype(BF16), vj, NN)))
            return tuple(new)

        init = (jnp.full((_PART, 1), NEG, F32), jnp.zeros((_PART, 1), F32), jnp.zeros((_PART, HEAD_DIM), F32))
        done = lax.fori_loop(0, last + 1, step, (init,) * len(_ROW_PARTS))
        for n, r in enumerate(_ROW_PARTS):
            m, l, acc = done[n]
            o_ref[r, :] = acc / l
            l_ref[r, :] = jnp.broadcast_to(m + jnp.log(l), (_PART, HEAD_DIM))

    full = lambda base: pl.BlockSpec((None, T, HEAD_DIM), lambda b, h, i: (b, 0, base + h))
    blk = lambda base: pl.BlockSpec((None, TQ, HEAD_DIM), lambda b, h, i: (b, i, base + h))
    sds = jax.ShapeDtypeStruct((B, T, Hd * HEAD_DIM), F32)
    return _hosted_call(
        body, name, (B, Hd, nq),
        [blk(0), full(Hd), full(2 * Hd), pl.BlockSpec(bias.shape, lambda b, h, i: (0, 0, 0))],
        [blk(0), blk(0)], [sds, sds], [], (proj3, proj3, proj3, bias), jobs, ("parallel", "parallel", "arbitrary"))


def _dil_bwd(proj3, bias, tabs, d_o3, o3, lse3, Hd, name, jobs=()):
    B, T, N3 = proj3.shape
    nq = T // TQ
    per = TK_DIL // TQ
    scale = HEAD_DIM ** -0.5

    def body(q_ref, k_ref, v_ref, b_ref, do_ref, o_ref, l_ref, cq, s1q, s2q, ct, s1t, s2t,
             dq_ref, dk_ref, dv_ref, dks, dvs):
        i = pl.program_id(2)

        @pl.when(i == 0)
        def _():
            dks[...] = jnp.zeros(dks.shape, F32)
            dvs[...] = jnp.zeros(dvs.shape, F32)

        q = q_ref[...]
        do = do_ref[...]
        dob = do.astype(BF16)
        dterm = jnp.sum(do * o_ref[...], axis=-1, keepdims=True)
        lse = jnp.concatenate([l_ref[...]] * (TK_DIL // HEAD_DIM), axis=1)
        last = i // per

        def step(j, dqs):
            off = pl.multiple_of(j * TK_DIL, TK_DIL)
            kj = k_ref[pl.ds(off, TK_DIL), :]
            vj = v_ref[pl.ds(off, TK_DIL), :]
            ps, dss, new = [], [], []
            for n, r in enumerate(_ROW_PARTS):
                s = _dot(q[r], kj, NT) * scale + b_ref[i - per * j, r, :]
                p = jnp.exp(s - lse[r])
                dp = _dot(dob[r], vj, NT)
                ds = (p * (dp - dterm[r]) * scale).astype(BF16)
                new.append(dqs[n] + _dot(ds, kj, NN))
                ps.append(p.astype(BF16))
                dss.append(ds)
            dks[pl.ds(off, TK_DIL), :] += _dot(jnp.concatenate(dss, axis=0), q, TN)
            dvs[pl.ds(off, TK_DIL), :] += _dot(jnp.concatenate(ps, axis=0), dob, TN)
            return tuple(new)

        zero = jnp.zeros((_PART, HEAD_DIM), F32)
        dq = jnp.concatenate(lax.fori_loop(0, last + 1, step, (zero,) * len(_ROW_PARTS)), axis=0)
        dq_ref[...] = _rope(dq, cq[...], -s1q[...], -s2q[...]).astype(BF16)

        @pl.when(i == nq - 1)
        def _():
            dk_ref[...] = _rope(dks[...], ct[...], -s1t[...], -s2t[...]).astype(BF16)
            dv_ref[...] = dvs[...].astype(BF16)

    full = lambda base: pl.BlockSpec((None, T, HEAD_DIM), lambda b, h, i: (b, 0, base + h))
    blk = lambda base: pl.BlockSpec((None, TQ, HEAD_DIM), lambda b, h, i: (b, i, base + h))
    tab_q = pl.BlockSpec((None, TQ, HEAD_DIM), lambda b, h, i: (b, i, 0))
    tab_t = pl.BlockSpec((None, T, HEAD_DIM), lambda b, h, i: (b, 0, 0))
    sds = jax.ShapeDtypeStruct((B, T, Hd * HEAD_DIM), BF16)
    return _hosted_call(
        body, name, (B, Hd, nq),
        [blk(0), full(Hd), full(2 * Hd), pl.BlockSpec(bias.shape, lambda b, h, i: (0, 0, 0)),
         blk(0), blk(0), blk(0), tab_q, tab_q, tab_q, tab_t, tab_t, tab_t],
        [blk(0), full(0), full(0)], [sds, sds, sds],
        [pltpu.VMEM((T, HEAD_DIM), F32), pltpu.VMEM((T, HEAD_DIM), F32)],
        (proj3, proj3, proj3, bias, d_o3, o3, lse3, *tabs, *tabs), jobs, ("parallel", "parallel", "arbitrary"))


def _softplus(z):
    return jnp.maximum(z, 0.0) + jnp.log(1.0 + jnp.exp(-jnp.abs(z)))


def _tri_sum(x, m01):
    n = x.shape[0]
    hi = x.astype(BF16)
    lo = (x - hi.astype(F32)).astype(BF16)
    both = _dot(jnp.concatenate([hi, lo], axis=0), m01, NN)
    return both[:n] + both[n:]


def _sb_iota(rows=TQ, first=0):
    row = lax.broadcasted_iota(jnp.int32, (rows, TK_SB), 0) + first
    col = lax.broadcasted_iota(jnp.int32, (rows, TK_SB), 1)
    return row, col


def _sb_fwd(proj3, Hd, Hs, name, jobs=()):
    B, T, N3 = proj3.shape
    nq = T // TQ
    qb, kb_, vb_ = 3 * Hd, 3 * Hd + Hs, 3 * Hd + 2 * Hs
    scale = HEAD_DIM ** -0.5

    assert TQ == 2 * TK_SB
    low = slice(TK_SB, TQ)

    def body(q_ref, k_ref, v_ref, o_ref, lt_ref):
        i = pl.program_id(2)
        q = q_ref[...]
        row, col = _sb_iota(TK_SB)
        after = jnp.where(row > col, 1.0, 0.0).astype(BF16)

        def chunk(j, qr, rows, first_row, acc, cs):
            off = pl.multiple_of(j * TK_SB, TK_SB)
            kj = k_ref[pl.ds(off, TK_SB), :]
            vj = v_ref[pl.ds(off, TK_SB), :]
            z = _dot(qr, kj, NT) * scale
            sp = _softplus(z)
            ln = -sp
            if first_row is not None:
                r_i, c_i = _sb_iota(rows, first_row)
                msk = c_i < r_i + (i * TQ - j * TK_SB)
                ln = jnp.where(msk, ln, 0.0)
            a = jnp.exp(z - sp + _tri_sum(ln, after) + cs)
            if first_row is not None:
                a = jnp.where(msk, a, 0.0)
            return acc + _dot(a.astype(BF16), vj, NN), cs + jnp.sum(ln, axis=-1, keepdims=True)

        acc_l, cs_l = chunk(2 * i + 1, q[low], TK_SB, TK_SB, jnp.zeros((TK_SB, HEAD_DIM), F32),
                            jnp.zeros((TK_SB, 1), F32))
        carry = chunk(2 * i, q, TQ, 0, jnp.concatenate([jnp.zeros((TK_SB, HEAD_DIM), F32), acc_l], axis=0),
                      jnp.concatenate([jnp.zeros((TK_SB, 1), F32), cs_l], axis=0))
        acc, cs = lax.fori_loop(0, 2 * i, lambda t, carry: chunk(2 * i - 1 - t, q, TQ, None, *carry), carry)
        o_ref[...] = acc
        lt_ref[...] = jnp.broadcast_to(cs, (TQ, HEAD_DIM))

    full = lambda base: pl.BlockSpec((None, T, HEAD_DIM), lambda b, h, i: (b, 0, base + h))
    blk = lambda base: pl.BlockSpec((None, TQ, HEAD_DIM), lambda b, h, i: (b, i, base + h))
    sds = jax.ShapeDtypeStruct((B, T, Hs * HEAD_DIM), F32)
    return _hosted_call(
        body, name, (B, Hs, nq), [blk(qb), full(kb_), full(vb_)], [blk(0), blk(0)], [sds, sds], [],
        (proj3, proj3, proj3), jobs, ("parallel", "parallel", "arbitrary"))


def _sb_bwd(proj3, d_o3, lt3, Hd, Hs, name, jobs=()):
    B, T, N3 = proj3.shape
    nq = T // TQ
    qb, kb_, vb_ = 3 * Hd, 3 * Hd + Hs, 3 * Hd + 2 * Hs
    scale = HEAD_DIM ** -0.5

    assert TQ == 2 * TK_SB
    low = slice(TK_SB, TQ)

    def body(q_ref, k_ref, v_ref, do_ref, lt_ref, dq_ref, dk_ref, dv_ref, dks, dvs):
        i = pl.program_id(2)

        @pl.when(i == 0)
        def _():
            dks[...] = jnp.zeros(dks.shape, F32)
            dvs[...] = jnp.zeros(dvs.shape, F32)

        q = q_ref[...]
        dob = do_ref[...].astype(BF16)
        total = jnp.concatenate([lt_ref[...]] * (TK_SB // HEAD_DIM), axis=1)
        row, col = _sb_iota(TK_SB)
        before = jnp.where(row < col, 1.0, 0.0).astype(BF16)

        def chunk(j, qr, dor, tot, rows, first_row, dq, pc, gc):
            off = pl.multiple_of(j * TK_SB, TK_SB)
            kj = k_ref[pl.ds(off, TK_SB), :]
            vj = v_ref[pl.ds(off, TK_SB), :]
            z = _dot(qr, kj, NT) * scale
            sp = _softplus(z)
            ln = -sp
            if first_row is not None:
                r_i, c_i = _sb_iota(rows, first_row)
                msk = c_i < r_i + (i * TQ - j * TK_SB)
                ln = jnp.where(msk, ln, 0.0)
            excl = tot - (_tri_sum(ln, before) + ln + pc)
            a = jnp.exp(z - sp + excl)
            if first_row is not None:
                a = jnp.where(msk, a, 0.0)
            g = a * _dot(dor, vj, NT)
            big = _tri_sum(g, before) + gc
            dz = g - (g + big) * jnp.exp(z - sp)
            if first_row is not None:
                dz = jnp.where(msk, dz, 0.0)
            dz = dz.astype(BF16)
            dks[pl.ds(off, TK_SB), :] += _dot(dz, qr, TN)
            dvs[pl.ds(off, TK_SB), :] += _dot(a.astype(BF16), dor, TN)
            return (dq + _dot(dz, kj, NN), pc + jnp.sum(ln, axis=-1, keepdims=True),
                    gc + jnp.sum(g, axis=-1, keepdims=True))

        zero1 = jnp.zeros((TQ, 1), F32)
        carry = lax.fori_loop(0, 2 * i, lambda j, carry: chunk(j, q, dob, total, TQ, None, *carry),
                              (jnp.zeros((TQ, HEAD_DIM), F32), zero1, zero1))
        dq, pc, gc = chunk(2 * i, q, dob, total, TQ, 0, *carry)
        dq_l, _, _ = chunk(2 * i + 1, q[low], dob[low], total[low], TK_SB, TK_SB, dq[low], pc[low], gc[low])
        dq_ref[...] = (jnp.concatenate([dq[:TK_SB], dq_l], axis=0) * scale).astype(BF16)

        @pl.when(i == nq - 1)
        def _():
            dk_ref[...] = (dks[...] * scale).astype(BF16)
            dv_ref[...] = dvs[...].astype(BF16)

    full = lambda base: pl.BlockSpec((None, T, HEAD_DIM), lambda b, h, i: (b, 0, base + h))
    blk = lambda base: pl.BlockSpec((None, TQ, HEAD_DIM), lambda b, h, i: (b, i, base + h))
    sds = jax.ShapeDtypeStruct((B, T, Hs * HEAD_DIM), BF16)
    return _hosted_call(
        body, name, (B, Hs, nq), [blk(qb), full(kb_), full(vb_), blk(Hd), blk(0)],
        [blk(0), full(0), full(0)], [sds, sds, sds],
        [pltpu.VMEM((T, HEAD_DIM), F32), pltpu.VMEM((T, HEAD_DIM), F32)],
        (proj3, proj3, proj3, d_o3, lt3), jobs, ("parallel", "parallel", "arbitrary"))


def _place():
    x, y, c = lax.axis_index("x"), lax.axis_index("y"), lax.axis_index("c")
    chips = [(1 - x, y), (x, 1 - y), (1 - x, 1 - y)]
    return x, y, c, chips


class _Weight:
    def __init__(self, kind, shard_shape, slot):
        self.kind, self.slot = kind, slot
        self.R, self.C = shard_shape
        if kind == "col":
            assert slot == self.C
            self.full = (self.R, N_CHIPS * slot)
            self.half = (self.R // 2, N_CHIPS * slot)
            self.piece = (self.R // 2, slot)
        else:
            self.full = (N_CHIPS * slot, self.C)
            self.half = (N_CHIPS * slot, self.C // 2)
            self.piece = (self.R, self.C // 2)

    def _rows(self, jj, n):
        return pl.ds(pl.multiple_of(jj * self.slot, 16), n)

    def full_half(self, ref, jj, hc):
        if self.kind == "col":
            return ref.at[pl.ds(hc * (self.R // 2), self.R // 2), pl.ds(pl.multiple_of(jj * self.slot, LANES), self.C)]
        return ref.at[self._rows(jj, self.R), pl.ds(hc * (self.C // 2), self.C // 2)]

    def region_half(self, ref, hc):
        if self.kind == "col":
            return ref.at[pl.ds(hc * (self.R // 2), self.R // 2), :]
        return ref.at[:, pl.ds(hc * (self.C // 2), self.C // 2)]

    def half_piece(self, ref, jj):
        if self.kind == "col":
            return ref.at[:, pl.ds(pl.multiple_of(jj * self.slot, LANES), self.slot)]
        return ref.at[self._rows(jj, self.R), :]


def _cast_into_full(w, geo, j_arr, name):
    R, C = w.shape
    tr = _tile(R, 256, 16)
    if geo.kind == "col":
        o_spec = pl.BlockSpec((tr, C), lambda i, j_ref: (i, j_ref[0]))
    else:
        while geo.slot % tr or R % tr:
            tr -= 16
        spb = geo.slot // tr
        o_spec = pl.BlockSpec((tr, C), lambda i, j_ref: (j_ref[0] * spb + i, 0))

    def body(j_ref, w_ref, o_ref):
        o_ref[...] = w_ref[...].astype(BF16)

    return pl.pallas_call(
        body, name=name,
        grid_spec=pltpu.PrefetchScalarGridSpec(
            num_scalar_prefetch=1, grid=(R // tr,),
            in_specs=[pl.BlockSpec((tr, C), lambda i, j_ref: (i, 0))], out_specs=o_spec),
        out_shape=jax.ShapeDtypeStruct(geo.full, BF16),
        compiler_params=_params(("parallel",)),
    )(j_arr, w)


def _zero_pad(full, geo, name):
    pad = geo.slot - geo.R
    assert geo.kind == "row" and pad > 0 and geo.R % pad == 0 and geo.slot % pad == 0

    def body(f_ref, o_ref):
        o_ref[...] = jnp.zeros(o_ref.shape, BF16)

    return pl.pallas_call(
        body, name=name, grid=(N_CHIPS,), in_specs=[ANY],
        out_specs=pl.BlockSpec((pad, geo.C), lambda jj: ((jj * geo.slot + geo.R) // pad, 0)),
        out_shape=jax.ShapeDtypeStruct(geo.full, BF16), input_output_aliases={0: 0},
        compiler_params=_params(("arbitrary",)),
    )(full)


def _gather_job(full, geo, stage):
    def copies(ins, outs, send, recv):
        x, y, c, chips = _place()
        j = 2 * x + y

        def cp(src, dst, k, to):
            return pltpu.make_async_remote_copy(src_ref=src, dst_ref=dst, send_sem=send.at[k], recv_sem=recv.at[k],
                                                device_id=to, device_id_type=MESH)

        started, landing = [], []
        for k, (cx, cy) in enumerate(chips):
            jk = 2 * cx + cy
            if stage == "ici":
                started.append(cp(geo.full_half(ins[0], j, c), geo.full_half(outs[0], j, c), k, (cx, cy, c)))
                part = geo.full_half(outs[0], jk, c)
            else:
                started.append(cp(geo.full_half(ins[0], jk, c), geo.full_half(outs[0], jk, c), k, (x, y, 1 - c)))
                part = geo.full_half(outs[0], jk, 1 - c)
            landing.append(functools.partial(cp, part, part, k, (x, y, c)))
        return started, landing

    return _Job([full], [jax.ShapeDtypeStruct(geo.full, BF16)], {0: 0}, 3, copies)


def _sibling_job(grad_full, geo):
    def copies(ins, outs, send, recv):
        x, y, c, _ = _place()
        mk = lambda src, to: pltpu.make_async_remote_copy(
            src_ref=src, dst_ref=outs[0], send_sem=send.at[0], recv_sem=recv.at[0], device_id=to, device_id_type=MESH)
        return [mk(geo.region_half(ins[0], 1 - c), (x, y, 1 - c))], [functools.partial(mk, outs[0], (x, y, c))]

    return _Job([grad_full], [jax.ShapeDtypeStruct(geo.half, BF16)], {}, 1, copies)


def _chips_job(chip_sum, geo, part=(0, 1), so_far=None):
    p, n = part
    rows = geo.piece[0] // n
    assert geo.piece[0] % n == 0 and rows % 16 == 0

    def copies(ins, outs, send, recv):
        x, y, c, chips = _place()
        started, landing = [], []

        def mk(k, src, to):
            dst = outs[0].at[k, pl.ds(p * rows, rows), :]
            return pltpu.make_async_remote_copy(src_ref=dst if src is None else src, dst_ref=dst, send_sem=send.at[k],
                                                recv_sem=recv.at[k], device_id=to, device_id_type=MESH)

        for k, (cx, cy) in enumerate(chips):
            started.append(mk(k, geo.half_piece(ins[0], 2 * cx + cy).at[pl.ds(p * rows, rows), :], (cx, cy, c)))
            landing.append(functools.partial(mk, k, None, (x, y, c)))
        return started, landing

    inputs = [chip_sum] + ([] if so_far is None else [so_far])
    return _Job(inputs, [jax.ShapeDtypeStruct((3,) + geo.piece, BF16)], {} if so_far is None else {1: 0}, 3, copies)


def _final_job(half, geo):
    def copies(ins, outs, send, recv):
        x, y, c, _ = _place()
        mk = lambda src, to: pltpu.make_async_remote_copy(
            src_ref=src, dst_ref=outs[0], send_sem=send.at[0], recv_sem=recv.at[0], device_id=to, device_id_type=MESH)
        return [mk(ins[0], (x, y, 1 - c))], [functools.partial(mk, outs[0], (x, y, c))]

    return _Job([half], [jax.ShapeDtypeStruct(geo.piece, F32)], {}, 1, copies)


def _gather_two_step(full, geo, name):
    R = geo.R
    assert geo.kind == "col" and R % 64 == 0

    def body(own_ref, f_ref, send, recv):
        x, y, c, _ = _place()
        j, jx, jy, jd = 2 * x + y, 2 * (1 - x) + y, 2 * x + (1 - y), 2 * (1 - x) + (1 - y)
        me, xn, yn, sib = (x, y, c), (1 - x, y, c), (x, 1 - y, c), (x, y, 1 - c)

        def half(ref, jj, hc):
            return geo.full_half(ref, jj, hc)

        def quarter(ref, jj, hc, q):
            return ref.at[pl.ds(hc * (R // 2) + q * (R // 4), R // 4), pl.ds(pl.multiple_of(jj * geo.slot, LANES), geo.C)]

        def cp(k, src, dst, to):
            return pltpu.make_async_remote_copy(src_ref=src, dst_ref=dst, send_sem=send.at[k], recv_sem=recv.at[k],
                                                device_id=to, device_id_type=MESH)

        def arrived(k, part):
            cp(k, part, part, me).wait_recv()

        def pass_on(k, part, to):
            d = cp(k, part, part, to)
            d.start()
            return d

        sent = [cp(0, half(own_ref, j, c), half(f_ref, j, c), xn), cp(1, half(own_ref, j, c), half(f_ref, j, c), yn)]
        for d in sent:
            d.start()
        arrived(0, half(f_ref, jx, c))
        sent.append(pass_on(2, quarter(f_ref, jx, c, 0), yn))
        sent.append(pass_on(4, half(f_ref, jx, c), sib))
        arrived(1, half(f_ref, jy, c))
        sent.append(pass_on(3, quarter(f_ref, jy, c, 1), xn))
        sent.append(pass_on(5, half(f_ref, jy, c), sib))
        arrived(2, quarter(f_ref, jd, c, 0))
        sent.append(pass_on(6, quarter(f_ref, jd, c, 0), sib))
        arrived(3, quarter(f_ref, jd, c, 1))
        sent.append(pass_on(7, quarter(f_ref, jd, c, 1), sib))
        arrived(4, half(f_ref, jx, 1 - c))
        arrived(5, half(f_ref, jy, 1 - c))
        arrived(6, quarter(f_ref, jd, 1 - c, 0))
        arrived(7, quarter(f_ref, jd, 1 - c, 1))
        for d in sent:
            d.wait_send()

    return pl.pallas_call(
        body, name=name, in_specs=[ANY], out_specs=ANY, out_shape=jax.ShapeDtypeStruct(geo.full, BF16),
        input_output_aliases={0: 0},
        scratch_shapes=[pltpu.SemaphoreType.DMA((8,)), pltpu.SemaphoreType.DMA((8,))],
    )(full)


def _chip_sum(g_full, got, geo, c_arr, name):
    Rh, Ch = geo.half
    tr, tc = _tile(Rh, 256, 16), _tile(Ch, 2048)
    nrb, ncb = Rh // tr, Ch // tc

    def body(c_ref, a_ref, b_ref, o_ref):
        o_ref[...] = (a_ref[...].astype(F32) + b_ref[...].astype(F32)).astype(BF16)

    if geo.kind == "col":
        a_spec = pl.BlockSpec((tr, tc), lambda i, j, c_ref: (c_ref[0] * nrb + i, j))
    else:
        a_spec = pl.BlockSpec((tr, tc), lambda i, j, c_ref: (i, c_ref[0] * ncb + j))
    spec = pl.BlockSpec((tr, tc), lambda i, j, c_ref: (i, j))
    return pl.pallas_call(
        body, name=name,
        grid_spec=pltpu.PrefetchScalarGridSpec(num_scalar_prefetch=1, grid=(nrb, ncb),
                                               in_specs=[a_spec, spec], out_specs=spec),
        out_shape=jax.ShapeDtypeStruct(geo.half, BF16),
        compiler_params=_params(("parallel", "parallel")),
    )(c_arr, g_full, got)


def _final_sum(chip_sum, got, geo, j_arr, name):
    Rp, Cp = geo.piece
    if geo.kind == "col":
        tr = _tile(Rp, 128, 16)
        a_spec = pl.BlockSpec((tr, Cp), lambda i, j_ref: (i, j_ref[0]))
    else:
        tr = _tile(Rp, 128, 16)
        while geo.slot % tr:
            tr -= 16
        assert Rp % tr == 0 and geo.slot % tr == 0
        spb = geo.slot // tr
        a_spec = pl.BlockSpec((tr, Cp), lambda i, j_ref: (j_ref[0] * spb + i, 0))
    r_specs = [pl.BlockSpec((None, tr, Cp), functools.partial(lambda i, j_ref, k: (k, i, 0), k=k)) for k in range(3)]
    o_spec = pl.BlockSpec((tr, Cp), lambda i, j_ref: (i, 0))

    def body(j_ref, a_ref, r0, r1, r2, o_ref):
        o_ref[...] = ((a_ref[...].astype(F32) + r0[...].astype(F32)) + r1[...].astype(F32)) + r2[...].astype(F32)

    return pl.pallas_call(
        body, name=name,
        grid_spec=pltpu.PrefetchScalarGridSpec(num_scalar_prefetch=1, grid=(Rp // tr,),
                                               in_specs=[a_spec] + r_specs, out_specs=o_spec),
        out_shape=jax.ShapeDtypeStruct(geo.piece, F32),
        compiler_params=_params(("parallel",)),
    )(j_arr, chip_sum, got, got, got)


def _run_job(job, name):
    ni = len(job.inputs)

    def body(*refs):
        ins, outs, (send, recv) = refs[:ni], refs[ni:-2], refs[-2:]
        job.start(ins, outs, send, recv)
        job.finish(ins, outs, send, recv)

    return pl.pallas_call(
        body, name=name, in_specs=[ANY] * ni, out_specs=[ANY] * len(job.outputs), out_shape=list(job.outputs),
        input_output_aliases=dict(job.aliases),
        scratch_shapes=[pltpu.SemaphoreType.DMA((job.nsem,)), pltpu.SemaphoreType.DMA((job.nsem,))],
    )(*job.inputs)


def _adam_math(w, g, m, v):
    m = ADAM_B1 * m + (1.0 - ADAM_B1) * g
    v = ADAM_B2 * v + (1.0 - ADAM_B2) * (g * g)
    m_hat = m / (1.0 - ADAM_B1 ** ADAM_STEP)
    v_hat = v / (1.0 - ADAM_B2 ** ADAM_STEP)
    delta = -ADAM_LR * (m_hat / (jnp.sqrt(v_hat) + ADAM_EPS) + ADAM_WD * w)
    return delta, m, v


def _adam(ws, mines, theirs, ms, vs, geo, name, jobs=()):
    n = len(ws)
    R, C = ws[0].shape
    tr = _tile(R, 128 if n == 1 else 32, 8)
    nrb = R // tr
    col = geo.kind == "col"
    assert not col or nrb % 2 == 0

    def rows(k, halved):
        def index(w, i):
            r = jnp.where(w == k, i, jnp.where(w < k, 0, nrb - 1))
            return (r % (nrb // 2) if halved else r, 0)
        return index

    def body(*refs):
        ins, outs = refs[:5 * n], refs[5 * n:]
        wi, i = pl.program_id(0), pl.program_id(1)
        c = lax.axis_index("c")
        for k in range(n):
            @pl.when(wi == k)
            def _():
                w_ref, a_ref, b_ref, m_ref, v_ref = ins[5 * k:5 * k + 5]
                if col:
                    gv = jnp.where(i // (nrb // 2) == c, a_ref[...], b_ref[...])
                else:
                    a, b = a_ref[...], b_ref[...]
                    gv = jnp.concatenate([jnp.where(c == 0, a, b), jnp.where(c == 0, b, a)], axis=1)
                delta, mn, vn = _adam_math(w_ref[...], gv, m_ref[...], v_ref[...])
                og, od, om, ov = outs[4 * k:4 * k + 4]
                og[...] = gv
                od[...] = delta
                om[...] = mn
                ov[...] = vn

    in_specs, out_specs, args = [], [], []
    for k in range(n):
        spec = pl.BlockSpec((tr, C), rows(k, False))
        h_spec = pl.BlockSpec((tr, C), rows(k, True)) if col else pl.BlockSpec((tr, C // 2), rows(k, False))
        in_specs += [spec, h_spec, h_spec, spec, spec]
        out_specs += [spec] * 4
        args += [ws[k], mines[k], theirs[k], ms[k], vs[k]]
    outs, job_outs = _hosted_call(body, name, (n, nrb), in_specs, out_specs,
                                  [jax.ShapeDtypeStruct((R, C), F32)] * (4 * n), [], args, jobs,
                                  ("arbitrary", "arbitrary"))
    return [outs[4 * k:4 * k + 4] for k in range(n)], job_outs


def _small_all_reduce(parts, loss_part, D):
    n = len(parts)

    def body(*refs):
        p_refs, l_ref, o_ref, vec, buf, send, recv = refs[:n], refs[n], refs[n + 1], *refs[n + 2:]
        x, y, c, _ = _place()
        me = 4 * x + 2 * y + c
        vec[...] = jnp.zeros(vec.shape, F32)
        for r in range(n):
            vec[r:r + 1, :] = jnp.sum(p_refs[r][...], axis=0, keepdims=True)
        vec[n:n + 1, 0:LANES] = l_ref[0:1, :]
        buf[me] = vec[...]
        cps = []
        for dd in range(1, 8):
            bx, by, bc = (dd >> 2) & 1, (dd >> 1) & 1, dd & 1
            peer = (x + bx - 2 * x * bx, y + by - 2 * y * by, c + bc - 2 * c * bc)
            cp = pltpu.make_async_remote_copy(
                src_ref=vec, dst_ref=buf.at[me], send_sem=send.at[dd - 1], recv_sem=recv.at[dd - 1],
                device_id=peer, device_id_type=MESH)
            cp.start()
            cps.append(cp)
        for cp in cps:
            cp.wait()
        tot = buf[0]
        for s in range(1, 8):
            tot = tot + buf[s]
        o_ref[...] = tot

    vm = pl.BlockSpec(memory_space=pltpu.VMEM)
    return pl.pallas_call(
        body, name="small_all_reduce",
        in_specs=[vm] * (n + 1), out_specs=vm,
        out_shape=jax.ShapeDtypeStruct((8, D), F32),
        scratch_shapes=[pltpu.VMEM((8, D), F32), pltpu.VMEM((8, 8, D), F32),
                        pltpu.SemaphoreType.DMA((7,)), pltpu.SemaphoreType.DMA((7,))],
    )(*parts, loss_part)


def _small_adam(tot, ws, ms, vs, rows):
    n = len(ws)

    def body(*refs):
        t_ref = refs[0]
        w_refs, m_refs, v_refs = refs[1:1 + n], refs[1 + n:1 + 2 * n], refs[1 + 2 * n:1 + 3 * n]
        outs = refs[1 + 3 * n:]
        for i in range(n):
            r, c0 = rows[i]
            width = w_refs[i].shape[1]
            g = t_ref[r:r + 1, c0:c0 + width]
            delta, mn, vn = _adam_math(w_refs[i][...], g, m_refs[i][...], v_refs[i][...])
            outs[4 * i][...] = g
            outs[4 * i + 1][...] = delta
            outs[4 * i + 2][...] = mn
            outs[4 * i + 3][...] = vn

    vm = pl.BlockSpec(memory_space=pltpu.VMEM)
    out_shape = []
    for w in ws:
        out_shape += [jax.ShapeDtypeStruct(w.shape, F32)] * 4
    return pl.pallas_call(
        body, name="small_adam",
        in_specs=[vm] * (1 + 3 * n), out_specs=[vm] * (4 * n), out_shape=out_shape,
    )(tot, *ws, *ms, *vs)


def kernel(x, positions, norm_mix_g, w_in, norm_out_dil_g, norm_out_sb_g, w_out, norm_ffn_g, w_gate, w_up, w_down, norm_final_g, loss_target, m_norm_mix_g, m_w_in, m_norm_out_dil_g, m_norm_out_sb_g, m_w_out, m_norm_ffn_g, m_w_gate, m_w_up, m_w_down, m_norm_final_g, v_norm_mix_g, v_w_in, v_norm_out_dil_g, v_norm_out_sb_g, v_w_out, v_norm_ffn_g, v_w_gate, v_w_up, v_w_down, v_norm_final_g):
    B, T, D = x.shape
    M = B * T
    n_heads = D // HEAD_DIM
    Hd = n_heads // 2
    Hs = n_heads - Hd
    W = Hd * HEAD_DIM
    N3 = 3 * D
    fs = w_gate.shape[2]
    fp = -(-fs // LANES) * LANES
    assert T % TK_DIL == 0 and W == Hs * HEAD_DIM and fp > fs

    x2 = x.reshape(M, D)
    tgt = loss_target.reshape(M, D)
    tr_ = jnp.transpose
    big = [w_in[0], w_out[0], tr_(w_gate[0]), tr_(w_up[0]), w_down[0]]
    big_m = [m_w_in[0], m_w_out[0], tr_(m_w_gate[0]), tr_(m_w_up[0]), m_w_down[0]]
    big_v = [v_w_in[0], v_w_out[0], tr_(v_w_gate[0]), tr_(v_w_up[0]), v_w_down[0]]
    names = ["w_in", "w_out", "w_gate", "w_up", "w_down"]
    c_arr = jnp.reshape(lax.axis_index("c"), (1,)).astype(jnp.int32)
    j_arr = jnp.reshape(2 * lax.axis_index("x") + lax.axis_index("y"), (1,)).astype(jnp.int32)

    ns_in = w_in.shape[2]
    ks_out = w_out.shape[1]
    geo = [_Weight("col", (D, ns_in), ns_in), _Weight("row", (ks_out, D), ks_out),
           _Weight("row", (fs, D), fp), _Weight("row", (fs, D), fp), _Weight("row", (fs, D), fp)]
    g_in, g_out, g_gate, g_up, g_down = geo
    fulls = [_cast_into_full(big[w], geo[w], j_arr, "cast_" + names[w]) for w in range(5)]
    for w in (2, 3, 4):
        fulls[w] = _zero_pad(fulls[w], geo[w], "zero_pad_" + names[w])

    def chip_sum(w, grad_full, from_sibling):
        return _chip_sum(grad_full, from_sibling, geo[w], c_arr, "chip_sum_" + names[w])

    def final_sum(w, cs, from_chips):
        return _final_sum(cs, from_chips, geo[w], j_arr, "final_sum_" + names[w])

    Win = _gather_two_step(fulls[0], g_in, "gather_w_in")
    tabs3 = _rope_tables(positions)
    tabs2 = [t.reshape(M, HEAD_DIM) for t in tabs3]
    hn = _rmsnorm_fwd(x2, norm_mix_g, "norm_mix")
    proj, (WgT,) = _matmul(hn, Win, "nn", BF16, "in_proj", rope=(tabs2, 2 * W),
                           jobs=[_gather_job(fulls[2], g_gate, "ici")])
    proj3 = proj.reshape(B, T, N3)
    bias = _dil_bias(T)
    (o_dil3, lse3), (WgT, Wout) = _dil_fwd(
        proj3, bias, Hd, "dil_fwd", jobs=[_gather_job(WgT, g_gate, "d2d"), _gather_job(fulls[1], g_out, "ici")])
    (o_sb3, lt3), (Wout, WuT) = _sb_fwd(
        proj3, Hd, Hs, "sb_fwd", jobs=[_gather_job(Wout, g_out, "d2d"), _gather_job(fulls[3], g_up, "ici")])
    o_dil, o_sb = o_dil3.reshape(M, W), o_sb3.reshape(M, W)
    g_heads = jnp.concatenate([norm_out_dil_g, norm_out_sb_g], axis=1)
    o_mix = _headnorm_fwd(o_dil, o_sb, g_heads, "head_norm")
    h1, (WuT,) = _matmul(o_mix, Wout, "nn", F32, "out_proj", res=x2, jobs=[_gather_job(WuT, g_up, "d2d")])
    hn2 = _rmsnorm_fwd(h1, norm_ffn_g, "norm_ffn")
    (gate, up, act), (Wd,) = _gate_up(hn2, WgT, WuT, "gate_up", jobs=[_gather_job(fulls[4], g_down, "ici")])
    (Wd,) = _run_job(_gather_job(Wd, g_down, "d2d"), "gather_w_down_d2d")
    h2 = _matmul(act, Wd, "nn", F32, "down_proj", res=h1)
    dh2, dh2_b, loss_part, dg_final = _loss_head(h2, tgt, norm_final_g.reshape(1, D), "loss_head")

    dWd = _matmul(act, dh2_b, "tn", BF16, "dw_down")
    (d_gate, d_up), (sib_d,) = _matmul(dh2_b, Wd, "nt", BF16, "d_act", swiglu=(gate, up), tn=512,
                                       jobs=[_sibling_job(dWd, g_down)])
    cs_d = chip_sum(4, dWd, sib_d)
    dWg, (chips_d,) = _matmul(d_gate, hn2, "tn", BF16, "dw_gate", jobs=[_chips_job(cs_d, g_down, (0, 2))])
    dWu, (chips_d, sib_g) = _matmul(d_up, hn2, "tn", BF16, "dw_up",
                                    jobs=[_chips_job(cs_d, g_down, (1, 2), chips_d), _sibling_job(dWg, g_gate)])
    half_d = final_sum(4, cs_d, chips_d)
    cs_g = chip_sum(2, dWg, sib_g)
    d_hn2, (chips_g, other_d) = _matmul(d_gate, WgT, "nn", F32, "d_hn2_gate",
                                        jobs=[_chips_job(cs_g, g_gate, (0, 2)), _final_job(half_d, g_down)])
    d_hn2, (chips_g, sib_u) = _matmul(d_up, WuT, "nn", F32, "d_hn2_up", res=d_hn2,
                                      jobs=[_chips_job(cs_g, g_gate, (1, 2), chips_g), _sibling_job(dWu, g_up)])
    half_g = final_sum(2, cs_g, chips_g)
    cs_u = chip_sum(3, dWu, sib_u)
    dh1, dh1_b, dg_ffn = _rmsnorm_bwd(h1, d_hn2, norm_ffn_g, dh2, "norm_ffn_bwd", True)
    dWo, (other_g,) = _matmul(o_mix, dh1_b, "tn", BF16, "dw_out", jobs=[_final_job(half_g, g_gate)])
    d_mix, (sib_o,) = _matmul(dh1_b, Wout, "nt", BF16, "d_mix", jobs=[_sibling_job(dWo, g_out)])
    cs_o = chip_sum(1, dWo, sib_o)
    d_o, dg_heads = _headnorm_bwd(d_mix, o_dil, o_sb, g_heads, "head_norm_bwd")
    d_o3 = d_o.reshape(B, T, D)
    dqkv_dil, (chips_u,) = _dil_bwd(proj3, bias, tabs3, d_o3, o_dil3, lse3, Hd, "dil_bwd",
                                    jobs=[_chips_job(cs_u, g_up, (0, 2))])
    dqkv_sb, (chips_u, chips_o) = _sb_bwd(proj3, d_o3, lt3, Hd, Hs, "sb_bwd",
                                          jobs=[_chips_job(cs_u, g_up, (1, 2), chips_u), _chips_job(cs_o, g_out)])
    half_u = final_sum(3, cs_u, chips_u)
    half_o = final_sum(1, cs_o, chips_o)
    dproj = jnp.concatenate([*dqkv_dil, *dqkv_sb], axis=-1).reshape(M, N3)
    dWin, (other_u, other_o) = _matmul(hn, dproj, "tn", BF16, "dw_in",
                                       jobs=[_final_job(half_u, g_up), _final_job(half_o, g_out)])
    (sib_i,) = _run_job(_sibling_job(dWin, g_in), "sibling_w_in")
    cs_i = chip_sum(0, dWin, sib_i)
    d_hn, (chips_i,) = _matmul(dproj, Win, "nt", F32, "d_hn", jobs=[_chips_job(cs_i, g_in)])
    half_i = final_sum(0, cs_i, chips_i)
    dx, dg_mix = _rmsnorm_bwd(x2, d_hn, norm_mix_g, dh1, "norm_mix_bwd", False)
    (other_i,) = _run_job(_final_job(half_i, g_in), "final_w_in")

    mine = [half_i, half_o, half_g, half_u, half_d]
    theirs = [other_i, other_o, other_g, other_u, other_d]
    big_out = [_adam([big[w]], [mine[w]], [theirs[w]], [big_m[w]], [big_v[w]], geo[w], "adam_" + names[w])[0][0]
               for w in range(5)]
    for w in (2, 3):
        big_out[w] = [tr_(a) for a in big_out[w]]

    tot = _small_all_reduce([dg_mix, dg_heads, dg_ffn, dg_final], loss_part, D)
    loss = tot[4, 0]
    small_w = [norm_mix_g, norm_out_dil_g, norm_out_sb_g, norm_ffn_g, norm_final_g.reshape(1, D)]
    small_m = [m_norm_mix_g, m_norm_out_dil_g, m_norm_out_sb_g, m_norm_ffn_g, m_norm_final_g.reshape(1, D)]
    small_v = [v_norm_mix_g, v_norm_out_dil_g, v_norm_out_sb_g, v_norm_ffn_g, v_norm_final_g.reshape(1, D)]
    so = _small_adam(tot, small_w, small_m, small_v, [(0, 0), (1, 0), (1, W), (2, 0), (3, 0)])
    small_out = [so[4 * i:4 * i + 4] for i in range(5)]
    small_out[4] = [a.reshape(D) for a in small_out[4]]

    per_weight = [small_out[0], big_out[0], small_out[1], small_out[2], big_out[1], small_out[3],
                  big_out[2], big_out[3], big_out[4], small_out[4]]

    def field(i):
        res = []
        for n_, o in enumerate(per_weight):
            a = o[i]
            res.append(a[None] if n_ in (1, 4, 6, 7, 8) else a)
        return res

    return (loss, dx.reshape(B, T, D), *field(0), *field(1), *field(2), *field(3))
```

```python
import functools
import math

import jax
import jax.numpy as jnp
from jax import lax
from jax.experimental import pallas as pl
from jax.experimental.pallas import tpu as pltpu

F32 = jnp.float32
BF16 = jnp.bfloat16
MESH = pl.DeviceIdType.MESH
ANY = pl.BlockSpec(memory_space=pl.ANY)

LANES = 128
HEAD_DIM = 128
DIL_STEPS = 128
DILATIONS = (1, 4, 16)
TQ = 512
TK_DIL = 512
TK_SB = 256
NEG = -1e30
_PART = TQ
_ROW_PARTS = tuple(slice(p, p + _PART) for p in range(0, TQ, _PART))
ROPE_DIM = 32
ROPE_THETA = 500000.0
RMS_EPS = 1e-5
ADAM_LR, ADAM_B1, ADAM_B2, ADAM_EPS, ADAM_WD, ADAM_STEP = 0.001, 0.9, 0.999, 1e-08, 0.01, 10
N_CHIPS = 4
VMEM_LIMIT = 56 * 1024 * 1024
MATMUL_VMEM = 52 * 1024 * 1024

NN = ((1,), (0,))
NT = ((1,), (1,))
TN = ((0,), (0,))


def _dot(a, b, dims):
    return lax.dot_general(a, b, (dims, ((), ())), preferred_element_type=F32)


def _tile(dim, pref, unit=LANES):
    if dim <= pref:
        return dim
    t = (pref // unit) * unit
    while t > unit and dim % t:
        t -= unit
    assert dim % t == 0, (dim, pref, unit)
    return t


def _params(sem=None):
    return pltpu.CompilerParams(dimension_semantics=sem, vmem_limit_bytes=VMEM_LIMIT)


class _Job:
    def __init__(self, inputs, outputs, aliases, nsem, copies):
        self.inputs, self.outputs, self.aliases, self.nsem, self.copies = inputs, outputs, aliases, nsem, copies

    def start(self, ins, outs, send, recv):
        for cp in self.copies(ins, outs, send, recv)[0]:
            cp.start()

    def finish(self, ins, outs, send, recv):
        started, landing = self.copies(ins, outs, send, recv)
        for make in landing:
            make().wait_recv()
        for cp in started:
            cp.wait_send()


def _hosted_call(body, name, grid, in_specs, out_specs, out_shape, scratch_shapes, args, jobs, semantics):
    nbi, nbo, nbs = len(in_specs), len(out_specs), len(scratch_shapes)
    if not jobs:
        res = pl.pallas_call(
            body, name=name, grid=grid, in_specs=list(in_specs), out_specs=list(out_specs), out_shape=list(out_shape),
            scratch_shapes=list(scratch_shapes), compiler_params=_params(semantics))(*args)
        return list(res), []
    j_in = [a for jb in jobs for a in jb.inputs]
    j_out = [o for jb in jobs for o in jb.outputs]
    aliases, ii, oo, sems = {}, nbi, nbo, []
    for jb in jobs:
        for a, b in jb.aliases.items():
            aliases[ii + a] = oo + b
        ii += len(jb.inputs)
        oo += len(jb.outputs)
        sems += [pltpu.SemaphoreType.DMA((jb.nsem,)), pltpu.SemaphoreType.DMA((jb.nsem,))]

    def wrapped(*refs):
        p = nbi + len(j_in)
        b_in, ji = refs[:nbi], refs[nbi:p]
        b_out, jo = refs[p:p + nbo], refs[p + nbo:p + nbo + len(j_out)]
        p += nbo + len(j_out)
        b_scr, js = refs[p:p + nbs], refs[p + nbs:]
        pids = [pl.program_id(k) for k in range(len(grid))]
        first = functools.reduce(jnp.logical_and, [pid == 0 for pid in pids])
        last = functools.reduce(jnp.logical_and, [pid == g - 1 for pid, g in zip(pids, grid)])

        def each(what):
            a = b = 0
            for n, jb in enumerate(jobs):
                getattr(jb, what)(ji[a:a + len(jb.inputs)], jo[b:b + len(jb.outputs)], js[2 * n], js[2 * n + 1])
                a += len(jb.inputs)
                b += len(jb.outputs)

        @pl.when(first)
        def _():
            each("start")

        body(*b_in, *b_out, *b_scr)

        @pl.when(last)
        def _():
            each("finish")

    res = pl.pallas_call(
        wrapped, name=name, grid=grid,
        in_specs=list(in_specs) + [ANY] * len(j_in), out_specs=list(out_specs) + [ANY] * len(j_out),
        out_shape=list(out_shape) + j_out, input_output_aliases=aliases,
        scratch_shapes=list(scratch_shapes) + sems,
        compiler_params=_params(("arbitrary",) * len(grid)))(*args, *j_in)
    return list(res[:nbo]), list(res[nbo:])


def _rope(a, c, s1, s2):
    half = ROPE_DIM // 2
    return a * c + pltpu.roll(a, HEAD_DIM - half, 1) * s1 + pltpu.roll(a, half, 1) * s2


def _matmul(a, b, mode, out_dtype, name, res=None, rope=None, swiglu=None, jobs=(), tm=1024, tn=1024, tk=2816):
    if mode == "nn":
        (M, K), (_, N) = a.shape, b.shape
    elif mode == "nt":
        (M, K), (N, _) = a.shape, b.shape
    else:
        (K, M), (_, N) = a.shape, b.shape
    tm = _tile(M, tm)
    tn = _tile(N, tn) if rope is None else _tile(math.gcd(N, rope[1]), tn)
    n_out = 2 if swiglu is not None else 1

    def vmem_bytes(t):
        osz = jnp.dtype(out_dtype).itemsize
        return (4 * t * (tm + tn) + 4 * tm * tn + (4 * tm * tn if t < K else 0) + 2 * n_out * tm * tn * osz
                + (8 * tm * tn if res is not None else 0) + (24 * tm * tn if swiglu is not None else 0)
                + (4 * tm * tn + 24 * tm * HEAD_DIM if rope is not None else 0))

    tk = next(t for t in (K, _tile(K, tk), _tile(K, 2048), _tile(K, 1024)) if vmem_bytes(t) <= MATMUL_VMEM)
    nk = K // tk
    dims = {"nn": NN, "nt": NT, "tn": TN}[mode]
    a_spec = (pl.BlockSpec((tk, tm), lambda i, j, k: (k, i)) if mode == "tn"
              else pl.BlockSpec((tm, tk), lambda i, j, k: (i, k)))
    b_spec = (pl.BlockSpec((tn, tk), lambda i, j, k: (j, k)) if mode == "nt"
              else pl.BlockSpec((tk, tn), lambda i, j, k: (k, j)))
    o_spec = pl.BlockSpec((tm, tn), lambda i, j, k: (i, j))
    n_extra = (1 if res is not None else 0) + (3 if rope is not None else 0) + (2 if swiglu is not None else 0)
    use_acc = nk > 1 or rope is not None
    if rope is not None:
        assert rope[1] % tn == 0

    def body(*refs):
        a_ref, b_ref = refs[:2]
        extra = refs[2:2 + n_extra]
        o_ref = refs[2 + n_extra]
        k = pl.program_id(2)
        prod = _dot(a_ref[...], b_ref[...], dims)
        if use_acc:
            acc_ref = refs[-1]
            if nk > 1:
                @pl.when(k == 0)
                def _():
                    acc_ref[...] = jnp.zeros(acc_ref.shape, F32)

                acc_ref[...] += prod
            else:
                acc_ref[...] = prod
            total = lambda: acc_ref[...]
        else:
            total = lambda: prod
        at_end = pl.when(k == nk - 1) if nk > 1 else (lambda f: f())

        if swiglu is not None:
            @at_end
            def _():
                d = total()
                g = extra[0][...].astype(F32)
                u = extra[1][...].astype(F32)
                sig = jax.nn.sigmoid(g)
                o_ref[...] = (d * u * sig * (1.0 + g * (1.0 - sig))).astype(out_dtype)
                refs[3 + n_extra][...] = (d * g * sig).astype(out_dtype)
        elif rope is None:
            @at_end
            def _():
                out = total()
                if res is not None:
                    out = out + extra[0][...]
                o_ref[...] = out.astype(out_dtype)
        else:
            roped = pl.program_id(1) * tn < rope[1]

            @pl.when(jnp.logical_and(k == nk - 1, roped))
            def _():
                c, s1, s2 = extra[0][...], extra[1][...], extra[2][...]
                for h in range(tn // HEAD_DIM):
                    cols = slice(h * HEAD_DIM, (h + 1) * HEAD_DIM)
                    o_ref[:, cols] = _rope(acc_ref[:, cols], c, s1, s2).astype(out_dtype)

            @pl.when(jnp.logical_and(k == nk - 1, jnp.logical_not(roped)))
            def _():
                o_ref[...] = acc_ref[...].astype(out_dtype)

    in_specs, args = [a_spec, b_spec], [a, b]
    if res is not None:
        in_specs.append(o_spec)
        args.append(res)
    if rope is not None:
        in_specs += [pl.BlockSpec((tm, HEAD_DIM), lambda i, j, k: (i, 0))] * 3
        args += list(rope[0])
    if swiglu is not None:
        in_specs += [o_spec, o_spec]
        args += list(swiglu)
    outs, extra_out = _hosted_call(
        body, name, (M // tm, N // tn, nk), in_specs, [o_spec] * n_out,
        [jax.ShapeDtypeStruct((M, N), out_dtype)] * n_out,
        [pltpu.VMEM((tm, tn), F32)] if use_acc else [], args, jobs, ("parallel", "parallel", "arbitrary"))
    out = outs[0] if n_out == 1 else tuple(outs)
    return (out, extra_out) if jobs else out


def _gate_up(hn, wg, wu, name, jobs=()):
    M, K = hn.shape
    N = wg.shape[0]
    tm, tn, tk = _tile(M, 1024), _tile(N, 512), _tile(K, 4096)
    nk = K // tk

    def body(a_ref, g_ref, u_ref, og_ref, ou_ref, oa_ref, *accs):
        k = pl.program_id(2)
        a = a_ref[...]
        pg = _dot(a, g_ref[...], NT)
        pu = _dot(a, u_ref[...], NT)

        def finish(g, u):
            og_ref[...] = g.astype(BF16)
            ou_ref[...] = u.astype(BF16)
            oa_ref[...] = (g * jax.nn.sigmoid(g) * u).astype(BF16)

        if nk == 1:
            finish(pg, pu)
        else:
            accg, accu = accs

            @pl.when(k == 0)
            def _():
                accg[...] = jnp.zeros(accg.shape, F32)
                accu[...] = jnp.zeros(accu.shape, F32)

            accg[...] += pg
            accu[...] += pu

            @pl.when(k == nk - 1)
            def _():
                finish(accg[...], accu[...])

    w_spec = pl.BlockSpec((tn, tk), lambda i, j, k: (j, k))
    o_spec = pl.BlockSpec((tm, tn), lambda i, j, k: (i, j))
    sds = jax.ShapeDtypeStruct((M, N), BF16)
    return _hosted_call(
        body, name, (M // tm, N // tn, nk), [pl.BlockSpec((tm, tk), lambda i, j, k: (i, k)), w_spec, w_spec],
        [o_spec, o_spec, o_spec], [sds, sds, sds],
        [pltpu.VMEM((tm, tn), F32), pltpu.VMEM((tm, tn), F32)] if nk > 1 else [],
        (hn, wg, wu), jobs, ("parallel", "parallel", "arbitrary"))


def _rmsnorm_fwd(x, g, name):
    M, D = x.shape
    tm = _tile(M, 256, 16)

    def body(x_ref, g_ref, o_ref):
        xv = x_ref[...]
        r = lax.rsqrt(jnp.mean(xv * xv, axis=-1, keepdims=True) + RMS_EPS)
        o_ref[...] = (xv * r * g_ref[...]).astype(BF16)

    return pl.pallas_call(
        body, name=name, grid=(M // tm,),
        in_specs=[pl.BlockSpec((tm, D), lambda i: (i, 0)), pl.BlockSpec((1, D), lambda i: (0, 0))],
        out_specs=pl.BlockSpec((tm, D), lambda i: (i, 0)),
        out_shape=jax.ShapeDtypeStruct((M, D), BF16),
        compiler_params=_params(("parallel",)),
    )(x, g)


def _fold8(v):
    tm, D = v.shape
    return jnp.sum(v.reshape(tm // 8, 8, D), axis=0)


def _rmsnorm_bwd(x, dy, g, res, name, want_bf16):
    M, D = x.shape
    tm = _tile(M, 128, 16)

    def body(x_ref, dy_ref, g_ref, r_ref, *outs):
        dx_ref, dg_ref = outs[0], outs[-1]
        xv = x_ref[...]
        dyv = dy_ref[...].astype(F32)
        r = lax.rsqrt(jnp.mean(xv * xv, axis=-1, keepdims=True) + RMS_EPS)
        u = dyv * g_ref[...]
        dot = jnp.sum(xv * u, axis=-1, keepdims=True)
        dx = r * u - xv * (r * r * r * (1.0 / D)) * dot + r_ref[...]
        dx_ref[...] = dx
        if want_bf16:
            outs[1][...] = dx.astype(BF16)
        part = _fold8(dyv * xv * r)

        @pl.when(pl.program_id(0) == 0)
        def _():
            dg_ref[...] = part

        @pl.when(pl.program_id(0) > 0)
        def _():
            dg_ref[...] += part

    row = pl.BlockSpec((tm, D), lambda i: (i, 0))
    out_specs = [row] + ([row] if want_bf16 else []) + [pl.BlockSpec((8, D), lambda i: (0, 0))]
    out_shape = ([jax.ShapeDtypeStruct((M, D), F32)] + ([jax.ShapeDtypeStruct((M, D), BF16)] if want_bf16 else [])
                 + [jax.ShapeDtypeStruct((8, D), F32)])
    return pl.pallas_call(
        body, name=name, grid=(M // tm,),
        in_specs=[row, row, pl.BlockSpec((1, D), lambda i: (0, 0)), row],
        out_specs=out_specs, out_shape=out_shape,
        compiler_params=_params(("arbitrary",)),
    )(x, dy, g, res)


def _loss_head(h, tgt, g, name):
    M, D = h.shape
    tm = _tile(M, 128, 16)

    def body(h_ref, t_ref, g_ref, dh_ref, dhb_ref, l_ref, dg_ref):
        hv = h_ref[...]
        gv = g_ref[...]
        r = lax.rsqrt(jnp.mean(hv * hv, axis=-1, keepdims=True) + RMS_EPS)
        n = hv * r
        e = n * gv - t_ref[...]
        dy = e * (1.0 / D)
        u = dy * gv
        dot = jnp.sum(hv * u, axis=-1, keepdims=True)
        dh = r * u - hv * (r * r * r * (1.0 / D)) * dot
        dh_ref[...] = dh
        dhb_ref[...] = dh.astype(BF16)
        rows = jnp.sum(e * e, axis=-1, keepdims=True)
        lpart = jnp.broadcast_to(jnp.sum(rows, axis=0, keepdims=True) * (0.5 / D), (8, LANES))
        gpart = _fold8(dy * n)

        @pl.when(pl.program_id(0) == 0)
        def _():
            l_ref[...] = lpart
            dg_ref[...] = gpart

        @pl.when(pl.program_id(0) > 0)
        def _():
            l_ref[...] += lpart
            dg_ref[...] += gpart

    row = pl.BlockSpec((tm, D), lambda i: (i, 0))
    return pl.pallas_call(
        body, name=name, grid=(M // tm,),
        in_specs=[row, row, pl.BlockSpec((1, D), lambda i: (0, 0))],
        out_specs=[row, row, pl.BlockSpec((8, LANES), lambda i: (0, 0)), pl.BlockSpec((8, D), lambda i: (0, 0))],
        out_shape=[jax.ShapeDtypeStruct((M, D), F32), jax.ShapeDtypeStruct((M, D), BF16),
                   jax.ShapeDtypeStruct((8, LANES), F32), jax.ShapeDtypeStruct((8, D), F32)],
        compiler_params=_params(("arbitrary",)),
    )(h, tgt, g)


def _headnorm_fwd(o_dil, o_sb, g, name):
    M, W = o_dil.shape
    D = 2 * W
    tm = _tile(M, 256, 16)

    def body(a_ref, b_ref, g_ref, o_ref):
        for h in range(D // HEAD_DIM):
            src = a_ref if h < W // HEAD_DIM else b_ref
            lo = (h * HEAD_DIM) % W
            v = src[:, lo:lo + HEAD_DIM]
            r = lax.rsqrt(jnp.mean(v * v, axis=-1, keepdims=True) + RMS_EPS)
            o_ref[:, h * HEAD_DIM:(h + 1) * HEAD_DIM] = (v * r * g_ref[:, h * HEAD_DIM:(h + 1) * HEAD_DIM]).astype(BF16)

    half = pl.BlockSpec((tm, W), lambda i: (i, 0))
    return pl.pallas_call(
        body, name=name, grid=(M // tm,),
        in_specs=[half, half, pl.BlockSpec((1, D), lambda i: (0, 0))],
        out_specs=pl.BlockSpec((tm, D), lambda i: (i, 0)),
        out_shape=jax.ShapeDtypeStruct((M, D), BF16),
        compiler_params=_params(("parallel",)),
    )(o_dil, o_sb, g)


def _headnorm_bwd(d_mix, o_dil, o_sb, g, name):
    M, W = o_dil.shape
    D = 2 * W
    tm = _tile(M, 128, 16)

    def body(d_ref, a_ref, b_ref, g_ref, do_ref, dg_ref):
        @pl.when(pl.program_id(0) == 0)
        def _():
            dg_ref[...] = jnp.zeros(dg_ref.shape, F32)

        for h in range(D // HEAD_DIM):
            src = a_ref if h < W // HEAD_DIM else b_ref
            lo = (h * HEAD_DIM) % W
            cols = slice(h * HEAD_DIM, (h + 1) * HEAD_DIM)
            v = src[:, lo:lo + HEAD_DIM]
            dy = d_ref[:, cols].astype(F32)
            r = lax.rsqrt(jnp.mean(v * v, axis=-1, keepdims=True) + RMS_EPS)
            u = dy * g_ref[:, cols]
            dot = jnp.sum(v * u, axis=-1, keepdims=True)
            do_ref[:, cols] = r * u - v * (r * r * r * (1.0 / HEAD_DIM)) * dot
            dg_ref[:, cols] += _fold8(dy * v * r)

    half = pl.BlockSpec((tm, W), lambda i: (i, 0))
    row = pl.BlockSpec((tm, D), lambda i: (i, 0))
    return pl.pallas_call(
        body, name=name, grid=(M // tm,),
        in_specs=[row, half, half, pl.BlockSpec((1, D), lambda i: (0, 0))],
        out_specs=[row, pl.BlockSpec((8, D), lambda i: (0, 0))],
        out_shape=[jax.ShapeDtypeStruct((M, D), F32), jax.ShapeDtypeStruct((8, D), F32)],
        compiler_params=_params(("arbitrary",)),
    )(d_mix, o_dil, o_sb, g)


def _rope_tables(positions):
    half = ROPE_DIM // 2
    inv_freq = jnp.power(jnp.float32(ROPE_THETA), -jnp.arange(half, dtype=F32) / half)
    ang = positions.astype(F32)[..., None] * inv_freq
    cos, sin = jnp.cos(ang), jnp.sin(ang)
    rest = HEAD_DIM - ROPE_DIM
    one = jnp.ones(cos.shape[:-1] + (rest,), F32)
    z16 = jnp.zeros_like(sin)
    zr = jnp.zeros_like(one)
    c = jnp.concatenate([cos, cos, one], axis=-1)
    s1 = jnp.concatenate([-sin, z16, zr], axis=-1)
    s2 = jnp.concatenate([z16, sin, zr], axis=-1)
    return c, s1, s2


def _dil_bias(T):
    ne = T // TQ
    e = jnp.arange(ne, dtype=jnp.int32)[:, None, None]
    r = jnp.arange(TQ, dtype=jnp.int32)[None, :, None]
    c = jnp.arange(TK_DIL, dtype=jnp.int32)[None, None, :]
    dist = e * TQ + r - c
    mult = jnp.zeros(dist.shape, F32)
    for d in DILATIONS:
        mult = mult + jnp.where((dist % d == 0) & (dist <= DIL_STEPS * d), 1.0, 0.0)
    return jnp.where((dist >= 0) & (mult > 0), jnp.log(jnp.maximum(mult, 1.0)), NEG)


def _dil_fwd(proj3, bias, Hd, name, jobs=()):
    B, T, N3 = proj3.shape
    nq = T // TQ
    per = TK_DIL // TQ
    scale = HEAD_DIM ** -0.5

    def body(q_ref, k_ref, v_ref, b_ref, o_ref, l_ref):
        i = pl.program_id(2)
        qs = [q_ref[r, :] for r in _ROW_PARTS]
        last = i // per

        def step(t, carry):
            j = last - t
            off = pl.multiple_of(j * TK_DIL, TK_DIL)
            kj = k_ref[pl.ds(off, TK_DIL), :]
            vj = v_ref[pl.ds(off, TK_DIL), :]
            new = []
            for n, r in enumerate(_ROW_PARTS):
                m, l, acc = carry[n]
                s = _dot(qs[n], kj, NT) * scale + b_ref[i - per * j, r, :]
                m_new = jnp.maximum(m, jnp.max(s, axis=-1, keepdims=True))
                p = jnp.exp(s - m_new)
                corr = jnp.exp(m - m_new)
                new.append((m_new, l * corr + jnp.sum(p, axis=-1, keepdims=True),
                            acc * corr + _dot(p.astype(BF16), vj, NN)))
            return tuple(new)

        init = (jnp.full((_PART, 1), NEG, F32), jnp.zeros((_PART, 1), F32), jnp.zeros((_PART, HEAD_DIM), F32))
        done = lax.fori_loop(0, last + 1, step, (init,) * len(_ROW_PARTS))
        for n, r in enumerate(_ROW_PARTS):
            m, l, acc = done[n]
            o_ref[r, :] = acc / l
            l_ref[r, :] = jnp.broadcast_to(m + jnp.log(l), (_PART, HEAD_DIM))

    full = lambda base: pl.BlockSpec((None, T, HEAD_DIM), lambda b, h, i: (b, 0, base + h))
    blk = lambda base: pl.BlockSpec((None, TQ, HEAD_DIM), lambda b, h, i: (b, i, base + h))
    sds = jax.ShapeDtypeStruct((B, T, Hd * HEAD_DIM), F32)
    return _hosted_call(
        body, name, (B, Hd, nq),
        [blk(0), full(Hd), full(2 * Hd), pl.BlockSpec(bias.shape, lambda b, h, i: (0, 0, 0))],
        [blk(0), blk(0)], [sds, sds], [], (proj3, proj3, proj3, bias), jobs, ("parallel", "parallel", "arbitrary"))


def _dil_bwd(proj3, bias, tabs, d_o3, o3, lse3, Hd, name, jobs=()):
    B, T, N3 = proj3.shape
    nq = T // TQ
    per = TK_DIL // TQ
    scale = HEAD_DIM ** -0.5

    def body(q_ref, k_ref, v_ref, b_ref, do_ref, o_ref, l_ref, cq, s1q, s2q, ct, s1t, s2t,
             dq_ref, dk_ref, dv_ref, dks, dvs):
        i = pl.program_id(2)

        @pl.when(i == 0)
        def _():
            dks[...] = jnp.zeros(dks.shape, F32)
            dvs[...] = jnp.zeros(dvs.shape, F32)

        q = q_ref[...]
        do = do_ref[...]
        dob = do.astype(BF16)
        dterm = jnp.sum(do * o_ref[...], axis=-1, keepdims=True)
        lse = jnp.concatenate([l_ref[...]] * (TK_DIL // HEAD_DIM), axis=1)
        last = i // per

        def step(j, dqs):
            off = pl.multiple_of(j * TK_DIL, TK_DIL)
            kj = k_ref[pl.ds(off, TK_DIL), :]
            vj = v_ref[pl.ds(off, TK_DIL), :]
            ps, dss, new = [], [], []
            for n, r in enumerate(_ROW_PARTS):
                s = _dot(q[r], kj, NT) * scale + b_ref[i - per * j, r, :]
                p = jnp.exp(s - lse[r])
                dp = _dot(dob[r], vj, NT)
                ds = (p * (dp - dterm[r]) * scale).astype(BF16)
                new.append(dqs[n] + _dot(ds, kj, NN))
                ps.append(p.astype(BF16))
                dss.append(ds)
            dks[pl.ds(off, TK_DIL), :] += _dot(jnp.concatenate(dss, axis=0), q, TN)
            dvs[pl.ds(off, TK_DIL), :] += _dot(jnp.concatenate(ps, axis=0), dob, TN)
            return tuple(new)

        zero = jnp.zeros((_PART, HEAD_DIM), F32)
        dq = jnp.concatenate(lax.fori_loop(0, last + 1, step, (zero,) * len(_ROW_PARTS)), axis=0)
        dq_ref[...] = _rope(dq, cq[...], -s1q[...], -s2q[...]).astype(BF16)

        @pl.when(i == nq - 1)
        def _():
            dk_ref[...] = _rope(dks[...], ct[...], -s1t[...], -s2t[...]).astype(BF16)
            dv_ref[...] = dvs[...].astype(BF16)

    full = lambda base: pl.BlockSpec((None, T, HEAD_DIM), lambda b, h, i: (b, 0, base + h))
    blk = lambda base: pl.BlockSpec((None, TQ, HEAD_DIM), lambda b, h, i: (b, i, base + h))
    tab_q = pl.BlockSpec((None, TQ, HEAD_DIM), lambda b, h, i: (b, i, 0))
    tab_t = pl.BlockSpec((None, T, HEAD_DIM), lambda b, h, i: (b, 0, 0))
    sds = jax.ShapeDtypeStruct((B, T, Hd * HEAD_DIM), BF16)
    return _hosted_call(
        body, name, (B, Hd, nq),
        [blk(0), full(Hd), full(2 * Hd), pl.BlockSpec(bias.shape, lambda b, h, i: (0, 0, 0)),
         blk(0), blk(0), blk(0), tab_q, tab_q, tab_q, tab_t, tab_t, tab_t],
        [blk(0), full(0), full(0)], [sds, sds, sds],
        [pltpu.VMEM((T, HEAD_DIM), F32), pltpu.VMEM((T, HEAD_DIM), F32)],
        (proj3, proj3, proj3, bias, d_o3, o3, lse3, *tabs, *tabs), jobs, ("parallel", "parallel", "arbitrary"))


def _softplus(z):
    return jnp.maximum(z, 0.0) + jnp.log(1.0 + jnp.exp(-jnp.abs(z)))


def _tri_sum(x, m01):
    n = x.shape[0]
    hi = x.astype(BF16)
    lo = (x - hi.astype(F32)).astype(BF16)
    both = _dot(jnp.concatenate([hi, lo], axis=0), m01, NN)
    return both[:n] + both[n:]


def _sb_iota(rows=TQ, first=0):
    row = lax.broadcasted_iota(jnp.int32, (rows, TK_SB), 0) + first
    col = lax.broadcasted_iota(jnp.int32, (rows, TK_SB), 1)
    return row, col


def _sb_fwd(proj3, Hd, Hs, name, jobs=()):
    B, T, N3 = proj3.shape
    nq = T // TQ
    qb, kb_, vb_ = 3 * Hd, 3 * Hd + Hs, 3 * Hd + 2 * Hs
    scale = HEAD_DIM ** -0.5

    assert TQ == 2 * TK_SB
    low = slice(TK_SB, TQ)

    def body(q_ref, k_ref, v_ref, o_ref, lt_ref):
        i = pl.program_id(2)
        q = q_ref[...]
        row, col = _sb_iota(TK_SB)
        after = jnp.where(row > col, 1.0, 0.0).astype(BF16)

        def chunk(j, qr, rows, first_row, acc, cs):
            off = pl.multiple_of(j * TK_SB, TK_SB)
            kj = k_ref[pl.ds(off, TK_SB), :]
            vj = v_ref[pl.ds(off, TK_SB), :]
            z = _dot(qr, kj, NT) * scale
            sp = _softplus(z)
            ln = -sp
            if first_row is not None:
                r_i, c_i = _sb_iota(rows, first_row)
                msk = c_i < r_i + (i * TQ - j * TK_SB)
                ln = jnp.where(msk, ln, 0.0)
            a = jnp.exp(z - sp + _tri_sum(ln, after) + cs)
            if first_row is not None:
                a = jnp.where(msk, a, 0.0)
            return acc + _dot(a.astype(BF16), vj, NN), cs + jnp.sum(ln, axis=-1, keepdims=True)

        acc_l, cs_l = chunk(2 * i + 1, q[low], TK_SB, TK_SB, jnp.zeros((TK_SB, HEAD_DIM), F32),
                            jnp.zeros((TK_SB, 1), F32))
        carry = chunk(2 * i, q, TQ, 0, jnp.concatenate([jnp.zeros((TK_SB, HEAD_DIM), F32), acc_l], axis=0),
                      jnp.concatenate([jnp.zeros((TK_SB, 1), F32), cs_l], axis=0))
        acc, cs = lax.fori_loop(0, 2 * i, lambda t, carry: chunk(2 * i - 1 - t, q, TQ, None, *carry), carry)
        o_ref[...] = acc
        lt_ref[...] = jnp.broadcast_to(cs, (TQ, HEAD_DIM))

    full = lambda base: pl.BlockSpec((None, T, HEAD_DIM), lambda b, h, i: (b, 0, base + h))
    blk = lambda base: pl.BlockSpec((None, TQ, HEAD_DIM), lambda b, h, i: (b, i, base + h))
    sds = jax.ShapeDtypeStruct((B, T, Hs * HEAD_DIM), F32)
    return _hosted_call(
        body, name, (B, Hs, nq), [blk(qb), full(kb_), full(vb_)], [blk(0), blk(0)], [sds, sds], [],
        (proj3, proj3, proj3), jobs, ("parallel", "parallel", "arbitrary"))


def _sb_bwd(proj3, d_o3, lt3, Hd, Hs, name, jobs=()):
    B, T, N3 = proj3.shape
    nq = T // TQ
    qb, kb_, vb_ = 3 * Hd, 3 * Hd + Hs, 3 * Hd + 2 * Hs
    scale = HEAD_DIM ** -0.5

    assert TQ == 2 * TK_SB
    low = slice(TK_SB, TQ)

    def body(q_ref, k_ref, v_ref, do_ref, lt_ref, dq_ref, dk_ref, dv_ref, dks, dvs):
        i = pl.program_id(2)

        @pl.when(i == 0)
        def _():
            dks[...] = jnp.zeros(dks.shape, F32)
            dvs[...] = jnp.zeros(dvs.shape, F32)

        q = q_ref[...]
        dob = do_ref[...].astype(BF16)
        total = jnp.concatenate([lt_ref[...]] * (TK_SB // HEAD_DIM), axis=1)
        row, col = _sb_iota(TK_SB)
        before = jnp.where(row < col, 1.0, 0.0).astype(BF16)

        def chunk(j, qr, dor, tot, rows, first_row, dq, pc, gc):
            off = pl.multiple_of(j * TK_SB, TK_SB)
            kj = k_ref[pl.ds(off, TK_SB), :]
            vj = v_ref[pl.ds(off, TK_SB), :]
            z = _dot(qr, kj, NT) * scale
            sp = _softplus(z)
            ln = -sp
            if first_row is not None:
                r_i, c_i = _sb_iota(rows, first_row)
                msk = c_i < r_i + (i * TQ - j * TK_SB)
                ln = jnp.where(msk, ln, 0.0)
            excl = tot - (_tri_sum(ln, before) + ln + pc)
            a = jnp.exp(z - sp + excl)
            if first_row is not None:
                a = jnp.where(msk, a, 0.0)
            g = a * _dot(dor, vj, NT)
            big = _dot(g.astype(BF16), before, NN) + gc
            dz = g - (g + big) * jnp.exp(z - sp)
            if first_row is not None:
                dz = jnp.where(msk, dz, 0.0)
            dz = dz.astype(BF16)
            dks[pl.ds(off, TK_SB), :] += _dot(dz, qr, TN)
            dvs[pl.ds(off, TK_SB), :] += _dot(a.astype(BF16), dor, TN)
            return (dq + _dot(dz, kj, NN), pc + jnp.sum(ln, axis=-1, keepdims=True),
                    gc + jnp.sum(g, axis=-1, keepdims=True))

        zero1 = jnp.zeros((TQ, 1), F32)
        carry = lax.fori_loop(0, 2 * i, lambda j, carry: chunk(j, q, dob, total, TQ, None, *carry),
                              (jnp.zeros((TQ, HEAD_DIM), F32), zero1, zero1))
        dq, pc, gc = chunk(2 * i, q, dob, total, TQ, 0, *carry)
        dq_l, _, _ = chunk(2 * i + 1, q[low], dob[low], total[low], TK_SB, TK_SB, dq[low], pc[low], gc[low])
        dq_ref[...] = (jnp.concatenate([dq[:TK_SB], dq_l], axis=0) * scale).astype(BF16)

        @pl.when(i == nq - 1)
        def _():
            dk_ref[...] = (dks[...] * scale).astype(BF16)
            dv_ref[...] = dvs[...].astype(BF16)

    full = lambda base: pl.BlockSpec((None, T, HEAD_DIM), lambda b, h, i: (b, 0, base + h))
    blk = lambda base: pl.BlockSpec((None, TQ, HEAD_DIM), lambda b, h, i: (b, i, base + h))
    sds = jax.ShapeDtypeStruct((B, T, Hs * HEAD_DIM), BF16)
    return _hosted_call(
        body, name, (B, Hs, nq), [blk(qb), full(kb_), full(vb_), blk(Hd), blk(0)],
        [blk(0), full(0), full(0)], [sds, sds, sds],
        [pltpu.VMEM((T, HEAD_DIM), F32), pltpu.VMEM((T, HEAD_DIM), F32)],
        (proj3, proj3, proj3, d_o3, lt3), jobs, ("parallel", "parallel", "arbitrary"))


def _place():
    x, y, c = lax.axis_index("x"), lax.axis_index("y"), lax.axis_index("c")
    chips = [(1 - x, y), (x, 1 - y), (1 - x, 1 - y)]
    return x, y, c, chips


class _Weight:
    def __init__(self, kind, shard_shape, slot):
        self.kind, self.slot = kind, slot
        self.R, self.C = shard_shape
        if kind == "col":
            assert slot == self.C
            self.full = (self.R, N_CHIPS * slot)
            self.half = (self.R // 2, N_CHIPS * slot)
            self.piece = (self.R // 2, slot)
        else:
            self.full = (N_CHIPS * slot, self.C)
            self.half = (N_CHIPS * slot, self.C // 2)
            self.piece = (self.R, self.C // 2)

    def _rows(self, jj, n):
        return pl.ds(pl.multiple_of(jj * self.slot, 16), n)

    def full_half(self, ref, jj, hc):
        if self.kind == "col":
            return ref.at[pl.ds(hc * (self.R // 2), self.R // 2), pl.ds(pl.multiple_of(jj * self.slot, LANES), self.C)]
        return ref.at[self._rows(jj, self.R), pl.ds(hc * (self.C // 2), self.C // 2)]

    def region_half(self, ref, hc):
        if self.kind == "col":
            return ref.at[pl.ds(hc * (self.R // 2), self.R // 2), :]
        return ref.at[:, pl.ds(hc * (self.C // 2), self.C // 2)]

    def half_piece(self, ref, jj):
        if self.kind == "col":
            return ref.at[:, pl.ds(pl.multiple_of(jj * self.slot, LANES), self.slot)]
        return ref.at[self._rows(jj, self.R), :]


def _cast_into_full(w, geo, j_arr, name):
    R, C = w.shape
    tr = _tile(R, 256, 16)
    if geo.kind == "col":
        o_spec = pl.BlockSpec((tr, C), lambda i, j_ref: (i, j_ref[0]))
    else:
        while geo.slot % tr or R % tr:
            tr -= 16
        spb = geo.slot // tr
        o_spec = pl.BlockSpec((tr, C), lambda i, j_ref: (j_ref[0] * spb + i, 0))

    def body(j_ref, w_ref, o_ref):
        o_ref[...] = w_ref[...].astype(BF16)

    return pl.pallas_call(
        body, name=name,
        grid_spec=pltpu.PrefetchScalarGridSpec(
            num_scalar_prefetch=1, grid=(R // tr,),
            in_specs=[pl.BlockSpec((tr, C), lambda i, j_ref: (i, 0))], out_specs=o_spec),
        out_shape=jax.ShapeDtypeStruct(geo.full, BF16),
        compiler_params=_params(("parallel",)),
    )(j_arr, w)


def _zero_pad(full, geo, name):
    pad = geo.slot - geo.R
    assert geo.kind == "row" and pad > 0 and geo.R % pad == 0 and geo.slot % pad == 0

    def body(f_ref, o_ref):
        o_ref[...] = jnp.zeros(o_ref.shape, BF16)

    return pl.pallas_call(
        body, name=name, grid=(N_CHIPS,), in_specs=[ANY],
        out_specs=pl.BlockSpec((pad, geo.C), lambda jj: ((jj * geo.slot + geo.R) // pad, 0)),
        out_shape=jax.ShapeDtypeStruct(geo.full, BF16), input_output_aliases={0: 0},
        compiler_params=_params(("arbitrary",)),
    )(full)


def _gather_job(full, geo, stage):
    def copies(ins, outs, send, recv):
        x, y, c, chips = _place()
        j = 2 * x + y

        def cp(src, dst, k, to):
            return pltpu.make_async_remote_copy(src_ref=src, dst_ref=dst, send_sem=send.at[k], recv_sem=recv.at[k],
                                                device_id=to, device_id_type=MESH)

        started, landing = [], []
        for k, (cx, cy) in enumerate(chips):
            jk = 2 * cx + cy
            if stage == "ici":
                started.append(cp(geo.full_half(ins[0], j, c), geo.full_half(outs[0], j, c), k, (cx, cy, c)))
                part = geo.full_half(outs[0], jk, c)
            else:
                started.append(cp(geo.full_half(ins[0], jk, c), geo.full_half(outs[0], jk, c), k, (x, y, 1 - c)))
                part = geo.full_half(outs[0], jk, 1 - c)
            landing.append(functools.partial(cp, part, part, k, (x, y, c)))
        return started, landing

    return _Job([full], [jax.ShapeDtypeStruct(geo.full, BF16)], {0: 0}, 3, copies)


def _sibling_job(grad_full, geo):
    def copies(ins, outs, send, recv):
        x, y, c, _ = _place()
        mk = lambda src, to: pltpu.make_async_remote_copy(
            src_ref=src, dst_ref=outs[0], send_sem=send.at[0], recv_sem=recv.at[0], device_id=to, device_id_type=MESH)
        return [mk(geo.region_half(ins[0], 1 - c), (x, y, 1 - c))], [functools.partial(mk, outs[0], (x, y, c))]

    return _Job([grad_full], [jax.ShapeDtypeStruct(geo.half, BF16)], {}, 1, copies)


def _chips_job(chip_sum, geo, part=(0, 1), so_far=None):
    p, n = part
    rows = geo.piece[0] // n
    assert geo.piece[0] % n == 0 and rows % 16 == 0

    def copies(ins, outs, send, recv):
        x, y, c, chips = _place()
        started, landing = [], []

        def mk(k, src, to):
            dst = outs[0].at[k, pl.ds(p * rows, rows), :]
            return pltpu.make_async_remote_copy(src_ref=dst if src is None else src, dst_ref=dst, send_sem=send.at[k],
                                                recv_sem=recv.at[k], device_id=to, device_id_type=MESH)

        for k, (cx, cy) in enumerate(chips):
            started.append(mk(k, geo.half_piece(ins[0], 2 * cx + cy).at[pl.ds(p * rows, rows), :], (cx, cy, c)))
            landing.append(functools.partial(mk, k, None, (x, y, c)))
        return started, landing

    inputs = [chip_sum] + ([] if so_far is None else [so_far])
    return _Job(inputs, [jax.ShapeDtypeStruct((3,) + geo.piece, BF16)], {} if so_far is None else {1: 0}, 3, copies)


def _final_job(half, geo):
    def copies(ins, outs, send, recv):
        x, y, c, _ = _place()
        mk = lambda src, to: pltpu.make_async_remote_copy(
            src_ref=src, dst_ref=outs[0], send_sem=send.at[0], recv_sem=recv.at[0], device_id=to, device_id_type=MESH)
        return [mk(ins[0], (x, y, 1 - c))], [functools.partial(mk, outs[0], (x, y, c))]

    return _Job([half], [jax.ShapeDtypeStruct(geo.piece, F32)], {}, 1, copies)


def _gather_two_step(full, geo, name):
    R = geo.R
    assert geo.kind == "col" and R % 64 == 0

    def body(own_ref, f_ref, send, recv):
        x, y, c, _ = _place()
        j, jx, jy, jd = 2 * x + y, 2 * (1 - x) + y, 2 * x + (1 - y), 2 * (1 - x) + (1 - y)
        me, xn, yn, sib = (x, y, c), (1 - x, y, c), (x, 1 - y, c), (x, y, 1 - c)

        def half(ref, jj, hc):
            return geo.full_half(ref, jj, hc)

        def quarter(ref, jj, hc, q):
            return ref.at[pl.ds(hc * (R // 2) + q * (R // 4), R // 4), pl.ds(pl.multiple_of(jj * geo.slot, LANES), geo.C)]

        def cp(k, src, dst, to):
            return pltpu.make_async_remote_copy(src_ref=src, dst_ref=dst, send_sem=send.at[k], recv_sem=recv.at[k],
                                                device_id=to, device_id_type=MESH)

        def arrived(k, part):
            cp(k, part, part, me).wait_recv()

        def pass_on(k, part, to):
            d = cp(k, part, part, to)
            d.start()
            return d

        sent = [cp(0, half(own_ref, j, c), half(f_ref, j, c), xn), cp(1, half(own_ref, j, c), half(f_ref, j, c), yn)]
        for d in sent:
            d.start()
        arrived(0, half(f_ref, jx, c))
        sent.append(pass_on(2, quarter(f_ref, jx, c, 0), yn))
        sent.append(pass_on(4, half(f_ref, jx, c), sib))
        arrived(1, half(f_ref, jy, c))
        sent.append(pass_on(3, quarter(f_ref, jy, c, 1), xn))
        sent.append(pass_on(5, half(f_ref, jy, c), sib))
        arrived(2, quarter(f_ref, jd, c, 0))
        sent.append(pass_on(6, quarter(f_ref, jd, c, 0), sib))
        arrived(3, quarter(f_ref, jd, c, 1))
        sent.append(pass_on(7, quarter(f_ref, jd, c, 1), sib))
        arrived(4, half(f_ref, jx, 1 - c))
        arrived(5, half(f_ref, jy, 1 - c))
        arrived(6, quarter(f_ref, jd, 1 - c, 0))
        arrived(7, quarter(f_ref, jd, 1 - c, 1))
        for d in sent:
            d.wait_send()

    return pl.pallas_call(
        body, name=name, in_specs=[ANY], out_specs=ANY, out_shape=jax.ShapeDtypeStruct(geo.full, BF16),
        input_output_aliases={0: 0},
        scratch_shapes=[pltpu.SemaphoreType.DMA((8,)), pltpu.SemaphoreType.DMA((8,))],
    )(full)


def _chip_sum(g_full, got, geo, c_arr, name):
    Rh, Ch = geo.half
    tr, tc = _tile(Rh, 256, 16), _tile(Ch, 2048)
    nrb, ncb = Rh // tr, Ch // tc

    def body(c_ref, a_ref, b_ref, o_ref):
        o_ref[...] = (a_ref[...].astype(F32) + b_ref[...].astype(F32)).astype(BF16)

    if geo.kind == "col":
        a_spec = pl.BlockSpec((tr, tc), lambda i, j, c_ref: (c_ref[0] * nrb + i, j))
    else:
        a_spec = pl.BlockSpec((tr, tc), lambda i, j, c_ref: (i, c_ref[0] * ncb + j))
    spec = pl.BlockSpec((tr, tc), lambda i, j, c_ref: (i, j))
    return pl.pallas_call(
        body, name=name,
        grid_spec=pltpu.PrefetchScalarGridSpec(num_scalar_prefetch=1, grid=(nrb, ncb),
                                               in_specs=[a_spec, spec], out_specs=spec),
        out_shape=jax.ShapeDtypeStruct(geo.half, BF16),
        compiler_params=_params(("parallel", "parallel")),
    )(c_arr, g_full, got)


def _final_sum(chip_sum, got, geo, j_arr, name):
    Rp, Cp = geo.piece
    if geo.kind == "col":
        tr = _tile(Rp, 128, 16)
        a_spec = pl.BlockSpec((tr, Cp), lambda i, j_ref: (i, j_ref[0]))
    else:
        tr = _tile(Rp, 128, 16)
        while geo.slot % tr:
            tr -= 16
        assert Rp % tr == 0 and geo.slot % tr == 0
        spb = geo.slot // tr
        a_spec = pl.BlockSpec((tr, Cp), lambda i, j_ref: (j_ref[0] * spb + i, 0))
    r_specs = [pl.BlockSpec((None, tr, Cp), functools.partial(lambda i, j_ref, k: (k, i, 0), k=k)) for k in range(3)]
    o_spec = pl.BlockSpec((tr, Cp), lambda i, j_ref: (i, 0))

    def body(j_ref, a_ref, r0, r1, r2, o_ref):
        o_ref[...] = ((a_ref[...].astype(F32) + r0[...].astype(F32)) + r1[...].astype(F32)) + r2[...].astype(F32)

    return pl.pallas_call(
        body, name=name,
        grid_spec=pltpu.PrefetchScalarGridSpec(num_scalar_prefetch=1, grid=(Rp // tr,),
                                               in_specs=[a_spec] + r_specs, out_specs=o_spec),
        out_shape=jax.ShapeDtypeStruct(geo.piece, F32),
        compiler_params=_params(("parallel",)),
    )(j_arr, chip_sum, got, got, got)


def _run_job(job, name):
    ni = len(job.inputs)

    def body(*refs):
        ins, outs, (send, recv) = refs[:ni], refs[ni:-2], refs[-2:]
        job.start(ins, outs, send, recv)
        job.finish(ins, outs, send, recv)

    return pl.pallas_call(
        body, name=name, in_specs=[ANY] * ni, out_specs=[ANY] * len(job.outputs), out_shape=list(job.outputs),
        input_output_aliases=dict(job.aliases),
        scratch_shapes=[pltpu.SemaphoreType.DMA((job.nsem,)), pltpu.SemaphoreType.DMA((job.nsem,))],
    )(*job.inputs)


def _adam_math(w, g, m, v):
    m = ADAM_B1 * m + (1.0 - ADAM_B1) * g
    v = ADAM_B2 * v + (1.0 - ADAM_B2) * (g * g)
    m_hat = m / (1.0 - ADAM_B1 ** ADAM_STEP)
    v_hat = v / (1.0 - ADAM_B2 ** ADAM_STEP)
    delta = -ADAM_LR * (m_hat / (jnp.sqrt(v_hat) + ADAM_EPS) + ADAM_WD * w)
    return delta, m, v


def _adam(ws, mines, theirs, ms, vs, geo, name, jobs=()):
    n = len(ws)
    R, C = ws[0].shape
    tr = _tile(R, 128 if n == 1 else 32, 8)
    nrb = R // tr
    col = geo.kind == "col"
    assert not col or nrb % 2 == 0

    def rows(k, halved):
        def index(w, i):
            r = jnp.where(w == k, i, jnp.where(w < k, 0, nrb - 1))
            return (r % (nrb // 2) if halved else r, 0)
        return index

    def body(*refs):
        ins, outs = refs[:5 * n], refs[5 * n:]
        wi, i = pl.program_id(0), pl.program_id(1)
        c = lax.axis_index("c")
        for k in range(n):
            @pl.when(wi == k)
            def _():
                w_ref, a_ref, b_ref, m_ref, v_ref = ins[5 * k:5 * k + 5]
                if col:
                    gv = jnp.where(i // (nrb // 2) == c, a_ref[...], b_ref[...])
                else:
                    a, b = a_ref[...], b_ref[...]
                    gv = jnp.concatenate([jnp.where(c == 0, a, b), jnp.where(c == 0, b, a)], axis=1)
                delta, mn, vn = _adam_math(w_ref[...], gv, m_ref[...], v_ref[...])
                og, od, om, ov = outs[4 * k:4 * k + 4]
                og[...] = gv
                od[...] = delta
                om[...] = mn
                ov[...] = vn

    in_specs, out_specs, args = [], [], []
    for k in range(n):
        spec = pl.BlockSpec((tr, C), rows(k, False))
        h_spec = pl.BlockSpec((tr, C), rows(k, True)) if col else pl.BlockSpec((tr, C // 2), rows(k, False))
        in_specs += [spec, h_spec, h_spec, spec, spec]
        out_specs += [spec] * 4
        args += [ws[k], mines[k], theirs[k], ms[k], vs[k]]
    outs, job_outs = _hosted_call(body, name, (n, nrb), in_specs, out_specs,
                                  [jax.ShapeDtypeStruct((R, C), F32)] * (4 * n), [], args, jobs,
                                  ("arbitrary", "arbitrary"))
    return [outs[4 * k:4 * k + 4] for k in range(n)], job_outs


def _small_all_reduce(parts, loss_part, D):
    n = len(parts)

    def body(*refs):
        p_refs, l_ref, o_ref, vec, buf, send, recv = refs[:n], refs[n], refs[n + 1], *refs[n + 2:]
        x, y, c, _ = _place()
        me = 4 * x + 2 * y + c
        vec[...] = jnp.zeros(vec.shape, F32)
        for r in range(n):
            vec[r:r + 1, :] = jnp.sum(p_refs[r][...], axis=0, keepdims=True)
        vec[n:n + 1, 0:LANES] = l_ref[0:1, :]
        buf[me] = vec[...]
        cps = []
        for dd in range(1, 8):
            bx, by, bc = (dd >> 2) & 1, (dd >> 1) & 1, dd & 1
            peer = (x + bx - 2 * x * bx, y + by - 2 * y * by, c + bc - 2 * c * bc)
            cp = pltpu.make_async_remote_copy(
                src_ref=vec, dst_ref=buf.at[me], send_sem=send.at[dd - 1], recv_sem=recv.at[dd - 1],
                device_id=peer, device_id_type=MESH)
            cp.start()
            cps.append(cp)
        for cp in cps:
            cp.wait()
        tot = buf[0]
        for s in range(1, 8):
            tot = tot + buf[s]
        o_ref[...] = tot

    vm = pl.BlockSpec(memory_space=pltpu.VMEM)
    return pl.pallas_call(
        body, name="small_all_reduce",
        in_specs=[vm] * (n + 1), out_specs=vm,
        out_shape=jax.ShapeDtypeStruct((8, D), F32),
        scratch_shapes=[pltpu.VMEM((8, D), F32), pltpu.VMEM((8, 8, D), F32),
                        pltpu.SemaphoreType.DMA((7,)), pltpu.SemaphoreType.DMA((7,))],
    )(*parts, loss_part)


def _small_adam(tot, ws, ms, vs, rows):
    n = len(ws)

    def body(*refs):
        t_ref = refs[0]
        w_refs, m_refs, v_refs = refs[1:1 + n], refs[1 + n:1 + 2 * n], refs[1 + 2 * n:1 + 3 * n]
        outs = refs[1 + 3 * n:]
        for i in range(n):
            r, c0 = rows[i]
            width = w_refs[i].shape[1]
            g = t_ref[r:r + 1, c0:c0 + width]
            delta, mn, vn = _adam_math(w_refs[i][...], g, m_refs[i][...], v_refs[i][...])
            outs[4 * i][...] = g
            outs[4 * i + 1][...] = delta
            outs[4 * i + 2][...] = mn
            outs[4 * i + 3][...] = vn

    vm = pl.BlockSpec(memory_space=pltpu.VMEM)
    out_shape = []
    for w in ws:
        out_shape += [jax.ShapeDtypeStruct(w.shape, F32)] * 4
    return pl.pallas_call(
        body, name="small_adam",
        in_specs=[vm] * (1 + 3 * n), out_specs=[vm] * (4 * n), out_shape=out_shape,
    )(tot, *ws, *ms, *vs)


def kernel(x, positions, norm_mix_g, w_in, norm_out_dil_g, norm_out_sb_g, w_out, norm_ffn_g, w_gate, w_up, w_down, norm_final_g, loss_target, m_norm_mix_g, m_w_in, m_norm_out_dil_g, m_norm_out_sb_g, m_w_out, m_norm_ffn_g, m_w_gate, m_w_up, m_w_down, m_norm_final_g, v_norm_mix_g, v_w_in, v_norm_out_dil_g, v_norm_out_sb_g, v_w_out, v_norm_ffn_g, v_w_gate, v_w_up, v_w_down, v_norm_final_g):
    B, T, D = x.shape
    M = B * T
    n_heads = D // HEAD_DIM
    Hd = n_heads // 2
    Hs = n_heads - Hd
    W = Hd * HEAD_DIM
    N3 = 3 * D
    fs = w_gate.shape[2]
    fp = -(-fs // LANES) * LANES
    assert T % TK_DIL == 0 and W == Hs * HEAD_DIM and fp > fs

    x2 = x.reshape(M, D)
    tgt = loss_target.reshape(M, D)
    tr_ = jnp.transpose
    big = [w_in[0], w_out[0], tr_(w_gate[0]), tr_(w_up[0]), w_down[0]]
    big_m = [m_w_in[0], m_w_out[0], tr_(m_w_gate[0]), tr_(m_w_up[0]), m_w_down[0]]
    big_v = [v_w_in[0], v_w_out[0], tr_(v_w_gate[0]), tr_(v_w_up[0]), v_w_down[0]]
    names = ["w_in", "w_out", "w_gate", "w_up", "w_down"]
    c_arr = jnp.reshape(lax.axis_index("c"), (1,)).astype(jnp.int32)
    j_arr = jnp.reshape(2 * lax.axis_index("x") + lax.axis_index("y"), (1,)).astype(jnp.int32)

    ns_in = w_in.shape[2]
    ks_out = w_out.shape[1]
    geo = [_Weight("col", (D, ns_in), ns_in), _Weight("row", (ks_out, D), ks_out),
           _Weight("row", (fs, D), fp), _Weight("row", (fs, D), fp), _Weight("row", (fs, D), fp)]
    g_in, g_out, g_gate, g_up, g_down = geo
    fulls = [_cast_into_full(big[w], geo[w], j_arr, "cast_" + names[w]) for w in range(5)]
    for w in (2, 3, 4):
        fulls[w] = _zero_pad(fulls[w], geo[w], "zero_pad_" + names[w])

    def chip_sum(w, grad_full, from_sibling):
        return _chip_sum(grad_full, from_sibling, geo[w], c_arr, "chip_sum_" + names[w])

    def final_sum(w, cs, from_chips):
        return _final_sum(cs, from_chips, geo[w], j_arr, "final_sum_" + names[w])

    Win = _gather_two_step(fulls[0], g_in, "gather_w_in")
    tabs3 = _rope_tables(positions)
    tabs2 = [t.reshape(M, HEAD_DIM) for t in tabs3]
    hn = _rmsnorm_fwd(x2, norm_mix_g, "norm_mix")
    proj, (WgT,) = _matmul(hn, Win, "nn", BF16, "in_proj", rope=(tabs2, 2 * W),
                           jobs=[_gather_job(fulls[2], g_gate, "ici")])
    proj3 = proj.reshape(B, T, N3)
    bias = _dil_bias(T)
    (o_dil3, lse3), (WgT, Wout) = _dil_fwd(
        proj3, bias, Hd, "dil_fwd", jobs=[_gather_job(WgT, g_gate, "d2d"), _gather_job(fulls[1], g_out, "ici")])
    (o_sb3, lt3), (Wout, WuT) = _sb_fwd(
        proj3, Hd, Hs, "sb_fwd", jobs=[_gather_job(Wout, g_out, "d2d"), _gather_job(fulls[3], g_up, "ici")])
    o_dil, o_sb = o_dil3.reshape(M, W), o_sb3.reshape(M, W)
    g_heads = jnp.concatenate([norm_out_dil_g, norm_out_sb_g], axis=1)
    o_mix = _headnorm_fwd(o_dil, o_sb, g_heads, "head_norm")
    h1, (WuT,) = _matmul(o_mix, Wout, "nn", F32, "out_proj", res=x2, jobs=[_gather_job(WuT, g_up, "d2d")])
    hn2 = _rmsnorm_fwd(h1, norm_ffn_g, "norm_ffn")
    (gate, up, act), (Wd,) = _gate_up(hn2, WgT, WuT, "gate_up", jobs=[_gather_job(fulls[4], g_down, "ici")])
    (Wd,) = _run_job(_gather_job(Wd, g_down, "d2d"), "gather_w_down_d2d")
    h2 = _matmul(act, Wd, "nn", F32, "down_proj", res=h1)
    dh2, dh2_b, loss_part, dg_final = _loss_head(h2, tgt, norm_final_g.reshape(1, D), "loss_head")

    dWd = _matmul(act, dh2_b, "tn", BF16, "dw_down")
    (d_gate, d_up), (sib_d,) = _matmul(dh2_b, Wd, "nt", BF16, "d_act", swiglu=(gate, up), tn=512,
                                       jobs=[_sibling_job(dWd, g_down)])
    cs_d = chip_sum(4, dWd, sib_d)
    dWg, (chips_d,) = _matmul(d_gate, hn2, "tn", BF16, "dw_gate", jobs=[_chips_job(cs_d, g_down, (0, 2))])
    dWu, (chips_d, sib_g) = _matmul(d_up, hn2, "tn", BF16, "dw_up",
                                    jobs=[_chips_job(cs_d, g_down, (1, 2), chips_d), _sibling_job(dWg, g_gate)])
    half_d = final_sum(4, cs_d, chips_d)
    cs_g = chip_sum(2, dWg, sib_g)
    d_hn2, (chips_g, other_d) = _matmul(d_gate, WgT, "nn", F32, "d_hn2_gate",
                                        jobs=[_chips_job(cs_g, g_gate, (0, 2)), _final_job(half_d, g_down)])
    d_hn2, (chips_g, sib_u) = _matmul(d_up, WuT, "nn", F32, "d_hn2_up", res=d_hn2,
                                      jobs=[_chips_job(cs_g, g_gate, (1, 2), chips_g), _sibling_job(dWu, g_up)])
    half_g = final_sum(2, cs_g, chips_g)
    cs_u = chip_sum(3, dWu, sib_u)
    dh1, dh1_b, dg_ffn = _rmsnorm_bwd(h1, d_hn2, norm_ffn_g, dh2, "norm_ffn_bwd", True)
    dWo, (other_g,) = _matmul(o_mix, dh1_b, "tn", BF16, "dw_out", jobs=[_final_job(half_g, g_gate)])
    d_mix, (sib_o,) = _matmul(dh1_b, Wout, "nt", BF16, "d_mix", jobs=[_sibling_job(dWo, g_out)])
    cs_o = chip_sum(1, dWo, sib_o)
    d_o, dg_heads = _headnorm_bwd(d_mix, o_dil, o_sb, g_heads, "head_norm_bwd")
    d_o3 = d_o.reshape(B, T, D)
    dqkv_dil, (chips_u,) = _dil_bwd(proj3, bias, tabs3, d_o3, o_dil3, lse3, Hd, "dil_bwd",
                                    jobs=[_chips_job(cs_u, g_up, (0, 2))])
    dqkv_sb, (chips_u, chips_o) = _sb_bwd(proj3, d_o3, lt3, Hd, Hs, "sb_bwd",
                                          jobs=[_chips_job(cs_u, g_up, (1, 2), chips_u), _chips_job(cs_o, g_out)])
    half_u = final_sum(3, cs_u, chips_u)
    half_o = final_sum(1, cs_o, chips_o)
    dproj = jnp.concatenate([*dqkv_dil, *dqkv_sb], axis=-1).reshape(M, N3)
    dWin, (other_u, other_o) = _matmul(hn, dproj, "tn", BF16, "dw_in",
                                       jobs=[_final_job(half_u, g_up), _final_job(half_o, g_out)])
    (sib_i,) = _run_job(_sibling_job(dWin, g_in), "sibling_w_in")
    cs_i = chip_sum(0, dWin, sib_i)
    d_hn, (chips_i,) = _matmul(dproj, Win, "nt", F32, "d_hn", jobs=[_chips_job(cs_i, g_in)])
    half_i = final_sum(0, cs_i, chips_i)
    dx, dg_mix = _rmsnorm_bwd(x2, d_hn, norm_mix_g, dh1, "norm_mix_bwd", False)
    (other_i,) = _run_job(_final_job(half_i, g_in), "final_w_in")

    mine = [half_i, half_o, half_g, half_u, half_d]
    theirs = [other_i, other_o, other_g, other_u, other_d]
    big_out = [_adam([big[w]], [mine[w]], [theirs[w]], [big_m[w]], [big_v[w]], geo[w], "adam_" + names[w])[0][0]
               for w in range(5)]
    for w in (2, 3):
        big_out[w] = [tr_(a) for a in big_out[w]]

    tot = _small_all_reduce([dg_mix, dg_heads, dg_ffn, dg_final], loss_part, D)
    loss = tot[4, 0]
    small_w = [norm_mix_g, norm_out_dil_g, norm_out_sb_g, norm_ffn_g, norm_final_g.reshape(1, D)]
    small_m = [m_norm_mix_g, m_norm_out_dil_g, m_norm_out_sb_g, m_norm_ffn_g, m_norm_final_g.reshape(1, D)]
    small_v = [v_norm_mix_g, v_norm_out_dil_g, v_norm_out_sb_g, v_norm_ffn_g, v_norm_final_g.reshape(1, D)]
    so = _small_adam(tot, small_w, small_m, small_v, [(0, 0), (1, 0), (1, W), (2, 0), (3, 0)])
    small_out = [so[4 * i:4 * i + 4] for i in range(5)]
    small_out[4] = [a.reshape(D) for a in small_out[4]]

    per_weight = [small_out[0], big_out[0], small_out[1], small_out[2], big_out[1], small_out[3],
                  big_out[2], big_out[3], big_out[4], small_out[4]]

    def field(i):
        res = []
        for n_, o in enumerate(per_weight):
            a = o[i]
            res.append(a[None] if n_ in (1, 4, 6, 7, 8) else a)
        return res

    return (loss, dx.reshape(B, T, D), *field(0), *field(1), *field(2), *field(3))
```
